```python
import math
import jax, jax.numpy as jnp
from jax import lax
import numpy as np

D_MODEL = 1024
BATCH = 4
SEQ = 4096
DEPTH = 2
DEC_BATCH = 32
DEC_SEQ = 1
PAST_LEN = 8192
PAGE_SIZE = 128

N_EVEN = (DEPTH + 1) // 2
N_ODD = DEPTH // 2
HA = 4
DA = D_MODEL // (4 * HA)
DVA = 2 * DA
HB = 4
DKB = D_MODEL // (2 * HB)
DVB = D_MODEL // (2 * HB)
C_QKV_B = 2 * HB * DKB + HB * DVB
GDN_CONV = 4
GDN_CHUNK = 64
D_CONV = D_MODEL
CONV_W = 31
D_FF = -(-8 * D_MODEL // (3 * 256)) * 256
NUM_BUCKETS = 32
MAX_DISTANCE = 128
QBLOCK = 128
RMS_EPS = 1e-6
LN_EPS = 1e-5
IN_SIZES = (HA * 2 * DA, HA * 2 * DA, HA * DVA, C_QKV_B, HB * DVB, HB, HB)
IN_COLS = sum(IN_SIZES)
IN_SPLITS = tuple(int(s) for s in np.cumsum(IN_SIZES)[:-1])
MIX_OUT = HA * DVA + HB * DVB

kernel_name = 'hybrid_diffattn_gdn_conformer_step'


def rms_norm(x, w, eps=RMS_EPS):
    xf = x.astype(jnp.float32)
    y = xf * lax.rsqrt(jnp.mean(xf * xf, axis=-1, keepdims=True) + eps)
    return (y * w.astype(jnp.float32)).astype(x.dtype)


def layer_norm(x, g, b, eps=LN_EPS):
    xf = x.astype(jnp.float32)
    mu = jnp.mean(xf, axis=-1, keepdims=True)
    xc = xf - mu
    var = jnp.mean(xc * xc, axis=-1, keepdims=True)
    y = xc * lax.rsqrt(var + eps) * g.astype(jnp.float32) + b.astype(jnp.float32)
    return y.astype(x.dtype)


def l2_norm(x):
    xf = x.astype(jnp.float32)
    return xf * lax.rsqrt(jnp.sum(xf * xf, axis=-1, keepdims=True) + 1e-6)


def swiglu(x, wg, wu, wd):
    return (jax.nn.silu(x @ wg) * (x @ wu)) @ wd


def causal_dwconv(x_ext, w):
    return lax.conv_general_dilated(
        x_ext, w[:, None, :], window_strides=(1,), padding='VALID',
        dimension_numbers=('NWC', 'WIO', 'NWC'), feature_group_count=x_ext.shape[-1])


def t5_bucket(q_pos, k_pos):
    n = jnp.maximum(q_pos[:, None] - k_pos[None, :], 0)
    max_exact = NUM_BUCKETS // 2
    nf = jnp.maximum(n, 1).astype(jnp.float32)
    large = max_exact + (jnp.log(nf / max_exact) / math.log(MAX_DISTANCE / max_exact)
                         * (NUM_BUCKETS - max_exact)).astype(jnp.int32)
    large = jnp.minimum(large, NUM_BUCKETS - 1)
    return jnp.where(n < max_exact, n, large)


def diff_attn_core(q, k, v, q_pos, k_pos, lam, rel_bias):
    bias = jnp.transpose(rel_bias[t5_bucket(q_pos, k_pos)], (2, 0, 1)).astype(jnp.float32)
    s = jnp.einsum('bqhmd,bkhmd->bhmqk', q, k, preferred_element_type=jnp.float32) * DA ** -0.5
    s = s + bias[None, :, None]
    s = jnp.where(k_pos[None, :] <= q_pos[:, None], s, -jnp.inf)
    p = jax.nn.softmax(s, axis=-1)
    a = p[:, :, 0] - lam * p[:, :, 1]
    return jnp.einsum('bhqk,bkhe->bqhe', a.astype(v.dtype), v)


def diff_attn_prompt(q, k, v, lam, rel_bias):
    B, T = q.shape[:2]
    nb = T // QBLOCK
    qb = jnp.moveaxis(q.reshape(B, nb, QBLOCK, HA, 2, DA), 1, 0)
    k_pos = jnp.arange(T)

    def block(args):
        q_blk, i = args
        q_pos = i * QBLOCK + jnp.arange(QBLOCK)
        return diff_attn_core(q_blk, k, v, q_pos, k_pos, lam, rel_bias)

    o = lax.map(block, (qb, jnp.arange(nb)))
    return jnp.moveaxis(o, 0, 1).reshape(B, T, HA, DVA)


def diff_attn_sample(q, k_new, v_new, k_pool, v_pool, page_table, lam, rel_bias):
    db, s = q.shape[:2]
    past = page_table.shape[1] * k_pool.shape[1]
    k_past = k_pool[page_table].reshape(db, past, HA, 2, DA)
    v_past = v_pool[page_table].reshape(db, past, HA, DVA)
    k = jnp.concatenate([k_past, k_new.astype(k_past.dtype)], axis=1)
    v = jnp.concatenate([v_past, v_new.astype(v_past.dtype)], axis=1)
    q_pos = past + jnp.arange(s)
    k_pos = jnp.arange(past + s)
    return diff_attn_core(q, k, v, q_pos, k_pos, lam, rel_bias)


def gated_delta_chunked(q, k, v, beta, g, s0):
    f32 = jnp.float32
    B, T, H, DK = q.shape
    DV = v.shape[-1]
    C = min(GDN_CHUNK, T)
    pad = (-T) % C
    n = (T + pad) // C

    def prep(x):
        x = jnp.pad(x.astype(f32), [(0, 0), (0, pad)] + [(0, 0)] * (x.ndim - 2))
        x = x.reshape((B, n, C) + x.shape[2:])
        return jnp.moveaxis(jnp.moveaxis(x, 3, 2), 1, 0)

    q = prep(q) * DK ** -0.5
    k, v, beta, g = prep(k), prep(v), prep(beta), prep(g)
    g = jnp.cumsum(g, axis=-1)
    idx = jnp.arange(C)
    incl = idx[:, None] >= idx[None, :]
    strict = idx[:, None] > idx[None, :]
    diff = g[..., :, None] - g[..., None, :]
    decay = jnp.where(incl, jnp.exp(jnp.where(incl, diff, 0.0)), 0.0)
    k_beta = k * beta[..., None]
    v_beta = v * beta[..., None]
    L = jnp.where(strict, jnp.einsum('...id,...jd->...ij', k_beta, k) * decay, 0.0)
    eye = jnp.eye(C, dtype=f32)
    t_inv = lax.linalg.triangular_solve(eye + L, jnp.broadcast_to(eye, L.shape),
                                        left_side=True, lower=True, unit_diagonal=True)
    u = t_inv @ v_beta
    w = t_inv @ (k_beta * jnp.exp(g)[..., None])
    attn = jnp.einsum('...id,...jd->...ij', q, k) * decay
    g_last = g[..., -1]
    k_dec = k * jnp.exp(g_last[..., None] - g)[..., None]
    q_dec = q * jnp.exp(g)[..., None]

    def step(S, xs):
        u_c, w_c, attn_c, q_c, k_c, gl = xs
        v_new = u_c - w_c @ S
        o = q_c @ S + attn_c @ v_new
        S = S * jnp.exp(gl)[..., None, None] + jnp.einsum('bhck,bhcv->bhkv', k_c, v_new)
        return S, o

    S, o = lax.scan(step, s0.astype(f32), (u, w, attn, q_dec, k_dec, g_last))
    o = jnp.moveaxis(jnp.moveaxis(o, 0, 1), 2, 3).reshape(B, n * C, H, DV)[:, :T]
    return o, S


def gated_deltanet(qkv, beta_in, a_in, z, conv_state, s0, conv_w, a_log, dt_bias, norm_w):
    B, T = qkv.shape[:2]
    x_ext = jnp.concatenate([conv_state.astype(qkv.dtype), qkv], axis=1)
    conv_new = x_ext[:, -(GDN_CONV - 1):]
    h = jax.nn.silu(causal_dwconv(x_ext, conv_w))
    q, k, v = jnp.split(h, [HB * DKB, 2 * HB * DKB], axis=-1)
    q = l2_norm(q.reshape(B, T, HB, DKB))
    k = l2_norm(k.reshape(B, T, HB, DKB))
    v = v.reshape(B, T, HB, DVB)
    beta = jax.nn.sigmoid(beta_in.astype(jnp.float32))
    g = -jnp.exp(a_log.astype(jnp.float32)) * jax.nn.softplus(
        a_in.astype(jnp.float32) + dt_bias.astype(jnp.float32))
    o, s_new = gated_delta_chunked(q, k, v, beta, g, s0)
    o = rms_norm(o, norm_w) * jax.nn.silu(z.reshape(B, T, HB, DVB).astype(jnp.float32))
    return o.reshape(B, T, HB * DVB).astype(qkv.dtype), conv_new, s_new.astype(qkv.dtype)


def conformer_conv(xn, buf, w_pw1, b_pw1, w_dw, b_dw, ln_g, ln_b, w_pw2, b_pw2):
    a, gate = jnp.split(xn @ w_pw1 + b_pw1, 2, axis=-1)
    h = a * jax.nn.sigmoid(gate)
    x_ext = jnp.concatenate([buf.astype(h.dtype), h], axis=1)
    buf_new = x_ext[:, -(CONV_W - 1):]
    h = causal_dwconv(x_ext, w_dw) + b_dw
    h = jax.nn.silu(layer_norm(h, ln_g, ln_b))
    return h @ w_pw2 + b_pw2, buf_new


def setup_inputs(seed: int = 0) -> dict:
    key = jax.random.key(seed)
    keys = iter(jax.random.split(key, 48))
    f32 = jnp.float32

    def normal(shape, scale):
        return jax.random.normal(next(keys), shape, f32) * scale

    def gain(shape):
        return 1.0 + normal(shape, 0.05)

    n_pages = PAST_LEN // PAGE_SIZE
    n_used = DEC_BATCH * n_pages
    n_pool = n_used + max(1, n_used // 4)
    page_table = jax.random.permutation(next(keys), n_pool)[:n_used].reshape(
        DEC_BATCH, n_pages).astype(jnp.int32)
    dt = jnp.exp(jax.random.uniform(next(keys), (N_EVEN, HB), f32, math.log(1e-3), math.log(1e-1)))
    a_log = jnp.log(jax.random.uniform(next(keys), (N_EVEN, HB), f32, 1.0, 16.0))
    return {
        'x_prompt': normal((BATCH, SEQ, D_MODEL), 1.0),
        'x_sample': normal((DEC_BATCH, DEC_SEQ, D_MODEL), 1.0),
        'cache_attn_k': normal((N_EVEN, n_pool, PAGE_SIZE, HA, 2 * DA), 1.0),
        'cache_attn_v': normal((N_EVEN, n_pool, PAGE_SIZE, HA, DVA), 1.0),
        'page_table': page_table,
        'state_gdn_conv': normal((N_EVEN, DEC_BATCH, GDN_CONV - 1, C_QKV_B), 1.0),
        'state_gdn_s': normal((N_EVEN, DEC_BATCH, HB, DKB, DVB), 0.3),
        'state_conv_buf': normal((N_ODD, DEC_BATCH, CONV_W - 1, D_CONV), 0.5),
        'norm_mix': gain((DEPTH, D_MODEL)),
        'norm_ffn': gain((DEPTH, D_MODEL)),
        'norm_final': gain((D_MODEL,)),
        'w_in': normal((N_EVEN, D_MODEL, IN_COLS), D_MODEL ** -0.5),
        'w_out': normal((N_EVEN, MIX_OUT, D_MODEL), MIX_OUT ** -0.5),
        'gdn_conv_w': normal((N_EVEN, GDN_CONV, C_QKV_B), GDN_CONV ** -0.5),
        'gdn_A_log': a_log,
        'gdn_dt_bias': dt + jnp.log(-jnp.expm1(-dt)),
        'gdn_norm_w': gain((N_EVEN, DVB)),
        'lam_q1': normal((N_EVEN, DA), 0.1),
        'lam_k1': normal((N_EVEN, DA), 0.1),
        'lam_q2': normal((N_EVEN, DA), 0.1),
        'lam_k2': normal((N_EVEN, DA), 0.1),
        'diff_subln_w': gain((N_EVEN, DVA)),
        'rel_bias': normal((NUM_BUCKETS, HA), 0.5),
        'conv_w_pw1': normal((N_ODD, D_MODEL, 2 * D_CONV), D_MODEL ** -0.5),
        'conv_b_pw1': normal((N_ODD, 2 * D_CONV), 0.02),
        'conv_w_dw': normal((N_ODD, CONV_W, D_CONV), CONV_W ** -0.5),
        'conv_b_dw': normal((N_ODD, D_CONV), 0.02),
        'conv_ln_g': gain((N_ODD, D_CONV)),
        'conv_ln_b': normal((N_ODD, D_CONV), 0.02),
        'conv_w_pw2': normal((N_ODD, D_CONV, D_MODEL), D_CONV ** -0.5),
        'conv_b_pw2': normal((N_ODD, D_MODEL), 0.02),
        'ffn_w_gate': normal((DEPTH, D_MODEL, D_FF), D_MODEL ** -0.5),
        'ffn_w_up': normal((DEPTH, D_MODEL, D_FF), D_MODEL ** -0.5),
        'ffn_w_down': normal((DEPTH, D_FF, D_MODEL), D_FF ** -0.5),
    }


def reference(x_prompt, x_sample, cache_attn_k, cache_attn_v, page_table,
              state_gdn_conv, state_gdn_s, state_conv_buf,
              norm_mix, norm_ffn, norm_final, w_in, w_out,
              gdn_conv_w, gdn_A_log, gdn_dt_bias, gdn_norm_w,
              lam_q1, lam_k1, lam_q2, lam_k2, diff_subln_w, rel_bias,
              conv_w_pw1, conv_b_pw1, conv_w_dw, conv_b_dw, conv_ln_g, conv_ln_b,
              conv_w_pw2, conv_b_pw2, ffn_w_gate, ffn_w_up, ffn_w_down):
    f32 = jnp.float32
    xp, xs = x_prompt, x_sample
    k_p, v_p, k_s, v_s = [], [], [], []
    gc_p, gc_s, gs_p, gs_s = [], [], [], []
    cb_p, cb_s = [], []
    for layer in range(DEPTH):
        if layer % 2 == 0:
            e = layer // 2
            lam_init = 0.8 - 0.6 * math.exp(-0.3 * layer)
            lam = (jnp.exp(jnp.sum(lam_q1[e].astype(f32) * lam_k1[e].astype(f32)))
                   - jnp.exp(jnp.sum(lam_q2[e].astype(f32) * lam_k2[e].astype(f32))) + lam_init)

            def mix_even(x, attend, conv_state, s0):
                B, T, _ = x.shape
                h = rms_norm(x, norm_mix[layer]) @ w_in[e]
                qa, ka, va, qkv_b, z_b, beta_b, a_b = jnp.split(h, IN_SPLITS, axis=-1)
                ka = ka.reshape(B, T, HA, 2, DA)
                va = va.reshape(B, T, HA, DVA)
                oa = attend(qa.reshape(B, T, HA, 2, DA), ka, va)
                oa = (rms_norm(oa, diff_subln_w[e], LN_EPS) * (1.0 - lam_init)).reshape(B, T, HA * DVA)
                ob, conv_new, s_new = gated_deltanet(qkv_b, beta_b, a_b, z_b, conv_state, s0,
                                                     gdn_conv_w[e], gdn_A_log[e], gdn_dt_bias[e],
                                                     gdn_norm_w[e])
                y = jnp.concatenate([oa.astype(x.dtype), ob], axis=-1) @ w_out[e]
                return x + y, ka.reshape(B, T, HA, 2 * DA), va, conv_new, s_new

            bp = xp.shape[0]
            xp, kr, vr, cn, sn = mix_even(
                xp, lambda q, k, v: diff_attn_prompt(q, k, v, lam, rel_bias),
                jnp.zeros((bp, GDN_CONV - 1, C_QKV_B), xp.dtype),
                jnp.zeros((bp, HB, DKB, DVB), f32))
            k_p.append(kr); v_p.append(vr); gc_p.append(cn); gs_p.append(sn)
            xs, kr, vr, cn, sn = mix_even(
                xs, lambda q, k, v: diff_attn_sample(q, k, v, cache_attn_k[e], cache_attn_v[e],
                                                     page_table, lam, rel_bias),
                state_gdn_conv[e], state_gdn_s[e])
            k_s.append(kr); v_s.append(vr); gc_s.append(cn); gs_s.append(sn)
        else:
            c = layer // 2

            def mix_odd(x, buf):
                y, buf_new = conformer_conv(rms_norm(x, norm_mix[layer]), buf,
                                            conv_w_pw1[c], conv_b_pw1[c], conv_w_dw[c], conv_b_dw[c],
                                            conv_ln_g[c], conv_ln_b[c], conv_w_pw2[c], conv_b_pw2[c])
                return x + y, buf_new

            xp, bn = mix_odd(xp, jnp.zeros((xp.shape[0], CONV_W - 1, D_CONV), xp.dtype))
            cb_p.append(bn)
            xs, bn = mix_odd(xs, state_conv_buf[c])
            cb_s.append(bn)
        xp = xp + swiglu(rms_norm(xp, norm_ffn[layer]), ffn_w_gate[layer], ffn_w_up[layer], ffn_w_down[layer])
        xs = xs + swiglu(rms_norm(xs, norm_ffn[layer]), ffn_w_gate[layer], ffn_w_up[layer], ffn_w_down[layer])
    y_prompt = rms_norm(xp, norm_final)
    y_sample = rms_norm(xs, norm_final)
    new_k_prompt = jnp.stack(k_p)
    new_v_prompt = jnp.stack(v_p)
    new_k_sample = jnp.stack(k_s)
    new_v_sample = jnp.stack(v_s)
    new_gdn_conv_prompt = jnp.stack(gc_p)
    new_gdn_conv_sample = jnp.stack(gc_s)
    new_gdn_s_prompt = jnp.stack(gs_p)
    new_gdn_s_sample = jnp.stack(gs_s)
    new_conv_buf_prompt = jnp.stack(cb_p)
    new_conv_buf_sample = jnp.stack(cb_s)
    return (y_prompt, y_sample, new_k_prompt, new_v_prompt, new_k_sample, new_v_sample,
            new_gdn_conv_prompt, new_gdn_conv_sample, new_gdn_s_prompt, new_gdn_s_sample,
            new_conv_buf_prompt, new_conv_buf_sample)
```

```python
import functools
import math

import jax
import jax.numpy as jnp
from jax import lax
from jax.experimental import pallas as pl
from jax.experimental.pallas import tpu as pltpu

F32 = jnp.float32
BF16 = jnp.bfloat16

RMS_EPS = 1e-6
LN_EPS = 1e-5
L2_EPS = 1e-6
NUM_BUCKETS = 32
MAX_DISTANCE = 128
GDN_CHUNK = 64
NEG_BIG = -1e30
LANES = 128
VMEM_LIMIT = 48 * 1024 * 1024

_HI = lax.Precision.HIGHEST


def _cparams(*sem):
    return pltpu.CompilerParams(dimension_semantics=sem, vmem_limit_bytes=VMEM_LIMIT)


def _mm(a, b):
    return jnp.dot(a.astype(BF16), b.astype(BF16), preferred_element_type=F32)


def _mm_nt(a, b):
    return lax.dot_general(a.astype(BF16), b.astype(BF16), (((1,), (1,)), ((), ())),
                           preferred_element_type=F32)


def _mm_tn(a, b):
    return lax.dot_general(a.astype(BF16), b.astype(BF16), (((0,), (0,)), ((), ())),
                           preferred_element_type=F32)


def _mm_hi(a, b):
    return jnp.dot(a, b, precision=_HI, preferred_element_type=F32)


def _mm_nt_hi(a, b):
    return lax.dot_general(a, b, (((1,), (1,)), ((), ())), precision=_HI,
                           preferred_element_type=F32)


def _rms(x, w, eps):
    return x * lax.rsqrt(jnp.mean(x * x, axis=-1, keepdims=True) + eps) * w


def _sigmoid(x):
    return 1.0 / (1.0 + jnp.exp(-x))


def _silu(x):
    return x * _sigmoid(x)


def _lane_pick(x, lane_idx, k):
    return jnp.sum(jnp.where(lane_idx == k, x, 0.0), axis=-1, keepdims=True)


def _inproj_kernel(x_ref, nw_ref, w_ref, *out_refs, groups):
    xn = _rms(x_ref[...], nw_ref[...], RMS_EPS).astype(BF16)
    for o_ref, (off, width) in zip(out_refs, groups):
        o_ref[...] = jnp.dot(xn, w_ref[:, off:off + width], preferred_element_type=F32)


def _in_projection(x, norm_w, w_bf16, groups, tm):
    m, d = x.shape
    n = w_bf16.shape[1]
    return pl.pallas_call(
        functools.partial(_inproj_kernel, groups=groups),
        grid=(pl.cdiv(m, tm),),
        in_specs=[pl.BlockSpec((tm, d), lambda i: (i, 0)),
                  pl.BlockSpec((1, d), lambda i: (0, 0)),
                  pl.BlockSpec((d, n), lambda i: (0, 0))],
        out_specs=[pl.BlockSpec((tm, wd), lambda i: (i, 0)) for _, wd in groups],
        out_shape=[jax.ShapeDtypeStruct((m, wd), F32) for _, wd in groups],
        compiler_params=_cparams("parallel"),
        name="in_projection",
    )(x, norm_w.reshape(1, d), w_bf16)


def _glu_proj_kernel(x_ref, nw_ref, w_ref, b_ref, o_ref, *, dc):
    xn = _rms(x_ref[...], nw_ref[...], RMS_EPS).astype(BF16)
    a = jnp.dot(xn, w_ref[:, :dc], preferred_element_type=F32) + b_ref[:, :dc]
    g = jnp.dot(xn, w_ref[:, dc:], preferred_element_type=F32) + b_ref[:, dc:]
    o_ref[...] = a * _sigmoid(g)


def _glu_projection(x, norm_w, w_bf16, b, tm):
    m, d = x.shape
    n = w_bf16.shape[1]
    dc = n // 2
    return pl.pallas_call(
        functools.partial(_glu_proj_kernel, dc=dc),
        grid=(pl.cdiv(m, tm),),
        in_specs=[pl.BlockSpec((tm, d), lambda i: (i, 0)),
                  pl.BlockSpec((1, d), lambda i: (0, 0)),
                  pl.BlockSpec((d, n), lambda i: (0, 0)),
                  pl.BlockSpec((1, n), lambda i: (0, 0))],
        out_specs=pl.BlockSpec((tm, dc), lambda i: (i, 0)),
        out_shape=jax.ShapeDtypeStruct((m, dc), F32),
        compiler_params=_cparams("parallel"),
        name="glu_projection",
    )(x, norm_w.reshape(1, d), w_bf16, b.reshape(1, n))


def _proj_res_kernel(*refs, n_in):
    a_refs = refs[:n_in]
    w_refs = refs[n_in:2 * n_in]
    b_ref, r_ref, o_ref = refs[2 * n_in:]
    acc = r_ref[...] + b_ref[...]
    for a_ref, w_ref in zip(a_refs, w_refs):
        acc = acc + _mm(a_ref[...], w_ref[...])
    o_ref[...] = acc


def _projection_residual(acts, weights_bf16, bias, res, tm):
    m, d = res.shape
    n_in = len(acts)
    in_specs = ([pl.BlockSpec((tm, a.shape[1]), lambda i: (i, 0)) for a in acts]
                + [pl.BlockSpec(w.shape, lambda i: (0, 0)) for w in weights_bf16]
                + [pl.BlockSpec((1, d), lambda i: (0, 0)),
                   pl.BlockSpec((tm, d), lambda i: (i, 0))])
    return pl.pallas_call(
        functools.partial(_proj_res_kernel, n_in=n_in),
        grid=(pl.cdiv(m, tm),),
        in_specs=in_specs,
        out_specs=pl.BlockSpec((tm, d), lambda i: (i, 0)),
        out_shape=jax.ShapeDtypeStruct((m, d), F32),
        compiler_params=_cparams("parallel"),
        name="projection_residual",
    )(*acts, *weights_bf16, bias.reshape(1, d), res)


def _ffn_kernel(x_ref, nw_ref, wg_ref, wu_ref, wd_ref, fw_ref, o_ref, xn_ref, acc_ref, *,
                final_norm):
    k = pl.program_id(1)

    @pl.when(k == 0)
    def _():
        xn_ref[...] = _rms(x_ref[...], nw_ref[...], RMS_EPS).astype(BF16)
        acc_ref[...] = x_ref[...]

    xn = xn_ref[...]
    g = jnp.dot(xn, wg_ref[...], preferred_element_type=F32)
    u = jnp.dot(xn, wu_ref[...], preferred_element_type=F32)
    acc_ref[...] += _mm(_silu(g) * u, wd_ref[...])

    @pl.when(k == pl.num_programs(1) - 1)
    def _():
        y = acc_ref[...]
        if final_norm:
            y = _rms(y, fw_ref[...], RMS_EPS)
        o_ref[...] = y


def _ffn(x, norm_w, wg, wu, wd, final_w, final_norm, tm, tf):
    m, d = x.shape
    f = wg.shape[1]
    return pl.pallas_call(
        functools.partial(_ffn_kernel, final_norm=final_norm),
        grid=(pl.cdiv(m, tm), f // tf),
        in_specs=[pl.BlockSpec((tm, d), lambda i, k: (i, 0)),
                  pl.BlockSpec((1, d), lambda i, k: (0, 0)),
                  pl.BlockSpec((d, tf), lambda i, k: (0, k)),
                  pl.BlockSpec((d, tf), lambda i, k: (0, k)),
                  pl.BlockSpec((tf, d), lambda i, k: (k, 0)),
                  pl.BlockSpec((1, d), lambda i, k: (0, 0))],
        out_specs=pl.BlockSpec((tm, d), lambda i, k: (i, 0)),
        out_shape=jax.ShapeDtypeStruct((m, d), F32),
        scratch_shapes=[pltpu.VMEM((tm, d), BF16), pltpu.VMEM((tm, d), F32)],
        compiler_params=_cparams("parallel", "arbitrary"),
        name="swiglu_ffn",
    )(x, norm_w.reshape(1, d), wg, wu, wd, final_w.reshape(1, d))


def _t5_bucket(n):
    max_exact = NUM_BUCKETS // 2
    nf = jnp.maximum(n, 1).astype(F32)
    large = max_exact + (jnp.log(nf / max_exact) / math.log(MAX_DISTANCE / max_exact)
                         * (NUM_BUCKETS - max_exact)).astype(jnp.int32)
    large = jnp.minimum(large, NUM_BUCKETS - 1)
    return jnp.where(n < max_exact, n, large)


def _lambda(lq1_ref, lk1_ref, lq2_ref, lk2_ref, lam_init):
    s1 = jnp.sum(lq1_ref[...] * lk1_ref[...], axis=-1, keepdims=True)
    s2 = jnp.sum(lq2_ref[...] * lk2_ref[...], axis=-1, keepdims=True)
    return jnp.exp(s1) - jnp.exp(s2) + lam_init


def _attn_prompt_kernel(q_ref, k_ref, v_ref, bias_ref, lq1_ref, lk1_ref, lq2_ref, lk2_ref,
                        sw_ref, o_ref, qs_ref, m_ref, l_ref, acc_ref, *, t, da, lam_init):
    i = pl.program_id(2)
    q = q_ref[...] * (da ** -0.5)
    lane = lax.broadcasted_iota(jnp.int32, q.shape, 1)
    first_map = lane < da
    qs_ref[0:t, :] = jnp.where(first_map, q, 0.0).astype(BF16)
    qs_ref[t:2 * t, :] = jnp.where(first_map, 0.0, q).astype(BF16)
    m_ref[...] = jnp.full(m_ref.shape, NEG_BIG, F32)
    l_ref[...] = jnp.zeros(l_ref.shape, F32)
    acc_ref[...] = jnp.zeros(acc_ref.shape, F32)

    def block(row0, bias):
        k = k_ref[pl.ds(row0, t), :]
        v = v_ref[pl.ds(row0, t), :]
        s = _mm_nt(qs_ref[...], k)
        if bias is not None:
            s = (s.reshape(2, t, t) + bias[None]).reshape(2 * t, t)
        m_prev = m_ref[...]
        m_new = jnp.maximum(m_prev, jnp.max(s, axis=-1, keepdims=True))
        p = jnp.exp(s - m_new)
        alpha = jnp.exp(m_prev - m_new)
        l_ref[...] = alpha * l_ref[...] + jnp.sum(p, axis=-1, keepdims=True)
        acc_ref[...] = alpha * acc_ref[...] + _mm(p, v)
        m_ref[...] = m_new

    def far_body(j, carry):
        block(pl.multiple_of(j * t, t), None)
        return carry

    lax.fori_loop(0, i - 1, far_body, 0)

    @pl.when(i >= 1)
    def _():
        block(pl.multiple_of((i - 1) * t, t), bias_ref[0])

    block(pl.multiple_of(i * t, t), bias_ref[1])

    o12 = acc_ref[...] / l_ref[...]
    lam = _lambda(lq1_ref, lk1_ref, lq2_ref, lk2_ref, lam_init)
    o = o12[0:t] - lam * o12[t:2 * t]
    o_ref[...] = _rms(o, sw_ref[...], LN_EPS) * (1.0 - lam_init)


def _prompt_bias_tiles(rel_bias, t):
    assert t >= MAX_DISTANCE
    qi = jnp.arange(t)[:, None]
    kj = jnp.arange(t)[None, :]
    far = rel_bias[NUM_BUCKETS - 1]
    prev = rel_bias[_t5_bucket(qi - kj + t)] - far
    diag = rel_bias[_t5_bucket(jnp.maximum(qi - kj, 0))] - far
    diag = jnp.where((kj <= qi)[:, :, None], diag, NEG_BIG)
    return jnp.transpose(jnp.stack([prev, diag]), (3, 0, 1, 2)).astype(F32)


def _attention_prompt(q, k, v, rel_bias, lam_vecs, subln_w, batch, seq, heads, lam_init, t):
    dv = q.shape[1] // heads
    da = dv // 2
    nq = seq // t
    bias = _prompt_bias_tiles(rel_bias, t)
    vec = lambda a: a.reshape(1, -1)
    const = lambda b, h, i: (0, 0)
    return pl.pallas_call(
        functools.partial(_attn_prompt_kernel, t=t, da=da, lam_init=lam_init),
        grid=(batch, heads, nq),
        in_specs=[pl.BlockSpec((t, dv), lambda b, h, i: (b * nq + i, h)),
                  pl.BlockSpec((seq, dv), lambda b, h, i: (b, h)),
                  pl.BlockSpec((seq, dv), lambda b, h, i: (b, h)),
                  pl.BlockSpec((None, 2, t, t), lambda b, h, i: (h, 0, 0, 0)),
                  pl.BlockSpec((1, da), const), pl.BlockSpec((1, da), const),
                  pl.BlockSpec((1, da), const), pl.BlockSpec((1, da), const),
                  pl.BlockSpec((1, dv), const)],
        out_specs=pl.BlockSpec((t, dv), lambda b, h, i: (b * nq + i, h)),
        out_shape=jax.ShapeDtypeStruct((batch * seq, heads * dv), F32),
        scratch_shapes=[pltpu.VMEM((2 * t, dv), BF16), pltpu.VMEM((2 * t, 1), F32),
                        pltpu.VMEM((2 * t, 1), F32), pltpu.VMEM((2 * t, dv), F32)],
        compiler_params=_cparams("parallel", "parallel", "arbitrary"),
        name="diff_attention_prompt",
    )(q, k, v, bias, *[vec(a) for a in lam_vecs], vec(subln_w))


def _attn_decode_kernel(pt_ref, q_ref, kn_ref, vn_ref, bias_ref, bnew_ref,
                        lq1_ref, lk1_ref, lq2_ref, lk2_ref, sw_ref, *rest,
                        pages, heads, da, lam_init):
    del pt_ref
    k_refs = rest[:pages]
    v_refs = rest[pages:2 * pages]
    o_ref, m_ref, l_ref, acc_ref = rest[2 * pages:]
    j = pl.program_id(1)

    @pl.when(j == 0)
    def _():
        m_ref[...] = jnp.full(m_ref.shape, NEG_BIG, F32)
        l_ref[...] = jnp.zeros(l_ref.shape, F32)
        acc_ref[...] = jnp.zeros(acc_ref.shape, F32)

    q = q_ref[...] * (da ** -0.5)
    row = lax.broadcasted_iota(jnp.int32, q.shape, 0)
    lane = lax.broadcasted_iota(jnp.int32, q.shape, 1)
    qs = jnp.where((row < heads) == (lane < da), q, 0.0)
    qs_bf = qs.astype(BF16)

    s = jnp.concatenate([_mm_nt(qs_bf, k_ref[...]) for k_ref in k_refs], axis=1)
    s = s + bias_ref[...]
    m_prev = m_ref[...]
    m_new = jnp.maximum(m_prev, jnp.max(s, axis=-1, keepdims=True))
    p = jnp.exp(s - m_new)
    alpha = jnp.exp(m_prev - m_new)
    l_ref[...] = alpha * l_ref[...] + jnp.sum(p, axis=-1, keepdims=True)
    rows_per_page = k_refs[0].shape[0]
    pv = acc_ref[...] * alpha
    for idx, v_ref in enumerate(v_refs):
        pv = pv + _mm(p[:, idx * rows_per_page:(idx + 1) * rows_per_page], v_ref[...])
    acc_ref[...] = pv
    m_ref[...] = m_new

    @pl.when(j == pl.num_programs(1) - 1)
    def _():
        s_new = jnp.sum(qs * kn_ref[...], axis=-1, keepdims=True) + bnew_ref[:, 0:1]
        m_prev = m_ref[...]
        m_fin = jnp.maximum(m_prev, s_new)
        p_new = jnp.exp(s_new - m_fin)
        alpha = jnp.exp(m_prev - m_fin)
        l_fin = alpha * l_ref[...] + p_new
        acc = alpha * acc_ref[...] + p_new * vn_ref[...]
        o12 = acc / l_fin
        lam = _lambda(lq1_ref, lk1_ref, lq2_ref, lk2_ref, lam_init)
        o = o12[0:heads] - lam * o12[heads:2 * heads]
        o_ref[...] = _rms(o, sw_ref[...], LN_EPS) * (1.0 - lam_init)


def _attention_decode(q, k_new, v_new, k_pool, v_pool, page_table, rel_bias, lam_vecs,
                      subln_w, heads, lam_init, pages):
    db = q.shape[0]
    n_pool, page, _, dv = k_pool.shape
    da = dv // 2
    n_pages = page_table.shape[1]
    past = n_pages * page
    rpp = page * heads
    kp = k_pool.reshape(n_pool, rpp, dv)
    vp = v_pool.reshape(n_pool, rpp, dv)

    def two_maps(a):
        a = a.reshape(db, 1, heads, dv)
        return jnp.broadcast_to(a, (db, 2, heads, dv)).reshape(db, 2 * heads, dv)

    dist = past - jnp.arange(past)
    b = rel_bias[_t5_bucket(dist)].astype(F32)
    same = jnp.eye(heads, dtype=bool)
    b = jnp.where(same[:, None, :], jnp.transpose(b)[:, :, None], NEG_BIG)
    b = jnp.tile(b.reshape(heads, past * heads), (2, 1))
    b_new = jnp.tile(jnp.broadcast_to(rel_bias[0].astype(F32)[:, None], (heads, LANES)), (2, 1))

    vec = lambda a: a.reshape(1, -1)
    const = lambda s, j, pt: (0, 0)

    def page_spec(idx):
        return pl.BlockSpec((None, rpp, dv), lambda s, j, pt: (pt[s, j * pages + idx], 0, 0))

    grid_spec = pltpu.PrefetchScalarGridSpec(
        num_scalar_prefetch=1,
        grid=(db, n_pages // pages),
        in_specs=([pl.BlockSpec((None, 2 * heads, dv), lambda s, j, pt: (s, 0, 0))] * 3
                  + [pl.BlockSpec((2 * heads, pages * rpp), lambda s, j, pt: (0, j)),
                     pl.BlockSpec((2 * heads, LANES), const),
                     pl.BlockSpec((1, da), const), pl.BlockSpec((1, da), const),
                     pl.BlockSpec((1, da), const), pl.BlockSpec((1, da), const),
                     pl.BlockSpec((1, dv), const)]
                  + [page_spec(idx) for idx in range(pages)] * 2),
        out_specs=pl.BlockSpec((None, heads, dv), lambda s, j, pt: (s, 0, 0)),
        scratch_shapes=[pltpu.VMEM((2 * heads, 1), F32), pltpu.VMEM((2 * heads, 1), F32),
                        pltpu.VMEM((2 * heads, dv), F32)],
    )
    out = pl.pallas_call(
        functools.partial(_attn_decode_kernel, pages=pages, heads=heads, da=da,
                          lam_init=lam_init),
        grid_spec=grid_spec,
        out_shape=jax.ShapeDtypeStruct((db, heads, dv), F32),
        compiler_params=_cparams("parallel", "arbitrary"),
        name="diff_attention_decode",
    )(page_table, two_maps(q), two_maps(k_new), two_maps(v_new), b, b_new,
      *[vec(a) for a in lam_vecs], vec(subln_w), *([kp] * pages), *([vp] * pages))
    return out.reshape(db, heads * dv)


def _gdn_pre_kernel(x_ref, st_ref, cw_ref, ba_ref, gp_ref,
                    u_ref, w_ref, qd_ref, kd_ref, at_ref, eg_ref,
                    xe_ref, qn_ref, kn_ref, vv_ref, gb_ref, *, tt, t_real, heads, dk):
    c = GDN_CHUNK
    i = pl.program_id(1)
    taps = cw_ref.shape[0]
    halo = 8
    hk = heads * dk

    @pl.when(i == 0)
    def _():
        xe_ref[0:halo, :] = st_ref[...]

    xe_ref[halo:halo + tt, :] = x_ref[...]
    acc = cw_ref[taps - 1:taps, :] * x_ref[...]
    for j in range(taps - 1):
        acc = acc + cw_ref[j:j + 1, :] * xe_ref[pl.ds(halo - (taps - 1) + j, tt), :]
    xe_ref[0:halo, :] = xe_ref[tt:tt + halo, :]
    hcv = _silu(acc)

    valid = (i * tt + lax.broadcasted_iota(jnp.int32, (tt, 1), 0)) < t_real
    for h in range(heads):
        sl = slice(h * dk, (h + 1) * dk)
        qh = hcv[:, h * dk:(h + 1) * dk]
        kh = hcv[:, hk + h * dk:hk + (h + 1) * dk]
        qn_ref[:, sl] = qh * lax.rsqrt(jnp.sum(qh * qh, axis=-1, keepdims=True) + L2_EPS)
        kn = kh * lax.rsqrt(jnp.sum(kh * kh, axis=-1, keepdims=True) + L2_EPS)
        kn_ref[:, sl] = jnp.where(valid, kn, 0.0)
    vv_ref[...] = jnp.where(valid, hcv[:, 2 * hk:], 0.0)

    ba = ba_ref[...]
    lane = lax.broadcasted_iota(jnp.int32, ba.shape, 1)
    xa = ba + gp_ref[1:2, :]
    softplus = jnp.maximum(xa, 0.0) + jnp.log1p(jnp.exp(-jnp.abs(xa)))
    gates = jnp.where(lane < heads, _sigmoid(ba), -jnp.exp(gp_ref[0:1, :]) * softplus)
    gb_ref[...] = jnp.where(valid & (lane < 2 * heads), gates, 0.0)

    ri = lax.broadcasted_iota(jnp.int32, (c, c), 0)
    ci = lax.broadcasted_iota(jnp.int32, (c, c), 1)
    incl = ri >= ci
    strict = ri > ci
    tri_l = incl.astype(F32)
    tri_u = (ri <= ci).astype(F32)
    eye_c = (ri == ci).astype(F32)
    pick = (lax.broadcasted_iota(jnp.int32, (8, LANES), 0)
            == lax.broadcasted_iota(jnp.int32, (8, LANES), 1)).astype(F32)
    lane_c = lax.broadcasted_iota(jnp.int32, (c, LANES), 1)
    scale = dk ** -0.5

    def chunk_body(ch, carry):
        r0 = pl.multiple_of(ch * c, c)
        gb = gb_ref[pl.ds(r0, c), :]
        gb_rows = _mm_nt_hi(pick, gb)
        gc_rows = _mm_hi(gb_rows, tri_u)
        gc_cols = _mm_hi(tri_l, gb)
        for h in range(heads):
            sl = slice(h * dk, (h + 1) * dk)
            beta = _lane_pick(gb, lane_c, h)
            gc_col = _lane_pick(gc_cols, lane_c, heads + h)
            gc_row = gc_rows[heads + h:heads + h + 1, :]
            decay = jnp.where(incl, jnp.exp(jnp.where(incl, gc_col - gc_row, 0.0)), 0.0)
            qh = qn_ref[pl.ds(r0, c), sl] * scale
            kh = kn_ref[pl.ds(r0, c), sl]
            vh = vv_ref[pl.ds(r0, c), sl]
            k_beta = kh * beta
            v_beta = vh * beta
            lmat = jnp.where(strict, _mm_nt(k_beta, kh) * decay, 0.0)
            tinv = eye_c - lmat
            pw = lmat
            span = 2
            while span < c:
                pw = _mm_hi(pw, pw)
                tinv = tinv + _mm_hi(tinv, pw)
                span *= 2
            egc = jnp.exp(gc_col)
            g_last = gc_col[c - 1:c, :]
            u_ref[pl.ds(r0, c), sl] = _mm(tinv, v_beta)
            w_ref[pl.ds(r0, c), sl] = _mm(tinv, k_beta * egc)
            at_ref[pl.ds(r0, c), h * c:(h + 1) * c] = _mm_nt(qh, kh) * decay
            kd_ref[pl.ds(r0, c), sl] = kh * jnp.exp(g_last - gc_col)
            qd_ref[pl.ds(r0, c), sl] = qh * egc
            eg_ref[ch, :, sl] = jnp.broadcast_to(jnp.exp(g_last), (8, dk))
        return carry

    lax.fori_loop(0, tt // c, chunk_body, 0)


def _gdn_scan_kernel(u_ref, w_ref, qd_ref, kd_ref, at_ref, eg_ref, z_ref, s0_ref, nw_ref,
                     o_ref, sout_ref, s_ref, *, tt, heads, dk):
    c = GDN_CHUNK
    i = pl.program_id(1)

    @pl.when(i == 0)
    def _():
        s_ref[...] = s0_ref[...]

    def chunk_body(ch, carry):
        r0 = pl.multiple_of(ch * c, c)
        for h in range(heads):
            sl = slice(h * dk, (h + 1) * dk)
            state = s_ref[h]
            v_new = u_ref[pl.ds(r0, c), sl] - _mm(w_ref[pl.ds(r0, c), sl], state)
            o = (_mm(qd_ref[pl.ds(r0, c), sl], state)
                 + _mm(at_ref[pl.ds(r0, c), h * c:(h + 1) * c], v_new))
            s_ref[h] = state * eg_ref[ch, 0:1, sl] + _mm_tn(kd_ref[pl.ds(r0, c), sl], v_new)
            o_ref[pl.ds(r0, c), sl] = (_rms(o, nw_ref[...], RMS_EPS)
                                       * _silu(z_ref[pl.ds(r0, c), sl]))
        return carry

    lax.fori_loop(0, tt // c, chunk_body, 0)

    @pl.when(i == pl.num_programs(1) - 1)
    def _():
        sout_ref[...] = s_ref[...]


def _gated_deltanet(qkv, ba, z, conv_state, s0, conv_w, a_log, dt_bias, norm_w,
                    batch, seq_pad, t_real, heads, tt):
    cq = qkv.shape[1]
    hk = cq // 3
    dk = hk // heads
    c = GDN_CHUNK
    nt = seq_pad // tt
    nc = tt // c
    taps = conv_w.shape[0]
    st = jnp.pad(conv_state, ((0, 0), (8 - (taps - 1), 0), (0, 0)))
    gp = jnp.zeros((8, LANES), F32)
    gp = gp.at[0, heads:2 * heads].set(a_log.astype(F32))
    gp = gp.at[1, heads:2 * heads].set(dt_bias.astype(F32))
    row_blk = lambda b, i: (b * nt + i, 0)
    const = lambda b, i: (0, 0)
    tok = lambda width: pl.BlockSpec((tt, width), row_blk)
    rows = batch * seq_pad
    u, w, qd, kd, at, eg = pl.pallas_call(
        functools.partial(_gdn_pre_kernel, tt=tt, t_real=t_real, heads=heads, dk=dk),
        grid=(batch, nt),
        in_specs=[tok(cq),
                  pl.BlockSpec((None, 8, cq), lambda b, i: (b, 0, 0)),
                  pl.BlockSpec((taps, cq), const),
                  tok(LANES),
                  pl.BlockSpec((8, LANES), const)],
        out_specs=[tok(hk), tok(hk), tok(hk), tok(hk), tok(heads * c),
                   pl.BlockSpec((None, nc, 8, hk), lambda b, i: (b, i, 0, 0))],
        out_shape=[jax.ShapeDtypeStruct((rows, hk), F32)] * 4
                  + [jax.ShapeDtypeStruct((rows, heads * c), F32),
                     jax.ShapeDtypeStruct((batch, nt * nc, 8, hk), F32)],
        scratch_shapes=[pltpu.VMEM((tt + 8, cq), F32), pltpu.VMEM((tt, hk), F32),
                        pltpu.VMEM((tt, hk), F32), pltpu.VMEM((tt, hk), F32),
                        pltpu.VMEM((tt, LANES), F32)],
        compiler_params=_cparams("parallel", "arbitrary"),
        name="gdn_chunk_prepare",
    )(qkv, st, conv_w, ba, gp)
    o, s_new = pl.pallas_call(
        functools.partial(_gdn_scan_kernel, tt=tt, heads=heads, dk=dk),
        grid=(batch, nt),
        in_specs=[tok(hk), tok(hk), tok(hk), tok(hk), tok(heads * c),
                  pl.BlockSpec((None, nc, 8, hk), lambda b, i: (b, i, 0, 0)),
                  tok(hk),
                  pl.BlockSpec((None, heads, dk, dk), lambda b, i: (b, 0, 0, 0)),
                  pl.BlockSpec((1, dk), const)],
        out_specs=[tok(hk),
                   pl.BlockSpec((None, heads, dk, dk), lambda b, i: (b, 0, 0, 0))],
        out_shape=[jax.ShapeDtypeStruct((rows, hk), F32),
                   jax.ShapeDtypeStruct((batch, heads, dk, dk), F32)],
        scratch_shapes=[pltpu.VMEM((heads, dk, dk), F32)],
        compiler_params=_cparams("parallel", "arbitrary"),
        name="gdn_chunk_scan",
    )(u, w, qd, kd, at, eg, z, s0, norm_w.reshape(1, dk))
    return o, s_new


def _dwconv_kernel(h_ref, buf_ref, w_ref, b_ref, g_ref, bb_ref, o_ref, xe_ref, *, tt, rb):
    i = pl.program_id(1)
    taps = w_ref.shape[0]
    halo = buf_ref.shape[0]
    first = halo - (taps - 1)

    @pl.when(i == 0)
    def _():
        xe_ref[0:halo, :] = buf_ref[...]

    xe_ref[halo:halo + tt, :] = h_ref[...]
    for r0 in range(0, tt, rb):
        acc = b_ref[...] + w_ref[0:1, :] * xe_ref[pl.ds(first + r0, rb), :]
        for j in range(1, taps):
            acc = acc + w_ref[j:j + 1, :] * xe_ref[pl.ds(first + r0 + j, rb), :]
        mu = jnp.mean(acc, axis=-1, keepdims=True)
        xc = acc - mu
        var = jnp.mean(xc * xc, axis=-1, keepdims=True)
        y = xc * lax.rsqrt(var + LN_EPS) * g_ref[...] + bb_ref[...]
        o_ref[pl.ds(r0, rb), :] = _silu(y)
    xe_ref[0:halo, :] = xe_ref[tt:tt + halo, :]


def _dwconv_ln_swish(h, buf, w_dw, b_dw, ln_g, ln_b, batch, seq_pad, tt):
    ch = h.shape[1]
    taps = w_dw.shape[0]
    halo = -(-(taps - 1) // 8) * 8
    bufp = jnp.pad(buf, ((0, 0), (halo - (taps - 1), 0), (0, 0)))
    nt = seq_pad // tt
    rb = min(tt, 16)
    const = lambda b, i: (0, 0)
    vec = lambda a: a.reshape(1, ch)
    return pl.pallas_call(
        functools.partial(_dwconv_kernel, tt=tt, rb=rb),
        grid=(batch, nt),
        in_specs=[pl.BlockSpec((tt, ch), lambda b, i: (b * nt + i, 0)),
                  pl.BlockSpec((None, halo, ch), lambda b, i: (b, 0, 0)),
                  pl.BlockSpec((taps, ch), const),
                  pl.BlockSpec((1, ch), const), pl.BlockSpec((1, ch), const),
                  pl.BlockSpec((1, ch), const)],
        out_specs=pl.BlockSpec((tt, ch), lambda b, i: (b * nt + i, 0)),
        out_shape=jax.ShapeDtypeStruct((batch * seq_pad, ch), F32),
        scratch_shapes=[pltpu.VMEM((tt + halo, ch), F32)],
        compiler_params=_cparams("parallel", "arbitrary"),
        name="dwconv_ln_swish",
    )(h, bufp, w_dw, vec(b_dw), vec(ln_g), vec(ln_b))


def _pad_time(a, batch, t, t_pad):
    a = a.reshape(batch, t, a.shape[-1])
    return jnp.pad(a, ((0, 0), (0, t_pad - t), (0, 0))).reshape(batch * t_pad, a.shape[-1])


def kernel(x_prompt, x_sample, cache_attn_k, cache_attn_v, page_table, state_gdn_conv, state_gdn_s, state_conv_buf, norm_mix, norm_ffn, norm_final, w_in, w_out, gdn_conv_w, gdn_A_log, gdn_dt_bias, gdn_norm_w, lam_q1, lam_k1, lam_q2, lam_k2, diff_subln_w, rel_bias, conv_w_pw1, conv_b_pw1, conv_w_dw, conv_b_dw, conv_ln_g, conv_ln_b, conv_w_pw2, conv_b_pw2, ffn_w_gate, ffn_w_up, ffn_w_down):
    bp, seq, d = x_prompt.shape
    db, dseq, _ = x_sample.shape
    assert dseq == 1
    depth = norm_mix.shape[0]
    ha, dva = cache_attn_k.shape[3], cache_attn_v.shape[4]
    hb, dkb = state_gdn_s.shape[2], state_gdn_s.shape[3]
    c_qkv = state_gdn_conv.shape[3]
    gdn_taps = gdn_conv_w.shape[1]
    conv_taps = conv_w_dw.shape[1]
    mp = bp * seq
    assert 2 * hb <= LANES and mp % 8 == 0

    tm = 256
    x = jnp.concatenate([x_prompt.reshape(mp, d), x_sample.reshape(db, d)], axis=0)
    zeros_d = jnp.zeros((d,), F32)

    sizes = (ha * dva, ha * dva, ha * dva, c_qkv, hb * dkb, LANES)
    offs = [0]
    for s_ in sizes:
        offs.append(offs[-1] + s_)
    groups = tuple(zip(offs[:-1], sizes))

    outs = {}
    k_p, v_p, k_s, v_s, gc_p, gc_s, gs_p, gs_s, cb_p, cb_s = ([] for _ in range(10))
    for layer in range(depth):
        if layer % 2 == 0:
            e = layer // 2
            lam_init = 0.8 - 0.6 * math.exp(-0.3 * layer)
            w_in_e = jnp.pad(w_in[e], ((0, 0), (0, offs[-1] - w_in.shape[2]))).astype(BF16)
            qa, ka, va, qkv_b, z_b, ba_b = _in_projection(x, norm_mix[layer], w_in_e, groups, tm)
            lam_vecs = (lam_q1[e], lam_k1[e], lam_q2[e], lam_k2[e])

            oa_p = _attention_prompt(qa, ka, va, rel_bias, lam_vecs, diff_subln_w[e],
                                     bp, seq, ha, lam_init, t=min(512, seq))
            oa_s = _attention_decode(qa[mp:], ka[mp:], va[mp:], cache_attn_k[e], cache_attn_v[e],
                                     page_table, rel_bias, lam_vecs, diff_subln_w[e],
                                     ha, lam_init, pages=8)

            ob_p, s_p = _gated_deltanet(
                qkv_b, ba_b, z_b, jnp.zeros((bp, gdn_taps - 1, c_qkv), F32),
                jnp.zeros((bp, hb, dkb, dkb), F32), gdn_conv_w[e], gdn_A_log[e], gdn_dt_bias[e],
                gdn_norm_w[e], bp, seq, seq, hb, tt=min(512, seq))
            cpad = GDN_CHUNK
            ob_s, s_s = _gated_deltanet(
                _pad_time(qkv_b[mp:], db, 1, cpad), _pad_time(ba_b[mp:], db, 1, cpad),
                _pad_time(z_b[mp:], db, 1, cpad), state_gdn_conv[e], state_gdn_s[e],
                gdn_conv_w[e], gdn_A_log[e], gdn_dt_bias[e], gdn_norm_w[e],
                db, cpad, 1, hb, tt=cpad)
            ob_s = ob_s.reshape(db, cpad, hb * dkb)[:, 0]

            oa = jnp.concatenate([oa_p, oa_s], axis=0)
            ob = jnp.concatenate([ob_p, ob_s], axis=0)
            w_o = w_out[e].astype(BF16)
            x = _projection_residual([oa, ob], [w_o[:ha * dva], w_o[ha * dva:]], zeros_d, x, tm)

            k_p.append(ka[:mp].reshape(bp, seq, ha, dva))
            v_p.append(va[:mp].reshape(bp, seq, ha, dva))
            k_s.append(ka[mp:].reshape(db, 1, ha, dva))
            v_s.append(va[mp:].reshape(db, 1, ha, dva))
            gc_p.append(qkv_b[:mp].reshape(bp, seq, c_qkv)[:, seq - (gdn_taps - 1):])
            gc_s.append(jnp.concatenate([state_gdn_conv[e], qkv_b[mp:].reshape(db, 1, c_qkv)],
                                        axis=1)[:, -(gdn_taps - 1):])
            gs_p.append(s_p)
            gs_s.append(s_s)
        else:
            cidx = layer // 2
            hg = _glu_projection(x, norm_mix[layer], conv_w_pw1[cidx].astype(BF16),
                                 conv_b_pw1[cidx], tm)
            dconv = hg.shape[1]
            hc_p = _dwconv_ln_swish(hg, jnp.zeros((bp, conv_taps - 1, dconv), F32),
                                    conv_w_dw[cidx], conv_b_dw[cidx], conv_ln_g[cidx],
                                    conv_ln_b[cidx], bp, seq, tt=min(128, seq))
            hg_s = hg[mp:]
            hc_s = _dwconv_ln_swish(_pad_time(hg_s, db, 1, 8), state_conv_buf[cidx],
                                    conv_w_dw[cidx], conv_b_dw[cidx], conv_ln_g[cidx],
                                    conv_ln_b[cidx], db, 8, tt=8)
            hc = jnp.concatenate([hc_p, hc_s.reshape(db, 8, dconv)[:, 0]], axis=0)
            x = _projection_residual([hc], [conv_w_pw2[cidx].astype(BF16)], conv_b_pw2[cidx], x, tm)
            cb_p.append(hg[:mp].reshape(bp, seq, dconv)[:, seq - (conv_taps - 1):])
            cb_s.append(jnp.concatenate([state_conv_buf[cidx], hg_s.reshape(db, 1, dconv)],
                                        axis=1)[:, -(conv_taps - 1):])
        last = layer == depth - 1
        d_ff = ffn_w_gate.shape[2]
        x = _ffn(x, norm_ffn[layer], ffn_w_gate[layer].astype(BF16), ffn_w_up[layer].astype(BF16),
                 ffn_w_down[layer].astype(BF16), norm_final, last, tm=512, tf=d_ff // 2)

    y_prompt = x[:mp].reshape(bp, seq, d)
    y_sample = x[mp:].reshape(db, 1, d)
    return (y_prompt, y_sample, jnp.stack(k_p), jnp.stack(v_p), jnp.stack(k_s), jnp.stack(v_s),
            jnp.stack(gc_p), jnp.stack(gc_s), jnp.stack(gs_p), jnp.stack(gs_s),
            jnp.stack(cb_p), jnp.stack(cb_s))
```

```python
import functools
import math

import jax
import jax.numpy as jnp
from jax import lax
from jax.experimental import pallas as pl
from jax.experimental.pallas import tpu as pltpu

F32 = jnp.float32
BF16 = jnp.bfloat16

RMS_EPS = 1e-6
LN_EPS = 1e-5
L2_EPS = 1e-6
NUM_BUCKETS = 32
MAX_DISTANCE = 128
GDN_CHUNK = 64
NEG_BIG = -1e30
LANES = 128
SUBLANES = 8
VMEM_LIMIT = 48 * 1024 * 1024

_HI = lax.Precision.HIGHEST


def _cparams(*sem):
    return pltpu.CompilerParams(dimension_semantics=sem, vmem_limit_bytes=VMEM_LIMIT)


def _mm(a, b):
    return jnp.dot(a.astype(BF16), b.astype(BF16), preferred_element_type=F32)


def _mm_nt(a, b):
    return lax.dot_general(a.astype(BF16), b.astype(BF16), (((1,), (1,)), ((), ())),
                           preferred_element_type=F32)


def _mm_tn(a, b):
    return lax.dot_general(a.astype(BF16), b.astype(BF16), (((0,), (0,)), ((), ())),
                           preferred_element_type=F32)


def _mm_hi(a, b):
    return jnp.dot(a, b, precision=_HI, preferred_element_type=F32)


def _mm_nt_hi(a, b):
    return lax.dot_general(a, b, (((1,), (1,)), ((), ())), precision=_HI,
                           preferred_element_type=F32)


def _rms(x, w, eps):
    return x * lax.rsqrt(jnp.mean(x * x, axis=-1, keepdims=True) + eps) * w


def _sigmoid(x):
    return 1.0 / (1.0 + jnp.exp(-x))


def _silu(x):
    return x * _sigmoid(x)


def _lane_pick(x, lane_idx, k):
    return jnp.sum(jnp.where(lane_idx == k, x, 0.0), axis=-1, keepdims=True)


def _inproj_kernel(x_ref, nw_ref, w_ref, *out_refs, groups):
    xn = _rms(x_ref[...], nw_ref[...], RMS_EPS).astype(BF16)
    for o_ref, (off, width) in zip(out_refs, groups):
        o_ref[...] = jnp.dot(xn, w_ref[:, off:off + width], preferred_element_type=F32)


def _in_projection(x, norm_w, w_bf16, groups, tm):
    m, d = x.shape
    n = w_bf16.shape[1]
    return pl.pallas_call(
        functools.partial(_inproj_kernel, groups=groups),
        grid=(pl.cdiv(m, tm),),
        in_specs=[pl.BlockSpec((tm, d), lambda i: (i, 0)),
                  pl.BlockSpec((1, d), lambda i: (0, 0)),
                  pl.BlockSpec((d, n), lambda i: (0, 0))],
        out_specs=[pl.BlockSpec((tm, wd), lambda i: (i, 0)) for _, wd in groups],
        out_shape=[jax.ShapeDtypeStruct((m, wd), F32) for _, wd in groups],
        compiler_params=_cparams("parallel"),
        name="in_projection",
    )(x, norm_w.reshape(1, d), w_bf16)


def _glu_proj_kernel(x_ref, nw_ref, w_ref, b_ref, o_ref, *, dc):
    xn = _rms(x_ref[...], nw_ref[...], RMS_EPS).astype(BF16)
    a = jnp.dot(xn, w_ref[:, :dc], preferred_element_type=F32) + b_ref[:, :dc]
    g = jnp.dot(xn, w_ref[:, dc:], preferred_element_type=F32) + b_ref[:, dc:]
    o_ref[...] = a * _sigmoid(g)


def _glu_projection(x, norm_w, w_bf16, b, tm):
    m, d = x.shape
    n = w_bf16.shape[1]
    dc = n // 2
    return pl.pallas_call(
        functools.partial(_glu_proj_kernel, dc=dc),
        grid=(pl.cdiv(m, tm),),
        in_specs=[pl.BlockSpec((tm, d), lambda i: (i, 0)),
                  pl.BlockSpec((1, d), lambda i: (0, 0)),
                  pl.BlockSpec((d, n), lambda i: (0, 0)),
                  pl.BlockSpec((1, n), lambda i: (0, 0))],
        out_specs=pl.BlockSpec((tm, dc), lambda i: (i, 0)),
        out_shape=jax.ShapeDtypeStruct((m, dc), F32),
        compiler_params=_cparams("parallel"),
        name="glu_projection",
    )(x, norm_w.reshape(1, d), w_bf16, b.reshape(1, n))


def _proj_res_kernel(*refs, n_in):
    a_refs = refs[:n_in]
    w_refs = refs[n_in:2 * n_in]
    b_ref, r_ref, o_ref = refs[2 * n_in:]
    acc = r_ref[...] + b_ref[...]
    for a_ref, w_ref in zip(a_refs, w_refs):
        acc = acc + _mm(a_ref[...], w_ref[...])
    o_ref[...] = acc


def _projection_residual(acts, weights_bf16, bias, res, tm):
    m, d = res.shape
    n_in = len(acts)
    in_specs = ([pl.BlockSpec((tm, a.shape[1]), lambda i: (i, 0)) for a in acts]
                + [pl.BlockSpec(w.shape, lambda i: (0, 0)) for w in weights_bf16]
                + [pl.BlockSpec((1, d), lambda i: (0, 0)),
                   pl.BlockSpec((tm, d), lambda i: (i, 0))])
    return pl.pallas_call(
        functools.partial(_proj_res_kernel, n_in=n_in),
        grid=(pl.cdiv(m, tm),),
        in_specs=in_specs,
        out_specs=pl.BlockSpec((tm, d), lambda i: (i, 0)),
        out_shape=jax.ShapeDtypeStruct((m, d), F32),
        compiler_params=_cparams("parallel"),
        name="projection_residual",
    )(*acts, *weights_bf16, bias.reshape(1, d), res)


def _ffn_kernel(x_ref, nw_ref, wg_ref, wu_ref, wd_ref, fw_ref, o_ref, xn_ref, acc_ref, *,
                final_norm):
    k = pl.program_id(1)

    @pl.when(k == 0)
    def _():
        xn_ref[...] = _rms(x_ref[...], nw_ref[...], RMS_EPS).astype(BF16)
        acc_ref[...] = x_ref[...]

    xn = xn_ref[...]
    g = jnp.dot(xn, wg_ref[...], preferred_element_type=F32)
    u = jnp.dot(xn, wu_ref[...], preferred_element_type=F32)
    acc_ref[...] += _mm(_silu(g) * u, wd_ref[...])

    @pl.when(k == pl.num_programs(1) - 1)
    def _():
        y = acc_ref[...]
        if final_norm:
            y = _rms(y, fw_ref[...], RMS_EPS)
        o_ref[...] = y


def _ffn(x, norm_w, wg, wu, wd, final_w, final_norm, tm, tf):
    m, d = x.shape
    f = wg.shape[1]
    return pl.pallas_call(
        functools.partial(_ffn_kernel, final_norm=final_norm),
        grid=(pl.cdiv(m, tm), f // tf),
        in_specs=[pl.BlockSpec((tm, d), lambda i, k: (i, 0)),
                  pl.BlockSpec((1, d), lambda i, k: (0, 0)),
                  pl.BlockSpec((d, tf), lambda i, k: (0, k)),
                  pl.BlockSpec((d, tf), lambda i, k: (0, k)),
                  pl.BlockSpec((tf, d), lambda i, k: (k, 0)),
                  pl.BlockSpec((1, d), lambda i, k: (0, 0))],
        out_specs=pl.BlockSpec((tm, d), lambda i, k: (i, 0)),
        out_shape=jax.ShapeDtypeStruct((m, d), F32),
        scratch_shapes=[pltpu.VMEM((tm, d), BF16), pltpu.VMEM((tm, d), F32)],
        compiler_params=_cparams("parallel", "arbitrary"),
        name="swiglu_ffn",
    )(x, norm_w.reshape(1, d), wg, wu, wd, final_w.reshape(1, d))


def _t5_bucket(n):
    max_exact = NUM_BUCKETS // 2
    nf = jnp.maximum(n, 1).astype(F32)
    large = max_exact + (jnp.log(nf / max_exact) / math.log(MAX_DISTANCE / max_exact)
                         * (NUM_BUCKETS - max_exact)).astype(jnp.int32)
    large = jnp.minimum(large, NUM_BUCKETS - 1)
    return jnp.where(n < max_exact, n, large)


def _lambda(lq1_ref, lk1_ref, lq2_ref, lk2_ref, lam_init):
    s1 = jnp.sum(lq1_ref[...] * lk1_ref[...], axis=-1, keepdims=True)
    s2 = jnp.sum(lq2_ref[...] * lk2_ref[...], axis=-1, keepdims=True)
    return jnp.exp(s1) - jnp.exp(s2) + lam_init


def _attn_prompt_kernel(q_ref, k_ref, v_ref, bias_ref, lq1_ref, lk1_ref, lq2_ref, lk2_ref,
                        sw_ref, o_ref, qt_ref, kb_ref, vt_ref, m_ref, l_ref, acc_ref, *,
                        t, da, lam_init):
    i = pl.program_id(2)
    nblk = kb_ref.shape[0]

    @pl.when(i == 0)
    def _():
        for c in range(nblk):
            kb_ref[c] = k_ref[c * t:(c + 1) * t, :].astype(BF16)
            vt_ref[c] = v_ref[c * t:(c + 1) * t, :].T.astype(BF16)

    qt = (q_ref[...] * (da ** -0.5)).T
    first_map = lax.broadcasted_iota(jnp.int32, qt.shape, 0) < da
    qt_ref[:, 0:t] = jnp.where(first_map, qt, 0.0).astype(BF16)
    qt_ref[:, t:2 * t] = jnp.where(first_map, 0.0, qt).astype(BF16)
    m_ref[...] = jnp.full(m_ref.shape, NEG_BIG, F32)
    l_ref[...] = jnp.zeros(l_ref.shape, F32)
    acc_ref[...] = jnp.zeros(acc_ref.shape, F32)

    def block(j, bias):
        s = jnp.dot(kb_ref[j], qt_ref[...], preferred_element_type=F32)
        if bias is not None:
            s = s + jnp.concatenate([bias, bias], axis=1)
        m_prev = m_ref[...]
        m_new = jnp.maximum(m_prev, jnp.max(s, axis=0, keepdims=True))
        p = jnp.exp(s - m_new)
        alpha = jnp.exp(m_prev - m_new)
        l_ref[...] = alpha * l_ref[...] + jnp.sum(p, axis=0, keepdims=True)
        acc_ref[...] = alpha * acc_ref[...] + jnp.dot(vt_ref[j], p.astype(BF16),
                                                      preferred_element_type=F32)
        m_ref[...] = m_new

    def far_body(j, carry):
        block(j, None)
        return carry

    lax.fori_loop(0, i - 1, far_body, 0)

    @pl.when(i >= 1)
    def _():
        block(i - 1, bias_ref[0])

    block(i, bias_ref[1])

    o12 = acc_ref[...] / l_ref[...]
    lam = _lambda(lq1_ref, lk1_ref, lq2_ref, lk2_ref, lam_init)
    o = (o12[:, 0:t] - lam * o12[:, t:2 * t]).T
    o_ref[...] = _rms(o, sw_ref[...], LN_EPS) * (1.0 - lam_init)


def _toeplitz(r, t):
    h, period = r.shape
    flat = jnp.tile(r, (1, t))[:, :t * (period - 1)]
    return flat.reshape(h, t, period - 1)[:, :, :t]


def _prompt_bias_tiles(rel_bias, t):
    assert t >= MAX_DISTANCE
    far = rel_bias[NUM_BUCKETS - 1]
    b1 = jnp.transpose(rel_bias[_t5_bucket(jnp.arange(2 * t))] - far).astype(F32)
    r_diag = jnp.concatenate([b1[:, :t], jnp.full_like(b1[:, :t], NEG_BIG)], axis=1)
    r_prev = jnp.concatenate([b1[:, t:], b1[:, :t]], axis=1)
    return jnp.stack([_toeplitz(r_prev, t), _toeplitz(r_diag, t)], axis=1)


def _attention_prompt(q, k, v, rel_bias, lam_vecs, subln_w, batch, seq, heads, lam_init, t):
    dv = q.shape[1] // heads
    da = dv // 2
    nq = seq // t
    bias = _prompt_bias_tiles(rel_bias, t)
    vec = lambda a: a.reshape(1, -1)
    const = lambda b, h, i: (0, 0)
    return pl.pallas_call(
        functools.partial(_attn_prompt_kernel, t=t, da=da, lam_init=lam_init),
        grid=(batch, heads, nq),
        in_specs=[pl.BlockSpec((t, dv), lambda b, h, i: (b * nq + i, h)),
                  pl.BlockSpec((seq, dv), lambda b, h, i: (b, h)),
                  pl.BlockSpec((seq, dv), lambda b, h, i: (b, h)),
                  pl.BlockSpec((None, 2, t, t), lambda b, h, i: (h, 0, 0, 0)),
                  pl.BlockSpec((1, da), const), pl.BlockSpec((1, da), const),
                  pl.BlockSpec((1, da), const), pl.BlockSpec((1, da), const),
                  pl.BlockSpec((1, dv), const)],
        out_specs=pl.BlockSpec((t, dv), lambda b, h, i: (b * nq + i, h)),
        out_shape=jax.ShapeDtypeStruct((batch * seq, heads * dv), F32),
        scratch_shapes=[pltpu.VMEM((dv, 2 * t), BF16), pltpu.VMEM((nq, t, dv), BF16),
                        pltpu.VMEM((nq, dv, t), BF16), pltpu.VMEM((1, 2 * t), F32),
                        pltpu.VMEM((1, 2 * t), F32), pltpu.VMEM((dv, 2 * t), F32)],
        compiler_params=_cparams("parallel", "parallel", "arbitrary"),
        name="diff_attention_prompt",
    )(q, k, v, bias, *[vec(a) for a in lam_vecs], vec(subln_w))


def _attn_decode_kernel(pt_ref, q_ref, kn_ref, vn_ref, bias_ref, bnew_ref,
                        lq1_ref, lk1_ref, lq2_ref, lk2_ref, sw_ref, *rest,
                        pages, heads, da, lam_init):
    del pt_ref
    k_refs = rest[:pages]
    v_refs = rest[pages:2 * pages]
    o_ref, m_ref, l_ref, acc_ref = rest[2 * pages:]
    j = pl.program_id(1)

    @pl.when(j == 0)
    def _():
        m_ref[...] = jnp.full(m_ref.shape, NEG_BIG, F32)
        l_ref[...] = jnp.zeros(l_ref.shape, F32)
        acc_ref[...] = jnp.zeros(acc_ref.shape, F32)

    q = q_ref[...] * (da ** -0.5)
    row = lax.broadcasted_iota(jnp.int32, q.shape, 0)
    lane = lax.broadcasted_iota(jnp.int32, q.shape, 1)
    qs = jnp.where((row < heads) == (lane < da), q, 0.0)
    qs_bf = qs.astype(BF16)

    s = jnp.concatenate([_mm_nt(qs_bf, k_ref[...]) for k_ref in k_refs], axis=1)
    s = s + bias_ref[...]
    m_prev = m_ref[...]
    m_new = jnp.maximum(m_prev, jnp.max(s, axis=-1, keepdims=True))
    p = jnp.exp(s - m_new)
    alpha = jnp.exp(m_prev - m_new)
    l_ref[...] = alpha * l_ref[...] + jnp.sum(p, axis=-1, keepdims=True)
    rows_per_page = k_refs[0].shape[0]
    pv = acc_ref[...] * alpha
    for idx, v_ref in enumerate(v_refs):
        pv = pv + _mm(p[:, idx * rows_per_page:(idx + 1) * rows_per_page], v_ref[...])
    acc_ref[...] = pv
    m_ref[...] = m_new

    @pl.when(j == pl.num_programs(1) - 1)
    def _():
        s_new = jnp.sum(qs * kn_ref[...], axis=-1, keepdims=True) + bnew_ref[:, 0:1]
        m_prev = m_ref[...]
        m_fin = jnp.maximum(m_prev, s_new)
        p_new = jnp.exp(s_new - m_fin)
        alpha = jnp.exp(m_prev - m_fin)
        l_fin = alpha * l_ref[...] + p_new
        acc = alpha * acc_ref[...] + p_new * vn_ref[...]
        o12 = acc / l_fin
        lam = _lambda(lq1_ref, lk1_ref, lq2_ref, lk2_ref, lam_init)
        o = o12[0:heads] - lam * o12[heads:2 * heads]
        o_ref[...] = _rms(o, sw_ref[...], LN_EPS) * (1.0 - lam_init)


def _attention_decode(q, k_new, v_new, k_pool, v_pool, page_table, rel_bias, lam_vecs,
                      subln_w, heads, lam_init, pages):
    db = q.shape[0]
    n_pool, page, _, dv = k_pool.shape
    da = dv // 2
    n_pages = page_table.shape[1]
    past = n_pages * page
    rpp = page * heads
    kp = k_pool.reshape(n_pool, rpp, dv)
    vp = v_pool.reshape(n_pool, rpp, dv)

    def two_maps(a):
        a = a.reshape(db, 1, heads, dv)
        return jnp.broadcast_to(a, (db, 2, heads, dv)).reshape(db, 2 * heads, dv)

    dist = past - jnp.arange(past)
    b = rel_bias[_t5_bucket(dist)].astype(F32)
    same = jnp.eye(heads, dtype=bool)
    b = jnp.where(same[:, None, :], jnp.transpose(b)[:, :, None], NEG_BIG)
    b = jnp.tile(b.reshape(heads, past * heads), (2, 1))
    b_new = jnp.tile(jnp.broadcast_to(rel_bias[0].astype(F32)[:, None], (heads, LANES)), (2, 1))

    vec = lambda a: a.reshape(1, -1)
    const = lambda s, j, pt: (0, 0)

    def page_spec(idx):
        return pl.BlockSpec((None, rpp, dv), lambda s, j, pt: (pt[s, j * pages + idx], 0, 0))

    grid_spec = pltpu.PrefetchScalarGridSpec(
        num_scalar_prefetch=1,
        grid=(db, n_pages // pages),
        in_specs=([pl.BlockSpec((None, 2 * heads, dv), lambda s, j, pt: (s, 0, 0))] * 3
                  + [pl.BlockSpec((2 * heads, pages * rpp), lambda s, j, pt: (0, j)),
                     pl.BlockSpec((2 * heads, LANES), const),
                     pl.BlockSpec((1, da), const), pl.BlockSpec((1, da), const),
                     pl.BlockSpec((1, da), const), pl.BlockSpec((1, da), const),
                     pl.BlockSpec((1, dv), const)]
                  + [page_spec(idx) for idx in range(pages)] * 2),
        out_specs=pl.BlockSpec((None, heads, dv), lambda s, j, pt: (s, 0, 0)),
        scratch_shapes=[pltpu.VMEM((2 * heads, 1), F32), pltpu.VMEM((2 * heads, 1), F32),
                        pltpu.VMEM((2 * heads, dv), F32)],
    )
    out = pl.pallas_call(
        functools.partial(_attn_decode_kernel, pages=pages, heads=heads, da=da,
                          lam_init=lam_init),
        grid_spec=grid_spec,
        out_shape=jax.ShapeDtypeStruct((db, heads, dv), F32),
        compiler_params=_cparams("parallel", "arbitrary"),
        name="diff_attention_decode",
    )(page_table, two_maps(q), two_maps(k_new), two_maps(v_new), b, b_new,
      *[vec(a) for a in lam_vecs], vec(subln_w), *([kp] * pages), *([vp] * pages))
    return out.reshape(db, heads * dv)


def _unit_lower_inverse(lmat, eye, levels):
    if levels == 0:
        return eye
    x = -lmat
    tinv = eye + x
    if levels == 1:
        return tinv
    n = lmat.shape[0]
    r = _mm(x, x)
    for k in range(1, levels):
        if k < levels - 1:
            both = _mm(jnp.concatenate([r, tinv], axis=0), r)
            r, tinv = both[:n], tinv + both[n:]
        else:
            tinv = tinv + _mm(tinv, r)
    return tinv


def _gdn_pre_kernel(x_ref, st_ref, cw_ref, ba_ref, gp_ref,
                    u_ref, w_ref, qd_ref, kd_ref, at_ref, eg_ref,
                    xe_ref, qn_ref, kn_ref, vv_ref, gb_ref, gc_ref, *, tt, t_real, heads, dk):
    c = GDN_CHUNK
    i = pl.program_id(1)
    taps = cw_ref.shape[0]
    halo = 8
    hk = heads * dk

    @pl.when(i == 0)
    def _():
        xe_ref[0:halo, :] = st_ref[...]

    xe_ref[halo:halo + tt, :] = x_ref[...]
    acc = cw_ref[taps - 1:taps, :] * x_ref[...]
    for j in range(taps - 1):
        acc = acc + cw_ref[j:j + 1, :] * xe_ref[pl.ds(halo - (taps - 1) + j, tt), :]
    xe_ref[0:halo, :] = xe_ref[tt:tt + halo, :]
    hcv = _silu(acc)

    valid = (i * tt + lax.broadcasted_iota(jnp.int32, (tt, 1), 0)) < t_real
    for h in range(heads):
        sl = slice(h * dk, (h + 1) * dk)
        qh = hcv[:, h * dk:(h + 1) * dk]
        kh = hcv[:, hk + h * dk:hk + (h + 1) * dk]
        qn_ref[:, sl] = qh * lax.rsqrt(jnp.sum(qh * qh, axis=-1, keepdims=True) + L2_EPS)
        kn = kh * lax.rsqrt(jnp.sum(kh * kh, axis=-1, keepdims=True) + L2_EPS)
        kn_ref[:, sl] = jnp.where(valid, kn, 0.0)
    vv_ref[...] = jnp.where(valid, hcv[:, 2 * hk:], 0.0)

    ba = ba_ref[...]
    lane = lax.broadcasted_iota(jnp.int32, ba.shape, 1)
    xa = ba + gp_ref[1:2, :]
    softplus = jnp.maximum(xa, 0.0) + jnp.log1p(jnp.exp(-jnp.abs(xa)))
    gates = jnp.where(lane < heads, _sigmoid(ba), -jnp.exp(gp_ref[0:1, :]) * softplus)
    gates = jnp.where(valid & (lane < 2 * heads), gates, 0.0)
    gb_ref[...] = gates

    tri_l = (lax.broadcasted_iota(jnp.int32, (c, c), 0)
             >= lax.broadcasted_iota(jnp.int32, (c, c), 1)).astype(F32)
    for ch in range(tt // c):
        gc_ref[ch * c:(ch + 1) * c, :] = _mm_hi(tri_l, gates[ch * c:(ch + 1) * c, :])

    hc = heads * c
    rs = lax.broadcasted_iota(jnp.int32, (hc, hc), 0)
    cs = lax.broadcasted_iota(jnp.int32, (hc, hc), 1)
    same_head = (rs ^ cs) < c
    incl = same_head & (rs >= cs)
    strict = same_head & (rs > cs)
    eye = (rs == cs).astype(F32)
    ones8 = jnp.ones((8, LANES), F32)
    lane_c = lax.broadcasted_iota(jnp.int32, (c, LANES), 1)
    scale = dk ** -0.5
    c_valid = c if t_real >= c else t_real
    levels = (c_valid - 1).bit_length()

    def stack(fn):
        return jnp.concatenate([fn(h) for h in range(heads)], axis=0)

    def chunk_body(ch, carry):
        r0 = pl.multiple_of(ch * c, c)
        rows = pl.ds(r0, c)
        gb = gb_ref[rows, :]
        gcc = gc_ref[rows, :]
        beta = stack(lambda h: _lane_pick(gb, lane_c, h))
        gsel = stack(lambda h: jnp.where(lane_c == heads + h, gcc, 0.0))
        gc_col = jnp.sum(gsel, axis=-1, keepdims=True)
        gc_row = _mm_nt_hi(ones8, gsel)[0:1, :]
        g_last = stack(lambda h: jnp.broadcast_to(gc_col[(h + 1) * c - 1:(h + 1) * c, :], (c, 1)))
        ks = stack(lambda h: kn_ref[rows, h * dk:(h + 1) * dk])
        qs = stack(lambda h: qn_ref[rows, h * dk:(h + 1) * dk]) * scale
        vs = stack(lambda h: vv_ref[rows, h * dk:(h + 1) * dk])
        decay = jnp.where(incl, jnp.exp(jnp.where(incl, gc_col - gc_row, 0.0)), 0.0)
        k_beta = ks * beta
        lmat = jnp.where(strict, _mm_nt(k_beta, ks) * decay, 0.0)
        tinv = _unit_lower_inverse(lmat, eye, levels)
        egc = jnp.exp(gc_col)
        u = _mm(tinv, vs * beta)
        w = _mm(tinv, k_beta * egc)
        attn = _mm_nt(qs, ks) * decay
        kd = ks * jnp.exp(g_last - gc_col)
        qd = qs * egc
        at = attn[0:c]
        for h in range(1, heads):
            at = at + attn[h * c:(h + 1) * c]
        at_ref[rows, :] = at
        for h in range(heads):
            sl = slice(h * dk, (h + 1) * dk)
            hr = slice(h * c, (h + 1) * c)
            u_ref[rows, sl] = u[hr]
            w_ref[rows, sl] = w[hr]
            kd_ref[rows, sl] = kd[hr]
            qd_ref[rows, sl] = qd[hr]
            eg_ref[ch, :, sl] = jnp.broadcast_to(jnp.exp(gc_col[(h + 1) * c - 1:(h + 1) * c, :]),
                                                 (8, dk))
        return carry

    lax.fori_loop(0, tt // c, chunk_body, 0)


def _gdn_scan_kernel(u_ref, w_ref, qd_ref, kd_ref, at_ref, eg_ref, z_ref, s0_ref, nw_ref,
                     o_ref, sout_ref, s_ref, *, tt, heads, dk):
    c = GDN_CHUNK
    i = pl.program_id(1)

    @pl.when(i == 0)
    def _():
        s_ref[...] = s0_ref[...]

    def chunk_body(ch, carry):
        r0 = pl.multiple_of(ch * c, c)
        for h in range(heads):
            sl = slice(h * dk, (h + 1) * dk)
            state = s_ref[h]
            v_new = u_ref[pl.ds(r0, c), sl] - _mm(w_ref[pl.ds(r0, c), sl], state)
            o = (_mm(qd_ref[pl.ds(r0, c), sl], state)
                 + _mm(at_ref[pl.ds(r0, c), h * c:(h + 1) * c], v_new))
            s_ref[h] = state * eg_ref[ch, 0:1, sl] + _mm_tn(kd_ref[pl.ds(r0, c), sl], v_new)
            o_ref[pl.ds(r0, c), sl] = (_rms(o, nw_ref[...], RMS_EPS)
                                       * _silu(z_ref[pl.ds(r0, c), sl]))
        return carry

    lax.fori_loop(0, tt // c, chunk_body, 0)

    @pl.when(i == pl.num_programs(1) - 1)
    def _():
        sout_ref[...] = s_ref[...]


def _gated_deltanet(qkv, ba, z, conv_state, s0, conv_w, a_log, dt_bias, norm_w,
                    batch, seq_pad, t_real, heads, tt):
    cq = qkv.shape[1]
    hk = cq // 3
    dk = hk // heads
    c = GDN_CHUNK
    nt = seq_pad // tt
    nc = tt // c
    taps = conv_w.shape[0]
    st = jnp.pad(conv_state, ((0, 0), (8 - (taps - 1), 0), (0, 0)))
    gp = jnp.zeros((8, LANES), F32)
    gp = gp.at[0, heads:2 * heads].set(a_log.astype(F32))
    gp = gp.at[1, heads:2 * heads].set(dt_bias.astype(F32))
    row_blk = lambda b, i: (b * nt + i, 0)
    const = lambda b, i: (0, 0)
    tok = lambda width: pl.BlockSpec((tt, width), row_blk)
    rows = batch * seq_pad
    u, w, qd, kd, at, eg = pl.pallas_call(
        functools.partial(_gdn_pre_kernel, tt=tt, t_real=t_real, heads=heads, dk=dk),
        grid=(batch, nt),
        in_specs=[tok(cq),
                  pl.BlockSpec((None, 8, cq), lambda b, i: (b, 0, 0)),
                  pl.BlockSpec((taps, cq), const),
                  tok(LANES),
                  pl.BlockSpec((8, LANES), const)],
        out_specs=[tok(hk), tok(hk), tok(hk), tok(hk), tok(heads * c),
                   pl.BlockSpec((None, nc, 8, hk), lambda b, i: (b, i, 0, 0))],
        out_shape=[jax.ShapeDtypeStruct((rows, hk), F32)] * 4
                  + [jax.ShapeDtypeStruct((rows, heads * c), F32),
                     jax.ShapeDtypeStruct((batch, nt * nc, 8, hk), F32)],
        scratch_shapes=[pltpu.VMEM((tt + 8, cq), F32), pltpu.VMEM((tt, hk), F32),
                        pltpu.VMEM((tt, hk), F32), pltpu.VMEM((tt, hk), F32),
                        pltpu.VMEM((tt, LANES), F32), pltpu.VMEM((tt, LANES), F32)],
        compiler_params=_cparams("parallel", "arbitrary"),
        name="gdn_chunk_prepare",
    )(qkv, st, conv_w, ba, gp)
    o, s_new = pl.pallas_call(
        functools.partial(_gdn_scan_kernel, tt=tt, heads=heads, dk=dk),
        grid=(batch, nt),
        in_specs=[tok(hk), tok(hk), tok(hk), tok(hk), tok(heads * c),
                  pl.BlockSpec((None, nc, 8, hk), lambda b, i: (b, i, 0, 0)),
                  tok(hk),
                  pl.BlockSpec((None, heads, dk, dk), lambda b, i: (b, 0, 0, 0)),
                  pl.BlockSpec((1, dk), const)],
        out_specs=[tok(hk),
                   pl.BlockSpec((None, heads, dk, dk), lambda b, i: (b, 0, 0, 0))],
        out_shape=[jax.ShapeDtypeStruct((rows, hk), F32),
                   jax.ShapeDtypeStruct((batch, heads, dk, dk), F32)],
        scratch_shapes=[pltpu.VMEM((heads, dk, dk), F32)],
        compiler_params=_cparams("parallel", "arbitrary"),
        name="gdn_chunk_scan",
    )(u, w, qd, kd, at, eg, z, s0, norm_w.reshape(1, dk))
    return o, s_new


def _dwconv_kernel(h_ref, buf_ref, w_ref, b_ref, g_ref, bb_ref, o_ref, xe_ref, xs_ref, *, tt, rb):
    i = pl.program_id(1)
    taps = w_ref.shape[0]
    halo = buf_ref.shape[0]
    first = halo - (taps - 1)
    sub = w_ref.shape[1]

    @pl.when(i == 0)
    def _():
        xe_ref[0:halo, :] = buf_ref[...]

    xe_ref[halo:halo + tt, :] = h_ref[...]
    span = tt + halo - sub
    for ph in range(1, sub):
        xs_ref[ph - 1, 0:span, :] = xe_ref[pl.ds(ph, span), :]

    def window(row, size):
        a, ph = divmod(row, sub)
        if ph == 0:
            return xe_ref[pl.ds(row, size), :]
        return xs_ref[ph - 1, pl.ds(a * sub, size), :]

    def weight(j):
        return w_ref[j] if rb == sub else jnp.tile(w_ref[j], (rb // sub, 1))

    for r0 in range(0, tt, rb):
        acc = b_ref[...] + weight(0) * window(first + r0, rb)
        for j in range(1, taps):
            acc = acc + weight(j) * window(first + r0 + j, rb)
        mu = jnp.mean(acc, axis=-1, keepdims=True)
        xc = acc - mu
        var = jnp.mean(xc * xc, axis=-1, keepdims=True)
        y = xc * lax.rsqrt(var + LN_EPS) * g_ref[...] + bb_ref[...]
        o_ref[pl.ds(r0, rb), :] = _silu(y)
    xe_ref[0:halo, :] = xe_ref[tt:tt + halo, :]


def _dwconv_ln_swish(h, buf, w_dw, b_dw, ln_g, ln_b, batch, seq_pad, tt):
    ch = h.shape[1]
    taps = w_dw.shape[0]
    halo = -(-(taps - 1) // SUBLANES) * SUBLANES
    bufp = jnp.pad(buf, ((0, 0), (halo - (taps - 1), 0), (0, 0)))
    nt = seq_pad // tt
    rb = min(tt, 16)
    const = lambda b, i: (0, 0)
    vec = lambda a: a.reshape(1, ch)
    w_rep = jnp.broadcast_to(w_dw[:, None, :], (taps, SUBLANES, ch))
    return pl.pallas_call(
        functools.partial(_dwconv_kernel, tt=tt, rb=rb),
        grid=(batch, nt),
        in_specs=[pl.BlockSpec((tt, ch), lambda b, i: (b * nt + i, 0)),
                  pl.BlockSpec((None, halo, ch), lambda b, i: (b, 0, 0)),
                  pl.BlockSpec((taps, SUBLANES, ch), lambda b, i: (0, 0, 0)),
                  pl.BlockSpec((1, ch), const), pl.BlockSpec((1, ch), const),
                  pl.BlockSpec((1, ch), const)],
        out_specs=pl.BlockSpec((tt, ch), lambda b, i: (b * nt + i, 0)),
        out_shape=jax.ShapeDtypeStruct((batch * seq_pad, ch), F32),
        scratch_shapes=[pltpu.VMEM((tt + halo, ch), F32),
                        pltpu.VMEM((SUBLANES - 1, tt + halo - SUBLANES, ch), F32)],
        compiler_params=_cparams("parallel", "arbitrary"),
        name="dwconv_ln_swish",
    )(h, bufp, w_rep, vec(b_dw), vec(ln_g), vec(ln_b))


def _pad_time(a, batch, t, t_pad):
    a = a.reshape(batch, t, a.shape[-1])
    return jnp.pad(a, ((0, 0), (0, t_pad - t), (0, 0))).reshape(batch * t_pad, a.shape[-1])


def kernel(x_prompt, x_sample, cache_attn_k, cache_attn_v, page_table, state_gdn_conv, state_gdn_s, state_conv_buf, norm_mix, norm_ffn, norm_final, w_in, w_out, gdn_conv_w, gdn_A_log, gdn_dt_bias, gdn_norm_w, lam_q1, lam_k1, lam_q2, lam_k2, diff_subln_w, rel_bias, conv_w_pw1, conv_b_pw1, conv_w_dw, conv_b_dw, conv_ln_g, conv_ln_b, conv_w_pw2, conv_b_pw2, ffn_w_gate, ffn_w_up, ffn_w_down):
    bp, seq, d = x_prompt.shape
    db, dseq, _ = x_sample.shape
    assert dseq == 1
    depth = norm_mix.shape[0]
    ha, dva = cache_attn_k.shape[3], cache_attn_v.shape[4]
    hb, dkb = state_gdn_s.shape[2], state_gdn_s.shape[3]
    c_qkv = state_gdn_conv.shape[3]
    gdn_taps = gdn_conv_w.shape[1]
    conv_taps = conv_w_dw.shape[1]
    mp = bp * seq
    assert 2 * hb <= LANES and mp % 8 == 0

    tm = 256
    x = jnp.concatenate([x_prompt.reshape(mp, d), x_sample.reshape(db, d)], axis=0)
    zeros_d = jnp.zeros((d,), F32)

    sizes = (ha * dva, ha * dva, ha * dva, c_qkv, hb * dkb, LANES)
    offs = [0]
    for s_ in sizes:
        offs.append(offs[-1] + s_)
    groups = tuple(zip(offs[:-1], sizes))

    def last_rows(a, n):
        return jnp.stack([a[(b + 1) * seq - n:(b + 1) * seq] for b in range(bp)])

    k_p, v_p, k_s, v_s, gc_p, gc_s, gs_p, gs_s, cb_p, cb_s = ([] for _ in range(10))
    for layer in range(depth):
        if layer % 2 == 0:
            e = layer // 2
            lam_init = 0.8 - 0.6 * math.exp(-0.3 * layer)
            w_in_e = jnp.pad(w_in[e], ((0, 0), (0, offs[-1] - w_in.shape[2]))).astype(BF16)
            qa, ka, va, qkv_b, z_b, ba_b = _in_projection(x, norm_mix[layer], w_in_e, groups, tm)
            lam_vecs = (lam_q1[e], lam_k1[e], lam_q2[e], lam_k2[e])

            oa_p = _attention_prompt(qa, ka, va, rel_bias, lam_vecs, diff_subln_w[e],
                                     bp, seq, ha, lam_init, t=min(512, seq))
            oa_s = _attention_decode(qa[mp:], ka[mp:], va[mp:], cache_attn_k[e], cache_attn_v[e],
                                     page_table, rel_bias, lam_vecs, diff_subln_w[e],
                                     ha, lam_init, pages=8)

            ob_p, s_p = _gated_deltanet(
                qkv_b, ba_b, z_b, jnp.zeros((bp, gdn_taps - 1, c_qkv), F32),
                jnp.zeros((bp, hb, dkb, dkb), F32), gdn_conv_w[e], gdn_A_log[e], gdn_dt_bias[e],
                gdn_norm_w[e], bp, seq, seq, hb, tt=min(512, seq))
            cpad = GDN_CHUNK
            ob_s, s_s = _gated_deltanet(
                _pad_time(qkv_b[mp:], db, 1, cpad), _pad_time(ba_b[mp:], db, 1, cpad),
                _pad_time(z_b[mp:], db, 1, cpad), state_gdn_conv[e], state_gdn_s[e],
                gdn_conv_w[e], gdn_A_log[e], gdn_dt_bias[e], gdn_norm_w[e],
                db, cpad, 1, hb, tt=cpad)
            ob_s = ob_s.reshape(db, cpad, hb * dkb)[:, 0]

            oa = jnp.concatenate([oa_p, oa_s], axis=0)
            ob = jnp.concatenate([ob_p, ob_s], axis=0)
            w_o = w_out[e].astype(BF16)
            x = _projection_residual([oa, ob], [w_o[:ha * dva], w_o[ha * dva:]], zeros_d, x, tm)

            k_p.append(ka[:mp].reshape(bp, seq, ha, dva))
            v_p.append(va[:mp].reshape(bp, seq, ha, dva))
            k_s.append(ka[mp:].reshape(db, 1, ha, dva))
            v_s.append(va[mp:].reshape(db, 1, ha, dva))
            gc_p.append(last_rows(qkv_b, gdn_taps - 1))
            gc_s.append(jnp.concatenate([state_gdn_conv[e], qkv_b[mp:].reshape(db, 1, c_qkv)],
                                        axis=1)[:, -(gdn_taps - 1):])
            gs_p.append(s_p)
            gs_s.append(s_s)
        else:
            cidx = layer // 2
            hg = _glu_projection(x, norm_mix[layer], conv_w_pw1[cidx].astype(BF16),
                                 conv_b_pw1[cidx], tm)
            dconv = hg.shape[1]
            hc_p = _dwconv_ln_swish(hg, jnp.zeros((bp, conv_taps - 1, dconv), F32),
                                    conv_w_dw[cidx], conv_b_dw[cidx], conv_ln_g[cidx],
                                    conv_ln_b[cidx], bp, seq, tt=min(128, seq))
            hg_s = hg[mp:]
            hc_s = _dwconv_ln_swish(_pad_time(hg_s, db, 1, 8), state_conv_buf[cidx],
                                    conv_w_dw[cidx], conv_b_dw[cidx], conv_ln_g[cidx],
                                    conv_ln_b[cidx], db, 8, tt=8)
            hc = jnp.concatenate([hc_p, hc_s.reshape(db, 8, dconv)[:, 0]], axis=0)
            x = _projection_residual([hc], [conv_w_pw2[cidx].astype(BF16)], conv_b_pw2[cidx], x, tm)
            cb_p.append(last_rows(hg, conv_taps - 1))
            cb_s.append(jnp.concatenate([state_conv_buf[cidx], hg_s.reshape(db, 1, dconv)],
                                        axis=1)[:, -(conv_taps - 1):])
        last = layer == depth - 1
        d_ff = ffn_w_gate.shape[2]
        x = _ffn(x, norm_ffn[layer], ffn_w_gate[layer].astype(BF16), ffn_w_up[layer].astype(BF16),
                 ffn_w_down[layer].astype(BF16), norm_final, last, tm=512, tf=d_ff // 2)

    y_prompt = x[:mp].reshape(bp, seq, d)
    y_sample = x[mp:].reshape(db, 1, d)
    return (y_prompt, y_sample, jnp.stack(k_p), jnp.stack(v_p), jnp.stack(k_s), jnp.stack(v_s),
            jnp.stack(gc_p), jnp.stack(gc_s), jnp.stack(gs_p), jnp.stack(gs_s),
            jnp.stack(cb_p), jnp.stack(cb_s))
```

```python
import functools
import math

import jax
import jax.numpy as jnp
from jax import lax
from jax.experimental import pallas as pl
from jax.experimental.pallas import tpu as pltpu

F32 = jnp.float32
BF16 = jnp.bfloat16

RMS_EPS = 1e-6
LN_EPS = 1e-5
L2_EPS = 1e-6
NUM_BUCKETS = 32
MAX_DISTANCE = 128
GDN_CHUNK = 64
NEG_BIG = -1e30
LOG2E = math.log2(math.e)
LANES = 128
SUBLANES = 8
VMEM_LIMIT = 48 * 1024 * 1024

_HI = lax.Precision.HIGHEST


def _cparams(*sem):
    return pltpu.CompilerParams(dimension_semantics=sem, vmem_limit_bytes=VMEM_LIMIT)


def _mm(a, b):
    return jnp.dot(a.astype(BF16), b.astype(BF16), preferred_element_type=F32)


def _mm_nt(a, b):
    return lax.dot_general(a.astype(BF16), b.astype(BF16), (((1,), (1,)), ((), ())),
                           preferred_element_type=F32)


def _mm_tn(a, b):
    return lax.dot_general(a.astype(BF16), b.astype(BF16), (((0,), (0,)), ((), ())),
                           preferred_element_type=F32)


def _mm_hi(a, b):
    return jnp.dot(a, b, precision=_HI, preferred_element_type=F32)


def _mm_nt_hi(a, b):
    return lax.dot_general(a, b, (((1,), (1,)), ((), ())), precision=_HI,
                           preferred_element_type=F32)


def _rms(x, w, eps):
    return x * lax.rsqrt(jnp.mean(x * x, axis=-1, keepdims=True) + eps) * w


def _sigmoid(x):
    return 1.0 / (1.0 + jnp.exp(-x))


def _silu(x):
    return x * _sigmoid(x)


def _lane_pick(x, lane_idx, k):
    return jnp.sum(jnp.where(lane_idx == k, x, 0.0), axis=-1, keepdims=True)


def _token_call(body, prompt_in, sample_in, shared, outs, tm, name, inner=1, scratch=()):
    mp, ms = prompt_in[0].shape[0], sample_in[0].shape[0]
    assert mp % tm == 0
    n = mp // tm
    n_in, n_sh, n_out = len(prompt_in), len(shared), len(outs)

    def kern(*refs):
        p_in, s_in = refs[:n_in], refs[n_in:2 * n_in]
        sh = refs[2 * n_in:2 * n_in + n_sh]
        o0 = 2 * n_in + n_sh
        p_out, s_out = refs[o0:o0 + n_out], refs[o0 + n_out:o0 + 2 * n_out]
        scr = refs[o0 + 2 * n_out:]
        i = pl.program_id(0)

        @pl.when(i < n)
        def _():
            body(p_in, sh, p_out, scr)

        @pl.when(i == n)
        def _():
            body(s_in, sh, s_out, scr)

    prow = lambda i, k: (jnp.minimum(i, n - 1), 0)
    srow = lambda i, k: (0, 0)
    in_specs = ([pl.BlockSpec((tm, a.shape[1]), prow) for a in prompt_in]
                + [pl.BlockSpec((ms, a.shape[1]), srow) for a in sample_in]
                + [pl.BlockSpec(bs, (lambda i, k, f=f: f(k))) for _, bs, f in shared])
    out_specs = ([pl.BlockSpec((tm * r, w), prow) for w, r in outs]
                 + [pl.BlockSpec((ms * r, w), srow) for w, r in outs])
    out_shape = ([jax.ShapeDtypeStruct((mp * r, w), F32) for w, r in outs]
                 + [jax.ShapeDtypeStruct((ms * r, w), F32) for w, r in outs])
    res = pl.pallas_call(
        kern, grid=(n + 1, inner), in_specs=in_specs, out_specs=out_specs, out_shape=out_shape,
        scratch_shapes=list(scratch), compiler_params=_cparams("arbitrary", "arbitrary"),
        name=name,
    )(*prompt_in, *sample_in, *[a for a, _, _ in shared])
    return res[:n_out], res[n_out:]


def _whole(a):
    return (a, a.shape, lambda k, nd=a.ndim: (0,) * nd)


def _inproj_body(ins, sh, outs, scr, *, groups, heads, kv_groups):
    x_ref, = ins
    nw_ref, w_ref = sh
    xn = _rms(x_ref[...], nw_ref[...], RMS_EPS).astype(BF16)
    rows = x_ref.shape[0]
    n_g = len(groups)
    for g, (o_ref, (off, width)) in enumerate(zip(outs[:n_g], groups)):
        y = jnp.dot(xn, w_ref[:, off:off + width], preferred_element_type=F32)
        o_ref[...] = y
        if g in kv_groups:
            o4_ref = outs[n_g + kv_groups.index(g)]
            dv = width // heads
            for h in range(heads):
                o4_ref[pl.ds(h, rows, stride=heads), :] = y[:, h * dv:(h + 1) * dv]


def _glu_body(ins, sh, outs, scr):
    x_ref, = ins
    nw_ref, w_ref, b_ref = sh
    o_ref, = outs
    dc = o_ref.shape[1]
    xn = _rms(x_ref[...], nw_ref[...], RMS_EPS).astype(BF16)
    a = jnp.dot(xn, w_ref[:, :dc], preferred_element_type=F32) + b_ref[:, :dc]
    g = jnp.dot(xn, w_ref[:, dc:], preferred_element_type=F32) + b_ref[:, dc:]
    o_ref[...] = a * _sigmoid(g)


def _proj_res_body(ins, sh, outs, scr):
    o_ref, = outs
    acc = ins[-1][...] + sh[-1][...]
    for a_ref, w_ref in zip(ins[:-1], sh[:-1]):
        acc = acc + _mm(a_ref[...], w_ref[...])
    o_ref[...] = acc


def _ffn_body(ins, sh, outs, scr, *, final_norm):
    x_ref, = ins
    nw_ref, wg_ref, wu_ref, wd_ref, fw_ref = sh
    o_ref, = outs
    xn_ref, acc_ref = scr
    rows = x_ref.shape[0]
    k = pl.program_id(1)

    @pl.when(k == 0)
    def _():
        xn_ref[0:rows, :] = _rms(x_ref[...], nw_ref[...], RMS_EPS).astype(BF16)
        acc_ref[0:rows, :] = x_ref[...]

    xn = xn_ref[0:rows, :]
    g = jnp.dot(xn, wg_ref[...], preferred_element_type=F32)
    u = jnp.dot(xn, wu_ref[...], preferred_element_type=F32)
    acc_ref[0:rows, :] += _mm(_silu(g) * u, wd_ref[...])

    @pl.when(k == pl.num_programs(1) - 1)
    def _():
        y = acc_ref[0:rows, :]
        if final_norm:
            y = _rms(y, fw_ref[...], RMS_EPS)
        o_ref[...] = y


def _t5_bucket(n):
    max_exact = NUM_BUCKETS // 2
    nf = jnp.maximum(n, 1).astype(F32)
    large = max_exact + (jnp.log(nf / max_exact) / math.log(MAX_DISTANCE / max_exact)
                         * (NUM_BUCKETS - max_exact)).astype(jnp.int32)
    large = jnp.minimum(large, NUM_BUCKETS - 1)
    return jnp.where(n < max_exact, n, large)


def _lambda(lq1_ref, lk1_ref, lq2_ref, lk2_ref, lam_init):
    s1 = jnp.sum(lq1_ref[...] * lk1_ref[...], axis=-1, keepdims=True)
    s2 = jnp.sum(lq2_ref[...] * lk2_ref[...], axis=-1, keepdims=True)
    return jnp.exp(s1) - jnp.exp(s2) + lam_init


def _attn_prompt_kernel(q_ref, k_ref, v_ref, bias_ref, lq1_ref, lk1_ref, lq2_ref, lk2_ref,
                        sw_ref, o_ref, qt_ref, kb_ref, vt_ref, m_ref, acc_ref, *,
                        t, da, lam_init):
    i = pl.program_id(2)
    nblk = kb_ref.shape[0]
    dv = v_ref.shape[1]

    @pl.when(i == 0)
    def _():
        for c in range(nblk):
            kb_ref[c] = k_ref[c * t:(c + 1) * t, :].astype(BF16)
            vt_ref[c, 0:dv, :] = v_ref[c * t:(c + 1) * t, :].T.astype(BF16)
            vt_ref[c, dv:, :] = jnp.ones((vt_ref.shape[1] - dv, t), BF16)

    qt = (q_ref[...] * (da ** -0.5 * LOG2E)).T
    first_map = lax.broadcasted_iota(jnp.int32, qt.shape, 0) < da
    qt_ref[:, 0:t] = jnp.where(first_map, qt, 0.0).astype(BF16)
    qt_ref[:, t:2 * t] = jnp.where(first_map, 0.0, qt).astype(BF16)
    m_ref[...] = jnp.full(m_ref.shape, NEG_BIG, F32)
    acc_ref[...] = jnp.zeros(acc_ref.shape, F32)

    def process(blocks):
        scores = []
        m_new = m_ref[...]
        for j, bias in blocks:
            s = jnp.dot(kb_ref[j], qt_ref[...], preferred_element_type=F32)
            if bias is not None:
                s = s + jnp.concatenate([bias, bias], axis=1)
            m_new = jnp.maximum(m_new, jnp.max(s, axis=0, keepdims=True))
            scores.append(s)
        acc = acc_ref[...] * jnp.exp2(m_ref[...] - m_new)
        for (j, _), s in zip(blocks, scores):
            acc = acc + jnp.dot(vt_ref[j], jnp.exp2(s - m_new).astype(BF16),
                                preferred_element_type=F32)
        acc_ref[...] = acc
        m_ref[...] = m_new

    n_far = i - 1

    def far_pair(jj, carry):
        process([(2 * jj, None), (2 * jj + 1, None)])
        return carry

    lax.fori_loop(0, n_far // 2, far_pair, 0)

    @pl.when((n_far > 0) & (n_far % 2 == 1))
    def _():
        process([(n_far - 1, None)])

    @pl.when(i >= 1)
    def _():
        process([(i - 1, bias_ref[0]), (i, bias_ref[1])])

    @pl.when(i == 0)
    def _():
        process([(i, bias_ref[1])])

    acc = acc_ref[...]
    o12 = acc[0:dv] / acc[dv:dv + 1]
    lam = _lambda(lq1_ref, lk1_ref, lq2_ref, lk2_ref, lam_init)
    o = (o12[:, 0:t] - lam * o12[:, t:2 * t]).T
    o_ref[...] = _rms(o, sw_ref[...], LN_EPS) * (1.0 - lam_init)


def _toeplitz(r, t):
    h, period = r.shape
    flat = jnp.tile(r, (1, t))[:, :t * (period - 1)]
    return flat.reshape(h, t, period - 1)[:, :, :t]


def _prompt_bias_tiles(rel_bias, t):
    assert t >= MAX_DISTANCE
    far = rel_bias[NUM_BUCKETS - 1]
    b1 = jnp.transpose(rel_bias[_t5_bucket(jnp.arange(2 * t))] - far).astype(F32) * LOG2E
    r_diag = jnp.concatenate([b1[:, :t], jnp.full_like(b1[:, :t], NEG_BIG)], axis=1)
    r_prev = jnp.concatenate([b1[:, t:], b1[:, :t]], axis=1)
    return jnp.stack([_toeplitz(r_prev, t), _toeplitz(r_diag, t)], axis=1)


def _attention_prompt(q, k, v, rel_bias, lam_vecs, subln_w, batch, seq, heads, lam_init, t):
    dv = q.shape[1] // heads
    da = dv // 2
    nq = seq // t
    bias = _prompt_bias_tiles(rel_bias, t)
    vec = lambda a: a.reshape(1, -1)
    const = lambda b, h, i: (0, 0)
    return pl.pallas_call(
        functools.partial(_attn_prompt_kernel, t=t, da=da, lam_init=lam_init),
        grid=(batch, heads, nq),
        in_specs=[pl.BlockSpec((t, dv), lambda b, h, i: (b * nq + i, h)),
                  pl.BlockSpec((seq, dv), lambda b, h, i: (b, h)),
                  pl.BlockSpec((seq, dv), lambda b, h, i: (b, h)),
                  pl.BlockSpec((None, 2, t, t), lambda b, h, i: (h, 0, 0, 0)),
                  pl.BlockSpec((1, da), const), pl.BlockSpec((1, da), const),
                  pl.BlockSpec((1, da), const), pl.BlockSpec((1, da), const),
                  pl.BlockSpec((1, dv), const)],
        out_specs=pl.BlockSpec((t, dv), lambda b, h, i: (b * nq + i, h)),
        out_shape=jax.ShapeDtypeStruct((batch * seq, heads * dv), F32),
        scratch_shapes=[pltpu.VMEM((dv, 2 * t), BF16), pltpu.VMEM((nq, t, dv), BF16),
                        pltpu.VMEM((nq, dv + 2 * SUBLANES, t), BF16),
                        pltpu.VMEM((1, 2 * t), F32),
                        pltpu.VMEM((dv + 2 * SUBLANES, 2 * t), F32)],
        compiler_params=_cparams("parallel", "parallel", "arbitrary"),
        name="diff_attention_prompt",
    )(q, k, v, bias, *[vec(a) for a in lam_vecs], vec(subln_w))


def _attn_decode_kernel(pt_ref, q_ref, kn_ref, vn_ref, bfar_ref, blast_ref, bnew_ref,
                        lq1_ref, lk1_ref, lq2_ref, lk2_ref, sw_ref, *rest,
                        pages, heads, da, lam_init):
    del pt_ref
    k_refs = rest[:pages]
    v_refs = rest[pages:2 * pages]
    o_ref, m_ref, l_ref, acc_ref = rest[2 * pages:]
    j = pl.program_id(1)
    last = pl.num_programs(1) - 1

    @pl.when(j == 0)
    def _():
        m_ref[...] = jnp.full(m_ref.shape, NEG_BIG, F32)
        l_ref[...] = jnp.zeros(l_ref.shape, F32)
        acc_ref[...] = jnp.zeros(acc_ref.shape, F32)

    q = q_ref[...] * (da ** -0.5)
    row = lax.broadcasted_iota(jnp.int32, q.shape, 0)
    lane = lax.broadcasted_iota(jnp.int32, q.shape, 1)
    qs = jnp.where((row < heads) == (lane < da), q, 0.0)
    qs_bf = qs.astype(BF16)

    s = jnp.concatenate([_mm_nt(qs_bf, k_ref[...]) for k_ref in k_refs], axis=1)
    s = s + jnp.where(j == last, blast_ref[...], bfar_ref[...])
    m_prev = m_ref[...]
    m_new = jnp.maximum(m_prev, jnp.max(s, axis=-1, keepdims=True))
    p = jnp.exp(s - m_new)
    alpha = jnp.exp(m_prev - m_new)
    l_ref[...] = alpha * l_ref[...] + jnp.sum(p, axis=-1, keepdims=True)
    rows_per_page = k_refs[0].shape[0]
    pv = acc_ref[...] * alpha
    for idx, v_ref in enumerate(v_refs):
        pv = pv + _mm(p[:, idx * rows_per_page:(idx + 1) * rows_per_page], v_ref[...])
    acc_ref[...] = pv
    m_ref[...] = m_new

    @pl.when(j == last)
    def _():
        s_new = jnp.sum(qs * kn_ref[...], axis=-1, keepdims=True) + bnew_ref[:, 0:1]
        m_prev = m_ref[...]
        m_fin = jnp.maximum(m_prev, s_new)
        p_new = jnp.exp(s_new - m_fin)
        alpha = jnp.exp(m_prev - m_fin)
        l_fin = alpha * l_ref[...] + p_new
        acc = alpha * acc_ref[...] + p_new * vn_ref[...]
        o12 = acc / l_fin
        lam = _lambda(lq1_ref, lk1_ref, lq2_ref, lk2_ref, lam_init)
        o = o12[0:heads] - lam * o12[heads:2 * heads]
        o_ref[...] = _rms(o, sw_ref[...], LN_EPS) * (1.0 - lam_init)


def _attention_decode(q, k_new, v_new, k_pool, v_pool, page_table, rel_bias, lam_vecs,
                      subln_w, heads, lam_init, pages):
    db = q.shape[0]
    n_pool, page, _, dv = k_pool.shape
    da = dv // 2
    n_pages = page_table.shape[1]
    past = n_pages * page
    rpp = page * heads
    span = pages * page
    assert n_pages % pages == 0 and span >= MAX_DISTANCE
    kp = k_pool.reshape(n_pool, rpp, dv)
    vp = v_pool.reshape(n_pool, rpp, dv)

    def two_maps(a):
        a = a.reshape(db, 1, heads, dv)
        return jnp.broadcast_to(a, (db, 2, heads, dv)).reshape(db, 2 * heads, dv)

    same = jnp.eye(heads, dtype=bool)

    def pair_bias(b):
        b = jnp.where(same[:, None, :], jnp.transpose(b)[:, :, None], NEG_BIG)
        return jnp.tile(b.reshape(heads, span * heads), (2, 1))

    rb = rel_bias.astype(F32)
    b_far = pair_bias(jnp.broadcast_to(rb[NUM_BUCKETS - 1], (span, heads)))
    b_last = pair_bias(rb[_t5_bucket(span - jnp.arange(span))])
    b_new = jnp.tile(jnp.broadcast_to(rb[0][:, None], (heads, LANES)), (2, 1))

    vec = lambda a: a.reshape(1, -1)
    const = lambda s, j, pt: (0, 0)

    def page_spec(idx):
        return pl.BlockSpec((None, rpp, dv), lambda s, j, pt: (pt[s, j * pages + idx], 0, 0))

    grid_spec = pltpu.PrefetchScalarGridSpec(
        num_scalar_prefetch=1,
        grid=(db, n_pages // pages),
        in_specs=([pl.BlockSpec((None, 2 * heads, dv), lambda s, j, pt: (s, 0, 0))] * 3
                  + [pl.BlockSpec((2 * heads, span * heads), const),
                     pl.BlockSpec((2 * heads, span * heads), const),
                     pl.BlockSpec((2 * heads, LANES), const),
                     pl.BlockSpec((1, da), const), pl.BlockSpec((1, da), const),
                     pl.BlockSpec((1, da), const), pl.BlockSpec((1, da), const),
                     pl.BlockSpec((1, dv), const)]
                  + [page_spec(idx) for idx in range(pages)] * 2),
        out_specs=pl.BlockSpec((None, heads, dv), lambda s, j, pt: (s, 0, 0)),
        scratch_shapes=[pltpu.VMEM((2 * heads, 1), F32), pltpu.VMEM((2 * heads, 1), F32),
                        pltpu.VMEM((2 * heads, dv), F32)],
    )
    out = pl.pallas_call(
        functools.partial(_attn_decode_kernel, pages=pages, heads=heads, da=da,
                          lam_init=lam_init),
        grid_spec=grid_spec,
        out_shape=jax.ShapeDtypeStruct((db, heads, dv), F32),
        compiler_params=_cparams("parallel", "arbitrary"),
        name="diff_attention_decode",
    )(page_table, two_maps(q), two_maps(k_new), two_maps(v_new), b_far, b_last, b_new,
      *[vec(a) for a in lam_vecs], vec(subln_w), *([kp] * pages), *([vp] * pages))
    return out.reshape(db, heads * dv)


def _unit_lower_inverse(lmat, eye, levels):
    if levels == 0:
        return eye
    x = -lmat
    tinv = eye + x
    if levels == 1:
        return tinv
    n = lmat.shape[0]
    r = _mm(x, x)
    for k in range(1, levels):
        if k < levels - 1:
            both = _mm(jnp.concatenate([r, tinv], axis=0), r)
            r, tinv = both[:n], tinv + both[n:]
        else:
            tinv = tinv + _mm(tinv, r)
    return tinv


def _gdn_pre_kernel(x_ref, st_ref, cw_ref, ba_ref, gp_ref,
                    u_ref, w_ref, qd_ref, kd_ref, at_ref, eg_ref,
                    xe_ref, qn_ref, kn_ref, vv_ref, gb_ref, gc_ref, *, tt, t_real, heads, dk):
    c = GDN_CHUNK
    i = pl.program_id(1)
    taps = cw_ref.shape[0]
    halo = SUBLANES
    hk = heads * dk

    @pl.when(i == 0)
    def _():
        xe_ref[0:halo, :] = st_ref[...]

    xe_ref[halo:halo + tt, :] = x_ref[...]
    acc = cw_ref[taps - 1:taps, :] * x_ref[...]
    for j in range(taps - 1):
        acc = acc + cw_ref[j:j + 1, :] * xe_ref[pl.ds(halo - (taps - 1) + j, tt), :]
    xe_ref[0:halo, :] = xe_ref[tt:tt + halo, :]
    hcv = _silu(acc)

    valid = (i * tt + lax.broadcasted_iota(jnp.int32, (tt, 1), 0)) < t_real
    for h in range(heads):
        sl = slice(h * dk, (h + 1) * dk)
        qh = hcv[:, h * dk:(h + 1) * dk]
        kh = hcv[:, hk + h * dk:hk + (h + 1) * dk]
        qn_ref[:, sl] = qh * lax.rsqrt(jnp.sum(qh * qh, axis=-1, keepdims=True) + L2_EPS)
        kn = kh * lax.rsqrt(jnp.sum(kh * kh, axis=-1, keepdims=True) + L2_EPS)
        kn_ref[:, sl] = jnp.where(valid, kn, 0.0)
    vv_ref[...] = jnp.where(valid, hcv[:, 2 * hk:], 0.0)

    ba = ba_ref[...]
    lane = lax.broadcasted_iota(jnp.int32, ba.shape, 1)
    xa = ba + gp_ref[1:2, :]
    softplus = jnp.maximum(xa, 0.0) + jnp.log1p(jnp.exp(-jnp.abs(xa)))
    gates = jnp.where(lane < heads, _sigmoid(ba), -jnp.exp(gp_ref[0:1, :]) * softplus)
    gates = jnp.where(valid & (lane < 2 * heads), gates, 0.0)
    gb_ref[...] = gates

    tri_l = (lax.broadcasted_iota(jnp.int32, (c, c), 0)
             >= lax.broadcasted_iota(jnp.int32, (c, c), 1)).astype(F32)
    for ch in range(tt // c):
        gc_ref[ch * c:(ch + 1) * c, :] = _mm_hi(tri_l, gates[ch * c:(ch + 1) * c, :])

    hc = heads * c
    rs = lax.broadcasted_iota(jnp.int32, (hc, hc), 0)
    cs = lax.broadcasted_iota(jnp.int32, (hc, hc), 1)
    same_head = (rs ^ cs) < c
    incl = same_head & (rs >= cs)
    strict = same_head & (rs > cs)
    eye = (rs == cs).astype(F32)
    ones8 = jnp.ones((SUBLANES, LANES), F32)
    lane_c = lax.broadcasted_iota(jnp.int32, (c, LANES), 1)
    scale = dk ** -0.5
    c_valid = c if t_real >= c else t_real
    levels = (c_valid - 1).bit_length()

    def stack(fn):
        return jnp.concatenate([fn(h) for h in range(heads)], axis=0)

    def chunk_body(ch, carry):
        r0 = pl.multiple_of(ch * c, c)
        rows = pl.ds(r0, c)
        gb = gb_ref[rows, :]
        gcc = gc_ref[rows, :]
        beta = stack(lambda h: _lane_pick(gb, lane_c, h))
        gsel = stack(lambda h: jnp.where(lane_c == heads + h, gcc, 0.0))
        gc_col = jnp.sum(gsel, axis=-1, keepdims=True)
        gc_row = _mm_nt_hi(ones8, gsel)[0:1, :]
        g_last = stack(lambda h: jnp.broadcast_to(gc_col[(h + 1) * c - 1:(h + 1) * c, :], (c, 1)))
        ks = stack(lambda h: kn_ref[rows, h * dk:(h + 1) * dk])
        qs = stack(lambda h: qn_ref[rows, h * dk:(h + 1) * dk]) * scale
        vs = stack(lambda h: vv_ref[rows, h * dk:(h + 1) * dk])
        decay = jnp.where(incl, jnp.exp(jnp.where(incl, gc_col - gc_row, 0.0)), 0.0)
        k_beta = ks * beta
        lmat = jnp.where(strict, _mm_nt(k_beta, ks) * decay, 0.0)
        tinv = _unit_lower_inverse(lmat, eye, levels)
        egc = jnp.exp(gc_col)
        u = _mm(tinv, vs * beta)
        w = _mm(tinv, k_beta * egc)
        attn = _mm_nt(qs, ks) * decay
        kd = ks * jnp.exp(g_last - gc_col)
        qd = qs * egc
        at = attn[0:c]
        for h in range(1, heads):
            at = at + attn[h * c:(h + 1) * c]
        at_ref[rows, :] = at
        for h in range(heads):
            sl = slice(h * dk, (h + 1) * dk)
            hr = slice(h * c, (h + 1) * c)
            u_ref[rows, sl] = u[hr]
            w_ref[rows, sl] = w[hr]
            kd_ref[rows, sl] = kd[hr]
            qd_ref[rows, sl] = qd[hr]
            eg_ref[ch, :, sl] = jnp.broadcast_to(jnp.exp(gc_col[(h + 1) * c - 1:(h + 1) * c, :]),
                                                 (SUBLANES, dk))
        return carry

    n_chunks = tt // c
    lax.fori_loop(0, n_chunks, chunk_body, 0, unroll=2 if n_chunks % 2 == 0 else 1)


def _gdn_scan_kernel(u_ref, w_ref, qd_ref, kd_ref, at_ref, eg_ref, z_ref, s0_ref, nw_ref,
                     o_ref, sout_ref, s_ref, *, tt, heads, dk):
    c = GDN_CHUNK
    i = pl.program_id(1)
    nb = u_ref.shape[0]

    @pl.when(i == 0)
    def _():
        s_ref[...] = s0_ref[...]

    def chunk_body(ch, carry):
        rows = pl.ds(pl.multiple_of(ch * c, c), c)
        for b in range(nb):
            for h in range(heads):
                sl = slice(h * dk, (h + 1) * dk)
                state = s_ref[b, h]
                v_new = u_ref[b, rows, sl] - _mm(w_ref[b, rows, sl], state)
                o = (_mm(qd_ref[b, rows, sl], state)
                     + _mm(at_ref[b, rows, h * c:(h + 1) * c], v_new))
                s_ref[b, h] = state * eg_ref[b, ch, 0:1, sl] + _mm_tn(kd_ref[b, rows, sl], v_new)
                o_ref[b, rows, sl] = _rms(o, nw_ref[...], RMS_EPS) * _silu(z_ref[b, rows, sl])
        return carry

    lax.fori_loop(0, tt // c, chunk_body, 0)

    @pl.when(i == pl.num_programs(1) - 1)
    def _():
        sout_ref[...] = s_ref[...]


def _gated_deltanet(qkv, ba, z, conv_state, s0, conv_w, a_log, dt_bias, norm_w,
                    batch, seq_pad, t_real, heads, tt, scan_batch, scan_tt):
    cq = qkv.shape[1]
    hk = cq // 3
    dk = hk // heads
    c = GDN_CHUNK
    assert c & (c - 1) == 0 and batch % scan_batch == 0
    nt = seq_pad // tt
    nc = tt // c
    taps = conv_w.shape[0]
    st = jnp.pad(conv_state, ((0, 0), (SUBLANES - (taps - 1), 0), (0, 0)))
    gp = jnp.zeros((SUBLANES, LANES), F32)
    gp = gp.at[0, heads:2 * heads].set(a_log.astype(F32))
    gp = gp.at[1, heads:2 * heads].set(dt_bias.astype(F32))
    row_blk = lambda b, i: (b * nt + i, 0)
    const = lambda b, i: (0, 0)
    tok = lambda width: pl.BlockSpec((tt, width), row_blk)
    rows = batch * seq_pad
    u, w, qd, kd, at, eg = pl.pallas_call(
        functools.partial(_gdn_pre_kernel, tt=tt, t_real=t_real, heads=heads, dk=dk),
        grid=(batch, nt),
        in_specs=[tok(cq),
                  pl.BlockSpec((None, SUBLANES, cq), lambda b, i: (b, 0, 0)),
                  pl.BlockSpec((taps, cq), const),
                  tok(LANES),
                  pl.BlockSpec((SUBLANES, LANES), const)],
        out_specs=[tok(hk), tok(hk), tok(hk), tok(hk), tok(heads * c),
                   pl.BlockSpec((None, nc, SUBLANES, hk), lambda b, i: (b, i, 0, 0))],
        out_shape=[jax.ShapeDtypeStruct((rows, hk), F32)] * 4
                  + [jax.ShapeDtypeStruct((rows, heads * c), F32),
                     jax.ShapeDtypeStruct((batch, nt * nc, SUBLANES, hk), F32)],
        scratch_shapes=[pltpu.VMEM((tt + SUBLANES, cq), F32), pltpu.VMEM((tt, hk), F32),
                        pltpu.VMEM((tt, hk), F32), pltpu.VMEM((tt, hk), F32),
                        pltpu.VMEM((tt, LANES), F32), pltpu.VMEM((tt, LANES), F32)],
        compiler_params=_cparams("parallel", "arbitrary"),
        name="gdn_chunk_prepare",
    )(qkv, st, conv_w, ba, gp)

    nb, stt = scan_batch, scan_tt
    snc = stt // c
    seq3 = lambda a: a.reshape(batch, seq_pad, a.shape[-1])
    blk3 = lambda width: pl.BlockSpec((nb, stt, width), lambda g, i: (g, i, 0))
    state_spec = pl.BlockSpec((nb, heads, dk, dk), lambda g, i: (g, 0, 0, 0))
    o, s_new = pl.pallas_call(
        functools.partial(_gdn_scan_kernel, tt=stt, heads=heads, dk=dk),
        grid=(batch // nb, seq_pad // stt),
        in_specs=[blk3(hk), blk3(hk), blk3(hk), blk3(hk), blk3(heads * c),
                  pl.BlockSpec((nb, snc, SUBLANES, hk), lambda g, i: (g, i, 0, 0)),
                  blk3(hk), state_spec,
                  pl.BlockSpec((1, dk), const)],
        out_specs=[blk3(hk), state_spec],
        out_shape=[jax.ShapeDtypeStruct((batch, seq_pad, hk), F32),
                   jax.ShapeDtypeStruct((batch, heads, dk, dk), F32)],
        scratch_shapes=[pltpu.VMEM((nb, heads, dk, dk), F32)],
        compiler_params=_cparams("parallel", "arbitrary"),
        name="gdn_chunk_scan",
    )(seq3(u), seq3(w), seq3(qd), seq3(kd), seq3(at), eg, seq3(z), s0, norm_w.reshape(1, dk))
    return o.reshape(rows, hk), s_new


def _dwconv_kernel(h_ref, buf_ref, w_ref, b_ref, g_ref, bb_ref, o_ref, xe_ref, xs_ref, *, tt, rb):
    i = pl.program_id(1)
    taps = w_ref.shape[0]
    halo = buf_ref.shape[0]
    first = halo - (taps - 1)
    sub = w_ref.shape[1]

    @pl.when(i == 0)
    def _():
        xe_ref[0:halo, :] = buf_ref[...]

    xe_ref[halo:halo + tt, :] = h_ref[...]
    span = tt + halo - sub
    for ph in range(1, sub):
        xs_ref[ph - 1, 0:span, :] = xe_ref[pl.ds(ph, span), :]

    def window(row, size):
        a, ph = divmod(row, sub)
        if ph == 0:
            return xe_ref[pl.ds(row, size), :]
        return xs_ref[ph - 1, pl.ds(a * sub, size), :]

    def weight(j):
        return w_ref[j] if rb == sub else jnp.tile(w_ref[j], (rb // sub, 1))

    for r0 in range(0, tt, rb):
        acc = b_ref[...] + weight(0) * window(first + r0, rb)
        for j in range(1, taps):
            acc = acc + weight(j) * window(first + r0 + j, rb)
        mu = jnp.mean(acc, axis=-1, keepdims=True)
        xc = acc - mu
        var = jnp.mean(xc * xc, axis=-1, keepdims=True)
        y = xc * lax.rsqrt(var + LN_EPS) * g_ref[...] + bb_ref[...]
        o_ref[pl.ds(r0, rb), :] = _silu(y)
    xe_ref[0:halo, :] = xe_ref[tt:tt + halo, :]


def _dwconv_ln_swish(h, buf, w_dw, b_dw, ln_g, ln_b, batch, seq_pad, tt):
    ch = h.shape[1]
    taps = w_dw.shape[0]
    halo = -(-(taps - 1) // SUBLANES) * SUBLANES
    bufp = jnp.pad(buf, ((0, 0), (halo - (taps - 1), 0), (0, 0)))
    nt = seq_pad // tt
    rb = min(tt, 2 * SUBLANES)
    const = lambda b, i: (0, 0)
    vec = lambda a: a.reshape(1, ch)
    w_rep = jnp.broadcast_to(w_dw[:, None, :], (taps, SUBLANES, ch))
    return pl.pallas_call(
        functools.partial(_dwconv_kernel, tt=tt, rb=rb),
        grid=(batch, nt),
        in_specs=[pl.BlockSpec((tt, ch), lambda b, i: (b * nt + i, 0)),
                  pl.BlockSpec((None, halo, ch), lambda b, i: (b, 0, 0)),
                  pl.BlockSpec((taps, SUBLANES, ch), lambda b, i: (0, 0, 0)),
                  pl.BlockSpec((1, ch), const), pl.BlockSpec((1, ch), const),
                  pl.BlockSpec((1, ch), const)],
        out_specs=pl.BlockSpec((tt, ch), lambda b, i: (b * nt + i, 0)),
        out_shape=jax.ShapeDtypeStruct((batch * seq_pad, ch), F32),
        scratch_shapes=[pltpu.VMEM((tt + halo, ch), F32),
                        pltpu.VMEM((SUBLANES - 1, tt + halo - SUBLANES, ch), F32)],
        compiler_params=_cparams("parallel", "arbitrary"),
        name="dwconv_ln_swish",
    )(h, bufp, w_rep, vec(b_dw), vec(ln_g), vec(ln_b))


def _pad_time(a, batch, t, t_pad):
    a = a.reshape(batch, t, a.shape[-1])
    return jnp.pad(a, ((0, 0), (0, t_pad - t), (0, 0))).reshape(batch * t_pad, a.shape[-1])


def kernel(x_prompt, x_sample, cache_attn_k, cache_attn_v, page_table, state_gdn_conv, state_gdn_s, state_conv_buf, norm_mix, norm_ffn, norm_final, w_in, w_out, gdn_conv_w, gdn_A_log, gdn_dt_bias, gdn_norm_w, lam_q1, lam_k1, lam_q2, lam_k2, diff_subln_w, rel_bias, conv_w_pw1, conv_b_pw1, conv_w_dw, conv_b_dw, conv_ln_g, conv_ln_b, conv_w_pw2, conv_b_pw2, ffn_w_gate, ffn_w_up, ffn_w_down):
    bp, seq, d = x_prompt.shape
    db, dseq, _ = x_sample.shape
    assert dseq == 1
    depth = norm_mix.shape[0]
    ha, dva = cache_attn_k.shape[3], cache_attn_v.shape[4]
    hb, dkb = state_gdn_s.shape[2], state_gdn_s.shape[3]
    c_qkv = state_gdn_conv.shape[3]
    gdn_taps = gdn_conv_w.shape[1]
    conv_taps = conv_w_dw.shape[1]
    d_ff = ffn_w_gate.shape[2]
    mp = bp * seq
    assert 2 * hb <= LANES

    tm, tm_ffn = 256, 512
    t_attn = min(512, seq)
    tt_gdn = min(512, seq)
    tt_scan = min(256, seq)
    tt_conv = min(128, seq)
    sample_pad_gdn, sample_pad_conv = GDN_CHUNK, SUBLANES

    xp = x_prompt.reshape(mp, d)
    xs = x_sample.reshape(db, d)
    row = lambda a: a.reshape(1, -1)

    sizes = (ha * dva, ha * dva, ha * dva, c_qkv, hb * dkb, LANES)
    offs = [0]
    for s_ in sizes:
        offs.append(offs[-1] + s_)
    groups = tuple(zip(offs[:-1], sizes))
    inproj_outs = [(wd, 1) for wd in sizes] + [(dva, ha), (dva, ha)]

    def last_rows(a, n):
        return jnp.stack([a[(b + 1) * seq - n:(b + 1) * seq] for b in range(bp)])

    k_p, v_p, k_s, v_s, gc_p, gc_s, gs_p, gs_s, cb_p, cb_s = ([] for _ in range(10))
    for layer in range(depth):
        if layer % 2 == 0:
            e = layer // 2
            lam_init = 0.8 - 0.6 * math.exp(-0.3 * layer)
            w_in_e = jnp.pad(w_in[e], ((0, 0), (0, offs[-1] - w_in.shape[2]))).astype(BF16)
            outs_p, outs_s = _token_call(
                functools.partial(_inproj_body, groups=groups, heads=ha, kv_groups=(1, 2)),
                [xp], [xs], [_whole(row(norm_mix[layer])), _whole(w_in_e)],
                inproj_outs, tm, "in_projection")
            qa_p, ka_p, va_p, qkv_p, z_p, ba_p, k4_p, v4_p = outs_p
            qa_s, ka_s, va_s, qkv_s, z_s, ba_s, k4_s, v4_s = outs_s
            lam_vecs = (lam_q1[e], lam_k1[e], lam_q2[e], lam_k2[e])

            oa_p = _attention_prompt(qa_p, ka_p, va_p, rel_bias, lam_vecs, diff_subln_w[e],
                                     bp, seq, ha, lam_init, t=t_attn)
            oa_s = _attention_decode(qa_s, ka_s, va_s, cache_attn_k[e], cache_attn_v[e],
                                     page_table, rel_bias, lam_vecs, diff_subln_w[e],
                                     ha, lam_init, pages=16)

            gdn_w = (gdn_conv_w[e], gdn_A_log[e], gdn_dt_bias[e], gdn_norm_w[e])
            ob_p, s_p = _gated_deltanet(
                qkv_p, ba_p, z_p, jnp.zeros((bp, gdn_taps - 1, c_qkv), F32),
                jnp.zeros((bp, hb, dkb, dkb), F32), *gdn_w, bp, seq, seq, hb,
                tt=tt_gdn, scan_batch=bp, scan_tt=tt_scan)
            cpad = sample_pad_gdn
            ob_s, s_s = _gated_deltanet(
                _pad_time(qkv_s, db, 1, cpad), _pad_time(ba_s, db, 1, cpad),
                _pad_time(z_s, db, 1, cpad), state_gdn_conv[e], state_gdn_s[e], *gdn_w,
                db, cpad, 1, hb, tt=cpad, scan_batch=min(db, 8), scan_tt=cpad)
            ob_s = ob_s.reshape(db, cpad, hb * dkb)[:, 0]

            w_o = w_out[e].astype(BF16)
            (xp,), (xs,) = _token_call(
                _proj_res_body, [oa_p, ob_p, xp], [oa_s, ob_s, xs],
                [_whole(w_o[:ha * dva]), _whole(w_o[ha * dva:]), _whole(jnp.zeros((1, d), F32))],
                [(d, 1)], tm, "projection_residual")

            k_p.append(k4_p.reshape(bp, seq, ha, dva))
            v_p.append(v4_p.reshape(bp, seq, ha, dva))
            k_s.append(k4_s.reshape(db, 1, ha, dva))
            v_s.append(v4_s.reshape(db, 1, ha, dva))
            gc_p.append(last_rows(qkv_p, gdn_taps - 1))
            gc_s.append(jnp.concatenate([state_gdn_conv[e], qkv_s.reshape(db, 1, c_qkv)],
                                        axis=1)[:, -(gdn_taps - 1):])
            gs_p.append(s_p)
            gs_s.append(s_s)
        else:
            cidx = layer // 2
            (hg_p,), (hg_s,) = _token_call(
                _glu_body, [xp], [xs],
                [_whole(row(norm_mix[layer])), _whole(conv_w_pw1[cidx].astype(BF16)),
                 _whole(row(conv_b_pw1[cidx]))],
                [(conv_w_pw1.shape[2] // 2, 1)], tm, "glu_projection")
            dconv = hg_p.shape[1]
            conv_w = (conv_w_dw[cidx], conv_b_dw[cidx], conv_ln_g[cidx], conv_ln_b[cidx])
            hc_p = _dwconv_ln_swish(hg_p, jnp.zeros((bp, conv_taps - 1, dconv), F32), *conv_w,
                                    bp, seq, tt=tt_conv)
            spad = sample_pad_conv
            hc_s = _dwconv_ln_swish(_pad_time(hg_s, db, 1, spad), state_conv_buf[cidx], *conv_w,
                                    db, spad, tt=spad)
            hc_s = hc_s.reshape(db, spad, dconv)[:, 0]
            (xp,), (xs,) = _token_call(
                _proj_res_body, [hc_p, xp], [hc_s, xs],
                [_whole(conv_w_pw2[cidx].astype(BF16)), _whole(row(conv_b_pw2[cidx]))],
                [(d, 1)], tm, "projection_residual")
            cb_p.append(last_rows(hg_p, conv_taps - 1))
            cb_s.append(jnp.concatenate([state_conv_buf[cidx], hg_s.reshape(db, 1, dconv)],
                                        axis=1)[:, -(conv_taps - 1):])
        tf = d_ff // 2
        (xp,), (xs,) = _token_call(
            functools.partial(_ffn_body, final_norm=(layer == depth - 1)), [xp], [xs],
            [_whole(row(norm_ffn[layer])),
             (ffn_w_gate[layer].astype(BF16), (d, tf), lambda k: (0, k)),
             (ffn_w_up[layer].astype(BF16), (d, tf), lambda k: (0, k)),
             (ffn_w_down[layer].astype(BF16), (tf, d), lambda k: (k, 0)),
             _whole(row(norm_final))],
            [(d, 1)], tm_ffn, "swiglu_ffn", inner=d_ff // tf,
            scratch=(pltpu.VMEM((tm_ffn, d), BF16), pltpu.VMEM((tm_ffn, d), F32)))

    y_prompt = xp.reshape(bp, seq, d)
    y_sample = xs.reshape(db, 1, d)
    return (y_prompt, y_sample, jnp.stack(k_p), jnp.stack(v_p), jnp.stack(k_s), jnp.stack(v_s),
            jnp.stack(gc_p), jnp.stack(gc_s), jnp.stack(gs_p), jnp.stack(gs_s),
            jnp.stack(cb_p), jnp.stack(cb_s))
```

```python
import functools
import math

import jax
import jax.numpy as jnp
from jax import lax
from jax.experimental import pallas as pl
from jax.experimental.pallas import tpu as pltpu

F32 = jnp.float32
BF16 = jnp.bfloat16

RMS_EPS = 1e-6
LN_EPS = 1e-5
L2_EPS = 1e-6
NUM_BUCKETS = 32
MAX_DISTANCE = 128
GDN_CHUNK = 64
NEG_BIG = -1e30
LOG2E = math.log2(math.e)
LANES = 128
SUBLANES = 8
VMEM_LIMIT = 48 * 1024 * 1024

_HI = lax.Precision.HIGHEST


def _cparams(*sem):
    return pltpu.CompilerParams(dimension_semantics=sem, vmem_limit_bytes=VMEM_LIMIT)


def _mm(a, b):
    return jnp.dot(a.astype(BF16), b.astype(BF16), preferred_element_type=F32)


def _mm_nt(a, b):
    return lax.dot_general(a.astype(BF16), b.astype(BF16), (((1,), (1,)), ((), ())),
                           preferred_element_type=F32)


def _mm_tn(a, b):
    return lax.dot_general(a.astype(BF16), b.astype(BF16), (((0,), (0,)), ((), ())),
                           preferred_element_type=F32)


def _mm_hi(a, b):
    return jnp.dot(a, b, precision=_HI, preferred_element_type=F32)


def _mm_nt_hi(a, b):
    return lax.dot_general(a, b, (((1,), (1,)), ((), ())), precision=_HI,
                           preferred_element_type=F32)


def _rms(x, w, eps):
    return x * lax.rsqrt(jnp.mean(x * x, axis=-1, keepdims=True) + eps) * w


def _sigmoid(x):
    return 1.0 / (1.0 + jnp.exp(-x))


def _silu(x):
    return x * _sigmoid(x)


def _lane_pick(x, lane_idx, k):
    return jnp.sum(jnp.where(lane_idx == k, x, 0.0), axis=-1, keepdims=True)


def _token_call(body, prompt_in, sample_in, shared, outs, tm, name, inner=1, scratch=()):
    mp, ms = prompt_in[0].shape[0], sample_in[0].shape[0]
    assert mp % tm == 0
    n = mp // tm
    n_in, n_sh, n_out = len(prompt_in), len(shared), len(outs)

    def kern(*refs):
        p_in, s_in = refs[:n_in], refs[n_in:2 * n_in]
        sh = refs[2 * n_in:2 * n_in + n_sh]
        o0 = 2 * n_in + n_sh
        p_out, s_out = refs[o0:o0 + n_out], refs[o0 + n_out:o0 + 2 * n_out]
        scr = refs[o0 + 2 * n_out:]
        i = pl.program_id(0)

        @pl.when(i < n)
        def _():
            body(p_in, sh, p_out, scr)

        @pl.when(i == n)
        def _():
            body(s_in, sh, s_out, scr)

    prow = lambda i, k: (jnp.minimum(i, n - 1), 0)
    srow = lambda i, k: (0, 0)
    in_specs = ([pl.BlockSpec((tm, a.shape[1]), prow) for a in prompt_in]
                + [pl.BlockSpec((ms, a.shape[1]), srow) for a in sample_in]
                + [pl.BlockSpec(bs, (lambda i, k, f=f: f(k))) for _, bs, f in shared])
    out_specs = ([pl.BlockSpec((tm * r, w), prow) for w, r in outs]
                 + [pl.BlockSpec((ms * r, w), srow) for w, r in outs])
    out_shape = ([jax.ShapeDtypeStruct((mp * r, w), F32) for w, r in outs]
                 + [jax.ShapeDtypeStruct((ms * r, w), F32) for w, r in outs])
    res = pl.pallas_call(
        kern, grid=(n + 1, inner), in_specs=in_specs, out_specs=out_specs, out_shape=out_shape,
        scratch_shapes=list(scratch), compiler_params=_cparams("arbitrary", "arbitrary"),
        name=name,
    )(*prompt_in, *sample_in, *[a for a, _, _ in shared])
    return res[:n_out], res[n_out:]


def _whole(a):
    return (a, a.shape, lambda k, nd=a.ndim: (0,) * nd)


def _inproj_body(ins, sh, outs, scr, *, groups, heads, kv_groups):
    x_ref, = ins
    nw_ref, w_ref = sh
    xn = _rms(x_ref[...], nw_ref[...], RMS_EPS).astype(BF16)
    rows = x_ref.shape[0]
    n_g = len(groups)
    for g, (o_ref, (off, width)) in enumerate(zip(outs[:n_g], groups)):
        y = jnp.dot(xn, w_ref[:, off:off + width], preferred_element_type=F32)
        o_ref[...] = y
        if g in kv_groups:
            o4_ref = outs[n_g + kv_groups.index(g)]
            dv = width // heads
            for h in range(heads):
                o4_ref[pl.ds(h, rows, stride=heads), :] = y[:, h * dv:(h + 1) * dv]


def _glu_body(ins, sh, outs, scr):
    x_ref, = ins
    nw_ref, w_ref, b_ref = sh
    o_ref, = outs
    dc = o_ref.shape[1]
    xn = _rms(x_ref[...], nw_ref[...], RMS_EPS).astype(BF16)
    a = jnp.dot(xn, w_ref[:, :dc], preferred_element_type=F32) + b_ref[:, :dc]
    g = jnp.dot(xn, w_ref[:, dc:], preferred_element_type=F32) + b_ref[:, dc:]
    o_ref[...] = a * _sigmoid(g)


def _proj_res_body(ins, sh, outs, scr):
    o_ref, = outs
    acc = ins[-1][...] + sh[-1][...]
    for a_ref, w_ref in zip(ins[:-1], sh[:-1]):
        acc = acc + _mm(a_ref[...], w_ref[...])
    o_ref[...] = acc


def _ffn_body(ins, sh, outs, scr, *, final_norm):
    x_ref, = ins
    nw_ref, wg_ref, wu_ref, wd_ref, fw_ref = sh
    o_ref, = outs
    xn_ref, acc_ref = scr
    rows = x_ref.shape[0]
    k = pl.program_id(1)

    @pl.when(k == 0)
    def _():
        xn_ref[0:rows, :] = _rms(x_ref[...], nw_ref[...], RMS_EPS).astype(BF16)
        acc_ref[0:rows, :] = x_ref[...]

    xn = xn_ref[0:rows, :]
    g = jnp.dot(xn, wg_ref[...], preferred_element_type=F32)
    u = jnp.dot(xn, wu_ref[...], preferred_element_type=F32)
    acc_ref[0:rows, :] += _mm(_silu(g) * u, wd_ref[...])

    @pl.when(k == pl.num_programs(1) - 1)
    def _():
        y = acc_ref[0:rows, :]
        if final_norm:
            y = _rms(y, fw_ref[...], RMS_EPS)
        o_ref[...] = y


def _t5_bucket(n):
    max_exact = NUM_BUCKETS // 2
    nf = jnp.maximum(n, 1).astype(F32)
    large = max_exact + (jnp.log(nf / max_exact) / math.log(MAX_DISTANCE / max_exact)
                         * (NUM_BUCKETS - max_exact)).astype(jnp.int32)
    large = jnp.minimum(large, NUM_BUCKETS - 1)
    return jnp.where(n < max_exact, n, large)


def _lambda(lq1_ref, lk1_ref, lq2_ref, lk2_ref, lam_init):
    s1 = jnp.sum(lq1_ref[...] * lk1_ref[...], axis=-1, keepdims=True)
    s2 = jnp.sum(lq2_ref[...] * lk2_ref[...], axis=-1, keepdims=True)
    return jnp.exp(s1) - jnp.exp(s2) + lam_init


def _attn_prompt_kernel(q_ref, k_ref, v_ref, bias_ref, lq1_ref, lk1_ref, lq2_ref, lk2_ref,
                        sw_ref, o_ref, qt_ref, kb_ref, vt_ref, m_ref, acc_ref, sa_ref, sb_ref, *,
                        t, da, lam_init):
    i = pl.program_id(2)
    nblk = kb_ref.shape[0]
    dv = v_ref.shape[1]

    @pl.when(i == 0)
    def _():
        for c in range(nblk):
            kb_ref[c] = k_ref[c * t:(c + 1) * t, :].astype(BF16)
            vt_ref[c, 0:dv, :] = v_ref[c * t:(c + 1) * t, :].T.astype(BF16)
            vt_ref[c, dv:, :] = jnp.ones((vt_ref.shape[1] - dv, t), BF16)

    qt = (q_ref[...] * (da ** -0.5 * LOG2E)).T
    first_map = lax.broadcasted_iota(jnp.int32, qt.shape, 0) < da
    qt_ref[:, 0:t] = jnp.where(first_map, qt, 0.0).astype(BF16)
    qt_ref[:, t:2 * t] = jnp.where(first_map, 0.0, qt).astype(BF16)
    m_ref[...] = jnp.full(m_ref.shape, NEG_BIG, F32)
    acc_ref[...] = jnp.zeros(acc_ref.shape, F32)

    def scores(j, buf_ref):
        buf_ref[...] = jnp.dot(kb_ref[j], qt_ref[...], preferred_element_type=F32)

    def update(j, buf_ref, bias):
        s = buf_ref[...]
        if bias is not None:
            s = s + jnp.concatenate([bias, bias], axis=1)
        m_prev = m_ref[...]
        m_new = jnp.maximum(m_prev, jnp.max(s, axis=0, keepdims=True))
        p = jnp.exp2(s - m_new).astype(BF16)
        acc_ref[...] = (acc_ref[...] * jnp.exp2(m_prev - m_new)
                        + jnp.dot(vt_ref[j], p, preferred_element_type=F32))
        m_ref[...] = m_new

    prev_bias, diag_bias = bias_ref.at[0], bias_ref.at[1]
    scores(0, sa_ref)

    def far_pair(jj, carry):
        j = 2 * jj
        scores(j + 1, sb_ref)
        update(j, sa_ref, None)
        scores(j + 2, sa_ref)
        update(j + 1, sb_ref, None)
        return carry

    n_far = i - 1
    lax.fori_loop(0, n_far // 2, far_pair, 0)

    @pl.when(i % 2 == 1)
    def _():
        scores(i, sb_ref)
        update(i - 1, sa_ref, prev_bias[...])
        update(i, sb_ref, diag_bias[...])

    @pl.when((i % 2 == 0) & (i >= 2))
    def _():
        scores(i - 1, sb_ref)
        update(i - 2, sa_ref, None)
        scores(i, sa_ref)
        update(i - 1, sb_ref, prev_bias[...])
        update(i, sa_ref, diag_bias[...])

    @pl.when(i == 0)
    def _():
        update(0, sa_ref, diag_bias[...])

    acc = acc_ref[...]
    o12 = acc[0:dv] / acc[dv:dv + 1]
    lam = _lambda(lq1_ref, lk1_ref, lq2_ref, lk2_ref, lam_init)
    o = (o12[:, 0:t] - lam * o12[:, t:2 * t]).T
    o_ref[...] = _rms(o, sw_ref[...], LN_EPS) * (1.0 - lam_init)


def _toeplitz(r, t):
    h, period = r.shape
    flat = jnp.tile(r, (1, t))[:, :t * (period - 1)]
    return flat.reshape(h, t, period - 1)[:, :, :t]


def _prompt_bias_tiles(rel_bias, t):
    assert t >= MAX_DISTANCE
    far = rel_bias[NUM_BUCKETS - 1]
    b1 = jnp.transpose(rel_bias[_t5_bucket(jnp.arange(2 * t))] - far).astype(F32) * LOG2E
    r_diag = jnp.concatenate([b1[:, :t], jnp.full_like(b1[:, :t], NEG_BIG)], axis=1)
    r_prev = jnp.concatenate([b1[:, t:], b1[:, :t]], axis=1)
    return jnp.stack([_toeplitz(r_prev, t), _toeplitz(r_diag, t)], axis=1)


def _attention_prompt(q, k, v, rel_bias, lam_vecs, subln_w, batch, seq, heads, lam_init, t):
    dv = q.shape[1] // heads
    da = dv // 2
    nq = seq // t
    bias = _prompt_bias_tiles(rel_bias, t)
    vec = lambda a: a.reshape(1, -1)
    const = lambda b, h, i: (0, 0)
    return pl.pallas_call(
        functools.partial(_attn_prompt_kernel, t=t, da=da, lam_init=lam_init),
        grid=(batch, heads, nq),
        in_specs=[pl.BlockSpec((t, dv), lambda b, h, i: (b * nq + i, h)),
                  pl.BlockSpec((seq, dv), lambda b, h, i: (b, h)),
                  pl.BlockSpec((seq, dv), lambda b, h, i: (b, h)),
                  pl.BlockSpec((None, 2, t, t), lambda b, h, i: (h, 0, 0, 0)),
                  pl.BlockSpec((1, da), const), pl.BlockSpec((1, da), const),
                  pl.BlockSpec((1, da), const), pl.BlockSpec((1, da), const),
                  pl.BlockSpec((1, dv), const)],
        out_specs=pl.BlockSpec((t, dv), lambda b, h, i: (b * nq + i, h)),
        out_shape=jax.ShapeDtypeStruct((batch * seq, heads * dv), F32),
        scratch_shapes=[pltpu.VMEM((dv, 2 * t), BF16), pltpu.VMEM((nq, t, dv), BF16),
                        pltpu.VMEM((nq, dv + 2 * SUBLANES, t), BF16),
                        pltpu.VMEM((1, 2 * t), F32),
                        pltpu.VMEM((dv + 2 * SUBLANES, 2 * t), F32),
                        pltpu.VMEM((t, 2 * t), F32), pltpu.VMEM((t, 2 * t), F32)],
        compiler_params=_cparams("parallel", "parallel", "arbitrary"),
        name="diff_attention_prompt",
    )(q, k, v, bias, *[vec(a) for a in lam_vecs], vec(subln_w))


def _attn_decode_kernel(pt_ref, q_ref, kn_ref, vn_ref, bfar_ref, blast_ref, bnew_ref,
                        lq1_ref, lk1_ref, lq2_ref, lk2_ref, sw_ref, *rest,
                        pages, heads, da, lam_init):
    del pt_ref
    k_refs = rest[:pages]
    v_refs = rest[pages:2 * pages]
    o_ref, m_ref, l_ref, acc_ref = rest[2 * pages:]
    j = pl.program_id(1)
    last = pl.num_programs(1) - 1

    @pl.when(j == 0)
    def _():
        m_ref[...] = jnp.full(m_ref.shape, NEG_BIG, F32)
        l_ref[...] = jnp.zeros(l_ref.shape, F32)
        acc_ref[...] = jnp.zeros(acc_ref.shape, F32)

    q = q_ref[...] * (da ** -0.5)
    row = lax.broadcasted_iota(jnp.int32, q.shape, 0)
    lane = lax.broadcasted_iota(jnp.int32, q.shape, 1)
    qs = jnp.where((row < heads) == (lane < da), q, 0.0)
    qs_bf = qs.astype(BF16)

    s = jnp.concatenate([_mm_nt(qs_bf, k_ref[...]) for k_ref in k_refs], axis=1)
    s = s + jnp.where(j == last, blast_ref[...], bfar_ref[...])
    m_prev = m_ref[...]
    m_new = jnp.maximum(m_prev, jnp.max(s, axis=-1, keepdims=True))
    p = jnp.exp(s - m_new)
    alpha = jnp.exp(m_prev - m_new)
    l_ref[...] = alpha * l_ref[...] + jnp.sum(p, axis=-1, keepdims=True)
    rows_per_page = k_refs[0].shape[0]
    pv = acc_ref[...] * alpha
    for idx, v_ref in enumerate(v_refs):
        pv = pv + _mm(p[:, idx * rows_per_page:(idx + 1) * rows_per_page], v_ref[...])
    acc_ref[...] = pv
    m_ref[...] = m_new

    @pl.when(j == last)
    def _():
        s_new = jnp.sum(qs * kn_ref[...], axis=-1, keepdims=True) + bnew_ref[:, 0:1]
        m_prev = m_ref[...]
        m_fin = jnp.maximum(m_prev, s_new)
        p_new = jnp.exp(s_new - m_fin)
        alpha = jnp.exp(m_prev - m_fin)
        l_fin = alpha * l_ref[...] + p_new
        acc = alpha * acc_ref[...] + p_new * vn_ref[...]
        o12 = acc / l_fin
        lam = _lambda(lq1_ref, lk1_ref, lq2_ref, lk2_ref, lam_init)
        o = o12[0:heads] - lam * o12[heads:2 * heads]
        o_ref[...] = _rms(o, sw_ref[...], LN_EPS) * (1.0 - lam_init)


def _attention_decode(q, k_new, v_new, k_pool, v_pool, page_table, rel_bias, lam_vecs,
                      subln_w, heads, lam_init, pages):
    db = q.shape[0]
    n_pool, page, _, dv = k_pool.shape
    da = dv // 2
    n_pages = page_table.shape[1]
    past = n_pages * page
    rpp = page * heads
    span = pages * page
    assert n_pages % pages == 0 and span >= MAX_DISTANCE
    kp = k_pool.reshape(n_pool, rpp, dv)
    vp = v_pool.reshape(n_pool, rpp, dv)

    def two_maps(a):
        a = a.reshape(db, 1, heads, dv)
        return jnp.broadcast_to(a, (db, 2, heads, dv)).reshape(db, 2 * heads, dv)

    same = jnp.eye(heads, dtype=bool)

    def pair_bias(b):
        b = jnp.where(same[:, None, :], jnp.transpose(b)[:, :, None], NEG_BIG)
        return jnp.tile(b.reshape(heads, span * heads), (2, 1))

    rb = rel_bias.astype(F32)
    b_far = pair_bias(jnp.broadcast_to(rb[NUM_BUCKETS - 1], (span, heads)))
    b_last = pair_bias(rb[_t5_bucket(span - jnp.arange(span))])
    b_new = jnp.tile(jnp.broadcast_to(rb[0][:, None], (heads, LANES)), (2, 1))

    vec = lambda a: a.reshape(1, -1)
    const = lambda s, j, pt: (0, 0)

    def page_spec(idx):
        return pl.BlockSpec((None, rpp, dv), lambda s, j, pt: (pt[s, j * pages + idx], 0, 0))

    grid_spec = pltpu.PrefetchScalarGridSpec(
        num_scalar_prefetch=1,
        grid=(db, n_pages // pages),
        in_specs=([pl.BlockSpec((None, 2 * heads, dv), lambda s, j, pt: (s, 0, 0))] * 3
                  + [pl.BlockSpec((2 * heads, span * heads), const),
                     pl.BlockSpec((2 * heads, span * heads), const),
                     pl.BlockSpec((2 * heads, LANES), const),
                     pl.BlockSpec((1, da), const), pl.BlockSpec((1, da), const),
                     pl.BlockSpec((1, da), const), pl.BlockSpec((1, da), const),
                     pl.BlockSpec((1, dv), const)]
                  + [page_spec(idx) for idx in range(pages)] * 2),
        out_specs=pl.BlockSpec((None, heads, dv), lambda s, j, pt: (s, 0, 0)),
        scratch_shapes=[pltpu.VMEM((2 * heads, 1), F32), pltpu.VMEM((2 * heads, 1), F32),
                        pltpu.VMEM((2 * heads, dv), F32)],
    )
    out = pl.pallas_call(
        functools.partial(_attn_decode_kernel, pages=pages, heads=heads, da=da,
                          lam_init=lam_init),
        grid_spec=grid_spec,
        out_shape=jax.ShapeDtypeStruct((db, heads, dv), F32),
        compiler_params=_cparams("parallel", "arbitrary"),
        name="diff_attention_decode",
    )(page_table, two_maps(q), two_maps(k_new), two_maps(v_new), b_far, b_last, b_new,
      *[vec(a) for a in lam_vecs], vec(subln_w), *([kp] * pages), *([vp] * pages))
    return out.reshape(db, heads * dv)


def _unit_lower_inverse(lmats, eye, levels):
    if levels == 0:
        return [eye for _ in lmats]
    xs = [-m for m in lmats]
    tinvs = [eye + x for x in xs]
    if levels == 1:
        return tinvs
    n = eye.shape[0]
    rs = [_mm(x, x) for x in xs]
    for k in range(1, levels):
        if k < levels - 1:
            both = [_mm(jnp.concatenate([r, t], axis=0), r) for r, t in zip(rs, tinvs)]
            rs = [b[:n] for b in both]
            tinvs = [t + b[n:] for t, b in zip(tinvs, both)]
        else:
            tinvs = [t + _mm(t, r) for t, r in zip(tinvs, rs)]
    return tinvs


def _gdn_pre_kernel(x_ref, st_ref, cw_ref, ba_ref, gp_ref,
                    u_ref, w_ref, qd_ref, kd_ref, at_ref, eg_ref,
                    xe_ref, qn_ref, kn_ref, vv_ref, gb_ref, gc_ref, *, tt, t_real, heads, dk):
    c = GDN_CHUNK
    i = pl.program_id(1)
    taps = cw_ref.shape[0]
    halo = SUBLANES
    hk = heads * dk

    @pl.when(i == 0)
    def _():
        xe_ref[0:halo, :] = st_ref[...]

    xe_ref[halo:halo + tt, :] = x_ref[...]
    acc = cw_ref[taps - 1:taps, :] * x_ref[...]
    for j in range(taps - 1):
        acc = acc + cw_ref[j:j + 1, :] * xe_ref[pl.ds(halo - (taps - 1) + j, tt), :]
    xe_ref[0:halo, :] = xe_ref[tt:tt + halo, :]
    hcv = _silu(acc)

    valid = (i * tt + lax.broadcasted_iota(jnp.int32, (tt, 1), 0)) < t_real
    for h in range(heads):
        sl = slice(h * dk, (h + 1) * dk)
        qh = hcv[:, h * dk:(h + 1) * dk]
        kh = hcv[:, hk + h * dk:hk + (h + 1) * dk]
        qn_ref[:, sl] = qh * lax.rsqrt(jnp.sum(qh * qh, axis=-1, keepdims=True) + L2_EPS)
        kn = kh * lax.rsqrt(jnp.sum(kh * kh, axis=-1, keepdims=True) + L2_EPS)
        kn_ref[:, sl] = jnp.where(valid, kn, 0.0)
    vv_ref[...] = jnp.where(valid, hcv[:, 2 * hk:], 0.0)

    ba = ba_ref[...]
    lane = lax.broadcasted_iota(jnp.int32, ba.shape, 1)
    xa = ba + gp_ref[1:2, :]
    softplus = jnp.maximum(xa, 0.0) + jnp.log1p(jnp.exp(-jnp.abs(xa)))
    gates = jnp.where(lane < heads, _sigmoid(ba), -jnp.exp(gp_ref[0:1, :]) * softplus)
    gates = jnp.where(valid & (lane < 2 * heads), gates, 0.0)
    gb_ref[...] = gates

    tri_l = (lax.broadcasted_iota(jnp.int32, (c, c), 0)
             >= lax.broadcasted_iota(jnp.int32, (c, c), 1)).astype(F32)
    for ch in range(tt // c):
        gc_ref[ch * c:(ch + 1) * c, :] = _mm_hi(tri_l, gates[ch * c:(ch + 1) * c, :])

    ri = lax.broadcasted_iota(jnp.int32, (c, c), 0)
    ci = lax.broadcasted_iota(jnp.int32, (c, c), 1)
    incl = ri >= ci
    strict = ri > ci
    eye = (ri == ci).astype(F32)
    pick = (lax.broadcasted_iota(jnp.int32, (SUBLANES, LANES), 0)
            == lax.broadcasted_iota(jnp.int32, (SUBLANES, LANES), 1)).astype(F32)
    lane_c = lax.broadcasted_iota(jnp.int32, (c, LANES), 1)
    scale = dk ** -0.5
    c_valid = c if t_real >= c else t_real
    levels = (c_valid - 1).bit_length()

    n_chunks = tt // c
    group = 2 if n_chunks % 2 == 0 else 1

    def group_body(gi, carry):
        probs = []
        for cc in range(group):
            ch = gi * group + cc
            rows = pl.ds(pl.multiple_of(ch * c, c), c)
            gb = gb_ref[rows, :]
            gcc = gc_ref[rows, :]
            gc_rows = _mm_nt_hi(pick, gcc)
            for h in range(heads):
                probs.append(dict(
                    ch=ch, rows=rows, h=h, sl=slice(h * dk, (h + 1) * dk),
                    beta=_lane_pick(gb, lane_c, h),
                    gc_col=_lane_pick(gcc, lane_c, heads + h),
                    gc_row=gc_rows[heads + h:heads + h + 1, :]))
        for p in probs:
            p["decay"] = jnp.where(
                incl, jnp.exp(jnp.where(incl, p["gc_col"] - p["gc_row"], 0.0)), 0.0)
            p["k"] = kn_ref[p["rows"], p["sl"]]
            p["k_beta"] = p["k"] * p["beta"]
        kk = [_mm_nt(p["k_beta"], p["k"]) for p in probs]
        lmats = [jnp.where(strict, m * p["decay"], 0.0) for m, p in zip(kk, probs)]
        tinvs = _unit_lower_inverse(lmats, eye, levels)
        for p in probs:
            p["egc"] = jnp.exp(p["gc_col"])
            p["q"] = qn_ref[p["rows"], p["sl"]] * scale
        us = [_mm(t, vv_ref[p["rows"], p["sl"]] * p["beta"]) for t, p in zip(tinvs, probs)]
        ws = [_mm(t, p["k_beta"] * p["egc"]) for t, p in zip(tinvs, probs)]
        ats = [_mm_nt(p["q"], p["k"]) for p in probs]
        for p, u, w, at in zip(probs, us, ws, ats):
            rows, sl, h = p["rows"], p["sl"], p["h"]
            g_last = p["gc_col"][c - 1:c, :]
            u_ref[rows, sl] = u
            w_ref[rows, sl] = w
            at_ref[rows, h * c:(h + 1) * c] = at * p["decay"]
            kd_ref[rows, sl] = p["k"] * jnp.exp(g_last - p["gc_col"])
            qd_ref[rows, sl] = p["q"] * p["egc"]
            eg_ref[p["ch"], :, sl] = jnp.broadcast_to(jnp.exp(g_last), (SUBLANES, dk))
        return carry

    lax.fori_loop(0, n_chunks // group, group_body, 0)


def _gdn_scan_kernel(u_ref, w_ref, qd_ref, kd_ref, at_ref, eg_ref, z_ref, s0_ref, nw_ref,
                     o_ref, sout_ref, s_ref, kt_ref, *, tt, heads, dk):
    c = GDN_CHUNK
    i = pl.program_id(1)
    nb = u_ref.shape[0]

    @pl.when(i == 0)
    def _():
        s_ref[...] = s0_ref[...]

    for b in range(nb):
        for ch in range(tt // c):
            for h in range(heads):
                kt_ref[b, ch * heads + h] = (
                    kd_ref[b, ch * c:(ch + 1) * c, h * dk:(h + 1) * dk].T.astype(BF16))

    chains = [(b, h) for b in range(nb) for h in range(heads)]
    cols = lambda h: slice(h * dk, (h + 1) * dk)

    def chunk_body(ch, carry):
        rows = pl.ds(pl.multiple_of(ch * c, c), c)
        s_bf = [s_ref[b, h].astype(BF16) for b, h in chains]
        w_s = [_mm(w_ref[b, rows, cols(h)], s) for (b, h), s in zip(chains, s_bf)]
        q_s = [_mm(qd_ref[b, rows, cols(h)], s) for (b, h), s in zip(chains, s_bf)]
        v_new = [(u_ref[b, rows, cols(h)] - ws).astype(BF16) for (b, h), ws in zip(chains, w_s)]
        a_v = [_mm(at_ref[b, rows, h * c:(h + 1) * c], v) for (b, h), v in zip(chains, v_new)]
        k_v = [_mm(kt_ref[b, ch * heads + h], v) for (b, h), v in zip(chains, v_new)]
        for (b, h), qs, av, kv in zip(chains, q_s, a_v, k_v):
            s_ref[b, h] = s_ref[b, h] * eg_ref[b, ch, 0:1, cols(h)] + kv
            o_ref[b, rows, cols(h)] = (_rms(qs + av, nw_ref[...], RMS_EPS)
                                       * _silu(z_ref[b, rows, cols(h)]))
        return carry

    lax.fori_loop(0, tt // c, chunk_body, 0)

    @pl.when(i == pl.num_programs(1) - 1)
    def _():
        sout_ref[...] = s_ref[...]


def _gated_deltanet(qkv, ba, z, conv_state, s0, conv_w, a_log, dt_bias, norm_w,
                    batch, seq_pad, t_real, heads, tt, scan_batch, scan_tt):
    cq = qkv.shape[1]
    hk = cq // 3
    dk = hk // heads
    c = GDN_CHUNK
    assert c & (c - 1) == 0 and batch % scan_batch == 0
    nt = seq_pad // tt
    nc = tt // c
    taps = conv_w.shape[0]
    st = jnp.pad(conv_state, ((0, 0), (SUBLANES - (taps - 1), 0), (0, 0)))
    gp = jnp.zeros((SUBLANES, LANES), F32)
    gp = gp.at[0, heads:2 * heads].set(a_log.astype(F32))
    gp = gp.at[1, heads:2 * heads].set(dt_bias.astype(F32))
    row_blk = lambda b, i: (b * nt + i, 0)
    const = lambda b, i: (0, 0)
    tok = lambda width: pl.BlockSpec((tt, width), row_blk)
    rows = batch * seq_pad
    u, w, qd, kd, at, eg = pl.pallas_call(
        functools.partial(_gdn_pre_kernel, tt=tt, t_real=t_real, heads=heads, dk=dk),
        grid=(batch, nt),
        in_specs=[tok(cq),
                  pl.BlockSpec((None, SUBLANES, cq), lambda b, i: (b, 0, 0)),
                  pl.BlockSpec((taps, cq), const),
                  tok(LANES),
                  pl.BlockSpec((SUBLANES, LANES), const)],
        out_specs=[tok(hk), tok(hk), tok(hk), tok(hk), tok(heads * c),
                   pl.BlockSpec((None, nc, SUBLANES, hk), lambda b, i: (b, i, 0, 0))],
        out_shape=[jax.ShapeDtypeStruct((rows, hk), F32)] * 4
                  + [jax.ShapeDtypeStruct((rows, heads * c), F32),
                     jax.ShapeDtypeStruct((batch, nt * nc, SUBLANES, hk), F32)],
        scratch_shapes=[pltpu.VMEM((tt + SUBLANES, cq), F32), pltpu.VMEM((tt, hk), F32),
                        pltpu.VMEM((tt, hk), F32), pltpu.VMEM((tt, hk), F32),
                        pltpu.VMEM((tt, LANES), F32), pltpu.VMEM((tt, LANES), F32)],
        compiler_params=_cparams("parallel", "arbitrary"),
        name="gdn_chunk_prepare",
    )(qkv, st, conv_w, ba, gp)

    nb, stt = scan_batch, scan_tt
    snc = stt // c
    seq3 = lambda a: a.reshape(batch, seq_pad, a.shape[-1])
    blk3 = lambda width: pl.BlockSpec((nb, stt, width), lambda g, i: (g, i, 0))
    state_spec = pl.BlockSpec((nb, heads, dk, dk), lambda g, i: (g, 0, 0, 0))
    o, s_new = pl.pallas_call(
        functools.partial(_gdn_scan_kernel, tt=stt, heads=heads, dk=dk),
        grid=(batch // nb, seq_pad // stt),
        in_specs=[blk3(hk), blk3(hk), blk3(hk), blk3(hk), blk3(heads * c),
                  pl.BlockSpec((nb, snc, SUBLANES, hk), lambda g, i: (g, i, 0, 0)),
                  blk3(hk), state_spec,
                  pl.BlockSpec((1, dk), const)],
        out_specs=[blk3(hk), state_spec],
        out_shape=[jax.ShapeDtypeStruct((batch, seq_pad, hk), F32),
                   jax.ShapeDtypeStruct((batch, heads, dk, dk), F32)],
        scratch_shapes=[pltpu.VMEM((nb, heads, dk, dk), F32),
                        pltpu.VMEM((nb, snc * heads, dk, c), BF16)],
        compiler_params=_cparams("parallel", "arbitrary"),
        name="gdn_chunk_scan",
    )(seq3(u), seq3(w), seq3(qd), seq3(kd), seq3(at), eg, seq3(z), s0, norm_w.reshape(1, dk))
    return o.reshape(rows, hk), s_new


def _dwconv_kernel(h_ref, buf_ref, w_ref, b_ref, g_ref, bb_ref, o_ref, xe_ref, xs_ref, *, tt, rb):
    i = pl.program_id(1)
    taps = w_ref.shape[0]
    halo = buf_ref.shape[0]
    first = halo - (taps - 1)
    sub = w_ref.shape[1]

    @pl.when(i == 0)
    def _():
        xe_ref[0:halo, :] = buf_ref[...]

    xe_ref[halo:halo + tt, :] = h_ref[...]
    span = tt + halo - sub
    for ph in range(1, sub):
        xs_ref[ph - 1, 0:span, :] = xe_ref[pl.ds(ph, span), :]

    def window(row, size):
        a, ph = divmod(row, sub)
        if ph == 0:
            return xe_ref[pl.ds(row, size), :]
        return xs_ref[ph - 1, pl.ds(a * sub, size), :]

    def weight(j):
        return w_ref[j] if rb == sub else jnp.tile(w_ref[j], (rb // sub, 1))

    for r0 in range(0, tt, rb):
        acc = b_ref[...] + weight(0) * window(first + r0, rb)
        for j in range(1, taps):
            acc = acc + weight(j) * window(first + r0 + j, rb)
        mu = jnp.mean(acc, axis=-1, keepdims=True)
        xc = acc - mu
        var = jnp.mean(xc * xc, axis=-1, keepdims=True)
        y = xc * lax.rsqrt(var + LN_EPS) * g_ref[...] + bb_ref[...]
        o_ref[pl.ds(r0, rb), :] = _silu(y)
    xe_ref[0:halo, :] = xe_ref[tt:tt + halo, :]


def _dwconv_ln_swish(h, buf, w_dw, b_dw, ln_g, ln_b, batch, seq_pad, tt):
    ch = h.shape[1]
    taps = w_dw.shape[0]
    halo = -(-(taps - 1) // SUBLANES) * SUBLANES
    bufp = jnp.pad(buf, ((0, 0), (halo - (taps - 1), 0), (0, 0)))
    nt = seq_pad // tt
    rb = min(tt, 2 * SUBLANES)
    const = lambda b, i: (0, 0)
    vec = lambda a: a.reshape(1, ch)
    w_rep = jnp.broadcast_to(w_dw[:, None, :], (taps, SUBLANES, ch))
    return pl.pallas_call(
        functools.partial(_dwconv_kernel, tt=tt, rb=rb),
        grid=(batch, nt),
        in_specs=[pl.BlockSpec((tt, ch), lambda b, i: (b * nt + i, 0)),
                  pl.BlockSpec((None, halo, ch), lambda b, i: (b, 0, 0)),
                  pl.BlockSpec((taps, SUBLANES, ch), lambda b, i: (0, 0, 0)),
                  pl.BlockSpec((1, ch), const), pl.BlockSpec((1, ch), const),
                  pl.BlockSpec((1, ch), const)],
        out_specs=pl.BlockSpec((tt, ch), lambda b, i: (b * nt + i, 0)),
        out_shape=jax.ShapeDtypeStruct((batch * seq_pad, ch), F32),
        scratch_shapes=[pltpu.VMEM((tt + halo, ch), F32),
                        pltpu.VMEM((SUBLANES - 1, tt + halo - SUBLANES, ch), F32)],
        compiler_params=_cparams("parallel", "arbitrary"),
        name="dwconv_ln_swish",
    )(h, bufp, w_rep, vec(b_dw), vec(ln_g), vec(ln_b))


def _pad_time(a, batch, t, t_pad):
    a = a.reshape(batch, t, a.shape[-1])
    return jnp.pad(a, ((0, 0), (0, t_pad - t), (0, 0))).reshape(batch * t_pad, a.shape[-1])


def kernel(x_prompt, x_sample, cache_attn_k, cache_attn_v, page_table, state_gdn_conv, state_gdn_s, state_conv_buf, norm_mix, norm_ffn, norm_final, w_in, w_out, gdn_conv_w, gdn_A_log, gdn_dt_bias, gdn_norm_w, lam_q1, lam_k1, lam_q2, lam_k2, diff_subln_w, rel_bias, conv_w_pw1, conv_b_pw1, conv_w_dw, conv_b_dw, conv_ln_g, conv_ln_b, conv_w_pw2, conv_b_pw2, ffn_w_gate, ffn_w_up, ffn_w_down):
    bp, seq, d = x_prompt.shape
    db, dseq, _ = x_sample.shape
    assert dseq == 1
    depth = norm_mix.shape[0]
    ha, dva = cache_attn_k.shape[3], cache_attn_v.shape[4]
    hb, dkb = state_gdn_s.shape[2], state_gdn_s.shape[3]
    c_qkv = state_gdn_conv.shape[3]
    gdn_taps = gdn_conv_w.shape[1]
    conv_taps = conv_w_dw.shape[1]
    d_ff = ffn_w_gate.shape[2]
    mp = bp * seq
    assert 2 * hb <= LANES

    tm, tm_ffn = 256, 512
    t_attn = min(512, seq)
    tt_gdn = min(512, seq)
    tt_scan = min(256, seq)
    tt_conv = min(256, seq)
    sample_pad_gdn, sample_pad_conv = GDN_CHUNK, SUBLANES

    xp = x_prompt.reshape(mp, d)
    xs = x_sample.reshape(db, d)
    row = lambda a: a.reshape(1, -1)

    sizes = (ha * dva, ha * dva, ha * dva, c_qkv, hb * dkb, LANES)
    offs = [0]
    for s_ in sizes:
        offs.append(offs[-1] + s_)
    groups = tuple(zip(offs[:-1], sizes))
    inproj_outs = [(wd, 1) for wd in sizes] + [(dva, ha), (dva, ha)]

    def last_rows(a, n):
        return jnp.stack([a[(b + 1) * seq - n:(b + 1) * seq] for b in range(bp)])

    k_p, v_p, k_s, v_s, gc_p, gc_s, gs_p, gs_s, cb_p, cb_s = ([] for _ in range(10))
    for layer in range(depth):
        if layer % 2 == 0:
            e = layer // 2
            lam_init = 0.8 - 0.6 * math.exp(-0.3 * layer)
            w_in_e = jnp.pad(w_in[e], ((0, 0), (0, offs[-1] - w_in.shape[2]))).astype(BF16)
            outs_p, outs_s = _token_call(
                functools.partial(_inproj_body, groups=groups, heads=ha, kv_groups=(1, 2)),
                [xp], [xs], [_whole(row(norm_mix[layer])), _whole(w_in_e)],
                inproj_outs, tm, "in_projection")
            qa_p, ka_p, va_p, qkv_p, z_p, ba_p, k4_p, v4_p = outs_p
            qa_s, ka_s, va_s, qkv_s, z_s, ba_s, k4_s, v4_s = outs_s
            lam_vecs = (lam_q1[e], lam_k1[e], lam_q2[e], lam_k2[e])

            oa_p = _attention_prompt(qa_p, ka_p, va_p, rel_bias, lam_vecs, diff_subln_w[e],
                                     bp, seq, ha, lam_init, t=t_attn)
            oa_s = _attention_decode(qa_s, ka_s, va_s, cache_attn_k[e], cache_attn_v[e],
                                     page_table, rel_bias, lam_vecs, diff_subln_w[e],
                                     ha, lam_init, pages=16)

            gdn_w = (gdn_conv_w[e], gdn_A_log[e], gdn_dt_bias[e], gdn_norm_w[e])
            ob_p, s_p = _gated_deltanet(
                qkv_p, ba_p, z_p, jnp.zeros((bp, gdn_taps - 1, c_qkv), F32),
                jnp.zeros((bp, hb, dkb, dkb), F32), *gdn_w, bp, seq, seq, hb,
                tt=tt_gdn, scan_batch=bp, scan_tt=tt_scan)
            cpad = sample_pad_gdn
            ob_s, s_s = _gated_deltanet(
                _pad_time(qkv_s, db, 1, cpad), _pad_time(ba_s, db, 1, cpad),
                _pad_time(z_s, db, 1, cpad), state_gdn_conv[e], state_gdn_s[e], *gdn_w,
                db, cpad, 1, hb, tt=cpad, scan_batch=min(db, 8), scan_tt=cpad)
            ob_s = ob_s.reshape(db, cpad, hb * dkb)[:, 0]

            w_o = w_out[e].astype(BF16)
            (xp,), (xs,) = _token_call(
                _proj_res_body, [oa_p, ob_p, xp], [oa_s, ob_s, xs],
                [_whole(w_o[:ha * dva]), _whole(w_o[ha * dva:]), _whole(jnp.zeros((1, d), F32))],
                [(d, 1)], tm, "projection_residual")

            k_p.append(k4_p.reshape(bp, seq, ha, dva))
            v_p.append(v4_p.reshape(bp, seq, ha, dva))
            k_s.append(k4_s.reshape(db, 1, ha, dva))
            v_s.append(v4_s.reshape(db, 1, ha, dva))
            gc_p.append(last_rows(qkv_p, gdn_taps - 1))
            gc_s.append(jnp.concatenate([state_gdn_conv[e], qkv_s.reshape(db, 1, c_qkv)],
                                        axis=1)[:, -(gdn_taps - 1):])
            gs_p.append(s_p)
            gs_s.append(s_s)
        else:
            cidx = layer // 2
            (hg_p,), (hg_s,) = _token_call(
                _glu_body, [xp], [xs],
                [_whole(row(norm_mix[layer])), _whole(conv_w_pw1[cidx].astype(BF16)),
                 _whole(row(conv_b_pw1[cidx]))],
                [(conv_w_pw1.shape[2] // 2, 1)], tm, "glu_projection")
            dconv = hg_p.shape[1]
            conv_w = (conv_w_dw[cidx], conv_b_dw[cidx], conv_ln_g[cidx], conv_ln_b[cidx])
            hc_p = _dwconv_ln_swish(hg_p, jnp.zeros((bp, conv_taps - 1, dconv), F32), *conv_w,
                                    bp, seq, tt=tt_conv)
            spad = sample_pad_conv
            hc_s = _dwconv_ln_swish(_pad_time(hg_s, db, 1, spad), state_conv_buf[cidx], *conv_w,
                                    db, spad, tt=spad)
            hc_s = hc_s.reshape(db, spad, dconv)[:, 0]
            (xp,), (xs,) = _token_call(
                _proj_res_body, [hc_p, xp], [hc_s, xs],
                [_whole(conv_w_pw2[cidx].astype(BF16)), _whole(row(conv_b_pw2[cidx]))],
                [(d, 1)], tm, "projection_residual")
            cb_p.append(last_rows(hg_p, conv_taps - 1))
            cb_s.append(jnp.concatenate([state_conv_buf[cidx], hg_s.reshape(db, 1, dconv)],
                                        axis=1)[:, -(conv_taps - 1):])
        tf = d_ff // 2
        (xp,), (xs,) = _token_call(
            functools.partial(_ffn_body, final_norm=(layer == depth - 1)), [xp], [xs],
            [_whole(row(norm_ffn[layer])),
             (ffn_w_gate[layer].astype(BF16), (d, tf), lambda k: (0, k)),
             (ffn_w_up[layer].astype(BF16), (d, tf), lambda k: (0, k)),
             (ffn_w_down[layer].astype(BF16), (tf, d), lambda k: (k, 0)),
             _whole(row(norm_final))],
            [(d, 1)], tm_ffn, "swiglu_ffn", inner=d_ff // tf,
            scratch=(pltpu.VMEM((tm_ffn, d), BF16), pltpu.VMEM((tm_ffn, d), F32)))

    y_prompt = xp.reshape(bp, seq, d)
    y_sample = xs.reshape(db, 1, d)
    return (y_prompt, y_sample, jnp.stack(k_p), jnp.stack(v_p), jnp.stack(k_s), jnp.stack(v_s),
            jnp.stack(gc_p), jnp.stack(gc_s), jnp.stack(gs_p), jnp.stack(gs_s),
            jnp.stack(cb_p), jnp.stack(cb_s))
```

```python
import functools
import math

import jax
import jax.numpy as jnp
from jax import lax
from jax.experimental import pallas as pl
from jax.experimental.pallas import tpu as pltpu

F32 = jnp.float32
BF16 = jnp.bfloat16

RMS_EPS = 1e-6
LN_EPS = 1e-5
L2_EPS = 1e-6
NUM_BUCKETS = 32
MAX_DISTANCE = 128
GDN_CHUNK = 64
NEG_BIG = -1e30
LOG2E = math.log2(math.e)
LANES = 128
SUBLANES = 8
VMEM_LIMIT = 48 * 1024 * 1024

_HI = lax.Precision.HIGHEST


def _cparams(*sem):
    return pltpu.CompilerParams(dimension_semantics=sem, vmem_limit_bytes=VMEM_LIMIT)


def _mm(a, b):
    return jnp.dot(a.astype(BF16), b.astype(BF16), preferred_element_type=F32)


def _mm_nt(a, b):
    return lax.dot_general(a.astype(BF16), b.astype(BF16), (((1,), (1,)), ((), ())),
                           preferred_element_type=F32)


def _mm_tn(a, b):
    return lax.dot_general(a.astype(BF16), b.astype(BF16), (((0,), (0,)), ((), ())),
                           preferred_element_type=F32)


def _mm_hi(a, b):
    return jnp.dot(a, b, precision=_HI, preferred_element_type=F32)


def _mm_nt_hi(a, b):
    return lax.dot_general(a, b, (((1,), (1,)), ((), ())), precision=_HI,
                           preferred_element_type=F32)


def _rms(x, w, eps):
    return x * lax.rsqrt(jnp.mean(x * x, axis=-1, keepdims=True) + eps) * w


def _sigmoid(x):
    return 1.0 / (1.0 + jnp.exp(-x))


def _silu(x):
    return x * _sigmoid(x)


def _lane_pick(x, lane_idx, k):
    return jnp.sum(jnp.where(lane_idx == k, x, 0.0), axis=-1, keepdims=True)


def _token_call(body, prompt_in, sample_in, shared, outs, tm, name, inner=1, scratch=()):
    mp, ms = prompt_in[0].shape[0], sample_in[0].shape[0]
    assert mp % tm == 0
    n = mp // tm
    n_in, n_sh, n_out = len(prompt_in), len(shared), len(outs)

    def kern(*refs):
        p_in, s_in = refs[:n_in], refs[n_in:2 * n_in]
        sh = refs[2 * n_in:2 * n_in + n_sh]
        o0 = 2 * n_in + n_sh
        p_out, s_out = refs[o0:o0 + n_out], refs[o0 + n_out:o0 + 2 * n_out]
        scr = refs[o0 + 2 * n_out:]
        i = pl.program_id(0)

        @pl.when(i < n)
        def _():
            body(p_in, sh, p_out, scr)

        @pl.when(i == n)
        def _():
            body(s_in, sh, s_out, scr)

    prow = lambda i, k: (jnp.minimum(i, n - 1), 0)
    srow = lambda i, k: (0, 0)
    in_specs = ([pl.BlockSpec((tm, a.shape[1]), prow) for a in prompt_in]
                + [pl.BlockSpec((ms, a.shape[1]), srow) for a in sample_in]
                + [pl.BlockSpec(bs, (lambda i, k, f=f: f(k))) for _, bs, f in shared])
    out_specs = ([pl.BlockSpec((tm * r, w), prow) for w, r in outs]
                 + [pl.BlockSpec((ms * r, w), srow) for w, r in outs])
    out_shape = ([jax.ShapeDtypeStruct((mp * r, w), F32) for w, r in outs]
                 + [jax.ShapeDtypeStruct((ms * r, w), F32) for w, r in outs])
    res = pl.pallas_call(
        kern, grid=(n + 1, inner), in_specs=in_specs, out_specs=out_specs, out_shape=out_shape,
        scratch_shapes=list(scratch), compiler_params=_cparams("arbitrary", "arbitrary"),
        name=name,
    )(*prompt_in, *sample_in, *[a for a, _, _ in shared])
    return res[:n_out], res[n_out:]


def _whole(a):
    return (a, a.shape, lambda k, nd=a.ndim: (0,) * nd)


def _inproj_body(ins, sh, outs, scr, *, groups, heads, kv_groups):
    x_ref, = ins
    nw_ref, w_ref = sh
    xn = _rms(x_ref[...], nw_ref[...], RMS_EPS).astype(BF16)
    rows = x_ref.shape[0]
    n_g = len(groups)
    for g, (o_ref, (off, width)) in enumerate(zip(outs[:n_g], groups)):
        y = jnp.dot(xn, w_ref[:, off:off + width], preferred_element_type=F32)
        o_ref[...] = y
        if g in kv_groups:
            o4_ref = outs[n_g + kv_groups.index(g)]
            dv = width // heads
            for h in range(heads):
                o4_ref[pl.ds(h, rows, stride=heads), :] = y[:, h * dv:(h + 1) * dv]


def _glu_body(ins, sh, outs, scr):
    x_ref, = ins
    nw_ref, w_ref, b_ref = sh
    o_ref, = outs
    dc = o_ref.shape[1]
    xn = _rms(x_ref[...], nw_ref[...], RMS_EPS).astype(BF16)
    a = jnp.dot(xn, w_ref[:, :dc], preferred_element_type=F32) + b_ref[:, :dc]
    g = jnp.dot(xn, w_ref[:, dc:], preferred_element_type=F32) + b_ref[:, dc:]
    o_ref[...] = a * _sigmoid(g)


def _mix_ffn_body(ins, sh, outs, scr, *, final_norm):
    res_ref = ins[-1]
    nw_ref, wg_ref, wu_ref, wd_ref, fw_ref = sh[:5]
    o_ref, = outs
    xn_ref, acc_ref = scr
    rows = res_ref.shape[0]
    k = pl.program_id(1)

    @pl.when(k == 0)
    def _():
        x = res_ref[...] + sh[-1][...]
        for a_ref, w_ref in zip(ins[:-1], sh[5:-1]):
            x = x + _mm(a_ref[...], w_ref[...])
        xn_ref[0:rows, :] = _rms(x, nw_ref[...], RMS_EPS).astype(BF16)
        acc_ref[0:rows, :] = x

    xn = xn_ref[0:rows, :]
    g = jnp.dot(xn, wg_ref[...], preferred_element_type=F32)
    u = jnp.dot(xn, wu_ref[...], preferred_element_type=F32)
    acc_ref[0:rows, :] += _mm(_silu(g) * u, wd_ref[...])

    @pl.when(k == pl.num_programs(1) - 1)
    def _():
        y = acc_ref[0:rows, :]
        if final_norm:
            y = _rms(y, fw_ref[...], RMS_EPS)
        o_ref[...] = y


def _t5_bucket(n):
    max_exact = NUM_BUCKETS // 2
    nf = jnp.maximum(n, 1).astype(F32)
    large = max_exact + (jnp.log(nf / max_exact) / math.log(MAX_DISTANCE / max_exact)
                         * (NUM_BUCKETS - max_exact)).astype(jnp.int32)
    large = jnp.minimum(large, NUM_BUCKETS - 1)
    return jnp.where(n < max_exact, n, large)


def _lambda(lq1_ref, lk1_ref, lq2_ref, lk2_ref, lam_init):
    s1 = jnp.sum(lq1_ref[...] * lk1_ref[...], axis=-1, keepdims=True)
    s2 = jnp.sum(lq2_ref[...] * lk2_ref[...], axis=-1, keepdims=True)
    return jnp.exp(s1) - jnp.exp(s2) + lam_init


def _attn_prompt_kernel(q_ref, k_ref, v_ref, bias_ref, lq1_ref, lk1_ref, lq2_ref, lk2_ref,
                        sw_ref, o_ref, qt_ref, kb_ref, vt_ref, m_ref, acc_ref, sa_ref, sb_ref, *,
                        t, da, lam_init):
    i = pl.program_id(2)
    nblk = kb_ref.shape[0]
    dv = v_ref.shape[1]

    @pl.when(i == 0)
    def _():
        for c in range(nblk):
            kb_ref[c] = k_ref[c * t:(c + 1) * t, :].astype(BF16)
            vt_ref[c, 0:dv, :] = v_ref[c * t:(c + 1) * t, :].T.astype(BF16)
            vt_ref[c, dv:, :] = jnp.ones((vt_ref.shape[1] - dv, t), BF16)

    qt = (q_ref[...] * (da ** -0.5 * LOG2E)).T
    first_map = lax.broadcasted_iota(jnp.int32, qt.shape, 0) < da
    qt_ref[:, 0:t] = jnp.where(first_map, qt, 0.0).astype(BF16)
    qt_ref[:, t:2 * t] = jnp.where(first_map, 0.0, qt).astype(BF16)
    m_ref[...] = jnp.full(m_ref.shape, NEG_BIG, F32)
    acc_ref[...] = jnp.zeros(acc_ref.shape, F32)

    def scores(j, buf_ref):
        buf_ref[...] = jnp.dot(kb_ref[j], qt_ref[...], preferred_element_type=F32)

    def update(j, buf_ref, bias):
        s = buf_ref[...]
        if bias is not None:
            s = s + jnp.concatenate([bias, bias], axis=1)
        m_prev = m_ref[...]
        m_new = jnp.maximum(m_prev, jnp.max(s, axis=0, keepdims=True))
        p = jnp.exp2(s - m_new).astype(BF16)
        acc_ref[...] = (acc_ref[...] * jnp.exp2(m_prev - m_new)
                        + jnp.dot(vt_ref[j], p, preferred_element_type=F32))
        m_ref[...] = m_new

    prev_bias, diag_bias = bias_ref.at[0], bias_ref.at[1]
    scores(0, sa_ref)

    def far_pair(jj, carry):
        j = 2 * jj
        scores(j + 1, sb_ref)
        update(j, sa_ref, None)
        scores(j + 2, sa_ref)
        update(j + 1, sb_ref, None)
        return carry

    n_far = i - 1
    lax.fori_loop(0, n_far // 2, far_pair, 0)

    @pl.when(i % 2 == 1)
    def _():
        scores(i, sb_ref)
        update(i - 1, sa_ref, prev_bias[...])
        update(i, sb_ref, diag_bias[...])

    @pl.when((i % 2 == 0) & (i >= 2))
    def _():
        scores(i - 1, sb_ref)
        update(i - 2, sa_ref, None)
        scores(i, sa_ref)
        update(i - 1, sb_ref, prev_bias[...])
        update(i, sa_ref, diag_bias[...])

    @pl.when(i == 0)
    def _():
        update(0, sa_ref, diag_bias[...])

    acc = acc_ref[...]
    o12 = acc[0:dv] / acc[dv:dv + 1]
    lam = _lambda(lq1_ref, lk1_ref, lq2_ref, lk2_ref, lam_init)
    o = (o12[:, 0:t] - lam * o12[:, t:2 * t]).T
    o_ref[...] = _rms(o, sw_ref[...], LN_EPS) * (1.0 - lam_init)


def _toeplitz(r, t):
    h, period = r.shape
    flat = jnp.tile(r, (1, t))[:, :t * (period - 1)]
    return flat.reshape(h, t, period - 1)[:, :, :t]


def _prompt_bias_tiles(rel_bias, t):
    assert t >= MAX_DISTANCE
    far = rel_bias[NUM_BUCKETS - 1]
    b1 = jnp.transpose(rel_bias[_t5_bucket(jnp.arange(2 * t))] - far).astype(F32) * LOG2E
    r_diag = jnp.concatenate([b1[:, :t], jnp.full_like(b1[:, :t], NEG_BIG)], axis=1)
    r_prev = jnp.concatenate([b1[:, t:], b1[:, :t]], axis=1)
    return jnp.stack([_toeplitz(r_prev, t), _toeplitz(r_diag, t)], axis=1)


def _attention_prompt(q, k, v, rel_bias, lam_vecs, subln_w, batch, seq, heads, lam_init, t):
    dv = q.shape[1] // heads
    da = dv // 2
    nq = seq // t
    bias = _prompt_bias_tiles(rel_bias, t)
    vec = lambda a: a.reshape(1, -1)
    const = lambda b, h, i: (0, 0)
    return pl.pallas_call(
        functools.partial(_attn_prompt_kernel, t=t, da=da, lam_init=lam_init),
        grid=(batch, heads, nq),
        in_specs=[pl.BlockSpec((t, dv), lambda b, h, i: (b * nq + i, h)),
                  pl.BlockSpec((seq, dv), lambda b, h, i: (b, h)),
                  pl.BlockSpec((seq, dv), lambda b, h, i: (b, h)),
                  pl.BlockSpec((None, 2, t, t), lambda b, h, i: (h, 0, 0, 0)),
                  pl.BlockSpec((1, da), const), pl.BlockSpec((1, da), const),
                  pl.BlockSpec((1, da), const), pl.BlockSpec((1, da), const),
                  pl.BlockSpec((1, dv), const)],
        out_specs=pl.BlockSpec((t, dv), lambda b, h, i: (b * nq + i, h)),
        out_shape=jax.ShapeDtypeStruct((batch * seq, heads * dv), F32),
        scratch_shapes=[pltpu.VMEM((dv, 2 * t), BF16), pltpu.VMEM((nq, t, dv), BF16),
                        pltpu.VMEM((nq, dv + 2 * SUBLANES, t), BF16),
                        pltpu.VMEM((1, 2 * t), F32),
                        pltpu.VMEM((dv + 2 * SUBLANES, 2 * t), F32),
                        pltpu.VMEM((t, 2 * t), F32), pltpu.VMEM((t, 2 * t), F32)],
        compiler_params=_cparams("parallel", "parallel", "arbitrary"),
        name="diff_attention_prompt",
    )(q, k, v, bias, *[vec(a) for a in lam_vecs], vec(subln_w))


def _attn_decode_kernel(pt_ref, q_ref, kn_ref, vn_ref, bfar_ref, blast_ref, bnew_ref,
                        lq1_ref, lk1_ref, lq2_ref, lk2_ref, sw_ref, *rest,
                        pages, heads, da, lam_init):
    del pt_ref
    k_refs = rest[:pages]
    v_refs = rest[pages:2 * pages]
    o_ref, m_ref, l_ref, acc_ref = rest[2 * pages:]
    j = pl.program_id(1)
    last = pl.num_programs(1) - 1

    @pl.when(j == 0)
    def _():
        m_ref[...] = jnp.full(m_ref.shape, NEG_BIG, F32)
        l_ref[...] = jnp.zeros(l_ref.shape, F32)
        acc_ref[...] = jnp.zeros(acc_ref.shape, F32)

    q = q_ref[...] * (da ** -0.5)
    row = lax.broadcasted_iota(jnp.int32, q.shape, 0)
    lane = lax.broadcasted_iota(jnp.int32, q.shape, 1)
    qs = jnp.where((row < heads) == (lane < da), q, 0.0)
    qs_bf = qs.astype(BF16)

    s = jnp.concatenate([_mm_nt(qs_bf, k_ref[...]) for k_ref in k_refs], axis=1)
    s = s + jnp.where(j == last, blast_ref[...], bfar_ref[...])
    m_prev = m_ref[...]
    m_new = jnp.maximum(m_prev, jnp.max(s, axis=-1, keepdims=True))
    p = jnp.exp(s - m_new)
    alpha = jnp.exp(m_prev - m_new)
    l_ref[...] = alpha * l_ref[...] + jnp.sum(p, axis=-1, keepdims=True)
    rows_per_page = k_refs[0].shape[0]
    pv = acc_ref[...] * alpha
    for idx, v_ref in enumerate(v_refs):
        pv = pv + _mm(p[:, idx * rows_per_page:(idx + 1) * rows_per_page], v_ref[...])
    acc_ref[...] = pv
    m_ref[...] = m_new

    @pl.when(j == last)
    def _():
        s_new = jnp.sum(qs * kn_ref[...], axis=-1, keepdims=True) + bnew_ref[:, 0:1]
        m_prev = m_ref[...]
        m_fin = jnp.maximum(m_prev, s_new)
        p_new = jnp.exp(s_new - m_fin)
        alpha = jnp.exp(m_prev - m_fin)
        l_fin = alpha * l_ref[...] + p_new
        acc = alpha * acc_ref[...] + p_new * vn_ref[...]
        o12 = acc / l_fin
        lam = _lambda(lq1_ref, lk1_ref, lq2_ref, lk2_ref, lam_init)
        o = o12[0:heads] - lam * o12[heads:2 * heads]
        o_ref[...] = _rms(o, sw_ref[...], LN_EPS) * (1.0 - lam_init)


def _attention_decode(q, k_new, v_new, k_pool, v_pool, page_table, rel_bias, lam_vecs,
                      subln_w, heads, lam_init, pages):
    db = q.shape[0]
    n_pool, page, _, dv = k_pool.shape
    da = dv // 2
    n_pages = page_table.shape[1]
    past = n_pages * page
    rpp = page * heads
    span = pages * page
    assert n_pages % pages == 0 and span >= MAX_DISTANCE
    kp = k_pool.reshape(n_pool, rpp, dv)
    vp = v_pool.reshape(n_pool, rpp, dv)

    def two_maps(a):
        a = a.reshape(db, 1, heads, dv)
        return jnp.broadcast_to(a, (db, 2, heads, dv)).reshape(db, 2 * heads, dv)

    same = jnp.eye(heads, dtype=bool)

    def pair_bias(b):
        b = jnp.where(same[:, None, :], jnp.transpose(b)[:, :, None], NEG_BIG)
        return jnp.tile(b.reshape(heads, span * heads), (2, 1))

    rb = rel_bias.astype(F32)
    b_far = pair_bias(jnp.broadcast_to(rb[NUM_BUCKETS - 1], (span, heads)))
    b_last = pair_bias(rb[_t5_bucket(span - jnp.arange(span))])
    b_new = jnp.tile(jnp.broadcast_to(rb[0][:, None], (heads, LANES)), (2, 1))

    vec = lambda a: a.reshape(1, -1)
    const = lambda s, j, pt: (0, 0)

    def page_spec(idx):
        return pl.BlockSpec((None, rpp, dv), lambda s, j, pt: (pt[s, j * pages + idx], 0, 0))

    grid_spec = pltpu.PrefetchScalarGridSpec(
        num_scalar_prefetch=1,
        grid=(db, n_pages // pages),
        in_specs=([pl.BlockSpec((None, 2 * heads, dv), lambda s, j, pt: (s, 0, 0))] * 3
                  + [pl.BlockSpec((2 * heads, span * heads), const),
                     pl.BlockSpec((2 * heads, span * heads), const),
                     pl.BlockSpec((2 * heads, LANES), const),
                     pl.BlockSpec((1, da), const), pl.BlockSpec((1, da), const),
                     pl.BlockSpec((1, da), const), pl.BlockSpec((1, da), const),
                     pl.BlockSpec((1, dv), const)]
                  + [page_spec(idx) for idx in range(pages)] * 2),
        out_specs=pl.BlockSpec((None, heads, dv), lambda s, j, pt: (s, 0, 0)),
        scratch_shapes=[pltpu.VMEM((2 * heads, 1), F32), pltpu.VMEM((2 * heads, 1), F32),
                        pltpu.VMEM((2 * heads, dv), F32)],
    )
    out = pl.pallas_call(
        functools.partial(_attn_decode_kernel, pages=pages, heads=heads, da=da,
                          lam_init=lam_init),
        grid_spec=grid_spec,
        out_shape=jax.ShapeDtypeStruct((db, heads, dv), F32),
        compiler_params=_cparams("parallel", "arbitrary"),
        name="diff_attention_decode",
    )(page_table, two_maps(q), two_maps(k_new), two_maps(v_new), b_far, b_last, b_new,
      *[vec(a) for a in lam_vecs], vec(subln_w), *([kp] * pages), *([vp] * pages))
    return out.reshape(db, heads * dv)


def _unit_lower_inverse(lmats, eye, levels):
    if levels == 0:
        return [eye for _ in lmats]
    xs = [-m for m in lmats]
    tinvs = [eye + x for x in xs]
    if levels == 1:
        return tinvs
    n = eye.shape[0]
    rs = [_mm(x, x) for x in xs]
    for k in range(1, levels):
        if k < levels - 1:
            both = [_mm(jnp.concatenate([r, t], axis=0), r) for r, t in zip(rs, tinvs)]
            rs = [b[:n] for b in both]
            tinvs = [t + b[n:] for t, b in zip(tinvs, both)]
        else:
            tinvs = [t + _mm(t, r) for t, r in zip(tinvs, rs)]
    return tinvs


def _gdn_pre_kernel(x_ref, st_ref, cw_ref, ba_ref, gp_ref,
                    u_ref, w_ref, qd_ref, kd_ref, at_ref, eg_ref,
                    xe_ref, qn_ref, kn_ref, vv_ref, gb_ref, gc_ref, *, tt, t_real, heads, dk):
    c = GDN_CHUNK
    i = pl.program_id(1)
    taps = cw_ref.shape[0]
    halo = SUBLANES
    hk = heads * dk

    @pl.when(i == 0)
    def _():
        xe_ref[0:halo, :] = st_ref[...]

    xe_ref[halo:halo + tt, :] = x_ref[...]
    acc = cw_ref[taps - 1:taps, :] * x_ref[...]
    for j in range(taps - 1):
        acc = acc + cw_ref[j:j + 1, :] * xe_ref[pl.ds(halo - (taps - 1) + j, tt), :]
    xe_ref[0:halo, :] = xe_ref[tt:tt + halo, :]
    hcv = _silu(acc)

    valid = (i * tt + lax.broadcasted_iota(jnp.int32, (tt, 1), 0)) < t_real
    for h in range(heads):
        sl = slice(h * dk, (h + 1) * dk)
        qh = hcv[:, h * dk:(h + 1) * dk]
        kh = hcv[:, hk + h * dk:hk + (h + 1) * dk]
        qn_ref[:, sl] = qh * lax.rsqrt(jnp.sum(qh * qh, axis=-1, keepdims=True) + L2_EPS)
        kn = kh * lax.rsqrt(jnp.sum(kh * kh, axis=-1, keepdims=True) + L2_EPS)
        kn_ref[:, sl] = jnp.where(valid, kn, 0.0)
    vv_ref[...] = jnp.where(valid, hcv[:, 2 * hk:], 0.0)

    ba = ba_ref[...]
    lane = lax.broadcasted_iota(jnp.int32, ba.shape, 1)
    xa = ba + gp_ref[1:2, :]
    softplus = jnp.maximum(xa, 0.0) + jnp.log1p(jnp.exp(-jnp.abs(xa)))
    gates = jnp.where(lane < heads, _sigmoid(ba), -jnp.exp(gp_ref[0:1, :]) * softplus)
    gates = jnp.where(valid & (lane < 2 * heads), gates, 0.0)
    gb_ref[...] = gates

    tri_l = (lax.broadcasted_iota(jnp.int32, (c, c), 0)
             >= lax.broadcasted_iota(jnp.int32, (c, c), 1)).astype(F32)
    for ch in range(tt // c):
        gc_ref[ch * c:(ch + 1) * c, :] = _mm_hi(tri_l, gates[ch * c:(ch + 1) * c, :])

    ri = lax.broadcasted_iota(jnp.int32, (c, c), 0)
    ci = lax.broadcasted_iota(jnp.int32, (c, c), 1)
    incl = ri >= ci
    strict = ri > ci
    eye = (ri == ci).astype(F32)
    pick = (lax.broadcasted_iota(jnp.int32, (SUBLANES, LANES), 0)
            == lax.broadcasted_iota(jnp.int32, (SUBLANES, LANES), 1)).astype(F32)
    lane_c = lax.broadcasted_iota(jnp.int32, (c, LANES), 1)
    scale = dk ** -0.5
    c_valid = c if t_real >= c else t_real
    levels = (c_valid - 1).bit_length()

    n_chunks = tt // c
    group = 2 if n_chunks % 2 == 0 else 1

    def group_body(gi, carry):
        probs = []
        for cc in range(group):
            ch = gi * group + cc
            rows = pl.ds(pl.multiple_of(ch * c, c), c)
            gb = gb_ref[rows, :]
            gcc = gc_ref[rows, :]
            gc_rows = _mm_nt_hi(pick, gcc)
            for h in range(heads):
                probs.append(dict(
                    ch=ch, rows=rows, h=h, sl=slice(h * dk, (h + 1) * dk),
                    beta=_lane_pick(gb, lane_c, h),
                    gc_col=_lane_pick(gcc, lane_c, heads + h),
                    gc_row=gc_rows[heads + h:heads + h + 1, :]))
        for p in probs:
            p["decay"] = jnp.where(
                incl, jnp.exp(jnp.where(incl, p["gc_col"] - p["gc_row"], 0.0)), 0.0)
            p["k"] = kn_ref[p["rows"], p["sl"]]
            p["k_beta"] = p["k"] * p["beta"]
        kk = [_mm_nt(p["k_beta"], p["k"]) for p in probs]
        lmats = [jnp.where(strict, m * p["decay"], 0.0) for m, p in zip(kk, probs)]
        tinvs = _unit_lower_inverse(lmats, eye, levels)
        for p in probs:
            p["egc"] = jnp.exp(p["gc_col"])
            p["q"] = qn_ref[p["rows"], p["sl"]] * scale
        us = [_mm(t, vv_ref[p["rows"], p["sl"]] * p["beta"]) for t, p in zip(tinvs, probs)]
        ws = [_mm(t, p["k_beta"] * p["egc"]) for t, p in zip(tinvs, probs)]
        ats = [_mm_nt(p["q"], p["k"]) for p in probs]
        for p, u, w, at in zip(probs, us, ws, ats):
            rows, sl, h = p["rows"], p["sl"], p["h"]
            g_last = p["gc_col"][c - 1:c, :]
            u_ref[rows, sl] = u
            w_ref[rows, sl] = w
            at_ref[rows, h * c:(h + 1) * c] = at * p["decay"]
            kd_ref[rows, sl] = p["k"] * jnp.exp(g_last - p["gc_col"])
            qd_ref[rows, sl] = p["q"] * p["egc"]
            eg_ref[p["ch"], :, sl] = jnp.broadcast_to(jnp.exp(g_last), (SUBLANES, dk))
        return carry

    lax.fori_loop(0, n_chunks // group, group_body, 0)


def _gdn_scan_kernel(u_ref, w_ref, qd_ref, kd_ref, at_ref, eg_ref, z_ref, s0_ref, nw_ref,
                     o_ref, sout_ref, s_ref, kt_ref, *, tt, heads, dk):
    c = GDN_CHUNK
    i = pl.program_id(1)
    nb = u_ref.shape[0]

    @pl.when(i == 0)
    def _():
        s_ref[...] = s0_ref[...]

    for b in range(nb):
        for ch in range(tt // c):
            for h in range(heads):
                kt_ref[b, ch * heads + h] = (
                    kd_ref[b, ch * c:(ch + 1) * c, h * dk:(h + 1) * dk].T.astype(BF16))

    chains = [(b, h) for b in range(nb) for h in range(heads)]
    cols = lambda h: slice(h * dk, (h + 1) * dk)

    def chunk_body(ch, carry):
        rows = pl.ds(pl.multiple_of(ch * c, c), c)
        s_bf = [s_ref[b, h].astype(BF16) for b, h in chains]
        w_s = [_mm(w_ref[b, rows, cols(h)], s) for (b, h), s in zip(chains, s_bf)]
        q_s = [_mm(qd_ref[b, rows, cols(h)], s) for (b, h), s in zip(chains, s_bf)]
        v_new = [(u_ref[b, rows, cols(h)] - ws).astype(BF16) for (b, h), ws in zip(chains, w_s)]
        a_v = [_mm(at_ref[b, rows, h * c:(h + 1) * c], v) for (b, h), v in zip(chains, v_new)]
        k_v = [_mm(kt_ref[b, ch * heads + h], v) for (b, h), v in zip(chains, v_new)]
        for (b, h), qs, av, kv in zip(chains, q_s, a_v, k_v):
            s_ref[b, h] = s_ref[b, h] * eg_ref[b, ch, 0:1, cols(h)] + kv
            o_ref[b, rows, cols(h)] = (_rms(qs + av, nw_ref[...], RMS_EPS)
                                       * _silu(z_ref[b, rows, cols(h)]))
        return carry

    lax.fori_loop(0, tt // c, chunk_body, 0)

    @pl.when(i == pl.num_programs(1) - 1)
    def _():
        sout_ref[...] = s_ref[...]


def _gated_deltanet(qkv, ba, z, conv_state, s0, conv_w, a_log, dt_bias, norm_w,
                    batch, seq_pad, t_real, heads, tt, scan_batch, scan_tt):
    cq = qkv.shape[1]
    hk = cq // 3
    dk = hk // heads
    c = GDN_CHUNK
    assert c & (c - 1) == 0 and batch % scan_batch == 0
    nt = seq_pad // tt
    nc = tt // c
    taps = conv_w.shape[0]
    st = jnp.pad(conv_state, ((0, 0), (SUBLANES - (taps - 1), 0), (0, 0)))
    gp = jnp.zeros((SUBLANES, LANES), F32)
    gp = gp.at[0, heads:2 * heads].set(a_log.astype(F32))
    gp = gp.at[1, heads:2 * heads].set(dt_bias.astype(F32))
    row_blk = lambda b, i: (b * nt + i, 0)
    const = lambda b, i: (0, 0)
    tok = lambda width: pl.BlockSpec((tt, width), row_blk)
    rows = batch * seq_pad
    u, w, qd, kd, at, eg = pl.pallas_call(
        functools.partial(_gdn_pre_kernel, tt=tt, t_real=t_real, heads=heads, dk=dk),
        grid=(batch, nt),
        in_specs=[tok(cq),
                  pl.BlockSpec((None, SUBLANES, cq), lambda b, i: (b, 0, 0)),
                  pl.BlockSpec((taps, cq), const),
                  tok(LANES),
                  pl.BlockSpec((SUBLANES, LANES), const)],
        out_specs=[tok(hk), tok(hk), tok(hk), tok(hk), tok(heads * c),
                   pl.BlockSpec((None, nc, SUBLANES, hk), lambda b, i: (b, i, 0, 0))],
        out_shape=[jax.ShapeDtypeStruct((rows, hk), F32)] * 4
                  + [jax.ShapeDtypeStruct((rows, heads * c), F32),
                     jax.ShapeDtypeStruct((batch, nt * nc, SUBLANES, hk), F32)],
        scratch_shapes=[pltpu.VMEM((tt + SUBLANES, cq), F32), pltpu.VMEM((tt, hk), F32),
                        pltpu.VMEM((tt, hk), F32), pltpu.VMEM((tt, hk), F32),
                        pltpu.VMEM((tt, LANES), F32), pltpu.VMEM((tt, LANES), F32)],
        compiler_params=_cparams("parallel", "arbitrary"),
        name="gdn_chunk_prepare",
    )(qkv, st, conv_w, ba, gp)

    nb, stt = scan_batch, scan_tt
    snc = stt // c
    seq3 = lambda a: a.reshape(batch, seq_pad, a.shape[-1])
    blk3 = lambda width: pl.BlockSpec((nb, stt, width), lambda g, i: (g, i, 0))
    state_spec = pl.BlockSpec((nb, heads, dk, dk), lambda g, i: (g, 0, 0, 0))
    o, s_new = pl.pallas_call(
        functools.partial(_gdn_scan_kernel, tt=stt, heads=heads, dk=dk),
        grid=(batch // nb, seq_pad // stt),
        in_specs=[blk3(hk), blk3(hk), blk3(hk), blk3(hk), blk3(heads * c),
                  pl.BlockSpec((nb, snc, SUBLANES, hk), lambda g, i: (g, i, 0, 0)),
                  blk3(hk), state_spec,
                  pl.BlockSpec((1, dk), const)],
        out_specs=[blk3(hk), state_spec],
        out_shape=[jax.ShapeDtypeStruct((batch, seq_pad, hk), F32),
                   jax.ShapeDtypeStruct((batch, heads, dk, dk), F32)],
        scratch_shapes=[pltpu.VMEM((nb, heads, dk, dk), F32),
                        pltpu.VMEM((nb, snc * heads, dk, c), BF16)],
        compiler_params=_cparams("parallel", "arbitrary"),
        name="gdn_chunk_scan",
    )(seq3(u), seq3(w), seq3(qd), seq3(kd), seq3(at), eg, seq3(z), s0, norm_w.reshape(1, dk))
    return o.reshape(rows, hk), s_new


def _dwconv_kernel(h_ref, buf_ref, w_ref, b_ref, g_ref, bb_ref, o_ref, xe_ref, xs_ref, *, tt, rb):
    i = pl.program_id(1)
    taps = w_ref.shape[0]
    halo = buf_ref.shape[0]
    first = halo - (taps - 1)
    sub = w_ref.shape[1]

    @pl.when(i == 0)
    def _():
        xe_ref[0:halo, :] = buf_ref[...]

    xe_ref[halo:halo + tt, :] = h_ref[...]
    span = tt + halo - sub
    for ph in range(1, sub):
        xs_ref[ph - 1, 0:span, :] = xe_ref[pl.ds(ph, span), :]

    def window(row, size):
        a, ph = divmod(row, sub)
        if ph == 0:
            return xe_ref[pl.ds(row, size), :]
        return xs_ref[ph - 1, pl.ds(a * sub, size), :]

    def weight(j):
        return w_ref[j] if rb == sub else jnp.tile(w_ref[j], (rb // sub, 1))

    for r0 in range(0, tt, rb):
        acc = b_ref[...] + weight(0) * window(first + r0, rb)
        for j in range(1, taps):
            acc = acc + weight(j) * window(first + r0 + j, rb)
        mu = jnp.mean(acc, axis=-1, keepdims=True)
        xc = acc - mu
        var = jnp.mean(xc * xc, axis=-1, keepdims=True)
        y = xc * lax.rsqrt(var + LN_EPS) * g_ref[...] + bb_ref[...]
        o_ref[pl.ds(r0, rb), :] = _silu(y)
    xe_ref[0:halo, :] = xe_ref[tt:tt + halo, :]


def _dwconv_ln_swish(h, buf, w_dw, b_dw, ln_g, ln_b, batch, seq_pad, tt):
    ch = h.shape[1]
    taps = w_dw.shape[0]
    halo = -(-(taps - 1) // SUBLANES) * SUBLANES
    bufp = jnp.pad(buf, ((0, 0), (halo - (taps - 1), 0), (0, 0)))
    nt = seq_pad // tt
    rb = min(tt, 2 * SUBLANES)
    const = lambda b, i: (0, 0)
    vec = lambda a: a.reshape(1, ch)
    w_rep = jnp.broadcast_to(w_dw[:, None, :], (taps, SUBLANES, ch))
    return pl.pallas_call(
        functools.partial(_dwconv_kernel, tt=tt, rb=rb),
        grid=(batch, nt),
        in_specs=[pl.BlockSpec((tt, ch), lambda b, i: (b * nt + i, 0)),
                  pl.BlockSpec((None, halo, ch), lambda b, i: (b, 0, 0)),
                  pl.BlockSpec((taps, SUBLANES, ch), lambda b, i: (0, 0, 0)),
                  pl.BlockSpec((1, ch), const), pl.BlockSpec((1, ch), const),
                  pl.BlockSpec((1, ch), const)],
        out_specs=pl.BlockSpec((tt, ch), lambda b, i: (b * nt + i, 0)),
        out_shape=jax.ShapeDtypeStruct((batch * seq_pad, ch), F32),
        scratch_shapes=[pltpu.VMEM((tt + halo, ch), F32),
                        pltpu.VMEM((SUBLANES - 1, tt + halo - SUBLANES, ch), F32)],
        compiler_params=_cparams("parallel", "arbitrary"),
        name="dwconv_ln_swish",
    )(h, bufp, w_rep, vec(b_dw), vec(ln_g), vec(ln_b))


def _pad_time(a, batch, t, t_pad):
    a = a.reshape(batch, t, a.shape[-1])
    return jnp.pad(a, ((0, 0), (0, t_pad - t), (0, 0))).reshape(batch * t_pad, a.shape[-1])


def kernel(x_prompt, x_sample, cache_attn_k, cache_attn_v, page_table, state_gdn_conv, state_gdn_s, state_conv_buf, norm_mix, norm_ffn, norm_final, w_in, w_out, gdn_conv_w, gdn_A_log, gdn_dt_bias, gdn_norm_w, lam_q1, lam_k1, lam_q2, lam_k2, diff_subln_w, rel_bias, conv_w_pw1, conv_b_pw1, conv_w_dw, conv_b_dw, conv_ln_g, conv_ln_b, conv_w_pw2, conv_b_pw2, ffn_w_gate, ffn_w_up, ffn_w_down):
    bp, seq, d = x_prompt.shape
    db, dseq, _ = x_sample.shape
    assert dseq == 1
    depth = norm_mix.shape[0]
    ha, dva = cache_attn_k.shape[3], cache_attn_v.shape[4]
    hb, dkb = state_gdn_s.shape[2], state_gdn_s.shape[3]
    c_qkv = state_gdn_conv.shape[3]
    gdn_taps = gdn_conv_w.shape[1]
    conv_taps = conv_w_dw.shape[1]
    d_ff = ffn_w_gate.shape[2]
    mp = bp * seq
    assert 2 * hb <= LANES

    tm, tm_ffn = 256, 512
    t_attn = min(512, seq)
    tt_gdn = min(512, seq)
    tt_scan = min(256, seq)
    tt_conv = min(256, seq)
    sample_pad_gdn, sample_pad_conv = GDN_CHUNK, SUBLANES

    xp = x_prompt.reshape(mp, d)
    xs = x_sample.reshape(db, d)
    row = lambda a: a.reshape(1, -1)

    sizes = (ha * dva, ha * dva, ha * dva, c_qkv, hb * dkb, LANES)
    offs = [0]
    for s_ in sizes:
        offs.append(offs[-1] + s_)
    groups = tuple(zip(offs[:-1], sizes))
    inproj_outs = [(wd, 1) for wd in sizes] + [(dva, ha), (dva, ha)]

    def last_rows(a, n):
        return jnp.stack([a[(b + 1) * seq - n:(b + 1) * seq] for b in range(bp)])

    k_p, v_p, k_s, v_s, gc_p, gc_s, gs_p, gs_s, cb_p, cb_s = ([] for _ in range(10))
    for layer in range(depth):
        if layer % 2 == 0:
            e = layer // 2
            lam_init = 0.8 - 0.6 * math.exp(-0.3 * layer)
            w_in_e = jnp.pad(w_in[e], ((0, 0), (0, offs[-1] - w_in.shape[2]))).astype(BF16)
            outs_p, outs_s = _token_call(
                functools.partial(_inproj_body, groups=groups, heads=ha, kv_groups=(1, 2)),
                [xp], [xs], [_whole(row(norm_mix[layer])), _whole(w_in_e)],
                inproj_outs, tm, "in_projection")
            qa_p, ka_p, va_p, qkv_p, z_p, ba_p, k4_p, v4_p = outs_p
            qa_s, ka_s, va_s, qkv_s, z_s, ba_s, k4_s, v4_s = outs_s
            lam_vecs = (lam_q1[e], lam_k1[e], lam_q2[e], lam_k2[e])

            oa_p = _attention_prompt(qa_p, ka_p, va_p, rel_bias, lam_vecs, diff_subln_w[e],
                                     bp, seq, ha, lam_init, t=t_attn)
            oa_s = _attention_decode(qa_s, ka_s, va_s, cache_attn_k[e], cache_attn_v[e],
                                     page_table, rel_bias, lam_vecs, diff_subln_w[e],
                                     ha, lam_init, pages=min(32, page_table.shape[1]))

            gdn_w = (gdn_conv_w[e], gdn_A_log[e], gdn_dt_bias[e], gdn_norm_w[e])
            ob_p, s_p = _gated_deltanet(
                qkv_p, ba_p, z_p, jnp.zeros((bp, gdn_taps - 1, c_qkv), F32),
                jnp.zeros((bp, hb, dkb, dkb), F32), *gdn_w, bp, seq, seq, hb,
                tt=tt_gdn, scan_batch=bp, scan_tt=tt_scan)
            cpad = sample_pad_gdn
            ob_s, s_s = _gated_deltanet(
                _pad_time(qkv_s, db, 1, cpad), _pad_time(ba_s, db, 1, cpad),
                _pad_time(z_s, db, 1, cpad), state_gdn_conv[e], state_gdn_s[e], *gdn_w,
                db, cpad, 1, hb, tt=cpad, scan_batch=min(db, 8), scan_tt=cpad)
            ob_s = ob_s.reshape(db, cpad, hb * dkb)[:, 0]

            w_o = w_out[e].astype(BF16)
            mix_p, mix_s = [oa_p, ob_p], [oa_s, ob_s]
            mix_w = [_whole(w_o[:ha * dva]), _whole(w_o[ha * dva:]),
                     _whole(jnp.zeros((1, d), F32))]

            k_p.append(k4_p.reshape(bp, seq, ha, dva))
            v_p.append(v4_p.reshape(bp, seq, ha, dva))
            k_s.append(k4_s.reshape(db, 1, ha, dva))
            v_s.append(v4_s.reshape(db, 1, ha, dva))
            gc_p.append(last_rows(qkv_p, gdn_taps - 1))
            gc_s.append(jnp.concatenate([state_gdn_conv[e], qkv_s.reshape(db, 1, c_qkv)],
                                        axis=1)[:, -(gdn_taps - 1):])
            gs_p.append(s_p)
            gs_s.append(s_s)
        else:
            cidx = layer // 2
            (hg_p,), (hg_s,) = _token_call(
                _glu_body, [xp], [xs],
                [_whole(row(norm_mix[layer])), _whole(conv_w_pw1[cidx].astype(BF16)),
                 _whole(row(conv_b_pw1[cidx]))],
                [(conv_w_pw1.shape[2] // 2, 1)], tm, "glu_projection")
            dconv = hg_p.shape[1]
            conv_w = (conv_w_dw[cidx], conv_b_dw[cidx], conv_ln_g[cidx], conv_ln_b[cidx])
            hc_p = _dwconv_ln_swish(hg_p, jnp.zeros((bp, conv_taps - 1, dconv), F32), *conv_w,
                                    bp, seq, tt=tt_conv)
            spad = sample_pad_conv
            hc_s = _dwconv_ln_swish(_pad_time(hg_s, db, 1, spad), state_conv_buf[cidx], *conv_w,
                                    db, spad, tt=spad)
            hc_s = hc_s.reshape(db, spad, dconv)[:, 0]
            mix_p, mix_s = [hc_p], [hc_s]
            mix_w = [_whole(conv_w_pw2[cidx].astype(BF16)), _whole(row(conv_b_pw2[cidx]))]
            cb_p.append(last_rows(hg_p, conv_taps - 1))
            cb_s.append(jnp.concatenate([state_conv_buf[cidx], hg_s.reshape(db, 1, dconv)],
                                        axis=1)[:, -(conv_taps - 1):])
        tf = d_ff // 2
        (xp,), (xs,) = _token_call(
            functools.partial(_mix_ffn_body, final_norm=(layer == depth - 1)),
            mix_p + [xp], mix_s + [xs],
            [_whole(row(norm_ffn[layer])),
             (ffn_w_gate[layer].astype(BF16), (d, tf), lambda k: (0, k)),
             (ffn_w_up[layer].astype(BF16), (d, tf), lambda k: (0, k)),
             (ffn_w_down[layer].astype(BF16), (tf, d), lambda k: (k, 0)),
             _whole(row(norm_final))] + mix_w,
            [(d, 1)], tm_ffn, "mixer_out_swiglu_ffn", inner=d_ff // tf,
            scratch=(pltpu.VMEM((tm_ffn, d), BF16), pltpu.VMEM((tm_ffn, d), F32)))

    y_prompt = xp.reshape(bp, seq, d)
    y_sample = xs.reshape(db, 1, d)
    return (y_prompt, y_sample, jnp.stack(k_p), jnp.stack(v_p), jnp.stack(k_s), jnp.stack(v_s),
            jnp.stack(gc_p), jnp.stack(gc_s), jnp.stack(gs_p), jnp.stack(gs_s),
            jnp.stack(cb_p), jnp.stack(cb_s))
```

```python
import functools
import math

import jax
import jax.numpy as jnp
from jax import lax
from jax.experimental import pallas as pl
from jax.experimental.pallas import tpu as pltpu

F32 = jnp.float32
BF16 = jnp.bfloat16

RMS_EPS = 1e-6
LN_EPS = 1e-5
L2_EPS = 1e-6
NUM_BUCKETS = 32
MAX_DISTANCE = 128
GDN_CHUNK = 64
NEG_BIG = -1e30
LOG2E = math.log2(math.e)
LANES = 128
SUBLANES = 8
VMEM_LIMIT = 48 * 1024 * 1024

_HI = lax.Precision.HIGHEST


def _cparams(*sem):
    return pltpu.CompilerParams(dimension_semantics=sem, vmem_limit_bytes=VMEM_LIMIT)


def _mm(a, b):
    return jnp.dot(a.astype(BF16), b.astype(BF16), preferred_element_type=F32)


def _mm_nt(a, b):
    return lax.dot_general(a.astype(BF16), b.astype(BF16), (((1,), (1,)), ((), ())),
                           preferred_element_type=F32)


def _mm_tn(a, b):
    return lax.dot_general(a.astype(BF16), b.astype(BF16), (((0,), (0,)), ((), ())),
                           preferred_element_type=F32)


def _mm_hi(a, b):
    return jnp.dot(a, b, precision=_HI, preferred_element_type=F32)


def _mm_nt_hi(a, b):
    return lax.dot_general(a, b, (((1,), (1,)), ((), ())), precision=_HI,
                           preferred_element_type=F32)


def _rms(x, w, eps):
    return x * lax.rsqrt(jnp.mean(x * x, axis=-1, keepdims=True) + eps) * w


def _sigmoid(x):
    return 1.0 / (1.0 + jnp.exp(-x))


def _silu(x):
    return x * _sigmoid(x)


def _lane_pick(x, lane_idx, k):
    return jnp.sum(jnp.where(lane_idx == k, x, 0.0), axis=-1, keepdims=True)


def _token_call(body, prompt_in, sample_in, shared, outs, tm, name, inner=1, scratch=()):
    mp, ms = prompt_in[0].shape[0], sample_in[0].shape[0]
    assert mp % tm == 0
    n = mp // tm
    n_in, n_sh, n_out = len(prompt_in), len(shared), len(outs)

    def kern(*refs):
        p_in, s_in = refs[:n_in], refs[n_in:2 * n_in]
        sh = refs[2 * n_in:2 * n_in + n_sh]
        o0 = 2 * n_in + n_sh
        p_out, s_out = refs[o0:o0 + n_out], refs[o0 + n_out:o0 + 2 * n_out]
        scr = refs[o0 + 2 * n_out:]
        i = pl.program_id(0)

        @pl.when(i < n)
        def _():
            body(p_in, sh, p_out, scr)

        @pl.when(i == n)
        def _():
            body(s_in, sh, s_out, scr)

    prow = lambda i, k: (jnp.minimum(i, n - 1), 0)
    srow = lambda i, k: (0, 0)
    in_specs = ([pl.BlockSpec((tm, a.shape[1]), prow) for a in prompt_in]
                + [pl.BlockSpec((ms, a.shape[1]), srow) for a in sample_in]
                + [pl.BlockSpec(bs, (lambda i, k, f=f: f(k))) for _, bs, f in shared])
    out_specs = ([pl.BlockSpec((tm * r, w), prow) for w, r in outs]
                 + [pl.BlockSpec((ms * r, w), srow) for w, r in outs])
    out_shape = ([jax.ShapeDtypeStruct((mp * r, w), F32) for w, r in outs]
                 + [jax.ShapeDtypeStruct((ms * r, w), F32) for w, r in outs])
    res = pl.pallas_call(
        kern, grid=(n + 1, inner), in_specs=in_specs, out_specs=out_specs, out_shape=out_shape,
        scratch_shapes=list(scratch), compiler_params=_cparams("arbitrary", "arbitrary"),
        name=name,
    )(*prompt_in, *sample_in, *[a for a, _, _ in shared])
    return res[:n_out], res[n_out:]


def _whole(a):
    return (a, a.shape, lambda k, nd=a.ndim: (0,) * nd)


def _inproj_body(ins, sh, outs, scr, *, groups, heads, kv_groups):
    x_ref, = ins
    nw_ref, w_ref = sh
    xn = _rms(x_ref[...], nw_ref[...], RMS_EPS).astype(BF16)
    rows = x_ref.shape[0]
    n_g = len(groups)
    for g, (o_ref, (off, width)) in enumerate(zip(outs[:n_g], groups)):
        y = jnp.dot(xn, w_ref[:, off:off + width], preferred_element_type=F32)
        o_ref[...] = y
        if g in kv_groups:
            o4_ref = outs[n_g + kv_groups.index(g)]
            dv = width // heads
            for h in range(heads):
                o4_ref[pl.ds(h, rows, stride=heads), :] = y[:, h * dv:(h + 1) * dv]


def _mix_ffn_body(ins, sh, outs, scr, *, final_norm):
    res_ref = ins[-1]
    nw_ref, wg_ref, wu_ref, wd_ref, fw_ref = sh[:5]
    o_ref, = outs
    xn_ref, acc_ref = scr
    rows = res_ref.shape[0]
    k = pl.program_id(1)

    @pl.when(k == 0)
    def _():
        x = res_ref[...] + sh[-1][...]
        for a_ref, w_ref in zip(ins[:-1], sh[5:-1]):
            x = x + _mm(a_ref[...], w_ref[...])
        xn_ref[0:rows, :] = _rms(x, nw_ref[...], RMS_EPS).astype(BF16)
        acc_ref[0:rows, :] = x

    xn = xn_ref[0:rows, :]
    g = jnp.dot(xn, wg_ref[...], preferred_element_type=F32)
    u = jnp.dot(xn, wu_ref[...], preferred_element_type=F32)
    acc_ref[0:rows, :] += _mm(_silu(g) * u, wd_ref[...])

    @pl.when(k == pl.num_programs(1) - 1)
    def _():
        y = acc_ref[0:rows, :]
        if final_norm:
            y = _rms(y, fw_ref[...], RMS_EPS)
        o_ref[...] = y


def _t5_bucket(n):
    max_exact = NUM_BUCKETS // 2
    nf = jnp.maximum(n, 1).astype(F32)
    large = max_exact + (jnp.log(nf / max_exact) / math.log(MAX_DISTANCE / max_exact)
                         * (NUM_BUCKETS - max_exact)).astype(jnp.int32)
    large = jnp.minimum(large, NUM_BUCKETS - 1)
    return jnp.where(n < max_exact, n, large)


def _lambda(lq1_ref, lk1_ref, lq2_ref, lk2_ref, lam_init):
    s1 = jnp.sum(lq1_ref[...] * lk1_ref[...], axis=-1, keepdims=True)
    s2 = jnp.sum(lq2_ref[...] * lk2_ref[...], axis=-1, keepdims=True)
    return jnp.exp(s1) - jnp.exp(s2) + lam_init


def _attn_prompt_kernel(q_ref, k_ref, v_ref, bias_ref, lq1_ref, lk1_ref, lq2_ref, lk2_ref,
                        sw_ref, o_ref, qt_ref, kb_ref, vt_ref, m_ref, acc_ref, sa_ref, sb_ref, *,
                        t, da, lam_init):
    i = pl.program_id(2)
    nblk = kb_ref.shape[0]
    dv = v_ref.shape[1]

    @pl.when(i == 0)
    def _():
        for c in range(nblk):
            kb_ref[c] = k_ref[c * t:(c + 1) * t, :].astype(BF16)
            vt_ref[c, 0:dv, :] = v_ref[c * t:(c + 1) * t, :].T.astype(BF16)
            vt_ref[c, dv:, :] = jnp.ones((vt_ref.shape[1] - dv, t), BF16)

    qt = (q_ref[...] * (da ** -0.5 * LOG2E)).T
    first_map = lax.broadcasted_iota(jnp.int32, qt.shape, 0) < da
    qt_ref[:, 0:t] = jnp.where(first_map, qt, 0.0).astype(BF16)
    qt_ref[:, t:2 * t] = jnp.where(first_map, 0.0, qt).astype(BF16)
    m_ref[...] = jnp.full(m_ref.shape, NEG_BIG, F32)
    acc_ref[...] = jnp.zeros(acc_ref.shape, F32)

    def scores(j, buf_ref):
        buf_ref[...] = jnp.dot(kb_ref[j], qt_ref[...], preferred_element_type=F32)

    def update(j, buf_ref, bias):
        s = buf_ref[...]
        if bias is not None:
            s = s + jnp.concatenate([bias, bias], axis=1)
        m_prev = m_ref[...]
        m_new = jnp.maximum(m_prev, jnp.max(s, axis=0, keepdims=True))
        p = jnp.exp2(s - m_new).astype(BF16)
        acc_ref[...] = (acc_ref[...] * jnp.exp2(m_prev - m_new)
                        + jnp.dot(vt_ref[j], p, preferred_element_type=F32))
        m_ref[...] = m_new

    prev_bias, diag_bias = bias_ref.at[0], bias_ref.at[1]
    scores(0, sa_ref)

    def far_pair(jj, carry):
        j = 2 * jj
        scores(j + 1, sb_ref)
        update(j, sa_ref, None)
        scores(j + 2, sa_ref)
        update(j + 1, sb_ref, None)
        return carry

    n_far = i - 1
    lax.fori_loop(0, n_far // 2, far_pair, 0)

    @pl.when(i % 2 == 1)
    def _():
        scores(i, sb_ref)
        update(i - 1, sa_ref, prev_bias[...])
        update(i, sb_ref, diag_bias[...])

    @pl.when((i % 2 == 0) & (i >= 2))
    def _():
        scores(i - 1, sb_ref)
        update(i - 2, sa_ref, None)
        scores(i, sa_ref)
        update(i - 1, sb_ref, prev_bias[...])
        update(i, sa_ref, diag_bias[...])

    @pl.when(i == 0)
    def _():
        update(0, sa_ref, diag_bias[...])

    acc = acc_ref[...]
    o12 = acc[0:dv] / acc[dv:dv + 1]
    lam = _lambda(lq1_ref, lk1_ref, lq2_ref, lk2_ref, lam_init)
    o = (o12[:, 0:t] - lam * o12[:, t:2 * t]).T
    o_ref[...] = _rms(o, sw_ref[...], LN_EPS) * (1.0 - lam_init)


def _toeplitz(r, t):
    h, period = r.shape
    flat = jnp.tile(r, (1, t))[:, :t * (period - 1)]
    return flat.reshape(h, t, period - 1)[:, :, :t]


def _prompt_bias_tiles(rel_bias, t):
    assert t >= MAX_DISTANCE
    far = rel_bias[NUM_BUCKETS - 1]
    b1 = jnp.transpose(rel_bias[_t5_bucket(jnp.arange(2 * t))] - far).astype(F32) * LOG2E
    r_diag = jnp.concatenate([b1[:, :t], jnp.full_like(b1[:, :t], NEG_BIG)], axis=1)
    r_prev = jnp.concatenate([b1[:, t:], b1[:, :t]], axis=1)
    return jnp.stack([_toeplitz(r_prev, t), _toeplitz(r_diag, t)], axis=1)


def _attention_prompt(q, k, v, rel_bias, lam_vecs, subln_w, batch, seq, heads, lam_init, t):
    dv = q.shape[1] // heads
    da = dv // 2
    nq = seq // t
    bias = _prompt_bias_tiles(rel_bias, t)
    vec = lambda a: a.reshape(1, -1)
    const = lambda b, h, i: (0, 0)
    return pl.pallas_call(
        functools.partial(_attn_prompt_kernel, t=t, da=da, lam_init=lam_init),
        grid=(batch, heads, nq),
        in_specs=[pl.BlockSpec((t, dv), lambda b, h, i: (b * nq + i, h)),
                  pl.BlockSpec((seq, dv), lambda b, h, i: (b, h)),
                  pl.BlockSpec((seq, dv), lambda b, h, i: (b, h)),
                  pl.BlockSpec((None, 2, t, t), lambda b, h, i: (h, 0, 0, 0)),
                  pl.BlockSpec((1, da), const), pl.BlockSpec((1, da), const),
                  pl.BlockSpec((1, da), const), pl.BlockSpec((1, da), const),
                  pl.BlockSpec((1, dv), const)],
        out_specs=pl.BlockSpec((t, dv), lambda b, h, i: (b * nq + i, h)),
        out_shape=jax.ShapeDtypeStruct((batch * seq, heads * dv), F32),
        scratch_shapes=[pltpu.VMEM((dv, 2 * t), BF16), pltpu.VMEM((nq, t, dv), BF16),
                        pltpu.VMEM((nq, dv + 2 * SUBLANES, t), BF16),
                        pltpu.VMEM((1, 2 * t), F32),
                        pltpu.VMEM((dv + 2 * SUBLANES, 2 * t), F32),
                        pltpu.VMEM((t, 2 * t), F32), pltpu.VMEM((t, 2 * t), F32)],
        compiler_params=_cparams("parallel", "parallel", "arbitrary"),
        name="diff_attention_prompt",
    )(q, k, v, bias, *[vec(a) for a in lam_vecs], vec(subln_w))


def _attn_decode_kernel(pt_ref, q_ref, kn_ref, vn_ref, bfar_ref, blast_ref, bnew_ref,
                        lq1_ref, lk1_ref, lq2_ref, lk2_ref, sw_ref, *rest,
                        pages, heads, da, lam_init):
    del pt_ref
    k_refs = rest[:pages]
    v_refs = rest[pages:2 * pages]
    o_ref, m_ref, l_ref, acc_ref = rest[2 * pages:]
    j = pl.program_id(1)
    last = pl.num_programs(1) - 1

    @pl.when(j == 0)
    def _():
        m_ref[...] = jnp.full(m_ref.shape, NEG_BIG, F32)
        l_ref[...] = jnp.zeros(l_ref.shape, F32)
        acc_ref[...] = jnp.zeros(acc_ref.shape, F32)

    q = q_ref[...] * (da ** -0.5)
    row = lax.broadcasted_iota(jnp.int32, q.shape, 0)
    lane = lax.broadcasted_iota(jnp.int32, q.shape, 1)
    qs = jnp.where((row < heads) == (lane < da), q, 0.0)
    qs_bf = qs.astype(BF16)

    s = jnp.concatenate([_mm_nt(qs_bf, k_ref[...]) for k_ref in k_refs], axis=1)
    s = s + jnp.where(j == last, blast_ref[...], bfar_ref[...])
    m_prev = m_ref[...]
    m_new = jnp.maximum(m_prev, jnp.max(s, axis=-1, keepdims=True))
    p = jnp.exp(s - m_new)
    alpha = jnp.exp(m_prev - m_new)
    l_ref[...] = alpha * l_ref[...] + jnp.sum(p, axis=-1, keepdims=True)
    rows_per_page = k_refs[0].shape[0]
    pv = acc_ref[...] * alpha
    for idx, v_ref in enumerate(v_refs):
        pv = pv + _mm(p[:, idx * rows_per_page:(idx + 1) * rows_per_page], v_ref[...])
    acc_ref[...] = pv
    m_ref[...] = m_new

    @pl.when(j == last)
    def _():
        s_new = jnp.sum(qs * kn_ref[...], axis=-1, keepdims=True) + bnew_ref[:, 0:1]
        m_prev = m_ref[...]
        m_fin = jnp.maximum(m_prev, s_new)
        p_new = jnp.exp(s_new - m_fin)
        alpha = jnp.exp(m_prev - m_fin)
        l_fin = alpha * l_ref[...] + p_new
        acc = alpha * acc_ref[...] + p_new * vn_ref[...]
        o12 = acc / l_fin
        lam = _lambda(lq1_ref, lk1_ref, lq2_ref, lk2_ref, lam_init)
        o = o12[0:heads] - lam * o12[heads:2 * heads]
        o_ref[...] = _rms(o, sw_ref[...], LN_EPS) * (1.0 - lam_init)


def _attention_decode(q, k_new, v_new, k_pool, v_pool, page_table, rel_bias, lam_vecs,
                      subln_w, heads, lam_init, pages):
    db = q.shape[0]
    n_pool, page, _, dv = k_pool.shape
    da = dv // 2
    n_pages = page_table.shape[1]
    past = n_pages * page
    rpp = page * heads
    span = pages * page
    assert n_pages % pages == 0 and span >= MAX_DISTANCE
    kp = k_pool.reshape(n_pool, rpp, dv)
    vp = v_pool.reshape(n_pool, rpp, dv)

    def two_maps(a):
        a = a.reshape(db, 1, heads, dv)
        return jnp.broadcast_to(a, (db, 2, heads, dv)).reshape(db, 2 * heads, dv)

    same = jnp.eye(heads, dtype=bool)

    def pair_bias(b):
        b = jnp.where(same[:, None, :], jnp.transpose(b)[:, :, None], NEG_BIG)
        return jnp.tile(b.reshape(heads, span * heads), (2, 1))

    rb = rel_bias.astype(F32)
    b_far = pair_bias(jnp.broadcast_to(rb[NUM_BUCKETS - 1], (span, heads)))
    b_last = pair_bias(rb[_t5_bucket(span - jnp.arange(span))])
    b_new = jnp.tile(jnp.broadcast_to(rb[0][:, None], (heads, LANES)), (2, 1))

    vec = lambda a: a.reshape(1, -1)
    const = lambda s, j, pt: (0, 0)

    def page_spec(idx):
        return pl.BlockSpec((None, rpp, dv), lambda s, j, pt: (pt[s, j * pages + idx], 0, 0))

    grid_spec = pltpu.PrefetchScalarGridSpec(
        num_scalar_prefetch=1,
        grid=(db, n_pages // pages),
        in_specs=([pl.BlockSpec((None, 2 * heads, dv), lambda s, j, pt: (s, 0, 0))] * 3
                  + [pl.BlockSpec((2 * heads, span * heads), const),
                     pl.BlockSpec((2 * heads, span * heads), const),
                     pl.BlockSpec((2 * heads, LANES), const),
                     pl.BlockSpec((1, da), const), pl.BlockSpec((1, da), const),
                     pl.BlockSpec((1, da), const), pl.BlockSpec((1, da), const),
                     pl.BlockSpec((1, dv), const)]
                  + [page_spec(idx) for idx in range(pages)] * 2),
        out_specs=pl.BlockSpec((None, heads, dv), lambda s, j, pt: (s, 0, 0)),
        scratch_shapes=[pltpu.VMEM((2 * heads, 1), F32), pltpu.VMEM((2 * heads, 1), F32),
                        pltpu.VMEM((2 * heads, dv), F32)],
    )
    out = pl.pallas_call(
        functools.partial(_attn_decode_kernel, pages=pages, heads=heads, da=da,
                          lam_init=lam_init),
        grid_spec=grid_spec,
        out_shape=jax.ShapeDtypeStruct((db, heads, dv), F32),
        compiler_params=_cparams("parallel", "arbitrary"),
        name="diff_attention_decode",
    )(page_table, two_maps(q), two_maps(k_new), two_maps(v_new), b_far, b_last, b_new,
      *[vec(a) for a in lam_vecs], vec(subln_w), *([kp] * pages), *([vp] * pages))
    return out.reshape(db, heads * dv)


def _unit_lower_inverse(lmats, eye, levels):
    if levels == 0:
        return [eye for _ in lmats]
    xs = [-m for m in lmats]
    tinvs = [eye + x for x in xs]
    if levels == 1:
        return tinvs
    n = eye.shape[0]
    rs = [_mm(x, x) for x in xs]
    for k in range(1, levels):
        if k < levels - 1:
            both = [_mm(jnp.concatenate([r, t], axis=0), r) for r, t in zip(rs, tinvs)]
            rs = [b[:n] for b in both]
            tinvs = [t + b[n:] for t, b in zip(tinvs, both)]
        else:
            tinvs = [t + _mm(t, r) for t, r in zip(tinvs, rs)]
    return tinvs


def _gdn_pre_kernel(x_ref, st_ref, cw_ref, ba_ref, gp_ref,
                    u_ref, w_ref, qd_ref, kd_ref, at_ref, eg_ref,
                    xe_ref, qn_ref, kn_ref, vv_ref, gb_ref, gc_ref, *, tt, t_real, heads, dk):
    c = GDN_CHUNK
    i = pl.program_id(1)
    taps = cw_ref.shape[0]
    halo = SUBLANES
    hk = heads * dk

    @pl.when(i == 0)
    def _():
        xe_ref[0:halo, :] = st_ref[...]

    xe_ref[halo:halo + tt, :] = x_ref[...]
    acc = cw_ref[taps - 1:taps, :] * x_ref[...]
    for j in range(taps - 1):
        acc = acc + cw_ref[j:j + 1, :] * xe_ref[pl.ds(halo - (taps - 1) + j, tt), :]
    xe_ref[0:halo, :] = xe_ref[tt:tt + halo, :]
    hcv = _silu(acc)

    valid = (i * tt + lax.broadcasted_iota(jnp.int32, (tt, 1), 0)) < t_real
    for h in range(heads):
        sl = slice(h * dk, (h + 1) * dk)
        qh = hcv[:, h * dk:(h + 1) * dk]
        kh = hcv[:, hk + h * dk:hk + (h + 1) * dk]
        qn_ref[:, sl] = qh * lax.rsqrt(jnp.sum(qh * qh, axis=-1, keepdims=True) + L2_EPS)
        kn = kh * lax.rsqrt(jnp.sum(kh * kh, axis=-1, keepdims=True) + L2_EPS)
        kn_ref[:, sl] = jnp.where(valid, kn, 0.0)
    vv_ref[...] = jnp.where(valid, hcv[:, 2 * hk:], 0.0)

    ba = ba_ref[...]
    lane = lax.broadcasted_iota(jnp.int32, ba.shape, 1)
    xa = ba + gp_ref[1:2, :]
    softplus = jnp.maximum(xa, 0.0) + jnp.log1p(jnp.exp(-jnp.abs(xa)))
    gates = jnp.where(lane < heads, _sigmoid(ba), -jnp.exp(gp_ref[0:1, :]) * softplus)
    gates = jnp.where(valid & (lane < 2 * heads), gates, 0.0)
    gb_ref[...] = gates

    tri_l = (lax.broadcasted_iota(jnp.int32, (c, c), 0)
             >= lax.broadcasted_iota(jnp.int32, (c, c), 1)).astype(F32)
    for ch in range(tt // c):
        gc_ref[ch * c:(ch + 1) * c, :] = _mm_hi(tri_l, gates[ch * c:(ch + 1) * c, :])

    ri = lax.broadcasted_iota(jnp.int32, (c, c), 0)
    ci = lax.broadcasted_iota(jnp.int32, (c, c), 1)
    incl = ri >= ci
    strict = ri > ci
    eye = (ri == ci).astype(F32)
    pick = (lax.broadcasted_iota(jnp.int32, (SUBLANES, LANES), 0)
            == lax.broadcasted_iota(jnp.int32, (SUBLANES, LANES), 1)).astype(F32)
    lane_c = lax.broadcasted_iota(jnp.int32, (c, LANES), 1)
    scale = dk ** -0.5
    c_valid = c if t_real >= c else t_real
    levels = (c_valid - 1).bit_length()

    n_chunks = tt // c
    group = 2 if n_chunks % 2 == 0 else 1

    def group_body(gi, carry):
        probs = []
        for cc in range(group):
            ch = gi * group + cc
            rows = pl.ds(pl.multiple_of(ch * c, c), c)
            gb = gb_ref[rows, :]
            gcc = gc_ref[rows, :]
            gc_rows = _mm_nt_hi(pick, gcc)
            for h in range(heads):
                probs.append(dict(
                    ch=ch, rows=rows, h=h, sl=slice(h * dk, (h + 1) * dk),
                    beta=_lane_pick(gb, lane_c, h),
                    gc_col=_lane_pick(gcc, lane_c, heads + h),
                    gc_row=gc_rows[heads + h:heads + h + 1, :]))
        for p in probs:
            p["decay"] = jnp.where(
                incl, jnp.exp(jnp.where(incl, p["gc_col"] - p["gc_row"], 0.0)), 0.0)
            p["k"] = kn_ref[p["rows"], p["sl"]]
            p["k_beta"] = p["k"] * p["beta"]
        kk = [_mm_nt(p["k_beta"], p["k"]) for p in probs]
        lmats = [jnp.where(strict, m * p["decay"], 0.0) for m, p in zip(kk, probs)]
        tinvs = _unit_lower_inverse(lmats, eye, levels)
        for p in probs:
            p["egc"] = jnp.exp(p["gc_col"])
            p["q"] = qn_ref[p["rows"], p["sl"]] * scale
        us = [_mm(t, vv_ref[p["rows"], p["sl"]] * p["beta"]) for t, p in zip(tinvs, probs)]
        ws = [_mm(t, p["k_beta"] * p["egc"]) for t, p in zip(tinvs, probs)]
        ats = [_mm_nt(p["q"], p["k"]) for p in probs]
        for p, u, w, at in zip(probs, us, ws, ats):
            rows, sl, h = p["rows"], p["sl"], p["h"]
            g_last = p["gc_col"][c - 1:c, :]
            u_ref[rows, sl] = u
            w_ref[rows, sl] = w.astype(w_ref.dtype)
            at_ref[rows, h * c:(h + 1) * c] = (at * p["decay"]).astype(at_ref.dtype)
            kd_ref[rows, sl] = p["k"] * jnp.exp(g_last - p["gc_col"])
            qd_ref[rows, sl] = (p["q"] * p["egc"]).astype(qd_ref.dtype)
            eg_ref[p["ch"], :, sl] = jnp.broadcast_to(jnp.exp(g_last), (SUBLANES, dk))
        return carry

    lax.fori_loop(0, n_chunks // group, group_body, 0)


def _gdn_scan_kernel(u_ref, w_ref, qd_ref, kd_ref, at_ref, eg_ref, z_ref, s0_ref, nw_ref,
                     o_ref, sout_ref, s_ref, kt_ref, *, tt, heads, dk):
    c = GDN_CHUNK
    i = pl.program_id(1)
    nb = u_ref.shape[0]

    @pl.when(i == 0)
    def _():
        s_ref[...] = s0_ref[...]

    for b in range(nb):
        for ch in range(tt // c):
            for h in range(heads):
                kt_ref[b, ch * heads + h] = (
                    kd_ref[b, ch * c:(ch + 1) * c, h * dk:(h + 1) * dk].T.astype(BF16))

    chains = [(b, h) for b in range(nb) for h in range(heads)]
    cols = lambda h: slice(h * dk, (h + 1) * dk)

    def chunk_body(ch, carry):
        rows = pl.ds(pl.multiple_of(ch * c, c), c)
        s_bf = [s_ref[b, h].astype(BF16) for b, h in chains]
        w_s = [_mm(w_ref[b, rows, cols(h)], s) for (b, h), s in zip(chains, s_bf)]
        q_s = [_mm(qd_ref[b, rows, cols(h)], s) for (b, h), s in zip(chains, s_bf)]
        v_new = [(u_ref[b, rows, cols(h)] - ws).astype(BF16) for (b, h), ws in zip(chains, w_s)]
        a_v = [_mm(at_ref[b, rows, h * c:(h + 1) * c], v) for (b, h), v in zip(chains, v_new)]
        k_v = [_mm(kt_ref[b, ch * heads + h], v) for (b, h), v in zip(chains, v_new)]
        for (b, h), qs, av, kv in zip(chains, q_s, a_v, k_v):
            s_ref[b, h] = s_ref[b, h] * eg_ref[b, ch, 0:1, cols(h)] + kv
            o_ref[b, rows, cols(h)] = (_rms(qs + av, nw_ref[...], RMS_EPS)
                                       * _silu(z_ref[b, rows, cols(h)]))
        return carry

    lax.fori_loop(0, tt // c, chunk_body, 0)

    @pl.when(i == pl.num_programs(1) - 1)
    def _():
        sout_ref[...] = s_ref[...]


def _gated_deltanet(qkv, ba, z, conv_state, s0, conv_w, a_log, dt_bias, norm_w,
                    batch, seq_pad, t_real, heads, tt, scan_batch, scan_tt):
    cq = qkv.shape[1]
    hk = cq // 3
    dk = hk // heads
    c = GDN_CHUNK
    assert c & (c - 1) == 0 and batch % scan_batch == 0
    nt = seq_pad // tt
    nc = tt // c
    taps = conv_w.shape[0]
    st = jnp.pad(conv_state, ((0, 0), (SUBLANES - (taps - 1), 0), (0, 0)))
    gp = jnp.zeros((SUBLANES, LANES), F32)
    gp = gp.at[0, heads:2 * heads].set(a_log.astype(F32))
    gp = gp.at[1, heads:2 * heads].set(dt_bias.astype(F32))
    row_blk = lambda b, i: (b * nt + i, 0)
    const = lambda b, i: (0, 0)
    tok = lambda width: pl.BlockSpec((tt, width), row_blk)
    rows = batch * seq_pad
    u, w, qd, kd, at, eg = pl.pallas_call(
        functools.partial(_gdn_pre_kernel, tt=tt, t_real=t_real, heads=heads, dk=dk),
        grid=(batch, nt),
        in_specs=[tok(cq),
                  pl.BlockSpec((None, SUBLANES, cq), lambda b, i: (b, 0, 0)),
                  pl.BlockSpec((taps, cq), const),
                  tok(LANES),
                  pl.BlockSpec((SUBLANES, LANES), const)],
        out_specs=[tok(hk), tok(hk), tok(hk), tok(hk), tok(heads * c),
                   pl.BlockSpec((None, nc, SUBLANES, hk), lambda b, i: (b, i, 0, 0))],
        out_shape=[jax.ShapeDtypeStruct((rows, hk), dt) for dt in (F32, BF16, BF16, F32)]
                  + [jax.ShapeDtypeStruct((rows, heads * c), BF16),
                     jax.ShapeDtypeStruct((batch, nt * nc, SUBLANES, hk), F32)],
        scratch_shapes=[pltpu.VMEM((tt + SUBLANES, cq), F32), pltpu.VMEM((tt, hk), F32),
                        pltpu.VMEM((tt, hk), F32), pltpu.VMEM((tt, hk), F32),
                        pltpu.VMEM((tt, LANES), F32), pltpu.VMEM((tt, LANES), F32)],
        compiler_params=_cparams("parallel", "arbitrary"),
        name="gdn_chunk_prepare",
    )(qkv, st, conv_w, ba, gp)

    nb, stt = scan_batch, scan_tt
    snc = stt // c
    seq3 = lambda a: a.reshape(batch, seq_pad, a.shape[-1])
    blk3 = lambda width: pl.BlockSpec((nb, stt, width), lambda g, i: (g, i, 0))
    state_spec = pl.BlockSpec((nb, heads, dk, dk), lambda g, i: (g, 0, 0, 0))
    o, s_new = pl.pallas_call(
        functools.partial(_gdn_scan_kernel, tt=stt, heads=heads, dk=dk),
        grid=(batch // nb, seq_pad // stt),
        in_specs=[blk3(hk), blk3(hk), blk3(hk), blk3(hk), blk3(heads * c),
                  pl.BlockSpec((nb, snc, SUBLANES, hk), lambda g, i: (g, i, 0, 0)),
                  blk3(hk), state_spec,
                  pl.BlockSpec((1, dk), const)],
        out_specs=[blk3(hk), state_spec],
        out_shape=[jax.ShapeDtypeStruct((batch, seq_pad, hk), F32),
                   jax.ShapeDtypeStruct((batch, heads, dk, dk), F32)],
        scratch_shapes=[pltpu.VMEM((nb, heads, dk, dk), F32),
                        pltpu.VMEM((nb, snc * heads, dk, c), BF16)],
        compiler_params=_cparams("parallel", "arbitrary"),
        name="gdn_chunk_scan",
    )(seq3(u), seq3(w), seq3(qd), seq3(kd), seq3(at), eg, seq3(z), s0, norm_w.reshape(1, dk))
    return o.reshape(rows, hk), s_new


def _conv_module_kernel(x_ref, buf_ref, nw_ref, w1_ref, b1_ref, w_ref, b_ref, g_ref, bb_ref,
                        o_ref, tail_ref, xe_ref, xs_ref, *, tt, rb):
    i = pl.program_id(1)
    last_tile = pl.num_programs(1) - 2
    taps = w_ref.shape[0]
    halo = buf_ref.shape[0]
    first = halo - (taps - 1)
    sub = w_ref.shape[1]
    dc = o_ref.shape[1]

    @pl.when(i == 0)
    def _():
        xe_ref[0:tt, :] = jnp.zeros((tt, dc), F32)
        xe_ref[tt:tt + halo, :] = buf_ref[...]

    xn = _rms(x_ref[...], nw_ref[...], RMS_EPS).astype(BF16)
    a = jnp.dot(xn, w1_ref[:, :dc], preferred_element_type=F32) + b1_ref[:, :dc]
    g = jnp.dot(xn, w1_ref[:, dc:], preferred_element_type=F32) + b1_ref[:, dc:]
    xe_ref[halo + tt:halo + 2 * tt, :] = a * _sigmoid(g)

    span = tt + halo - sub
    for ph in range(1, sub):
        xs_ref[ph - 1, 0:span, :] = xe_ref[pl.ds(ph, span), :]

    def window(row, size):
        blk, ph = divmod(row, sub)
        if ph == 0:
            return xe_ref[pl.ds(row, size), :]
        return xs_ref[ph - 1, pl.ds(blk * sub, size), :]

    def weight(j):
        return w_ref[j] if rb == sub else jnp.tile(w_ref[j], (rb // sub, 1))

    for r0 in range(0, tt, rb):
        acc = b_ref[...] + weight(0) * window(first + r0, rb)
        for j in range(1, taps):
            acc = acc + weight(j) * window(first + r0 + j, rb)
        mu = jnp.mean(acc, axis=-1, keepdims=True)
        xc = acc - mu
        var = jnp.mean(xc * xc, axis=-1, keepdims=True)
        y = xc * lax.rsqrt(var + LN_EPS) * g_ref[...] + bb_ref[...]
        o_ref[pl.ds(r0, rb), :] = _silu(y)

    xe_ref[0:halo, :] = xe_ref[tt:tt + halo, :]
    xe_ref[halo:halo + tt, :] = xe_ref[halo + tt:halo + 2 * tt, :]

    @pl.when(i == last_tile)
    def _():
        rows = tail_ref.shape[0]
        tail_ref[...] = xe_ref[halo + 2 * tt - rows:halo + 2 * tt, :]


def _conv_module(x, buf, norm_w, w_pw1_bf16, b_pw1, w_dw, b_dw, ln_g, ln_b, batch, seq_pad, tt):
    d = x.shape[1]
    ch = w_dw.shape[1]
    taps = w_dw.shape[0]
    halo = -(-(taps - 1) // SUBLANES) * SUBLANES
    bufp = jnp.pad(buf, ((0, 0), (halo - (taps - 1), 0), (0, 0)))
    nt = seq_pad // tt
    rb = min(tt, 2 * SUBLANES)
    tail_rows = min(tt, halo)
    const = lambda b, i: (0, 0)
    vec = lambda a: a.reshape(1, -1)
    w_rep = jnp.broadcast_to(w_dw[:, None, :], (taps, SUBLANES, ch))
    return pl.pallas_call(
        functools.partial(_conv_module_kernel, tt=tt, rb=rb),
        grid=(batch, nt + 1),
        in_specs=[pl.BlockSpec((tt, d), lambda b, i: (b * nt + jnp.minimum(i, nt - 1), 0)),
                  pl.BlockSpec((None, halo, ch), lambda b, i: (b, 0, 0)),
                  pl.BlockSpec((1, d), const),
                  pl.BlockSpec((d, 2 * ch), const),
                  pl.BlockSpec((1, 2 * ch), const),
                  pl.BlockSpec((taps, SUBLANES, ch), lambda b, i: (0, 0, 0)),
                  pl.BlockSpec((1, ch), const), pl.BlockSpec((1, ch), const),
                  pl.BlockSpec((1, ch), const)],
        out_specs=[pl.BlockSpec((tt, ch), lambda b, i: (b * nt + jnp.maximum(i - 1, 0), 0)),
                   pl.BlockSpec((None, tail_rows, ch), lambda b, i: (b, 0, 0))],
        out_shape=[jax.ShapeDtypeStruct((batch * seq_pad, ch), F32),
                   jax.ShapeDtypeStruct((batch, tail_rows, ch), F32)],
        scratch_shapes=[pltpu.VMEM((2 * tt + halo, ch), F32),
                        pltpu.VMEM((SUBLANES - 1, tt + halo - SUBLANES, ch), F32)],
        compiler_params=_cparams("parallel", "arbitrary"),
        name="conv_module",
    )(x, bufp, vec(norm_w), w_pw1_bf16, vec(b_pw1), w_rep, vec(b_dw), vec(ln_g), vec(ln_b))


def _pad_time(a, batch, t, t_pad):
    a = a.reshape(batch, t, a.shape[-1])
    return jnp.pad(a, ((0, 0), (0, t_pad - t), (0, 0))).reshape(batch * t_pad, a.shape[-1])


def kernel(x_prompt, x_sample, cache_attn_k, cache_attn_v, page_table, state_gdn_conv, state_gdn_s, state_conv_buf, norm_mix, norm_ffn, norm_final, w_in, w_out, gdn_conv_w, gdn_A_log, gdn_dt_bias, gdn_norm_w, lam_q1, lam_k1, lam_q2, lam_k2, diff_subln_w, rel_bias, conv_w_pw1, conv_b_pw1, conv_w_dw, conv_b_dw, conv_ln_g, conv_ln_b, conv_w_pw2, conv_b_pw2, ffn_w_gate, ffn_w_up, ffn_w_down):
    bp, seq, d = x_prompt.shape
    db, dseq, _ = x_sample.shape
    assert dseq == 1
    depth = norm_mix.shape[0]
    ha, dva = cache_attn_k.shape[3], cache_attn_v.shape[4]
    hb, dkb = state_gdn_s.shape[2], state_gdn_s.shape[3]
    c_qkv = state_gdn_conv.shape[3]
    gdn_taps = gdn_conv_w.shape[1]
    conv_taps = conv_w_dw.shape[1]
    d_ff = ffn_w_gate.shape[2]
    mp = bp * seq
    assert 2 * hb <= LANES

    tm, tm_ffn = 256, 512
    t_attn = min(512, seq)
    tt_gdn = min(512, seq)
    tt_scan = min(256, seq)
    tt_conv = min(256, seq)
    sample_pad_gdn, sample_pad_conv = GDN_CHUNK, SUBLANES
    assert tt_conv >= conv_taps - 1 and seq >= gdn_taps - 1

    xp = x_prompt.reshape(mp, d)
    xs = x_sample.reshape(db, d)
    row = lambda a: a.reshape(1, -1)

    sizes = (ha * dva, ha * dva, ha * dva, c_qkv, hb * dkb, LANES)
    offs = [0]
    for s_ in sizes:
        offs.append(offs[-1] + s_)
    groups = tuple(zip(offs[:-1], sizes))
    inproj_outs = [(wd, 1) for wd in sizes] + [(dva, ha), (dva, ha)]

    def last_rows(a, n):
        return jnp.stack([a[(b + 1) * seq - n:(b + 1) * seq] for b in range(bp)])

    k_p, v_p, k_s, v_s, gc_p, gc_s, gs_p, gs_s, cb_p, cb_s = ([] for _ in range(10))
    for layer in range(depth):
        if layer % 2 == 0:
            e = layer // 2
            lam_init = 0.8 - 0.6 * math.exp(-0.3 * layer)
            w_in_e = jnp.pad(w_in[e], ((0, 0), (0, offs[-1] - w_in.shape[2]))).astype(BF16)
            outs_p, outs_s = _token_call(
                functools.partial(_inproj_body, groups=groups, heads=ha, kv_groups=(1, 2)),
                [xp], [xs], [_whole(row(norm_mix[layer])), _whole(w_in_e)],
                inproj_outs, tm, "in_projection")
            qa_p, ka_p, va_p, qkv_p, z_p, ba_p, k4_p, v4_p = outs_p
            qa_s, ka_s, va_s, qkv_s, z_s, ba_s, k4_s, v4_s = outs_s
            lam_vecs = (lam_q1[e], lam_k1[e], lam_q2[e], lam_k2[e])

            oa_p = _attention_prompt(qa_p, ka_p, va_p, rel_bias, lam_vecs, diff_subln_w[e],
                                     bp, seq, ha, lam_init, t=t_attn)
            oa_s = _attention_decode(qa_s, ka_s, va_s, cache_attn_k[e], cache_attn_v[e],
                                     page_table, rel_bias, lam_vecs, diff_subln_w[e],
                                     ha, lam_init, pages=min(32, page_table.shape[1]))

            gdn_w = (gdn_conv_w[e], gdn_A_log[e], gdn_dt_bias[e], gdn_norm_w[e])
            ob_p, s_p = _gated_deltanet(
                qkv_p, ba_p, z_p, jnp.zeros((bp, gdn_taps - 1, c_qkv), F32),
                jnp.zeros((bp, hb, dkb, dkb), F32), *gdn_w, bp, seq, seq, hb,
                tt=tt_gdn, scan_batch=bp, scan_tt=tt_scan)
            cpad = sample_pad_gdn
            ob_s, s_s = _gated_deltanet(
                _pad_time(qkv_s, db, 1, cpad), _pad_time(ba_s, db, 1, cpad),
                _pad_time(z_s, db, 1, cpad), state_gdn_conv[e], state_gdn_s[e], *gdn_w,
                db, cpad, 1, hb, tt=cpad, scan_batch=min(db, 8), scan_tt=cpad)
            ob_s = ob_s.reshape(db, cpad, hb * dkb)[:, 0]

            w_o = w_out[e].astype(BF16)
            mix_p, mix_s = [oa_p, ob_p], [oa_s, ob_s]
            mix_w = [_whole(w_o[:ha * dva]), _whole(w_o[ha * dva:]),
                     _whole(jnp.zeros((1, d), F32))]

            k_p.append(k4_p.reshape(bp, seq, ha, dva))
            v_p.append(v4_p.reshape(bp, seq, ha, dva))
            k_s.append(k4_s.reshape(db, 1, ha, dva))
            v_s.append(v4_s.reshape(db, 1, ha, dva))
            gc_p.append(last_rows(qkv_p, gdn_taps - 1))
            gc_s.append(jnp.concatenate([state_gdn_conv[e], qkv_s.reshape(db, 1, c_qkv)],
                                        axis=1)[:, -(gdn_taps - 1):])
            gs_p.append(s_p)
            gs_s.append(s_s)
        else:
            cidx = layer // 2
            dconv = conv_w_dw.shape[2]
            conv_w = (norm_mix[layer], conv_w_pw1[cidx].astype(BF16), conv_b_pw1[cidx],
                      conv_w_dw[cidx], conv_b_dw[cidx], conv_ln_g[cidx], conv_ln_b[cidx])
            hc_p, tail_p = _conv_module(xp, jnp.zeros((bp, conv_taps - 1, dconv), F32), *conv_w,
                                        bp, seq, tt=tt_conv)
            spad = sample_pad_conv
            hc_s, tail_s = _conv_module(_pad_time(xs, db, 1, spad), state_conv_buf[cidx],
                                        *conv_w, db, spad, tt=spad)
            hc_s = hc_s.reshape(db, spad, dconv)[:, 0]
            mix_p, mix_s = [hc_p], [hc_s]
            mix_w = [_whole(conv_w_pw2[cidx].astype(BF16)), _whole(row(conv_b_pw2[cidx]))]
            cb_p.append(tail_p[:, tail_p.shape[1] - (conv_taps - 1):])
            cb_s.append(jnp.concatenate([state_conv_buf[cidx], tail_s[:, 0:1]],
                                        axis=1)[:, -(conv_taps - 1):])
        tf = d_ff // 2
        (xp,), (xs,) = _token_call(
            functools.partial(_mix_ffn_body, final_norm=(layer == depth - 1)),
            mix_p + [xp], mix_s + [xs],
            [_whole(row(norm_ffn[layer])),
             (ffn_w_gate[layer].astype(BF16), (d, tf), lambda k: (0, k)),
             (ffn_w_up[layer].astype(BF16), (d, tf), lambda k: (0, k)),
             (ffn_w_down[layer].astype(BF16), (tf, d), lambda k: (k, 0)),
             _whole(row(norm_final))] + mix_w,
            [(d, 1)], tm_ffn, "mixer_out_swiglu_ffn", inner=d_ff // tf,
            scratch=(pltpu.VMEM((tm_ffn, d), BF16), pltpu.VMEM((tm_ffn, d), F32)))

    y_prompt = xp.reshape(bp, seq, d)
    y_sample = xs.reshape(db, 1, d)
    return (y_prompt, y_sample, jnp.stack(k_p), jnp.stack(v_p), jnp.stack(k_s), jnp.stack(v_s),
            jnp.stack(gc_p), jnp.stack(gc_s), jnp.stack(gs_p), jnp.stack(gs_s),
            jnp.stack(cb_p), jnp.stack(cb_s))
```

```python
import functools
import math

import jax
import jax.numpy as jnp
from jax import lax
from jax.experimental import pallas as pl
from jax.experimental.pallas import tpu as pltpu

F32 = jnp.float32
BF16 = jnp.bfloat16

RMS_EPS = 1e-6
LN_EPS = 1e-5
L2_EPS = 1e-6
NUM_BUCKETS = 32
MAX_DISTANCE = 128
GDN_CHUNK = 64
NEG_BIG = -1e30
LOG2E = math.log2(math.e)
LANES = 128
SUBLANES = 8
VMEM_LIMIT = 48 * 1024 * 1024

_HI = lax.Precision.HIGHEST


def _cparams(*sem):
    return pltpu.CompilerParams(dimension_semantics=sem, vmem_limit_bytes=VMEM_LIMIT)


def _mm(a, b):
    return jnp.dot(a.astype(BF16), b.astype(BF16), preferred_element_type=F32)


def _mm_nt(a, b):
    return lax.dot_general(a.astype(BF16), b.astype(BF16), (((1,), (1,)), ((), ())),
                           preferred_element_type=F32)


def _mm_tn(a, b):
    return lax.dot_general(a.astype(BF16), b.astype(BF16), (((0,), (0,)), ((), ())),
                           preferred_element_type=F32)


def _mm_hi(a, b):
    return jnp.dot(a, b, precision=_HI, preferred_element_type=F32)


def _mm_nt_hi(a, b):
    return lax.dot_general(a, b, (((1,), (1,)), ((), ())), precision=_HI,
                           preferred_element_type=F32)


def _rms(x, w, eps):
    return x * lax.rsqrt(jnp.mean(x * x, axis=-1, keepdims=True) + eps) * w


def _sigmoid(x):
    return 1.0 / (1.0 + jnp.exp(-x))


def _silu(x):
    return x * _sigmoid(x)


def _lane_pick(x, lane_idx, k):
    return jnp.sum(jnp.where(lane_idx == k, x, 0.0), axis=-1, keepdims=True)


def _token_call(body, prompt_in, sample_in, shared, outs, tm, name, inner=1, scratch=()):
    mp, ms = prompt_in[0].shape[0], sample_in[0].shape[0]
    assert mp % tm == 0
    n = mp // tm
    n_in, n_sh, n_out = len(prompt_in), len(shared), len(outs)

    def kern(*refs):
        p_in, s_in = refs[:n_in], refs[n_in:2 * n_in]
        sh = refs[2 * n_in:2 * n_in + n_sh]
        o0 = 2 * n_in + n_sh
        p_out, s_out = refs[o0:o0 + n_out], refs[o0 + n_out:o0 + 2 * n_out]
        scr = refs[o0 + 2 * n_out:]
        i = pl.program_id(0)

        @pl.when(i < n)
        def _():
            body(p_in, sh, p_out, scr)

        @pl.when(i == n)
        def _():
            body(s_in, sh, s_out, scr)

    prow = lambda i, k: (jnp.minimum(i, n - 1), 0)
    srow = lambda i, k: (0, 0)
    in_specs = ([pl.BlockSpec((tm, a.shape[1]), prow) for a in prompt_in]
                + [pl.BlockSpec((ms, a.shape[1]), srow) for a in sample_in]
                + [pl.BlockSpec(bs, (lambda i, k, f=f: f(k))) for _, bs, f in shared])
    out_specs = ([pl.BlockSpec((tm * r, w), prow) for w, r in outs]
                 + [pl.BlockSpec((ms * r, w), srow) for w, r in outs])
    out_shape = ([jax.ShapeDtypeStruct((mp * r, w), F32) for w, r in outs]
                 + [jax.ShapeDtypeStruct((ms * r, w), F32) for w, r in outs])
    res = pl.pallas_call(
        kern, grid=(n + 1, inner), in_specs=in_specs, out_specs=out_specs, out_shape=out_shape,
        scratch_shapes=list(scratch), compiler_params=_cparams("arbitrary", "arbitrary"),
        name=name,
    )(*prompt_in, *sample_in, *[a for a, _, _ in shared])
    return res[:n_out], res[n_out:]


def _whole(a):
    return (a, a.shape, lambda k, nd=a.ndim: (0,) * nd)


def _inproj_body(ins, sh, outs, scr, *, groups, heads, kv_groups):
    x_ref, = ins
    nw_ref, w_ref = sh
    xn = _rms(x_ref[...], nw_ref[...], RMS_EPS).astype(BF16)
    rows = x_ref.shape[0]
    n_g = len(groups)
    for g, (o_ref, (off, width)) in enumerate(zip(outs[:n_g], groups)):
        y = jnp.dot(xn, w_ref[:, off:off + width], preferred_element_type=F32)
        o_ref[...] = y
        if g in kv_groups:
            o4_ref = outs[n_g + kv_groups.index(g)]
            dv = width // heads
            for h in range(heads):
                o4_ref[pl.ds(h, rows, stride=heads), :] = y[:, h * dv:(h + 1) * dv]


def _mix_ffn_body(ins, sh, outs, scr, *, final_norm):
    res_ref = ins[-1]
    nw_ref, wg_ref, wu_ref, wd_ref, fw_ref = sh[:5]
    o_ref, = outs
    xn_ref, acc_ref = scr
    rows = res_ref.shape[0]
    k = pl.program_id(1)

    @pl.when(k == 0)
    def _():
        x = res_ref[...] + sh[-1][...]
        for a_ref, w_ref in zip(ins[:-1], sh[5:-1]):
            x = x + _mm(a_ref[...], w_ref[...])
        xn_ref[0:rows, :] = _rms(x, nw_ref[...], RMS_EPS).astype(BF16)
        acc_ref[0:rows, :] = x

    xn = xn_ref[0:rows, :]
    g = jnp.dot(xn, wg_ref[...], preferred_element_type=F32)
    u = jnp.dot(xn, wu_ref[...], preferred_element_type=F32)
    acc_ref[0:rows, :] += _mm(_silu(g) * u, wd_ref[...])

    @pl.when(k == pl.num_programs(1) - 1)
    def _():
        y = acc_ref[0:rows, :]
        if final_norm:
            y = _rms(y, fw_ref[...], RMS_EPS)
        o_ref[...] = y


def _t5_bucket(n):
    max_exact = NUM_BUCKETS // 2
    nf = jnp.maximum(n, 1).astype(F32)
    large = max_exact + (jnp.log(nf / max_exact) / math.log(MAX_DISTANCE / max_exact)
                         * (NUM_BUCKETS - max_exact)).astype(jnp.int32)
    large = jnp.minimum(large, NUM_BUCKETS - 1)
    return jnp.where(n < max_exact, n, large)


def _lambda(lq1_ref, lk1_ref, lq2_ref, lk2_ref, lam_init):
    s1 = jnp.sum(lq1_ref[...] * lk1_ref[...], axis=-1, keepdims=True)
    s2 = jnp.sum(lq2_ref[...] * lk2_ref[...], axis=-1, keepdims=True)
    return jnp.exp(s1) - jnp.exp(s2) + lam_init


def _attn_prompt_kernel(q_ref, k_ref, v_ref, bias_ref, lq1_ref, lk1_ref, lq2_ref, lk2_ref,
                        sw_ref, o_ref, qt_ref, kb_ref, vt_ref, m_ref, acc_ref, sa_ref, sb_ref, *,
                        t, da, lam_init):
    i = pl.program_id(2)
    nblk = kb_ref.shape[0]
    dv = v_ref.shape[1]

    @pl.when(i == 0)
    def _():
        for c in range(nblk):
            kb_ref[c] = k_ref[c * t:(c + 1) * t, :].astype(BF16)
            vt_ref[c, 0:dv, :] = v_ref[c * t:(c + 1) * t, :].T.astype(BF16)
            vt_ref[c, dv:, :] = jnp.ones((vt_ref.shape[1] - dv, t), BF16)

    qt = (q_ref[...] * (da ** -0.5 * LOG2E)).T
    first_map = lax.broadcasted_iota(jnp.int32, qt.shape, 0) < da
    qt_ref[:, 0:t] = jnp.where(first_map, qt, 0.0).astype(BF16)
    qt_ref[:, t:2 * t] = jnp.where(first_map, 0.0, qt).astype(BF16)
    m_ref[...] = jnp.full(m_ref.shape, NEG_BIG, F32)
    acc_ref[...] = jnp.zeros(acc_ref.shape, F32)

    def scores(j, buf_ref):
        buf_ref[...] = jnp.dot(kb_ref[j], qt_ref[...], preferred_element_type=F32)

    def update(j, buf_ref, bias):
        s = buf_ref[...]
        if bias is not None:
            s = s + jnp.concatenate([bias, bias], axis=1)
        m_prev = m_ref[...]
        m_new = jnp.maximum(m_prev, jnp.max(s, axis=0, keepdims=True))
        p = jnp.exp2(s - m_new).astype(BF16)
        acc_ref[...] = (acc_ref[...] * jnp.exp2(m_prev - m_new)
                        + jnp.dot(vt_ref[j], p, preferred_element_type=F32))
        m_ref[...] = m_new

    prev_bias, diag_bias = bias_ref.at[0], bias_ref.at[1]
    scores(0, sa_ref)

    def far_pair(jj, carry):
        j = 2 * jj
        scores(j + 1, sb_ref)
        update(j, sa_ref, None)
        scores(j + 2, sa_ref)
        update(j + 1, sb_ref, None)
        return carry

    n_far = i - 1
    lax.fori_loop(0, n_far // 2, far_pair, 0)

    @pl.when(i % 2 == 1)
    def _():
        scores(i, sb_ref)
        update(i - 1, sa_ref, prev_bias[...])
        update(i, sb_ref, diag_bias[...])

    @pl.when((i % 2 == 0) & (i >= 2))
    def _():
        scores(i - 1, sb_ref)
        update(i - 2, sa_ref, None)
        scores(i, sa_ref)
        update(i - 1, sb_ref, prev_bias[...])
        update(i, sa_ref, diag_bias[...])

    @pl.when(i == 0)
    def _():
        update(0, sa_ref, diag_bias[...])

    acc = acc_ref[...]
    o12 = acc[0:dv] / acc[dv:dv + 1]
    lam = _lambda(lq1_ref, lk1_ref, lq2_ref, lk2_ref, lam_init)
    o = (o12[:, 0:t] - lam * o12[:, t:2 * t]).T
    o_ref[...] = _rms(o, sw_ref[...], LN_EPS) * (1.0 - lam_init)


def _toeplitz(r, t):
    h, period = r.shape
    flat = jnp.tile(r, (1, t))[:, :t * (period - 1)]
    return flat.reshape(h, t, period - 1)[:, :, :t]


def _prompt_bias_tiles(rel_bias, t):
    d = MAX_DISTANCE
    assert t % d == 0
    nb = t // d
    far = rel_bias[NUM_BUCKETS - 1]
    b1 = jnp.transpose(rel_bias[_t5_bucket(jnp.arange(d))] - far).astype(F32) * LOG2E
    zero = jnp.zeros_like(b1)
    neg = jnp.full_like(b1, NEG_BIG)
    tz = _toeplitz(jnp.concatenate([b1, zero, neg, neg], axis=1), 2 * d)
    g0, g1 = tz[:, :d, :d], tz[:, :d, d:]
    zero_blk, neg_blk = jnp.zeros_like(g0), jnp.full_like(g0, NEG_BIG)

    def diag_block(r, c):
        return neg_blk if c < r else g0 if c == r else g1 if c == r + 1 else zero_blk

    diag = jnp.block([[diag_block(r, c) for c in range(nb)] for r in range(nb)])
    prev = jnp.block([[g1 if (r, c) == (nb - 1, 0) else zero_blk for c in range(nb)]
                      for r in range(nb)])
    return jnp.stack([prev, diag], axis=1)


def _attention_prompt(q, k, v, rel_bias, lam_vecs, subln_w, batch, seq, heads, lam_init, t):
    dv = q.shape[1] // heads
    da = dv // 2
    nq = seq // t
    bias = _prompt_bias_tiles(rel_bias, t)
    vec = lambda a: a.reshape(1, -1)
    const = lambda b, h, i: (0, 0)
    return pl.pallas_call(
        functools.partial(_attn_prompt_kernel, t=t, da=da, lam_init=lam_init),
        grid=(batch, heads, nq),
        in_specs=[pl.BlockSpec((t, dv), lambda b, h, i: (b * nq + i, h)),
                  pl.BlockSpec((seq, dv), lambda b, h, i: (b, h)),
                  pl.BlockSpec((seq, dv), lambda b, h, i: (b, h)),
                  pl.BlockSpec((None, 2, t, t), lambda b, h, i: (h, 0, 0, 0)),
                  pl.BlockSpec((1, da), const), pl.BlockSpec((1, da), const),
                  pl.BlockSpec((1, da), const), pl.BlockSpec((1, da), const),
                  pl.BlockSpec((1, dv), const)],
        out_specs=pl.BlockSpec((t, dv), lambda b, h, i: (b * nq + i, h)),
        out_shape=jax.ShapeDtypeStruct((batch * seq, heads * dv), F32),
        scratch_shapes=[pltpu.VMEM((dv, 2 * t), BF16), pltpu.VMEM((nq, t, dv), BF16),
                        pltpu.VMEM((nq, dv + 2 * SUBLANES, t), BF16),
                        pltpu.VMEM((1, 2 * t), F32),
                        pltpu.VMEM((dv + 2 * SUBLANES, 2 * t), F32),
                        pltpu.VMEM((t, 2 * t), F32), pltpu.VMEM((t, 2 * t), F32)],
        compiler_params=_cparams("parallel", "parallel", "arbitrary"),
        name="diff_attention_prompt",
    )(q, k, v, bias, *[vec(a) for a in lam_vecs], vec(subln_w))


def _attn_decode_kernel(pt_ref, q_ref, kn_ref, vn_ref, bfar_ref, blast_ref, bnew_ref,
                        lq1_ref, lk1_ref, lq2_ref, lk2_ref, sw_ref, *rest,
                        pages, heads, da, lam_init):
    del pt_ref
    k_refs = rest[:pages]
    v_refs = rest[pages:2 * pages]
    o_ref, m_ref, l_ref, acc_ref = rest[2 * pages:]
    j = pl.program_id(1)
    last = pl.num_programs(1) - 1

    @pl.when(j == 0)
    def _():
        m_ref[...] = jnp.full(m_ref.shape, NEG_BIG, F32)
        l_ref[...] = jnp.zeros(l_ref.shape, F32)
        acc_ref[...] = jnp.zeros(acc_ref.shape, F32)

    q = q_ref[...] * (da ** -0.5)
    row = lax.broadcasted_iota(jnp.int32, q.shape, 0)
    lane = lax.broadcasted_iota(jnp.int32, q.shape, 1)
    qs = jnp.where((row < heads) == (lane < da), q, 0.0)
    qs_bf = qs.astype(BF16)

    s = jnp.concatenate([_mm_nt(qs_bf, k_ref[...]) for k_ref in k_refs], axis=1)
    s = s + jnp.where(j == last, blast_ref[...], bfar_ref[...])
    m_prev = m_ref[...]
    m_new = jnp.maximum(m_prev, jnp.max(s, axis=-1, keepdims=True))
    p = jnp.exp(s - m_new)
    alpha = jnp.exp(m_prev - m_new)
    l_ref[...] = alpha * l_ref[...] + jnp.sum(p, axis=-1, keepdims=True)
    rows_per_page = k_refs[0].shape[0]
    pv = acc_ref[...] * alpha
    for idx, v_ref in enumerate(v_refs):
        pv = pv + _mm(p[:, idx * rows_per_page:(idx + 1) * rows_per_page], v_ref[...])
    acc_ref[...] = pv
    m_ref[...] = m_new

    @pl.when(j == last)
    def _():
        s_new = jnp.sum(qs * kn_ref[...], axis=-1, keepdims=True) + bnew_ref[:, 0:1]
        m_prev = m_ref[...]
        m_fin = jnp.maximum(m_prev, s_new)
        p_new = jnp.exp(s_new - m_fin)
        alpha = jnp.exp(m_prev - m_fin)
        l_fin = alpha * l_ref[...] + p_new
        acc = alpha * acc_ref[...] + p_new * vn_ref[...]
        o12 = acc / l_fin
        lam = _lambda(lq1_ref, lk1_ref, lq2_ref, lk2_ref, lam_init)
        o = o12[0:heads] - lam * o12[heads:2 * heads]
        o_ref[...] = _rms(o, sw_ref[...], LN_EPS) * (1.0 - lam_init)


def _attention_decode(q, k_new, v_new, k_pool, v_pool, page_table, rel_bias, lam_vecs,
                      subln_w, heads, lam_init, pages):
    db = q.shape[0]
    n_pool, page, _, dv = k_pool.shape
    da = dv // 2
    n_pages = page_table.shape[1]
    past = n_pages * page
    rpp = page * heads
    span = pages * page
    assert n_pages % pages == 0 and span >= MAX_DISTANCE
    kp = k_pool.reshape(n_pool, rpp, dv)
    vp = v_pool.reshape(n_pool, rpp, dv)

    def two_maps(a):
        a = a.reshape(db, 1, heads, dv)
        return jnp.broadcast_to(a, (db, 2, heads, dv)).reshape(db, 2 * heads, dv)

    rb = rel_bias.astype(F32)
    near = MAX_DISTANCE
    row_head = jnp.arange(2 * heads) % heads
    same = (jnp.arange(span * heads) % heads)[None, :] == row_head[:, None]
    b_far = jnp.where(same, rb[NUM_BUCKETS - 1][row_head][:, None], NEG_BIG)
    tab = jnp.transpose(rb[_t5_bucket(near - jnp.arange(near))])[row_head]
    near_part = jnp.where(same[:, :near * heads], jnp.repeat(tab, heads, axis=1), NEG_BIG)
    b_last = jnp.concatenate([b_far[:, :(span - near) * heads], near_part], axis=1)
    b_new = jnp.tile(jnp.broadcast_to(rb[0][:, None], (heads, LANES)), (2, 1))

    vec = lambda a: a.reshape(1, -1)
    const = lambda s, j, pt: (0, 0)

    def page_spec(idx):
        return pl.BlockSpec((None, rpp, dv), lambda s, j, pt: (pt[s, j * pages + idx], 0, 0))

    grid_spec = pltpu.PrefetchScalarGridSpec(
        num_scalar_prefetch=1,
        grid=(db, n_pages // pages),
        in_specs=([pl.BlockSpec((None, 2 * heads, dv), lambda s, j, pt: (s, 0, 0))] * 3
                  + [pl.BlockSpec((2 * heads, span * heads), const),
                     pl.BlockSpec((2 * heads, span * heads), const),
                     pl.BlockSpec((2 * heads, LANES), const),
                     pl.BlockSpec((1, da), const), pl.BlockSpec((1, da), const),
                     pl.BlockSpec((1, da), const), pl.BlockSpec((1, da), const),
                     pl.BlockSpec((1, dv), const)]
                  + [page_spec(idx) for idx in range(pages)] * 2),
        out_specs=pl.BlockSpec((None, heads, dv), lambda s, j, pt: (s, 0, 0)),
        scratch_shapes=[pltpu.VMEM((2 * heads, 1), F32), pltpu.VMEM((2 * heads, 1), F32),
                        pltpu.VMEM((2 * heads, dv), F32)],
    )
    out = pl.pallas_call(
        functools.partial(_attn_decode_kernel, pages=pages, heads=heads, da=da,
                          lam_init=lam_init),
        grid_spec=grid_spec,
        out_shape=jax.ShapeDtypeStruct((db, heads, dv), F32),
        compiler_params=_cparams("parallel", "arbitrary"),
        name="diff_attention_decode",
    )(page_table, two_maps(q), two_maps(k_new), two_maps(v_new), b_far, b_last, b_new,
      *[vec(a) for a in lam_vecs], vec(subln_w), *([kp] * pages), *([vp] * pages))
    return out.reshape(db, heads * dv)


def _unit_lower_inverse(lmats, eye, levels):
    if levels == 0:
        return [eye for _ in lmats]
    xs = [-m for m in lmats]
    tinvs = [eye + x for x in xs]
    if levels == 1:
        return tinvs
    n = eye.shape[0]
    rs = [_mm(x, x) for x in xs]
    for k in range(1, levels):
        if k < levels - 1:
            both = [_mm(jnp.concatenate([r, t], axis=0), r) for r, t in zip(rs, tinvs)]
            rs = [b[:n] for b in both]
            tinvs = [t + b[n:] for t, b in zip(tinvs, both)]
        else:
            tinvs = [t + _mm(t, r) for t, r in zip(tinvs, rs)]
    return tinvs


def _gdn_pre_kernel(x_ref, st_ref, cw_ref, ba_ref, gp_ref,
                    u_ref, w_ref, qd_ref, kd_ref, at_ref, eg_ref,
                    xe_ref, qn_ref, kn_ref, vv_ref, gb_ref, gc_ref, *, tt, t_real, heads, dk):
    c = GDN_CHUNK
    i = pl.program_id(1)
    taps = cw_ref.shape[0]
    halo = SUBLANES
    hk = heads * dk

    @pl.when(i == 0)
    def _():
        xe_ref[0:halo, :] = st_ref[...]

    xe_ref[halo:halo + tt, :] = x_ref[...]
    acc = cw_ref[taps - 1:taps, :] * x_ref[...]
    for j in range(taps - 1):
        acc = acc + cw_ref[j:j + 1, :] * xe_ref[pl.ds(halo - (taps - 1) + j, tt), :]
    xe_ref[0:halo, :] = xe_ref[tt:tt + halo, :]
    hcv = _silu(acc)

    valid = (i * tt + lax.broadcasted_iota(jnp.int32, (tt, 1), 0)) < t_real
    for h in range(heads):
        sl = slice(h * dk, (h + 1) * dk)
        qh = hcv[:, h * dk:(h + 1) * dk]
        kh = hcv[:, hk + h * dk:hk + (h + 1) * dk]
        qn_ref[:, sl] = qh * lax.rsqrt(jnp.sum(qh * qh, axis=-1, keepdims=True) + L2_EPS)
        kn = kh * lax.rsqrt(jnp.sum(kh * kh, axis=-1, keepdims=True) + L2_EPS)
        kn_ref[:, sl] = jnp.where(valid, kn, 0.0)
    vv_ref[...] = jnp.where(valid, hcv[:, 2 * hk:], 0.0)

    ba = ba_ref[...]
    lane = lax.broadcasted_iota(jnp.int32, ba.shape, 1)
    xa = ba + gp_ref[1:2, :]
    softplus = jnp.maximum(xa, 0.0) + jnp.log1p(jnp.exp(-jnp.abs(xa)))
    gates = jnp.where(lane < heads, _sigmoid(ba), -jnp.exp(gp_ref[0:1, :]) * softplus)
    gates = jnp.where(valid & (lane < 2 * heads), gates, 0.0)
    gb_ref[...] = gates

    tri_l = (lax.broadcasted_iota(jnp.int32, (c, c), 0)
             >= lax.broadcasted_iota(jnp.int32, (c, c), 1)).astype(F32)
    for ch in range(tt // c):
        gc_ref[ch * c:(ch + 1) * c, :] = _mm_hi(tri_l, gates[ch * c:(ch + 1) * c, :])

    ri = lax.broadcasted_iota(jnp.int32, (c, c), 0)
    ci = lax.broadcasted_iota(jnp.int32, (c, c), 1)
    incl = ri >= ci
    strict = ri > ci
    eye = (ri == ci).astype(F32)
    pick = (lax.broadcasted_iota(jnp.int32, (SUBLANES, LANES), 0)
            == lax.broadcasted_iota(jnp.int32, (SUBLANES, LANES), 1)).astype(F32)
    lane_c = lax.broadcasted_iota(jnp.int32, (c, LANES), 1)
    scale = dk ** -0.5
    c_valid = c if t_real >= c else t_real
    levels = (c_valid - 1).bit_length()

    n_chunks = tt // c
    group = 2 if n_chunks % 2 == 0 else 1

    def group_body(gi, carry):
        probs = []
        for cc in range(group):
            ch = gi * group + cc
            rows = pl.ds(pl.multiple_of(ch * c, c), c)
            gb = gb_ref[rows, :]
            gcc = gc_ref[rows, :]
            gc_rows = _mm_nt_hi(pick, gcc)
            for h in range(heads):
                probs.append(dict(
                    ch=ch, rows=rows, h=h, sl=slice(h * dk, (h + 1) * dk),
                    beta=_lane_pick(gb, lane_c, h),
                    gc_col=_lane_pick(gcc, lane_c, heads + h),
                    gc_row=gc_rows[heads + h:heads + h + 1, :]))
        for p in probs:
            p["decay"] = jnp.where(
                incl, jnp.exp(jnp.where(incl, p["gc_col"] - p["gc_row"], 0.0)), 0.0)
            p["k"] = kn_ref[p["rows"], p["sl"]]
            p["k_beta"] = p["k"] * p["beta"]
        kk = [_mm_nt(p["k_beta"], p["k"]) for p in probs]
        lmats = [jnp.where(strict, m * p["decay"], 0.0) for m, p in zip(kk, probs)]
        tinvs = _unit_lower_inverse(lmats, eye, levels)
        for p in probs:
            p["egc"] = jnp.exp(p["gc_col"])
            p["q"] = qn_ref[p["rows"], p["sl"]] * scale
        us = [_mm(t, vv_ref[p["rows"], p["sl"]] * p["beta"]) for t, p in zip(tinvs, probs)]
        ws = [_mm(t, p["k_beta"] * p["egc"]) for t, p in zip(tinvs, probs)]
        ats = [_mm_nt(p["q"], p["k"]) for p in probs]
        for p, u, w, at in zip(probs, us, ws, ats):
            rows, sl, h = p["rows"], p["sl"], p["h"]
            g_last = p["gc_col"][c - 1:c, :]
            u_ref[rows, sl] = u
            w_ref[rows, sl] = w.astype(w_ref.dtype)
            at_ref[rows, h * c:(h + 1) * c] = (at * p["decay"]).astype(at_ref.dtype)
            kd_ref[rows, sl] = p["k"] * jnp.exp(g_last - p["gc_col"])
            qd_ref[rows, sl] = (p["q"] * p["egc"]).astype(qd_ref.dtype)
            eg_ref[p["ch"], :, sl] = jnp.broadcast_to(jnp.exp(g_last), (SUBLANES, dk))
        return carry

    lax.fori_loop(0, n_chunks // group, group_body, 0)


def _gdn_scan_kernel(u_ref, w_ref, qd_ref, kd_ref, at_ref, eg_ref, z_ref, s0_ref, nw_ref,
                     o_ref, sout_ref, s_ref, kt_ref, *, tt, heads, dk):
    c = GDN_CHUNK
    i = pl.program_id(1)
    nb = u_ref.shape[0]

    @pl.when(i == 0)
    def _():
        s_ref[...] = s0_ref[...]

    for b in range(nb):
        for ch in range(tt // c):
            for h in range(heads):
                kt_ref[b, ch * heads + h] = (
                    kd_ref[b, ch * c:(ch + 1) * c, h * dk:(h + 1) * dk].T.astype(BF16))

    chains = [(b, h) for b in range(nb) for h in range(heads)]
    cols = lambda h: slice(h * dk, (h + 1) * dk)

    def chunk_body(ch, carry):
        rows = pl.ds(pl.multiple_of(ch * c, c), c)
        s_bf = [s_ref[b, h].astype(BF16) for b, h in chains]
        w_s = [_mm(w_ref[b, rows, cols(h)], s) for (b, h), s in zip(chains, s_bf)]
        q_s = [_mm(qd_ref[b, rows, cols(h)], s) for (b, h), s in zip(chains, s_bf)]
        v_new = [(u_ref[b, rows, cols(h)] - ws).astype(BF16) for (b, h), ws in zip(chains, w_s)]
        a_v = [_mm(at_ref[b, rows, h * c:(h + 1) * c], v) for (b, h), v in zip(chains, v_new)]
        k_v = [_mm(kt_ref[b, ch * heads + h], v) for (b, h), v in zip(chains, v_new)]
        for (b, h), qs, av, kv in zip(chains, q_s, a_v, k_v):
            s_ref[b, h] = s_ref[b, h] * eg_ref[b, ch, 0:1, cols(h)] + kv
            o_ref[b, rows, cols(h)] = (_rms(qs + av, nw_ref[...], RMS_EPS)
                                       * _silu(z_ref[b, rows, cols(h)]))
        return carry

    lax.fori_loop(0, tt // c, chunk_body, 0)

    @pl.when(i == pl.num_programs(1) - 1)
    def _():
        sout_ref[...] = s_ref[...]


def _gated_deltanet(qkv, ba, z, conv_state, s0, conv_w, a_log, dt_bias, norm_w,
                    batch, seq_pad, t_real, heads, tt, scan_batch, scan_tt):
    cq = qkv.shape[1]
    hk = cq // 3
    dk = hk // heads
    c = GDN_CHUNK
    assert c & (c - 1) == 0 and batch % scan_batch == 0
    nt = seq_pad // tt
    nc = tt // c
    taps = conv_w.shape[0]
    st = jnp.pad(conv_state, ((0, 0), (SUBLANES - (taps - 1), 0), (0, 0)))
    gp = jnp.zeros((SUBLANES, LANES), F32)
    gp = gp.at[0, heads:2 * heads].set(a_log.astype(F32))
    gp = gp.at[1, heads:2 * heads].set(dt_bias.astype(F32))
    row_blk = lambda b, i: (b * nt + i, 0)
    const = lambda b, i: (0, 0)
    tok = lambda width: pl.BlockSpec((tt, width), row_blk)
    rows = batch * seq_pad
    u, w, qd, kd, at, eg = pl.pallas_call(
        functools.partial(_gdn_pre_kernel, tt=tt, t_real=t_real, heads=heads, dk=dk),
        grid=(batch, nt),
        in_specs=[tok(cq),
                  pl.BlockSpec((None, SUBLANES, cq), lambda b, i: (b, 0, 0)),
                  pl.BlockSpec((taps, cq), const),
                  tok(LANES),
                  pl.BlockSpec((SUBLANES, LANES), const)],
        out_specs=[tok(hk), tok(hk), tok(hk), tok(hk), tok(heads * c),
                   pl.BlockSpec((None, nc, SUBLANES, hk), lambda b, i: (b, i, 0, 0))],
        out_shape=[jax.ShapeDtypeStruct((rows, hk), dt) for dt in (F32, BF16, BF16, F32)]
                  + [jax.ShapeDtypeStruct((rows, heads * c), BF16),
                     jax.ShapeDtypeStruct((batch, nt * nc, SUBLANES, hk), F32)],
        scratch_shapes=[pltpu.VMEM((tt + SUBLANES, cq), F32), pltpu.VMEM((tt, hk), F32),
                        pltpu.VMEM((tt, hk), F32), pltpu.VMEM((tt, hk), F32),
                        pltpu.VMEM((tt, LANES), F32), pltpu.VMEM((tt, LANES), F32)],
        compiler_params=_cparams("parallel", "arbitrary"),
        name="gdn_chunk_prepare",
    )(qkv, st, conv_w, ba, gp)

    nb, stt = scan_batch, scan_tt
    snc = stt // c
    seq3 = lambda a: a.reshape(batch, seq_pad, a.shape[-1])
    blk3 = lambda width: pl.BlockSpec((nb, stt, width), lambda g, i: (g, i, 0))
    state_spec = pl.BlockSpec((nb, heads, dk, dk), lambda g, i: (g, 0, 0, 0))
    o, s_new = pl.pallas_call(
        functools.partial(_gdn_scan_kernel, tt=stt, heads=heads, dk=dk),
        grid=(batch // nb, seq_pad // stt),
        in_specs=[blk3(hk), blk3(hk), blk3(hk), blk3(hk), blk3(heads * c),
                  pl.BlockSpec((nb, snc, SUBLANES, hk), lambda g, i: (g, i, 0, 0)),
                  blk3(hk), state_spec,
                  pl.BlockSpec((1, dk), const)],
        out_specs=[blk3(hk), state_spec],
        out_shape=[jax.ShapeDtypeStruct((batch, seq_pad, hk), F32),
                   jax.ShapeDtypeStruct((batch, heads, dk, dk), F32)],
        scratch_shapes=[pltpu.VMEM((nb, heads, dk, dk), F32),
                        pltpu.VMEM((nb, snc * heads, dk, c), BF16)],
        compiler_params=_cparams("parallel", "arbitrary"),
        name="gdn_chunk_scan",
    )(seq3(u), seq3(w), seq3(qd), seq3(kd), seq3(at), eg, seq3(z), s0, norm_w.reshape(1, dk))
    return o.reshape(rows, hk), s_new


def _conv_module_kernel(x_ref, buf_ref, nw_ref, w1_ref, b1_ref, w_ref, b_ref, g_ref, bb_ref,
                        o_ref, tail_ref, xe_ref, xs_ref, *, tt, rb, overlap):
    i = pl.program_id(1)
    last_tile = pl.num_programs(1) - 2
    taps = w_ref.shape[0]
    halo = buf_ref.shape[0]
    first = halo - (taps - 1)
    sub = w_ref.shape[1]
    dc = o_ref.shape[1]

    @pl.when(i == 0)
    def _():
        xe_ref[0:tt, :] = jnp.zeros((tt, dc), F32)
        xe_ref[tt:tt + halo, :] = buf_ref[...]

    def glu_stage():
        xn = _rms(x_ref[...], nw_ref[...], RMS_EPS).astype(BF16)
        a = jnp.dot(xn, w1_ref[:, :dc], preferred_element_type=F32) + b1_ref[:, :dc]
        g = jnp.dot(xn, w1_ref[:, dc:], preferred_element_type=F32) + b1_ref[:, dc:]
        xe_ref[halo + tt:halo + 2 * tt, :] = a * _sigmoid(g)

    def conv_stage():
        span = tt + halo - sub
        for ph in range(1, sub):
            xs_ref[ph - 1, 0:span, :] = xe_ref[pl.ds(ph, span), :]

        def window(row, size):
            blk, ph = divmod(row, sub)
            if ph == 0:
                return xe_ref[pl.ds(row, size), :]
            return xs_ref[ph - 1, pl.ds(blk * sub, size), :]

        def weight(j):
            return w_ref[j] if rb == sub else jnp.tile(w_ref[j], (rb // sub, 1))

        for r0 in range(0, tt, rb):
            acc = b_ref[...] + weight(0) * window(first + r0, rb)
            for j in range(1, taps):
                acc = acc + weight(j) * window(first + r0 + j, rb)
            mu = jnp.mean(acc, axis=-1, keepdims=True)
            xc = acc - mu
            var = jnp.mean(xc * xc, axis=-1, keepdims=True)
            y = xc * lax.rsqrt(var + LN_EPS) * g_ref[...] + bb_ref[...]
            o_ref[pl.ds(r0, rb), :] = _silu(y)

    if overlap:
        glu_stage()
        conv_stage()
    else:
        pl.when(i <= last_tile)(glu_stage)
        pl.when(i >= 1)(conv_stage)

    xe_ref[0:halo, :] = xe_ref[tt:tt + halo, :]
    xe_ref[halo:halo + tt, :] = xe_ref[halo + tt:halo + 2 * tt, :]

    @pl.when(i == last_tile)
    def _():
        rows = tail_ref.shape[0]
        tail_ref[...] = xe_ref[halo + 2 * tt - rows:halo + 2 * tt, :]


def _conv_module(x, buf, norm_w, w_pw1_bf16, b_pw1, w_dw, b_dw, ln_g, ln_b, batch, seq_pad, tt):
    d = x.shape[1]
    ch = w_dw.shape[1]
    taps = w_dw.shape[0]
    halo = -(-(taps - 1) // SUBLANES) * SUBLANES
    bufp = jnp.pad(buf, ((0, 0), (halo - (taps - 1), 0), (0, 0)))
    nt = seq_pad // tt
    rb = min(tt, 2 * SUBLANES)
    tail_rows = min(tt, halo)
    const = lambda b, i: (0, 0)
    vec = lambda a: a.reshape(1, -1)
    w_rep = jnp.broadcast_to(w_dw[:, None, :], (taps, SUBLANES, ch))
    return pl.pallas_call(
        functools.partial(_conv_module_kernel, tt=tt, rb=rb, overlap=nt > 1),
        grid=(batch, nt + 1),
        in_specs=[pl.BlockSpec((tt, d), lambda b, i: (b * nt + jnp.minimum(i, nt - 1), 0)),
                  pl.BlockSpec((None, halo, ch), lambda b, i: (b, 0, 0)),
                  pl.BlockSpec((1, d), const),
                  pl.BlockSpec((d, 2 * ch), const),
                  pl.BlockSpec((1, 2 * ch), const),
                  pl.BlockSpec((taps, SUBLANES, ch), lambda b, i: (0, 0, 0)),
                  pl.BlockSpec((1, ch), const), pl.BlockSpec((1, ch), const),
                  pl.BlockSpec((1, ch), const)],
        out_specs=[pl.BlockSpec((tt, ch), lambda b, i: (b * nt + jnp.maximum(i - 1, 0), 0)),
                   pl.BlockSpec((None, tail_rows, ch), lambda b, i: (b, 0, 0))],
        out_shape=[jax.ShapeDtypeStruct((batch * seq_pad, ch), F32),
                   jax.ShapeDtypeStruct((batch, tail_rows, ch), F32)],
        scratch_shapes=[pltpu.VMEM((2 * tt + halo, ch), F32),
                        pltpu.VMEM((SUBLANES - 1, tt + halo - SUBLANES, ch), F32)],
        compiler_params=_cparams("parallel", "arbitrary"),
        name="conv_module",
    )(x, bufp, vec(norm_w), w_pw1_bf16, vec(b_pw1), w_rep, vec(b_dw), vec(ln_g), vec(ln_b))


def _pad_time(a, batch, t, t_pad):
    a = a.reshape(batch, t, a.shape[-1])
    return jnp.pad(a, ((0, 0), (0, t_pad - t), (0, 0))).reshape(batch * t_pad, a.shape[-1])


def kernel(x_prompt, x_sample, cache_attn_k, cache_attn_v, page_table, state_gdn_conv, state_gdn_s, state_conv_buf, norm_mix, norm_ffn, norm_final, w_in, w_out, gdn_conv_w, gdn_A_log, gdn_dt_bias, gdn_norm_w, lam_q1, lam_k1, lam_q2, lam_k2, diff_subln_w, rel_bias, conv_w_pw1, conv_b_pw1, conv_w_dw, conv_b_dw, conv_ln_g, conv_ln_b, conv_w_pw2, conv_b_pw2, ffn_w_gate, ffn_w_up, ffn_w_down):
    bp, seq, d = x_prompt.shape
    db, dseq, _ = x_sample.shape
    assert dseq == 1
    depth = norm_mix.shape[0]
    ha, dva = cache_attn_k.shape[3], cache_attn_v.shape[4]
    hb, dkb = state_gdn_s.shape[2], state_gdn_s.shape[3]
    c_qkv = state_gdn_conv.shape[3]
    gdn_taps = gdn_conv_w.shape[1]
    conv_taps = conv_w_dw.shape[1]
    d_ff = ffn_w_gate.shape[2]
    mp = bp * seq
    assert 2 * hb <= LANES

    tm_in, tm_ffn = 512, 512
    t_attn = min(512, seq)
    tt_gdn = min(512, seq)
    tt_scan = min(256, seq)
    tt_conv = min(256, seq)
    sample_pad_gdn, sample_pad_conv = GDN_CHUNK, SUBLANES
    assert tt_conv >= conv_taps - 1 and seq >= gdn_taps - 1

    xp = x_prompt.reshape(mp, d)
    xs = x_sample.reshape(db, d)
    row = lambda a: a.reshape(1, -1)

    sizes = (ha * dva, ha * dva, ha * dva, c_qkv, hb * dkb, LANES)
    offs = [0]
    for s_ in sizes:
        offs.append(offs[-1] + s_)
    groups = tuple(zip(offs[:-1], sizes))
    inproj_outs = [(wd, 1) for wd in sizes] + [(dva, ha), (dva, ha)]

    def last_rows(a, n):
        return jnp.stack([a[(b + 1) * seq - n:(b + 1) * seq] for b in range(bp)])

    w_gate_bf, w_up_bf, w_down_bf = (w.astype(BF16) for w in (ffn_w_gate, ffn_w_up, ffn_w_down))

    k_p, v_p, k_s, v_s, gc_p, gc_s, gs_p, gs_s, cb_p, cb_s = ([] for _ in range(10))
    for layer in range(depth):
        if layer % 2 == 0:
            e = layer // 2
            lam_init = 0.8 - 0.6 * math.exp(-0.3 * layer)
            w_in_e = jnp.pad(w_in[e], ((0, 0), (0, offs[-1] - w_in.shape[2]))).astype(BF16)
            outs_p, outs_s = _token_call(
                functools.partial(_inproj_body, groups=groups, heads=ha, kv_groups=(1, 2)),
                [xp], [xs], [_whole(row(norm_mix[layer])), _whole(w_in_e)],
                inproj_outs, tm_in, "in_projection")
            qa_p, ka_p, va_p, qkv_p, z_p, ba_p, k4_p, v4_p = outs_p
            qa_s, ka_s, va_s, qkv_s, z_s, ba_s, k4_s, v4_s = outs_s
            lam_vecs = (lam_q1[e], lam_k1[e], lam_q2[e], lam_k2[e])

            oa_p = _attention_prompt(qa_p, ka_p, va_p, rel_bias, lam_vecs, diff_subln_w[e],
                                     bp, seq, ha, lam_init, t=t_attn)
            oa_s = _attention_decode(qa_s, ka_s, va_s, cache_attn_k[e], cache_attn_v[e],
                                     page_table, rel_bias, lam_vecs, diff_subln_w[e],
                                     ha, lam_init, pages=min(32, page_table.shape[1]))

            gdn_w = (gdn_conv_w[e], gdn_A_log[e], gdn_dt_bias[e], gdn_norm_w[e])
            ob_p, s_p = _gated_deltanet(
                qkv_p, ba_p, z_p, jnp.zeros((bp, gdn_taps - 1, c_qkv), F32),
                jnp.zeros((bp, hb, dkb, dkb), F32), *gdn_w, bp, seq, seq, hb,
                tt=tt_gdn, scan_batch=bp, scan_tt=tt_scan)
            cpad = sample_pad_gdn
            ob_s, s_s = _gated_deltanet(
                _pad_time(qkv_s, db, 1, cpad), _pad_time(ba_s, db, 1, cpad),
                _pad_time(z_s, db, 1, cpad), state_gdn_conv[e], state_gdn_s[e], *gdn_w,
                db, cpad, 1, hb, tt=cpad, scan_batch=min(db, 8), scan_tt=cpad)
            ob_s = ob_s.reshape(db, cpad, hb * dkb)[:, 0]

            w_o = w_out[e].astype(BF16)
            mix_p, mix_s = [oa_p, ob_p], [oa_s, ob_s]
            mix_w = [_whole(w_o[:ha * dva]), _whole(w_o[ha * dva:]),
                     _whole(jnp.zeros((1, d), F32))]

            k_p.append(k4_p.reshape(bp, seq, ha, dva))
            v_p.append(v4_p.reshape(bp, seq, ha, dva))
            k_s.append(k4_s.reshape(db, 1, ha, dva))
            v_s.append(v4_s.reshape(db, 1, ha, dva))
            gc_p.append(last_rows(qkv_p, gdn_taps - 1))
            gc_s.append(jnp.concatenate([state_gdn_conv[e], qkv_s.reshape(db, 1, c_qkv)],
                                        axis=1)[:, -(gdn_taps - 1):])
            gs_p.append(s_p)
            gs_s.append(s_s)
        else:
            cidx = layer // 2
            dconv = conv_w_dw.shape[2]
            conv_w = (norm_mix[layer], conv_w_pw1[cidx].astype(BF16), conv_b_pw1[cidx],
                      conv_w_dw[cidx], conv_b_dw[cidx], conv_ln_g[cidx], conv_ln_b[cidx])
            hc_p, tail_p = _conv_module(xp, jnp.zeros((bp, conv_taps - 1, dconv), F32), *conv_w,
                                        bp, seq, tt=tt_conv)
            spad = sample_pad_conv
            hc_s, tail_s = _conv_module(_pad_time(xs, db, 1, spad), state_conv_buf[cidx],
                                        *conv_w, db, spad, tt=spad)
            hc_s = hc_s.reshape(db, spad, dconv)[:, 0]
            mix_p, mix_s = [hc_p], [hc_s]
            mix_w = [_whole(conv_w_pw2[cidx].astype(BF16)), _whole(row(conv_b_pw2[cidx]))]
            cb_p.append(tail_p[:, tail_p.shape[1] - (conv_taps - 1):])
            cb_s.append(jnp.concatenate([state_conv_buf[cidx], tail_s[:, 0:1]],
                                        axis=1)[:, -(conv_taps - 1):])
        tf = d_ff // 2
        (xp,), (xs,) = _token_call(
            functools.partial(_mix_ffn_body, final_norm=(layer == depth - 1)),
            mix_p + [xp], mix_s + [xs],
            [_whole(row(norm_ffn[layer])),
             (w_gate_bf, (None, d, tf), lambda k, l=layer: (l, 0, k)),
             (w_up_bf, (None, d, tf), lambda k, l=layer: (l, 0, k)),
             (w_down_bf, (None, tf, d), lambda k, l=layer: (l, k, 0)),
             _whole(row(norm_final))] + mix_w,
            [(d, 1)], tm_ffn, "mixer_out_swiglu_ffn", inner=d_ff // tf,
            scratch=(pltpu.VMEM((tm_ffn, d), BF16), pltpu.VMEM((tm_ffn, d), F32)))

    y_prompt = xp.reshape(bp, seq, d)
    y_sample = xs.reshape(db, 1, d)
    return (y_prompt, y_sample, jnp.stack(k_p), jnp.stack(v_p), jnp.stack(k_s), jnp.stack(v_s),
            jnp.stack(gc_p), jnp.stack(gc_s), jnp.stack(gs_p), jnp.stack(gs_s),
            jnp.stack(cb_p), jnp.stack(cb_s))
```

```python
import functools
import math

import jax
import jax.numpy as jnp
from jax import lax
from jax.experimental import pallas as pl
from jax.experimental.pallas import tpu as pltpu

F32 = jnp.float32
BF16 = jnp.bfloat16

RMS_EPS = 1e-6
LN_EPS = 1e-5
L2_EPS = 1e-6
NUM_BUCKETS = 32
MAX_DISTANCE = 128
GDN_CHUNK = 64
NEG_BIG = -1e30
LOG2E = math.log2(math.e)
LANES = 128
SUBLANES = 8
VMEM_LIMIT = 48 * 1024 * 1024

_HI = lax.Precision.HIGHEST


def _cparams(*sem):
    return pltpu.CompilerParams(dimension_semantics=sem, vmem_limit_bytes=VMEM_LIMIT)


def _mm(a, b):
    return jnp.dot(a.astype(BF16), b.astype(BF16), preferred_element_type=F32)


def _mm_nt(a, b):
    return lax.dot_general(a.astype(BF16), b.astype(BF16), (((1,), (1,)), ((), ())),
                           preferred_element_type=F32)


def _mm_tn(a, b):
    return lax.dot_general(a.astype(BF16), b.astype(BF16), (((0,), (0,)), ((), ())),
                           preferred_element_type=F32)


def _mm_hi(a, b):
    return jnp.dot(a, b, precision=_HI, preferred_element_type=F32)


def _mm_nt_hi(a, b):
    return lax.dot_general(a, b, (((1,), (1,)), ((), ())), precision=_HI,
                           preferred_element_type=F32)


def _rms(x, w, eps):
    return x * lax.rsqrt(jnp.mean(x * x, axis=-1, keepdims=True) + eps) * w


def _sigmoid(x):
    return 1.0 / (1.0 + jnp.exp(-x))


def _silu(x):
    return x * _sigmoid(x)


def _lane_pick(x, lane_idx, k):
    return jnp.sum(jnp.where(lane_idx == k, x, 0.0), axis=-1, keepdims=True)


def _token_call(body, prompt_in, sample_in, shared, outs, tm, name):
    mp, ms = prompt_in[0].shape[0], sample_in[0].shape[0]
    assert mp % tm == 0
    n = mp // tm
    n_in, n_sh, n_out = len(prompt_in), len(shared), len(outs)

    def kern(*refs):
        p_in, s_in = refs[:n_in], refs[n_in:2 * n_in]
        sh = refs[2 * n_in:2 * n_in + n_sh]
        o0 = 2 * n_in + n_sh
        p_out, s_out = refs[o0:o0 + n_out], refs[o0 + n_out:]
        i = pl.program_id(0)

        @pl.when(i < n)
        def _():
            body(p_in, sh, p_out)

        @pl.when(i == n)
        def _():
            body(s_in, sh, s_out)

    prow = lambda i: (jnp.minimum(i, n - 1), 0)
    srow = lambda i: (0, 0)
    in_specs = ([pl.BlockSpec((tm, a.shape[1]), prow) for a in prompt_in]
                + [pl.BlockSpec((ms, a.shape[1]), srow) for a in sample_in]
                + [pl.BlockSpec(bs, (lambda i, idx=idx: idx), pipeline_mode=pl.Buffered(1))
                   for _, bs, idx in shared])
    out_specs = ([pl.BlockSpec((tm * r, w), prow) for w, r in outs]
                 + [pl.BlockSpec((ms * r, w), srow) for w, r in outs])
    out_shape = ([jax.ShapeDtypeStruct((mp * r, w), F32) for w, r in outs]
                 + [jax.ShapeDtypeStruct((ms * r, w), F32) for w, r in outs])
    res = pl.pallas_call(
        kern, grid=(n + 1,), in_specs=in_specs, out_specs=out_specs, out_shape=out_shape,
        compiler_params=_cparams("arbitrary"), name=name,
    )(*prompt_in, *sample_in, *[a for a, _, _ in shared])
    return res[:n_out], res[n_out:]


def _whole(a):
    return (a, a.shape, (0,) * a.ndim)


def _inproj_body(ins, sh, outs, *, groups, heads, kv_groups):
    x_ref, = ins
    nw_ref, w_ref = sh
    xn = _rms(x_ref[...], nw_ref[...], RMS_EPS).astype(BF16)
    rows = x_ref.shape[0]
    n_g = len(groups)
    for g, (o_ref, (off, width)) in enumerate(zip(outs[:n_g], groups)):
        y = jnp.dot(xn, w_ref[:, off:off + width], preferred_element_type=F32)
        o_ref[...] = y
        if g in kv_groups:
            o4_ref = outs[n_g + kv_groups.index(g)]
            dv = width // heads
            for h in range(heads):
                o4_ref[pl.ds(h, rows, stride=heads), :] = y[:, h * dv:(h + 1) * dv]


def _mix_ffn_body(ins, sh, outs, *, final_norm):
    res_ref = ins[-1]
    nw_ref, wg_ref, wu_ref, wd_ref, fw_ref = sh[:5]
    o_ref, = outs
    x = res_ref[...] + sh[-1][...]
    for a_ref, w_ref in zip(ins[:-1], sh[5:-1]):
        x = x + _mm(a_ref[...], w_ref[...])
    xn = _rms(x, nw_ref[...], RMS_EPS).astype(BF16)
    g = jnp.dot(xn, wg_ref[...], preferred_element_type=F32)
    u = jnp.dot(xn, wu_ref[...], preferred_element_type=F32)
    y = x + _mm(_silu(g) * u, wd_ref[...])
    if final_norm:
        y = _rms(y, fw_ref[...], RMS_EPS)
    o_ref[...] = y


def _t5_bucket(n):
    max_exact = NUM_BUCKETS // 2
    nf = jnp.maximum(n, 1).astype(F32)
    large = max_exact + (jnp.log(nf / max_exact) / math.log(MAX_DISTANCE / max_exact)
                         * (NUM_BUCKETS - max_exact)).astype(jnp.int32)
    large = jnp.minimum(large, NUM_BUCKETS - 1)
    return jnp.where(n < max_exact, n, large)


def _lambda(lq1_ref, lk1_ref, lq2_ref, lk2_ref, lam_init):
    s1 = jnp.sum(lq1_ref[...] * lk1_ref[...], axis=-1, keepdims=True)
    s2 = jnp.sum(lq2_ref[...] * lk2_ref[...], axis=-1, keepdims=True)
    return jnp.exp(s1) - jnp.exp(s2) + lam_init


def _attn_prompt_kernel(q_ref, k_ref, v_ref, bias_ref, lq1_ref, lk1_ref, lq2_ref, lk2_ref,
                        sw_ref, o_ref, qt_ref, kb_ref, vt_ref, m_ref, acc_ref, sa_ref, sb_ref, *,
                        t, da, lam_init):
    i = pl.program_id(2)
    nblk = kb_ref.shape[0]
    dv = v_ref.shape[1]

    @pl.when(i == 0)
    def _():
        for c in range(nblk):
            kb_ref[c] = k_ref[c * t:(c + 1) * t, :].astype(BF16)
            vt_ref[c, 0:dv, :] = v_ref[c * t:(c + 1) * t, :].T.astype(BF16)
            vt_ref[c, dv:, :] = jnp.ones((vt_ref.shape[1] - dv, t), BF16)

    qt = (q_ref[...] * (da ** -0.5 * LOG2E)).T
    first_map = lax.broadcasted_iota(jnp.int32, qt.shape, 0) < da
    qt_ref[:, 0:t] = jnp.where(first_map, qt, 0.0).astype(BF16)
    qt_ref[:, t:2 * t] = jnp.where(first_map, 0.0, qt).astype(BF16)
    m_ref[...] = jnp.full(m_ref.shape, NEG_BIG, F32)
    acc_ref[...] = jnp.zeros(acc_ref.shape, F32)

    def scores(j, buf_ref):
        buf_ref[...] = jnp.dot(kb_ref[j], qt_ref[...], preferred_element_type=F32)

    def update(j, buf_ref, bias):
        s = buf_ref[...]
        if bias is not None:
            s = s + jnp.concatenate([bias, bias], axis=1)
        m_prev = m_ref[...]
        m_new = jnp.maximum(m_prev, jnp.max(s, axis=0, keepdims=True))
        p = jnp.exp2(s - m_new).astype(BF16)
        acc_ref[...] = (acc_ref[...] * jnp.exp2(m_prev - m_new)
                        + jnp.dot(vt_ref[j], p, preferred_element_type=F32))
        m_ref[...] = m_new

    prev_bias, diag_bias = bias_ref.at[0], bias_ref.at[1]
    scores(0, sa_ref)

    def far_pair(jj, carry):
        j = 2 * jj
        scores(j + 1, sb_ref)
        update(j, sa_ref, None)
        scores(j + 2, sa_ref)
        update(j + 1, sb_ref, None)
        return carry

    n_far = i - 1
    lax.fori_loop(0, n_far // 2, far_pair, 0)

    @pl.when(i % 2 == 1)
    def _():
        scores(i, sb_ref)
        update(i - 1, sa_ref, prev_bias[...])
        update(i, sb_ref, diag_bias[...])

    @pl.when((i % 2 == 0) & (i >= 2))
    def _():
        scores(i - 1, sb_ref)
        update(i - 2, sa_ref, None)
        scores(i, sa_ref)
        update(i - 1, sb_ref, prev_bias[...])
        update(i, sa_ref, diag_bias[...])

    @pl.when(i == 0)
    def _():
        update(0, sa_ref, diag_bias[...])

    acc = acc_ref[...]
    o12 = acc[0:dv] / acc[dv:dv + 1]
    lam = _lambda(lq1_ref, lk1_ref, lq2_ref, lk2_ref, lam_init)
    o = (o12[:, 0:t] - lam * o12[:, t:2 * t]).T
    o_ref[...] = _rms(o, sw_ref[...], LN_EPS) * (1.0 - lam_init)


def _toeplitz(r, t):
    h, period = r.shape
    flat = jnp.tile(r, (1, t))[:, :t * (period - 1)]
    return flat.reshape(h, t, period - 1)[:, :, :t]


def _prompt_bias_tiles(rel_bias, t):
    d = MAX_DISTANCE
    assert t % d == 0
    nb = t // d
    far = rel_bias[NUM_BUCKETS - 1]
    b1 = jnp.transpose(rel_bias[_t5_bucket(jnp.arange(d))] - far).astype(F32) * LOG2E
    zero = jnp.zeros_like(b1)
    neg = jnp.full_like(b1, NEG_BIG)
    tz = _toeplitz(jnp.concatenate([b1, zero, neg, neg], axis=1), 2 * d)
    g0, g1 = tz[:, :d, :d], tz[:, :d, d:]
    zero_blk, neg_blk = jnp.zeros_like(g0), jnp.full_like(g0, NEG_BIG)

    def diag_block(r, c):
        return neg_blk if c < r else g0 if c == r else g1 if c == r + 1 else zero_blk

    diag = jnp.block([[diag_block(r, c) for c in range(nb)] for r in range(nb)])
    prev = jnp.block([[g1 if (r, c) == (nb - 1, 0) else zero_blk for c in range(nb)]
                      for r in range(nb)])
    return jnp.stack([prev, diag], axis=1)


def _attention_prompt(q, k, v, rel_bias, lam_vecs, subln_w, batch, seq, heads, lam_init, t):
    dv = q.shape[1] // heads
    da = dv // 2
    nq = seq // t
    bias = _prompt_bias_tiles(rel_bias, t)
    vec = lambda a: a.reshape(1, -1)
    const = lambda b, h, i: (0, 0)
    return pl.pallas_call(
        functools.partial(_attn_prompt_kernel, t=t, da=da, lam_init=lam_init),
        grid=(batch, heads, nq),
        in_specs=[pl.BlockSpec((t, dv), lambda b, h, i: (b * nq + i, h)),
                  pl.BlockSpec((seq, dv), lambda b, h, i: (b, h)),
                  pl.BlockSpec((seq, dv), lambda b, h, i: (b, h)),
                  pl.BlockSpec((None, 2, t, t), lambda b, h, i: (h, 0, 0, 0)),
                  pl.BlockSpec((1, da), const), pl.BlockSpec((1, da), const),
                  pl.BlockSpec((1, da), const), pl.BlockSpec((1, da), const),
                  pl.BlockSpec((1, dv), const)],
        out_specs=pl.BlockSpec((t, dv), lambda b, h, i: (b * nq + i, h)),
        out_shape=jax.ShapeDtypeStruct((batch * seq, heads * dv), F32),
        scratch_shapes=[pltpu.VMEM((dv, 2 * t), BF16), pltpu.VMEM((nq, t, dv), BF16),
                        pltpu.VMEM((nq, dv + 2 * SUBLANES, t), BF16),
                        pltpu.VMEM((1, 2 * t), F32),
                        pltpu.VMEM((dv + 2 * SUBLANES, 2 * t), F32),
                        pltpu.VMEM((t, 2 * t), F32), pltpu.VMEM((t, 2 * t), F32)],
        compiler_params=_cparams("parallel", "parallel", "arbitrary"),
        name="diff_attention_prompt",
    )(q, k, v, bias, *[vec(a) for a in lam_vecs], vec(subln_w))


def _attn_decode_kernel(pt_ref, q_ref, kn_ref, vn_ref, bfar_ref, blast_ref, bnew_ref,
                        lq1_ref, lk1_ref, lq2_ref, lk2_ref, sw_ref, *rest,
                        pages, heads, da, lam_init):
    del pt_ref
    k_refs = rest[:pages]
    v_refs = rest[pages:2 * pages]
    o_ref, m_ref, l_ref, acc_ref = rest[2 * pages:]
    j = pl.program_id(1)
    last = pl.num_programs(1) - 1

    @pl.when(j == 0)
    def _():
        m_ref[...] = jnp.full(m_ref.shape, NEG_BIG, F32)
        l_ref[...] = jnp.zeros(l_ref.shape, F32)
        acc_ref[...] = jnp.zeros(acc_ref.shape, F32)

    q = q_ref[...] * (da ** -0.5)
    row = lax.broadcasted_iota(jnp.int32, q.shape, 0)
    lane = lax.broadcasted_iota(jnp.int32, q.shape, 1)
    qs = jnp.where((row < heads) == (lane < da), q, 0.0)
    qs_bf = qs.astype(BF16)

    s = jnp.concatenate([_mm_nt(qs_bf, k_ref[...]) for k_ref in k_refs], axis=1)
    s = s + jnp.where(j == last, blast_ref[...], bfar_ref[...])
    m_prev = m_ref[...]
    m_new = jnp.maximum(m_prev, jnp.max(s, axis=-1, keepdims=True))
    p = jnp.exp(s - m_new)
    alpha = jnp.exp(m_prev - m_new)
    l_ref[...] = alpha * l_ref[...] + jnp.sum(p, axis=-1, keepdims=True)
    rows_per_page = k_refs[0].shape[0]
    pv = acc_ref[...] * alpha
    for idx, v_ref in enumerate(v_refs):
        pv = pv + _mm(p[:, idx * rows_per_page:(idx + 1) * rows_per_page], v_ref[...])
    acc_ref[...] = pv
    m_ref[...] = m_new

    @pl.when(j == last)
    def _():
        s_new = jnp.sum(qs * kn_ref[...], axis=-1, keepdims=True) + bnew_ref[:, 0:1]
        m_prev = m_ref[...]
        m_fin = jnp.maximum(m_prev, s_new)
        p_new = jnp.exp(s_new - m_fin)
        alpha = jnp.exp(m_prev - m_fin)
        l_fin = alpha * l_ref[...] + p_new
        acc = alpha * acc_ref[...] + p_new * vn_ref[...]
        o12 = acc / l_fin
        lam = _lambda(lq1_ref, lk1_ref, lq2_ref, lk2_ref, lam_init)
        o = o12[0:heads] - lam * o12[heads:2 * heads]
        o_ref[...] = _rms(o, sw_ref[...], LN_EPS) * (1.0 - lam_init)


def _attention_decode(q, k_new, v_new, k_pool, v_pool, page_table, rel_bias, lam_vecs,
                      subln_w, heads, lam_init, pages):
    db = q.shape[0]
    n_pool, page, _, dv = k_pool.shape
    da = dv // 2
    n_pages = page_table.shape[1]
    past = n_pages * page
    rpp = page * heads
    span = pages * page
    assert n_pages % pages == 0 and span >= MAX_DISTANCE
    kp = k_pool.reshape(n_pool, rpp, dv)
    vp = v_pool.reshape(n_pool, rpp, dv)

    def two_maps(a):
        a = a.reshape(db, 1, heads, dv)
        return jnp.broadcast_to(a, (db, 2, heads, dv)).reshape(db, 2 * heads, dv)

    rb = rel_bias.astype(F32)
    near = MAX_DISTANCE
    row_head = jnp.arange(2 * heads) % heads
    same = (jnp.arange(span * heads) % heads)[None, :] == row_head[:, None]
    b_far = jnp.where(same, rb[NUM_BUCKETS - 1][row_head][:, None], NEG_BIG)
    tab = jnp.transpose(rb[_t5_bucket(near - jnp.arange(near))])[row_head]
    near_part = jnp.where(same[:, :near * heads], jnp.repeat(tab, heads, axis=1), NEG_BIG)
    b_last = jnp.concatenate([b_far[:, :(span - near) * heads], near_part], axis=1)
    b_new = jnp.tile(jnp.broadcast_to(rb[0][:, None], (heads, LANES)), (2, 1))

    vec = lambda a: a.reshape(1, -1)
    const = lambda s, j, pt: (0, 0)

    def page_spec(idx):
        return pl.BlockSpec((None, rpp, dv), lambda s, j, pt: (pt[s, j * pages + idx], 0, 0))

    grid_spec = pltpu.PrefetchScalarGridSpec(
        num_scalar_prefetch=1,
        grid=(db, n_pages // pages),
        in_specs=([pl.BlockSpec((None, 2 * heads, dv), lambda s, j, pt: (s, 0, 0))] * 3
                  + [pl.BlockSpec((2 * heads, span * heads), const),
                     pl.BlockSpec((2 * heads, span * heads), const),
                     pl.BlockSpec((2 * heads, LANES), const),
                     pl.BlockSpec((1, da), const), pl.BlockSpec((1, da), const),
                     pl.BlockSpec((1, da), const), pl.BlockSpec((1, da), const),
                     pl.BlockSpec((1, dv), const)]
                  + [page_spec(idx) for idx in range(pages)] * 2),
        out_specs=pl.BlockSpec((None, heads, dv), lambda s, j, pt: (s, 0, 0)),
        scratch_shapes=[pltpu.VMEM((2 * heads, 1), F32), pltpu.VMEM((2 * heads, 1), F32),
                        pltpu.VMEM((2 * heads, dv), F32)],
    )
    out = pl.pallas_call(
        functools.partial(_attn_decode_kernel, pages=pages, heads=heads, da=da,
                          lam_init=lam_init),
        grid_spec=grid_spec,
        out_shape=jax.ShapeDtypeStruct((db, heads, dv), F32),
        compiler_params=_cparams("parallel", "arbitrary"),
        name="diff_attention_decode",
    )(page_table, two_maps(q), two_maps(k_new), two_maps(v_new), b_far, b_last, b_new,
      *[vec(a) for a in lam_vecs], vec(subln_w), *([kp] * pages), *([vp] * pages))
    return out.reshape(db, heads * dv)


def _unit_lower_inverse(lmats, eye, levels):
    if levels == 0:
        return [eye for _ in lmats]
    xs = [-m for m in lmats]
    tinvs = [eye + x for x in xs]
    if levels == 1:
        return tinvs
    n = eye.shape[0]
    rs = [_mm(x, x) for x in xs]
    for k in range(1, levels):
        if k < levels - 1:
            both = [_mm(jnp.concatenate([r, t], axis=0), r) for r, t in zip(rs, tinvs)]
            rs = [b[:n] for b in both]
            tinvs = [t + b[n:] for t, b in zip(tinvs, both)]
        else:
            tinvs = [t + _mm(t, r) for t, r in zip(tinvs, rs)]
    return tinvs


def _gdn_pre_kernel(x_ref, st_ref, cw_ref, ba_ref, gp_ref,
                    u_ref, w_ref, qd_ref, kd_ref, at_ref, eg_ref,
                    xe_ref, qn_ref, kn_ref, vv_ref, gb_ref, gc_ref, *, tt, t_real, heads, dk):
    c = GDN_CHUNK
    i = pl.program_id(1)
    taps = cw_ref.shape[0]
    halo = SUBLANES
    hk = heads * dk

    @pl.when(i == 0)
    def _():
        xe_ref[0:halo, :] = st_ref[...]

    xe_ref[halo:halo + tt, :] = x_ref[...]
    acc = cw_ref[taps - 1:taps, :] * x_ref[...]
    for j in range(taps - 1):
        acc = acc + cw_ref[j:j + 1, :] * xe_ref[pl.ds(halo - (taps - 1) + j, tt), :]
    xe_ref[0:halo, :] = xe_ref[tt:tt + halo, :]
    hcv = _silu(acc)

    valid = (i * tt + lax.broadcasted_iota(jnp.int32, (tt, 1), 0)) < t_real
    for h in range(heads):
        sl = slice(h * dk, (h + 1) * dk)
        qh = hcv[:, h * dk:(h + 1) * dk]
        kh = hcv[:, hk + h * dk:hk + (h + 1) * dk]
        qn_ref[:, sl] = qh * lax.rsqrt(jnp.sum(qh * qh, axis=-1, keepdims=True) + L2_EPS)
        kn = kh * lax.rsqrt(jnp.sum(kh * kh, axis=-1, keepdims=True) + L2_EPS)
        kn_ref[:, sl] = jnp.where(valid, kn, 0.0)
    vv_ref[...] = jnp.where(valid, hcv[:, 2 * hk:], 0.0)

    ba = ba_ref[...]
    lane = lax.broadcasted_iota(jnp.int32, ba.shape, 1)
    xa = ba + gp_ref[1:2, :]
    softplus = jnp.maximum(xa, 0.0) + jnp.log1p(jnp.exp(-jnp.abs(xa)))
    gates = jnp.where(lane < heads, _sigmoid(ba), -jnp.exp(gp_ref[0:1, :]) * softplus)
    gates = jnp.where(valid & (lane < 2 * heads), gates, 0.0)
    gb_ref[...] = gates

    tri_l = (lax.broadcasted_iota(jnp.int32, (c, c), 0)
             >= lax.broadcasted_iota(jnp.int32, (c, c), 1)).astype(F32)
    for ch in range(tt // c):
        gc_ref[ch * c:(ch + 1) * c, :] = _mm_hi(tri_l, gates[ch * c:(ch + 1) * c, :])

    ri = lax.broadcasted_iota(jnp.int32, (c, c), 0)
    ci = lax.broadcasted_iota(jnp.int32, (c, c), 1)
    incl = ri >= ci
    strict = ri > ci
    eye = (ri == ci).astype(F32)
    pick = (lax.broadcasted_iota(jnp.int32, (SUBLANES, LANES), 0)
            == lax.broadcasted_iota(jnp.int32, (SUBLANES, LANES), 1)).astype(F32)
    lane_c = lax.broadcasted_iota(jnp.int32, (c, LANES), 1)
    scale = dk ** -0.5
    c_valid = c if t_real >= c else t_real
    levels = (c_valid - 1).bit_length()

    n_chunks = tt // c
    group = 2 if n_chunks % 2 == 0 else 1

    def group_body(gi, carry):
        probs = []
        for cc in range(group):
            ch = gi * group + cc
            rows = pl.ds(pl.multiple_of(ch * c, c), c)
            gb = gb_ref[rows, :]
            gcc = gc_ref[rows, :]
            gc_rows = _mm_nt_hi(pick, gcc)
            for h in range(heads):
                probs.append(dict(
                    ch=ch, rows=rows, h=h, sl=slice(h * dk, (h + 1) * dk),
                    beta=_lane_pick(gb, lane_c, h),
                    gc_col=_lane_pick(gcc, lane_c, heads + h),
                    gc_row=gc_rows[heads + h:heads + h + 1, :]))
        for p in probs:
            p["decay"] = jnp.where(
                incl, jnp.exp(jnp.where(incl, p["gc_col"] - p["gc_row"], 0.0)), 0.0)
            p["k"] = kn_ref[p["rows"], p["sl"]]
            p["k_beta"] = p["k"] * p["beta"]
        kk = [_mm_nt(p["k_beta"], p["k"]) for p in probs]
        lmats = [jnp.where(strict, m * p["decay"], 0.0) for m, p in zip(kk, probs)]
        tinvs = _unit_lower_inverse(lmats, eye, levels)
        for p in probs:
            p["egc"] = jnp.exp(p["gc_col"])
            p["q"] = qn_ref[p["rows"], p["sl"]] * scale
        us = [_mm(t, vv_ref[p["rows"], p["sl"]] * p["beta"]) for t, p in zip(tinvs, probs)]
        ws = [_mm(t, p["k_beta"] * p["egc"]) for t, p in zip(tinvs, probs)]
        ats = [_mm_nt(p["q"], p["k"]) for p in probs]
        for p, u, w, at in zip(probs, us, ws, ats):
            rows, sl, h = p["rows"], p["sl"], p["h"]
            g_last = p["gc_col"][c - 1:c, :]
            u_ref[rows, sl] = u
            w_ref[rows, sl] = w.astype(w_ref.dtype)
            at_ref[rows, h * c:(h + 1) * c] = (at * p["decay"]).astype(at_ref.dtype)
            kd_ref[rows, sl] = p["k"] * jnp.exp(g_last - p["gc_col"])
            qd_ref[rows, sl] = (p["q"] * p["egc"]).astype(qd_ref.dtype)
            eg_ref[p["ch"], :, sl] = jnp.broadcast_to(jnp.exp(g_last), (SUBLANES, dk))
        return carry

    lax.fori_loop(0, n_chunks // group, group_body, 0)


def _gdn_scan_kernel(u_ref, w_ref, qd_ref, kd_ref, at_ref, eg_ref, z_ref, s0_ref, nw_ref,
                     o_ref, sout_ref, s_ref, kt_ref, *, tt, heads, dk):
    c = GDN_CHUNK
    i = pl.program_id(1)
    nb = u_ref.shape[0]

    @pl.when(i == 0)
    def _():
        s_ref[...] = s0_ref[...]

    for b in range(nb):
        for ch in range(tt // c):
            for h in range(heads):
                kt_ref[b, ch * heads + h] = (
                    kd_ref[b, ch * c:(ch + 1) * c, h * dk:(h + 1) * dk].T.astype(BF16))

    chains = [(b, h) for b in range(nb) for h in range(heads)]
    cols = lambda h: slice(h * dk, (h + 1) * dk)

    def chunk_body(ch, carry):
        rows = pl.ds(pl.multiple_of(ch * c, c), c)
        s_bf = [s_ref[b, h].astype(BF16) for b, h in chains]
        w_s = [_mm(w_ref[b, rows, cols(h)], s) for (b, h), s in zip(chains, s_bf)]
        q_s = [_mm(qd_ref[b, rows, cols(h)], s) for (b, h), s in zip(chains, s_bf)]
        v_new = [(u_ref[b, rows, cols(h)] - ws).astype(BF16) for (b, h), ws in zip(chains, w_s)]
        a_v = [_mm(at_ref[b, rows, h * c:(h + 1) * c], v) for (b, h), v in zip(chains, v_new)]
        k_v = [_mm(kt_ref[b, ch * heads + h], v) for (b, h), v in zip(chains, v_new)]
        for (b, h), qs, av, kv in zip(chains, q_s, a_v, k_v):
            s_ref[b, h] = s_ref[b, h] * eg_ref[b, ch, 0:1, cols(h)] + kv
            o_ref[b, rows, cols(h)] = (_rms(qs + av, nw_ref[...], RMS_EPS)
                                       * _silu(z_ref[b, rows, cols(h)]))
        return carry

    lax.fori_loop(0, tt // c, chunk_body, 0)

    @pl.when(i == pl.num_programs(1) - 1)
    def _():
        sout_ref[...] = s_ref[...]


def _gated_deltanet(qkv, ba, z, conv_state, s0, conv_w, a_log, dt_bias, norm_w,
                    batch, seq_pad, t_real, heads, tt, scan_batch, scan_tt):
    cq = qkv.shape[1]
    hk = cq // 3
    dk = hk // heads
    c = GDN_CHUNK
    assert c & (c - 1) == 0 and batch % scan_batch == 0
    nt = seq_pad // tt
    nc = tt // c
    taps = conv_w.shape[0]
    st = jnp.pad(conv_state, ((0, 0), (SUBLANES - (taps - 1), 0), (0, 0)))
    gp = jnp.zeros((SUBLANES, LANES), F32)
    gp = gp.at[0, heads:2 * heads].set(a_log.astype(F32))
    gp = gp.at[1, heads:2 * heads].set(dt_bias.astype(F32))
    row_blk = lambda b, i: (b * nt + i, 0)
    const = lambda b, i: (0, 0)
    tok = lambda width: pl.BlockSpec((tt, width), row_blk)
    rows = batch * seq_pad
    u, w, qd, kd, at, eg = pl.pallas_call(
        functools.partial(_gdn_pre_kernel, tt=tt, t_real=t_real, heads=heads, dk=dk),
        grid=(batch, nt),
        in_specs=[tok(cq),
                  pl.BlockSpec((None, SUBLANES, cq), lambda b, i: (b, 0, 0)),
                  pl.BlockSpec((taps, cq), const),
                  tok(LANES),
                  pl.BlockSpec((SUBLANES, LANES), const)],
        out_specs=[tok(hk), tok(hk), tok(hk), tok(hk), tok(heads * c),
                   pl.BlockSpec((None, nc, SUBLANES, hk), lambda b, i: (b, i, 0, 0))],
        out_shape=[jax.ShapeDtypeStruct((rows, hk), dt) for dt in (F32, BF16, BF16, F32)]
                  + [jax.ShapeDtypeStruct((rows, heads * c), BF16),
                     jax.ShapeDtypeStruct((batch, nt * nc, SUBLANES, hk), F32)],
        scratch_shapes=[pltpu.VMEM((tt + SUBLANES, cq), F32), pltpu.VMEM((tt, hk), F32),
                        pltpu.VMEM((tt, hk), F32), pltpu.VMEM((tt, hk), F32),
                        pltpu.VMEM((tt, LANES), F32), pltpu.VMEM((tt, LANES), F32)],
        compiler_params=_cparams("parallel", "arbitrary"),
        name="gdn_chunk_prepare",
    )(qkv, st, conv_w, ba, gp)

    nb, stt = scan_batch, scan_tt
    snc = stt // c
    seq3 = lambda a: a.reshape(batch, seq_pad, a.shape[-1])
    blk3 = lambda width: pl.BlockSpec((nb, stt, width), lambda g, i: (g, i, 0))
    state_spec = pl.BlockSpec((nb, heads, dk, dk), lambda g, i: (g, 0, 0, 0))
    o, s_new = pl.pallas_call(
        functools.partial(_gdn_scan_kernel, tt=stt, heads=heads, dk=dk),
        grid=(batch // nb, seq_pad // stt),
        in_specs=[blk3(hk), blk3(hk), blk3(hk), blk3(hk), blk3(heads * c),
                  pl.BlockSpec((nb, snc, SUBLANES, hk), lambda g, i: (g, i, 0, 0)),
                  blk3(hk), state_spec,
                  pl.BlockSpec((1, dk), const)],
        out_specs=[blk3(hk), state_spec],
        out_shape=[jax.ShapeDtypeStruct((batch, seq_pad, hk), F32),
                   jax.ShapeDtypeStruct((batch, heads, dk, dk), F32)],
        scratch_shapes=[pltpu.VMEM((nb, heads, dk, dk), F32),
                        pltpu.VMEM((nb, snc * heads, dk, c), BF16)],
        compiler_params=_cparams("parallel", "arbitrary"),
        name="gdn_chunk_scan",
    )(seq3(u), seq3(w), seq3(qd), seq3(kd), seq3(at), eg, seq3(z), s0, norm_w.reshape(1, dk))
    return o.reshape(rows, hk), s_new


def _conv_module_kernel(x_ref, buf_ref, nw_ref, w1_ref, b1_ref, w_ref, b_ref, g_ref, bb_ref,
                        o_ref, tail_ref, xe_ref, xs_ref, *, tt, rb, overlap):
    i = pl.program_id(1)
    last_tile = pl.num_programs(1) - 2
    taps = w_ref.shape[0]
    halo = buf_ref.shape[0]
    first = halo - (taps - 1)
    sub = w_ref.shape[1]
    dc = o_ref.shape[1]

    @pl.when(i == 0)
    def _():
        xe_ref[0:tt, :] = jnp.zeros((tt, dc), F32)
        xe_ref[tt:tt + halo, :] = buf_ref[...]

    def glu_stage():
        xn = _rms(x_ref[...], nw_ref[...], RMS_EPS).astype(BF16)
        a = jnp.dot(xn, w1_ref[:, :dc], preferred_element_type=F32) + b1_ref[:, :dc]
        g = jnp.dot(xn, w1_ref[:, dc:], preferred_element_type=F32) + b1_ref[:, dc:]
        xe_ref[halo + tt:halo + 2 * tt, :] = a * _sigmoid(g)

    def conv_stage():
        span = tt + halo - sub
        for ph in range(1, sub):
            xs_ref[ph - 1, 0:span, :] = xe_ref[pl.ds(ph, span), :]

        def window(row, size):
            blk, ph = divmod(row, sub)
            if ph == 0:
                return xe_ref[pl.ds(row, size), :]
            return xs_ref[ph - 1, pl.ds(blk * sub, size), :]

        def weight(j):
            return w_ref[j] if rb == sub else jnp.tile(w_ref[j], (rb // sub, 1))

        for r0 in range(0, tt, rb):
            acc = b_ref[...] + weight(0) * window(first + r0, rb)
            for j in range(1, taps):
                acc = acc + weight(j) * window(first + r0 + j, rb)
            mu = jnp.mean(acc, axis=-1, keepdims=True)
            xc = acc - mu
            var = jnp.mean(xc * xc, axis=-1, keepdims=True)
            y = xc * lax.rsqrt(var + LN_EPS) * g_ref[...] + bb_ref[...]
            o_ref[pl.ds(r0, rb), :] = _silu(y)

    if overlap:
        glu_stage()
        conv_stage()
    else:
        pl.when(i <= last_tile)(glu_stage)
        pl.when(i >= 1)(conv_stage)

    xe_ref[0:halo, :] = xe_ref[tt:tt + halo, :]
    xe_ref[halo:halo + tt, :] = xe_ref[halo + tt:halo + 2 * tt, :]

    @pl.when(i == last_tile)
    def _():
        rows = tail_ref.shape[0]
        tail_ref[...] = xe_ref[halo + 2 * tt - rows:halo + 2 * tt, :]


def _conv_module(x, buf, norm_w, w_pw1_bf16, b_pw1, w_dw, b_dw, ln_g, ln_b, batch, seq_pad, tt):
    d = x.shape[1]
    ch = w_dw.shape[1]
    taps = w_dw.shape[0]
    halo = -(-(taps - 1) // SUBLANES) * SUBLANES
    bufp = jnp.pad(buf, ((0, 0), (halo - (taps - 1), 0), (0, 0)))
    nt = seq_pad // tt
    rb = min(tt, 2 * SUBLANES)
    tail_rows = min(tt, halo)
    const = lambda b, i: (0, 0)
    vec = lambda a: a.reshape(1, -1)
    w_rep = jnp.broadcast_to(w_dw[:, None, :], (taps, SUBLANES, ch))
    return pl.pallas_call(
        functools.partial(_conv_module_kernel, tt=tt, rb=rb, overlap=nt > 1),
        grid=(batch, nt + 1),
        in_specs=[pl.BlockSpec((tt, d), lambda b, i: (b * nt + jnp.minimum(i, nt - 1), 0)),
                  pl.BlockSpec((None, halo, ch), lambda b, i: (b, 0, 0)),
                  pl.BlockSpec((1, d), const),
                  pl.BlockSpec((d, 2 * ch), const),
                  pl.BlockSpec((1, 2 * ch), const),
                  pl.BlockSpec((taps, SUBLANES, ch), lambda b, i: (0, 0, 0)),
                  pl.BlockSpec((1, ch), const), pl.BlockSpec((1, ch), const),
                  pl.BlockSpec((1, ch), const)],
        out_specs=[pl.BlockSpec((tt, ch), lambda b, i: (b * nt + jnp.maximum(i - 1, 0), 0)),
                   pl.BlockSpec((None, tail_rows, ch), lambda b, i: (b, 0, 0))],
        out_shape=[jax.ShapeDtypeStruct((batch * seq_pad, ch), F32),
                   jax.ShapeDtypeStruct((batch, tail_rows, ch), F32)],
        scratch_shapes=[pltpu.VMEM((2 * tt + halo, ch), F32),
                        pltpu.VMEM((SUBLANES - 1, tt + halo - SUBLANES, ch), F32)],
        compiler_params=_cparams("parallel", "arbitrary"),
        name="conv_module",
    )(x, bufp, vec(norm_w), w_pw1_bf16, vec(b_pw1), w_rep, vec(b_dw), vec(ln_g), vec(ln_b))


def _pad_time(a, batch, t, t_pad):
    a = a.reshape(batch, t, a.shape[-1])
    return jnp.pad(a, ((0, 0), (0, t_pad - t), (0, 0))).reshape(batch * t_pad, a.shape[-1])


def kernel(x_prompt, x_sample, cache_attn_k, cache_attn_v, page_table, state_gdn_conv, state_gdn_s, state_conv_buf, norm_mix, norm_ffn, norm_final, w_in, w_out, gdn_conv_w, gdn_A_log, gdn_dt_bias, gdn_norm_w, lam_q1, lam_k1, lam_q2, lam_k2, diff_subln_w, rel_bias, conv_w_pw1, conv_b_pw1, conv_w_dw, conv_b_dw, conv_ln_g, conv_ln_b, conv_w_pw2, conv_b_pw2, ffn_w_gate, ffn_w_up, ffn_w_down):
    bp, seq, d = x_prompt.shape
    db, dseq, _ = x_sample.shape
    assert dseq == 1
    depth = norm_mix.shape[0]
    ha, dva = cache_attn_k.shape[3], cache_attn_v.shape[4]
    hb, dkb = state_gdn_s.shape[2], state_gdn_s.shape[3]
    c_qkv = state_gdn_conv.shape[3]
    gdn_taps = gdn_conv_w.shape[1]
    conv_taps = conv_w_dw.shape[1]
    d_ff = ffn_w_gate.shape[2]
    mp = bp * seq
    assert 2 * hb <= LANES

    tm_in, tm_ffn = 512, 512
    t_attn = min(512, seq)
    tt_gdn = min(512, seq)
    tt_scan = min(256, seq)
    tt_conv = min(256, seq)
    sample_pad_gdn, sample_pad_conv = GDN_CHUNK, SUBLANES
    assert tt_conv >= conv_taps - 1 and seq >= gdn_taps - 1

    xp = x_prompt.reshape(mp, d)
    xs = x_sample.reshape(db, d)
    row = lambda a: a.reshape(1, -1)

    sizes = (ha * dva, ha * dva, ha * dva, c_qkv, hb * dkb, LANES)
    offs = [0]
    for s_ in sizes:
        offs.append(offs[-1] + s_)
    groups = tuple(zip(offs[:-1], sizes))
    inproj_outs = [(wd, 1) for wd in sizes] + [(dva, ha), (dva, ha)]

    def last_rows(a, n):
        return jnp.stack([a[(b + 1) * seq - n:(b + 1) * seq] for b in range(bp)])

    w_gate_bf, w_up_bf, w_down_bf = (w.astype(BF16) for w in (ffn_w_gate, ffn_w_up, ffn_w_down))

    k_p, v_p, k_s, v_s, gc_p, gc_s, gs_p, gs_s, cb_p, cb_s = ([] for _ in range(10))
    for layer in range(depth):
        if layer % 2 == 0:
            e = layer // 2
            lam_init = 0.8 - 0.6 * math.exp(-0.3 * layer)
            w_in_e = jnp.pad(w_in[e], ((0, 0), (0, offs[-1] - w_in.shape[2]))).astype(BF16)
            outs_p, outs_s = _token_call(
                functools.partial(_inproj_body, groups=groups, heads=ha, kv_groups=(1, 2)),
                [xp], [xs], [_whole(row(norm_mix[layer])), _whole(w_in_e)],
                inproj_outs, tm_in, "in_projection")
            qa_p, ka_p, va_p, qkv_p, z_p, ba_p, k4_p, v4_p = outs_p
            qa_s, ka_s, va_s, qkv_s, z_s, ba_s, k4_s, v4_s = outs_s
            lam_vecs = (lam_q1[e], lam_k1[e], lam_q2[e], lam_k2[e])

            oa_p = _attention_prompt(qa_p, ka_p, va_p, rel_bias, lam_vecs, diff_subln_w[e],
                                     bp, seq, ha, lam_init, t=t_attn)
            oa_s = _attention_decode(qa_s, ka_s, va_s, cache_attn_k[e], cache_attn_v[e],
                                     page_table, rel_bias, lam_vecs, diff_subln_w[e],
                                     ha, lam_init, pages=min(32, page_table.shape[1]))

            gdn_w = (gdn_conv_w[e], gdn_A_log[e], gdn_dt_bias[e], gdn_norm_w[e])
            ob_p, s_p = _gated_deltanet(
                qkv_p, ba_p, z_p, jnp.zeros((bp, gdn_taps - 1, c_qkv), F32),
                jnp.zeros((bp, hb, dkb, dkb), F32), *gdn_w, bp, seq, seq, hb,
                tt=tt_gdn, scan_batch=bp, scan_tt=tt_scan)
            cpad = sample_pad_gdn
            ob_s, s_s = _gated_deltanet(
                _pad_time(qkv_s, db, 1, cpad), _pad_time(ba_s, db, 1, cpad),
                _pad_time(z_s, db, 1, cpad), state_gdn_conv[e], state_gdn_s[e], *gdn_w,
                db, cpad, 1, hb, tt=cpad, scan_batch=min(db, 8), scan_tt=cpad)
            ob_s = ob_s.reshape(db, cpad, hb * dkb)[:, 0]

            w_o = w_out[e].astype(BF16)
            mix_p, mix_s = [oa_p, ob_p], [oa_s, ob_s]
            mix_w = [_whole(w_o[:ha * dva]), _whole(w_o[ha * dva:]),
                     _whole(jnp.zeros((1, d), F32))]

            k_p.append(k4_p.reshape(bp, seq, ha, dva))
            v_p.append(v4_p.reshape(bp, seq, ha, dva))
            k_s.append(k4_s.reshape(db, 1, ha, dva))
            v_s.append(v4_s.reshape(db, 1, ha, dva))
            gc_p.append(last_rows(qkv_p, gdn_taps - 1))
            gc_s.append(jnp.concatenate([state_gdn_conv[e], qkv_s.reshape(db, 1, c_qkv)],
                                        axis=1)[:, -(gdn_taps - 1):])
            gs_p.append(s_p)
            gs_s.append(s_s)
        else:
            cidx = layer // 2
            dconv = conv_w_dw.shape[2]
            conv_w = (norm_mix[layer], conv_w_pw1[cidx].astype(BF16), conv_b_pw1[cidx],
                      conv_w_dw[cidx], conv_b_dw[cidx], conv_ln_g[cidx], conv_ln_b[cidx])
            hc_p, tail_p = _conv_module(xp, jnp.zeros((bp, conv_taps - 1, dconv), F32), *conv_w,
                                        bp, seq, tt=tt_conv)
            spad = sample_pad_conv
            hc_s, tail_s = _conv_module(_pad_time(xs, db, 1, spad), state_conv_buf[cidx],
                                        *conv_w, db, spad, tt=spad)
            hc_s = hc_s.reshape(db, spad, dconv)[:, 0]
            mix_p, mix_s = [hc_p], [hc_s]
            mix_w = [_whole(conv_w_pw2[cidx].astype(BF16)), _whole(row(conv_b_pw2[cidx]))]
            cb_p.append(tail_p[:, tail_p.shape[1] - (conv_taps - 1):])
            cb_s.append(jnp.concatenate([state_conv_buf[cidx], tail_s[:, 0:1]],
                                        axis=1)[:, -(conv_taps - 1):])
        (xp,), (xs,) = _token_call(
            functools.partial(_mix_ffn_body, final_norm=(layer == depth - 1)),
            mix_p + [xp], mix_s + [xs],
            [_whole(row(norm_ffn[layer])),
             (w_gate_bf, (None, d, d_ff), (layer, 0, 0)),
             (w_up_bf, (None, d, d_ff), (layer, 0, 0)),
             (w_down_bf, (None, d_ff, d), (layer, 0, 0)),
             _whole(row(norm_final))] + mix_w,
            [(d, 1)], tm_ffn, "mixer_out_swiglu_ffn")

    y_prompt = xp.reshape(bp, seq, d)
    y_sample = xs.reshape(db, 1, d)
    return (y_prompt, y_sample, jnp.stack(k_p), jnp.stack(v_p), jnp.stack(k_s), jnp.stack(v_s),
            jnp.stack(gc_p), jnp.stack(gc_s), jnp.stack(gs_p), jnp.stack(gs_s),
            jnp.stack(cb_p), jnp.stack(cb_s))
```

```python
import functools
import math

import jax
import jax.numpy as jnp
from jax import lax
from jax.experimental import pallas as pl
from jax.experimental.pallas import tpu as pltpu

F32 = jnp.float32
BF16 = jnp.bfloat16

RMS_EPS = 1e-6
LN_EPS = 1e-5
L2_EPS = 1e-6
NUM_BUCKETS = 32
MAX_DISTANCE = 128
GDN_CHUNK = 64
NEG_BIG = -1e30
LOG2E = math.log2(math.e)
LANES = 128
SUBLANES = 8
VMEM_LIMIT = 48 * 1024 * 1024

_HI = lax.Precision.HIGHEST


def _cparams(*sem):
    return pltpu.CompilerParams(dimension_semantics=sem, vmem_limit_bytes=VMEM_LIMIT)


def _mm(a, b):
    return jnp.dot(a.astype(BF16), b.astype(BF16), preferred_element_type=F32)


def _mm_nt(a, b):
    return lax.dot_general(a.astype(BF16), b.astype(BF16), (((1,), (1,)), ((), ())),
                           preferred_element_type=F32)


def _mm_tn(a, b):
    return lax.dot_general(a.astype(BF16), b.astype(BF16), (((0,), (0,)), ((), ())),
                           preferred_element_type=F32)


def _mm_hi(a, b):
    return jnp.dot(a, b, precision=_HI, preferred_element_type=F32)


def _mm_nt_hi(a, b):
    return lax.dot_general(a, b, (((1,), (1,)), ((), ())), precision=_HI,
                           preferred_element_type=F32)


def _rms(x, w, eps):
    return x * lax.rsqrt(jnp.mean(x * x, axis=-1, keepdims=True) + eps) * w


def _sigmoid(x):
    return 1.0 / (1.0 + jnp.exp(-x))


def _silu(x):
    return x * _sigmoid(x)


def _lane_pick(x, lane_idx, k):
    return jnp.sum(jnp.where(lane_idx == k, x, 0.0), axis=-1, keepdims=True)


def _token_call(body, prompt_in, sample_in, shared, outs, tm, name):
    mp, ms = prompt_in[0].shape[0], sample_in[0].shape[0]
    assert mp % tm == 0
    n = mp // tm
    n_in, n_sh, n_out = len(prompt_in), len(shared), len(outs)

    def kern(*refs):
        p_in, s_in = refs[:n_in], refs[n_in:2 * n_in]
        sh = refs[2 * n_in:2 * n_in + n_sh]
        o0 = 2 * n_in + n_sh
        p_out, s_out = refs[o0:o0 + n_out], refs[o0 + n_out:]
        i = pl.program_id(0)

        @pl.when(i < n)
        def _():
            body(p_in, sh, p_out)

        @pl.when(i == n)
        def _():
            body(s_in, sh, s_out)

    prow = lambda i: (jnp.minimum(i, n - 1), 0)
    srow = lambda i: (0, 0)
    in_specs = ([pl.BlockSpec((tm, a.shape[1]), prow) for a in prompt_in]
                + [pl.BlockSpec((ms, a.shape[1]), srow) for a in sample_in]
                + [pl.BlockSpec(bs, (lambda i, idx=idx: idx), pipeline_mode=pl.Buffered(1))
                   for _, bs, idx in shared])
    out_specs = ([pl.BlockSpec((tm * r, w), prow) for w, r in outs]
                 + [pl.BlockSpec((ms * r, w), srow) for w, r in outs])
    out_shape = ([jax.ShapeDtypeStruct((mp * r, w), F32) for w, r in outs]
                 + [jax.ShapeDtypeStruct((ms * r, w), F32) for w, r in outs])
    res = pl.pallas_call(
        kern, grid=(n + 1,), in_specs=in_specs, out_specs=out_specs, out_shape=out_shape,
        compiler_params=_cparams("arbitrary"), name=name,
    )(*prompt_in, *sample_in, *[a for a, _, _ in shared])
    return res[:n_out], res[n_out:]


def _whole(a):
    return (a, a.shape, (0,) * a.ndim)


def _inproj_body(ins, sh, outs, *, groups, heads, kv_groups):
    x_ref, = ins
    nw_ref, w_ref = sh
    xn = _rms(x_ref[...], nw_ref[...], RMS_EPS).astype(BF16)
    rows = x_ref.shape[0]
    n_g = len(groups)
    for g, (o_ref, (off, width)) in enumerate(zip(outs[:n_g], groups)):
        y = jnp.dot(xn, w_ref[:, off:off + width], preferred_element_type=F32)
        o_ref[...] = y
        if g in kv_groups:
            o4_ref = outs[n_g + kv_groups.index(g)]
            dv = width // heads
            for h in range(heads):
                o4_ref[pl.ds(h, rows, stride=heads), :] = y[:, h * dv:(h + 1) * dv]


def _mix_ffn_body(ins, sh, outs, *, final_norm):
    res_ref = ins[-1]
    nw_ref, wg_ref, wu_ref, wd_ref, fw_ref = sh[:5]
    o_ref, = outs
    x = res_ref[...] + sh[-1][...]
    for a_ref, w_ref in zip(ins[:-1], sh[5:-1]):
        x = x + _mm(a_ref[...], w_ref[...])
    xn = _rms(x, nw_ref[...], RMS_EPS).astype(BF16)
    g = jnp.dot(xn, wg_ref[...], preferred_element_type=F32)
    u = jnp.dot(xn, wu_ref[...], preferred_element_type=F32)
    y = x + _mm(_silu(g) * u, wd_ref[...])
    if final_norm:
        y = _rms(y, fw_ref[...], RMS_EPS)
    o_ref[...] = y


def _t5_bucket(n):
    max_exact = NUM_BUCKETS // 2
    nf = jnp.maximum(n, 1).astype(F32)
    large = max_exact + (jnp.log(nf / max_exact) / math.log(MAX_DISTANCE / max_exact)
                         * (NUM_BUCKETS - max_exact)).astype(jnp.int32)
    large = jnp.minimum(large, NUM_BUCKETS - 1)
    return jnp.where(n < max_exact, n, large)


def _lambda(lq1_ref, lk1_ref, lq2_ref, lk2_ref, lam_init):
    s1 = jnp.sum(lq1_ref[...] * lk1_ref[...], axis=-1, keepdims=True)
    s2 = jnp.sum(lq2_ref[...] * lk2_ref[...], axis=-1, keepdims=True)
    return jnp.exp(s1) - jnp.exp(s2) + lam_init


def _attn_prompt_kernel(q_ref, k_ref, v_ref, bias_ref, lq1_ref, lk1_ref, lq2_ref, lk2_ref,
                        sw_ref, o_ref, qt_ref, kb_ref, vt_ref, m_ref, acc_ref, sa_ref, sb_ref, *,
                        t, da, lam_init):
    i = pl.program_id(2)
    nblk = kb_ref.shape[0]
    dv = v_ref.shape[1]

    @pl.when(i == 0)
    def _():
        for c in range(nblk):
            kb_ref[c] = k_ref[c * t:(c + 1) * t, :].astype(BF16)
            vt_ref[c, 0:dv, :] = v_ref[c * t:(c + 1) * t, :].T.astype(BF16)
            vt_ref[c, dv:, :] = jnp.ones((vt_ref.shape[1] - dv, t), BF16)

    qt = (q_ref[...] * (da ** -0.5 * LOG2E)).T
    first_map = lax.broadcasted_iota(jnp.int32, qt.shape, 0) < da
    qt_ref[:, 0:t] = jnp.where(first_map, qt, 0.0).astype(BF16)
    qt_ref[:, t:2 * t] = jnp.where(first_map, 0.0, qt).astype(BF16)
    m_ref[...] = jnp.full(m_ref.shape, NEG_BIG, F32)
    acc_ref[...] = jnp.zeros(acc_ref.shape, F32)

    def scores(j, buf_ref):
        buf_ref[...] = jnp.dot(kb_ref[j], qt_ref[...], preferred_element_type=F32)

    def update(j, buf_ref, bias):
        s = buf_ref[...]
        if bias is not None:
            s = s + jnp.concatenate([bias, bias], axis=1)
        m_prev = m_ref[...]
        m_new = jnp.maximum(m_prev, jnp.max(s, axis=0, keepdims=True))
        p = jnp.exp2(s - m_new).astype(BF16)
        acc_ref[...] = (acc_ref[...] * jnp.exp2(m_prev - m_new)
                        + jnp.dot(vt_ref[j], p, preferred_element_type=F32))
        m_ref[...] = m_new

    prev_bias, diag_bias = bias_ref.at[0], bias_ref.at[1]
    scores(0, sa_ref)

    def far_pair(jj, carry):
        j = 2 * jj
        scores(j + 1, sb_ref)
        update(j, sa_ref, None)
        scores(j + 2, sa_ref)
        update(j + 1, sb_ref, None)
        return carry

    n_far = i - 1
    lax.fori_loop(0, n_far // 2, far_pair, 0)

    @pl.when(i % 2 == 1)
    def _():
        scores(i, sb_ref)
        update(i - 1, sa_ref, prev_bias[...])
        update(i, sb_ref, diag_bias[...])

    @pl.when((i % 2 == 0) & (i >= 2))
    def _():
        scores(i - 1, sb_ref)
        update(i - 2, sa_ref, None)
        scores(i, sa_ref)
        update(i - 1, sb_ref, prev_bias[...])
        update(i, sa_ref, diag_bias[...])

    @pl.when(i == 0)
    def _():
        update(0, sa_ref, diag_bias[...])

    acc = acc_ref[...]
    o12 = acc[0:dv] / acc[dv:dv + 1]
    lam = _lambda(lq1_ref, lk1_ref, lq2_ref, lk2_ref, lam_init)
    o = (o12[:, 0:t] - lam * o12[:, t:2 * t]).T
    o_ref[...] = _rms(o, sw_ref[...], LN_EPS) * (1.0 - lam_init)


def _toeplitz(r, t):
    h, period = r.shape
    flat = jnp.tile(r, (1, t))[:, :t * (period - 1)]
    return flat.reshape(h, t, period - 1)[:, :, :t]


def _prompt_bias_tiles(rel_bias, t):
    d = MAX_DISTANCE
    assert t % d == 0
    nb = t // d
    far = rel_bias[NUM_BUCKETS - 1]
    b1 = jnp.transpose(rel_bias[_t5_bucket(jnp.arange(d))] - far).astype(F32) * LOG2E
    zero = jnp.zeros_like(b1)
    neg = jnp.full_like(b1, NEG_BIG)
    tz = _toeplitz(jnp.concatenate([b1, zero, neg, neg], axis=1), 2 * d)
    g0, g1 = tz[:, :d, :d], tz[:, :d, d:]
    zero_blk, neg_blk = jnp.zeros_like(g0), jnp.full_like(g0, NEG_BIG)

    def diag_block(r, c):
        return neg_blk if c < r else g0 if c == r else g1 if c == r + 1 else zero_blk

    diag = jnp.block([[diag_block(r, c) for c in range(nb)] for r in range(nb)])
    prev = jnp.block([[g1 if (r, c) == (nb - 1, 0) else zero_blk for c in range(nb)]
                      for r in range(nb)])
    return jnp.stack([prev, diag], axis=1)


def _attention_prompt(q, k, v, rel_bias, lam_vecs, subln_w, batch, seq, heads, lam_init, t):
    dv = q.shape[1] // heads
    da = dv // 2
    nq = seq // t
    bias = _prompt_bias_tiles(rel_bias, t)
    vec = lambda a: a.reshape(1, -1)
    const = lambda b, h, i: (0, 0)
    return pl.pallas_call(
        functools.partial(_attn_prompt_kernel, t=t, da=da, lam_init=lam_init),
        grid=(batch, heads, nq),
        in_specs=[pl.BlockSpec((t, dv), lambda b, h, i: (b * nq + i, h)),
                  pl.BlockSpec((seq, dv), lambda b, h, i: (b, h)),
                  pl.BlockSpec((seq, dv), lambda b, h, i: (b, h)),
                  pl.BlockSpec((None, 2, t, t), lambda b, h, i: (h, 0, 0, 0)),
                  pl.BlockSpec((1, da), const), pl.BlockSpec((1, da), const),
                  pl.BlockSpec((1, da), const), pl.BlockSpec((1, da), const),
                  pl.BlockSpec((1, dv), const)],
        out_specs=pl.BlockSpec((t, dv), lambda b, h, i: (b * nq + i, h)),
        out_shape=jax.ShapeDtypeStruct((batch * seq, heads * dv), F32),
        scratch_shapes=[pltpu.VMEM((dv, 2 * t), BF16), pltpu.VMEM((nq, t, dv), BF16),
                        pltpu.VMEM((nq, dv + 2 * SUBLANES, t), BF16),
                        pltpu.VMEM((1, 2 * t), F32),
                        pltpu.VMEM((dv + 2 * SUBLANES, 2 * t), F32),
                        pltpu.VMEM((t, 2 * t), F32), pltpu.VMEM((t, 2 * t), F32)],
        compiler_params=_cparams("parallel", "parallel", "arbitrary"),
        name="diff_attention_prompt",
    )(q, k, v, bias, *[vec(a) for a in lam_vecs], vec(subln_w))


def _attn_decode_kernel(pt_ref, q_ref, kn_ref, vn_ref, bfar_ref, blast_ref, bnew_ref,
                        lq1_ref, lk1_ref, lq2_ref, lk2_ref, sw_ref, *rest,
                        pages, heads, da, lam_init):
    del pt_ref
    k_refs = rest[:pages]
    v_refs = rest[pages:2 * pages]
    o_ref, m_ref, l_ref, acc_ref = rest[2 * pages:]
    j = pl.program_id(1)
    last = pl.num_programs(1) - 1

    @pl.when(j == 0)
    def _():
        m_ref[...] = jnp.full(m_ref.shape, NEG_BIG, F32)
        l_ref[...] = jnp.zeros(l_ref.shape, F32)
        acc_ref[...] = jnp.zeros(acc_ref.shape, F32)

    q = q_ref[...] * (da ** -0.5)
    row = lax.broadcasted_iota(jnp.int32, q.shape, 0)
    lane = lax.broadcasted_iota(jnp.int32, q.shape, 1)
    qs = jnp.where((row < heads) == (lane < da), q, 0.0)
    qs_bf = qs.astype(BF16)

    s = jnp.concatenate([_mm_nt(qs_bf, k_ref[...]) for k_ref in k_refs], axis=1)
    s = s + jnp.where(j == last, blast_ref[...], bfar_ref[...])
    m_prev = m_ref[...]
    m_new = jnp.maximum(m_prev, jnp.max(s, axis=-1, keepdims=True))
    p = jnp.exp(s - m_new)
    alpha = jnp.exp(m_prev - m_new)
    l_ref[...] = alpha * l_ref[...] + jnp.sum(p, axis=-1, keepdims=True)
    rows_per_page = k_refs[0].shape[0]
    pv = acc_ref[...] * alpha
    for idx, v_ref in enumerate(v_refs):
        pv = pv + _mm(p[:, idx * rows_per_page:(idx + 1) * rows_per_page], v_ref[...])
    acc_ref[...] = pv
    m_ref[...] = m_new

    @pl.when(j == last)
    def _():
        s_new = jnp.sum(qs * kn_ref[...], axis=-1, keepdims=True) + bnew_ref[:, 0:1]
        m_prev = m_ref[...]
        m_fin = jnp.maximum(m_prev, s_new)
        p_new = jnp.exp(s_new - m_fin)
        alpha = jnp.exp(m_prev - m_fin)
        l_fin = alpha * l_ref[...] + p_new
        acc = alpha * acc_ref[...] + p_new * vn_ref[...]
        o12 = acc / l_fin
        lam = _lambda(lq1_ref, lk1_ref, lq2_ref, lk2_ref, lam_init)
        o = o12[0:heads] - lam * o12[heads:2 * heads]
        o_ref[...] = _rms(o, sw_ref[...], LN_EPS) * (1.0 - lam_init)


def _attention_decode(q, k_new, v_new, k_pool, v_pool, page_table, rel_bias, lam_vecs,
                      subln_w, heads, lam_init, pages):
    db = q.shape[0]
    n_pool, page, _, dv = k_pool.shape
    da = dv // 2
    n_pages = page_table.shape[1]
    past = n_pages * page
    rpp = page * heads
    span = pages * page
    assert n_pages % pages == 0 and span >= MAX_DISTANCE
    kp = k_pool.reshape(n_pool, rpp, dv)
    vp = v_pool.reshape(n_pool, rpp, dv)

    def two_maps(a):
        a = a.reshape(db, 1, heads, dv)
        return jnp.broadcast_to(a, (db, 2, heads, dv)).reshape(db, 2 * heads, dv)

    rb = rel_bias.astype(F32)
    near = MAX_DISTANCE
    row_head = jnp.arange(2 * heads) % heads
    same = (jnp.arange(span * heads) % heads)[None, :] == row_head[:, None]
    b_far = jnp.where(same, rb[NUM_BUCKETS - 1][row_head][:, None], NEG_BIG)
    tab = jnp.transpose(rb[_t5_bucket(near - jnp.arange(near))])[row_head]
    near_part = jnp.where(same[:, :near * heads], jnp.repeat(tab, heads, axis=1), NEG_BIG)
    b_last = jnp.concatenate([b_far[:, :(span - near) * heads], near_part], axis=1)
    b_new = jnp.tile(jnp.broadcast_to(rb[0][:, None], (heads, LANES)), (2, 1))

    vec = lambda a: a.reshape(1, -1)
    const = lambda s, j, pt: (0, 0)

    def page_spec(idx):
        return pl.BlockSpec((None, rpp, dv), lambda s, j, pt: (pt[s, j * pages + idx], 0, 0))

    grid_spec = pltpu.PrefetchScalarGridSpec(
        num_scalar_prefetch=1,
        grid=(db, n_pages // pages),
        in_specs=([pl.BlockSpec((None, 2 * heads, dv), lambda s, j, pt: (s, 0, 0))] * 3
                  + [pl.BlockSpec((2 * heads, span * heads), const),
                     pl.BlockSpec((2 * heads, span * heads), const),
                     pl.BlockSpec((2 * heads, LANES), const),
                     pl.BlockSpec((1, da), const), pl.BlockSpec((1, da), const),
                     pl.BlockSpec((1, da), const), pl.BlockSpec((1, da), const),
                     pl.BlockSpec((1, dv), const)]
                  + [page_spec(idx) for idx in range(pages)] * 2),
        out_specs=pl.BlockSpec((None, heads, dv), lambda s, j, pt: (s, 0, 0)),
        scratch_shapes=[pltpu.VMEM((2 * heads, 1), F32), pltpu.VMEM((2 * heads, 1), F32),
                        pltpu.VMEM((2 * heads, dv), F32)],
    )
    out = pl.pallas_call(
        functools.partial(_attn_decode_kernel, pages=pages, heads=heads, da=da,
                          lam_init=lam_init),
        grid_spec=grid_spec,
        out_shape=jax.ShapeDtypeStruct((db, heads, dv), F32),
        compiler_params=_cparams("parallel", "arbitrary"),
        name="diff_attention_decode",
    )(page_table, two_maps(q), two_maps(k_new), two_maps(v_new), b_far, b_last, b_new,
      *[vec(a) for a in lam_vecs], vec(subln_w), *([kp] * pages), *([vp] * pages))
    return out.reshape(db, heads * dv)


def _unit_lower_inverse(lmats, eye, levels):
    if levels == 0:
        return [eye for _ in lmats]
    xs = [-m for m in lmats]
    tinvs = [eye + x for x in xs]
    if levels == 1:
        return tinvs
    n = eye.shape[0]
    rs = [_mm(x, x) for x in xs]
    for k in range(1, levels):
        if k < levels - 1:
            both = [_mm(jnp.concatenate([r, t], axis=0), r) for r, t in zip(rs, tinvs)]
            rs = [b[:n] for b in both]
            tinvs = [t + b[n:] for t, b in zip(tinvs, both)]
        else:
            tinvs = [t + _mm(t, r) for t, r in zip(tinvs, rs)]
    return tinvs


def _gdn_pre_kernel(x_ref, st_ref, cw_ref, ba_ref, gp_ref,
                    u_ref, w_ref, qd_ref, kd_ref, at_ref, eg_ref,
                    xe_ref, qn_ref, kn_ref, vv_ref, gb_ref, gc_ref, *, tt, t_real, heads, dk):
    c = GDN_CHUNK
    i = pl.program_id(1)
    taps = cw_ref.shape[0]
    halo = SUBLANES
    hk = heads * dk

    @pl.when(i == 0)
    def _():
        xe_ref[0:halo, :] = st_ref[...]

    xe_ref[halo:halo + tt, :] = x_ref[...]
    acc = cw_ref[taps - 1:taps, :] * x_ref[...]
    for j in range(taps - 1):
        acc = acc + cw_ref[j:j + 1, :] * xe_ref[pl.ds(halo - (taps - 1) + j, tt), :]
    xe_ref[0:halo, :] = xe_ref[tt:tt + halo, :]
    hcv = _silu(acc)

    valid = (i * tt + lax.broadcasted_iota(jnp.int32, (tt, 1), 0)) < t_real
    for h in range(heads):
        sl = slice(h * dk, (h + 1) * dk)
        qh = hcv[:, h * dk:(h + 1) * dk]
        kh = hcv[:, hk + h * dk:hk + (h + 1) * dk]
        qn_ref[:, sl] = qh * lax.rsqrt(jnp.sum(qh * qh, axis=-1, keepdims=True) + L2_EPS)
        kn = kh * lax.rsqrt(jnp.sum(kh * kh, axis=-1, keepdims=True) + L2_EPS)
        kn_ref[:, sl] = jnp.where(valid, kn, 0.0)
    vv_ref[...] = jnp.where(valid, hcv[:, 2 * hk:], 0.0)

    ba = ba_ref[...]
    lane = lax.broadcasted_iota(jnp.int32, ba.shape, 1)
    xa = ba + gp_ref[1:2, :]
    softplus = jnp.maximum(xa, 0.0) + jnp.log1p(jnp.exp(-jnp.abs(xa)))
    gates = jnp.where(lane < heads, _sigmoid(ba), -jnp.exp(gp_ref[0:1, :]) * softplus)
    gates = jnp.where(valid & (lane < 2 * heads), gates, 0.0)
    gb_ref[...] = gates

    tri_l = (lax.broadcasted_iota(jnp.int32, (c, c), 0)
             >= lax.broadcasted_iota(jnp.int32, (c, c), 1)).astype(F32)
    for ch in range(tt // c):
        gc_ref[ch * c:(ch + 1) * c, :] = _mm_hi(tri_l, gates[ch * c:(ch + 1) * c, :])

    ri = lax.broadcasted_iota(jnp.int32, (c, c), 0)
    ci = lax.broadcasted_iota(jnp.int32, (c, c), 1)
    incl = ri >= ci
    strict = ri > ci
    eye = (ri == ci).astype(F32)
    pick = (lax.broadcasted_iota(jnp.int32, (SUBLANES, LANES), 0)
            == lax.broadcasted_iota(jnp.int32, (SUBLANES, LANES), 1)).astype(F32)
    lane_c = lax.broadcasted_iota(jnp.int32, (c, LANES), 1)
    scale = dk ** -0.5
    c_valid = c if t_real >= c else t_real
    levels = (c_valid - 1).bit_length()

    n_chunks = tt // c
    group = 2 if n_chunks % 2 == 0 else 1

    def group_body(gi, carry):
        probs = []
        for cc in range(group):
            ch = gi * group + cc
            rows = pl.ds(pl.multiple_of(ch * c, c), c)
            gb = gb_ref[rows, :]
            gcc = gc_ref[rows, :]
            gc_rows = _mm_nt_hi(pick, gcc)
            for h in range(heads):
                probs.append(dict(
                    ch=ch, rows=rows, h=h, sl=slice(h * dk, (h + 1) * dk),
                    beta=_lane_pick(gb, lane_c, h),
                    gc_col=_lane_pick(gcc, lane_c, heads + h),
                    gc_row=gc_rows[heads + h:heads + h + 1, :]))
        for p in probs:
            p["decay"] = jnp.where(
                incl, jnp.exp(jnp.where(incl, p["gc_col"] - p["gc_row"], 0.0)), 0.0)
            p["k"] = kn_ref[p["rows"], p["sl"]]
            p["k_beta"] = p["k"] * p["beta"]
        kk = [_mm_nt(p["k_beta"], p["k"]) for p in probs]
        lmats = [jnp.where(strict, m * p["decay"], 0.0) for m, p in zip(kk, probs)]
        tinvs = _unit_lower_inverse(lmats, eye, levels)
        for p in probs:
            p["egc"] = jnp.exp(p["gc_col"])
            p["q"] = qn_ref[p["rows"], p["sl"]] * scale
        us = [_mm(t, vv_ref[p["rows"], p["sl"]] * p["beta"]) for t, p in zip(tinvs, probs)]
        ws = [_mm(t, p["k_beta"] * p["egc"]) for t, p in zip(tinvs, probs)]
        ats = [_mm_nt(p["q"], p["k"]) for p in probs]
        for p, u, w, at in zip(probs, us, ws, ats):
            rows, sl, h = p["rows"], p["sl"], p["h"]
            g_last = p["gc_col"][c - 1:c, :]
            u_ref[rows, sl] = u
            w_ref[rows, sl] = w.astype(w_ref.dtype)
            at_ref[rows, h * c:(h + 1) * c] = (at * p["decay"]).astype(at_ref.dtype)
            kd_ref[rows, sl] = p["k"] * jnp.exp(g_last - p["gc_col"])
            qd_ref[rows, sl] = (p["q"] * p["egc"]).astype(qd_ref.dtype)
            eg_ref[p["ch"], :, sl] = jnp.broadcast_to(jnp.exp(g_last), (SUBLANES, dk))
        return carry

    lax.fori_loop(0, n_chunks // group, group_body, 0)


def _gdn_scan_kernel(u_ref, w_ref, qd_ref, kd_ref, at_ref, eg_ref, z_ref, s0_ref, nw_ref,
                     o_ref, sout_ref, s_ref, kt_ref, *, tt, heads, dk):
    c = GDN_CHUNK
    i = pl.program_id(1)
    nb = u_ref.shape[0]

    @pl.when(i == 0)
    def _():
        s_ref[...] = s0_ref[...]

    for b in range(nb):
        for ch in range(tt // c):
            for h in range(heads):
                kt_ref[b, ch * heads + h] = (
                    kd_ref[b, ch * c:(ch + 1) * c, h * dk:(h + 1) * dk].T.astype(BF16))

    chains = [(b, h) for b in range(nb) for h in range(heads)]
    cols = lambda h: slice(h * dk, (h + 1) * dk)

    def chunk_body(ch, carry):
        rows = pl.ds(pl.multiple_of(ch * c, c), c)
        s_bf = [s_ref[b, h].astype(BF16) for b, h in chains]
        w_s = [_mm(w_ref[b, rows, cols(h)], s) for (b, h), s in zip(chains, s_bf)]
        q_s = [_mm(qd_ref[b, rows, cols(h)], s) for (b, h), s in zip(chains, s_bf)]
        v_new = [(u_ref[b, rows, cols(h)] - ws).astype(BF16) for (b, h), ws in zip(chains, w_s)]
        a_v = [_mm(at_ref[b, rows, h * c:(h + 1) * c], v) for (b, h), v in zip(chains, v_new)]
        k_v = [_mm(kt_ref[b, ch * heads + h], v) for (b, h), v in zip(chains, v_new)]
        for (b, h), qs, av, kv in zip(chains, q_s, a_v, k_v):
            s_ref[b, h] = s_ref[b, h] * eg_ref[b, ch, 0:1, cols(h)] + kv
            o_ref[b, rows, cols(h)] = (_rms(qs + av, nw_ref[...], RMS_EPS)
                                       * _silu(z_ref[b, rows, cols(h)]))
        return carry

    lax.fori_loop(0, tt // c, chunk_body, 0)

    @pl.when(i == pl.num_programs(1) - 1)
    def _():
        sout_ref[...] = s_ref[...]


def _gated_deltanet(qkv, ba, z, conv_state, s0, conv_w, a_log, dt_bias, norm_w,
                    batch, seq_pad, t_real, heads, tt, scan_batch, scan_tt):
    cq = qkv.shape[1]
    hk = cq // 3
    dk = hk // heads
    c = GDN_CHUNK
    assert c & (c - 1) == 0 and batch % scan_batch == 0
    nt = seq_pad // tt
    nc = tt // c
    taps = conv_w.shape[0]
    st = jnp.pad(conv_state, ((0, 0), (SUBLANES - (taps - 1), 0), (0, 0)))
    gp = jnp.zeros((SUBLANES, LANES), F32)
    gp = gp.at[0, heads:2 * heads].set(a_log.astype(F32))
    gp = gp.at[1, heads:2 * heads].set(dt_bias.astype(F32))
    row_blk = lambda b, i: (b * nt + i, 0)
    const = lambda b, i: (0, 0)
    tok = lambda width: pl.BlockSpec((tt, width), row_blk)
    rows = batch * seq_pad
    u, w, qd, kd, at, eg = pl.pallas_call(
        functools.partial(_gdn_pre_kernel, tt=tt, t_real=t_real, heads=heads, dk=dk),
        grid=(batch, nt),
        in_specs=[tok(cq),
                  pl.BlockSpec((None, SUBLANES, cq), lambda b, i: (b, 0, 0)),
                  pl.BlockSpec((taps, cq), const),
                  tok(LANES),
                  pl.BlockSpec((SUBLANES, LANES), const)],
        out_specs=[tok(hk), tok(hk), tok(hk), tok(hk), tok(heads * c),
                   pl.BlockSpec((None, nc, SUBLANES, hk), lambda b, i: (b, i, 0, 0))],
        out_shape=[jax.ShapeDtypeStruct((rows, hk), dt) for dt in (F32, BF16, BF16, F32)]
                  + [jax.ShapeDtypeStruct((rows, heads * c), BF16),
                     jax.ShapeDtypeStruct((batch, nt * nc, SUBLANES, hk), F32)],
        scratch_shapes=[pltpu.VMEM((tt + SUBLANES, cq), F32), pltpu.VMEM((tt, hk), F32),
                        pltpu.VMEM((tt, hk), F32), pltpu.VMEM((tt, hk), F32),
                        pltpu.VMEM((tt, LANES), F32), pltpu.VMEM((tt, LANES), F32)],
        compiler_params=_cparams("parallel", "arbitrary"),
        name="gdn_chunk_prepare",
    )(qkv, st, conv_w, ba, gp)

    nb, stt = scan_batch, scan_tt
    snc = stt // c
    seq3 = lambda a: a.reshape(batch, seq_pad, a.shape[-1])
    blk3 = lambda width: pl.BlockSpec((nb, stt, width), lambda g, i: (g, i, 0))
    state_spec = pl.BlockSpec((nb, heads, dk, dk), lambda g, i: (g, 0, 0, 0))
    o, s_new = pl.pallas_call(
        functools.partial(_gdn_scan_kernel, tt=stt, heads=heads, dk=dk),
        grid=(batch // nb, seq_pad // stt),
        in_specs=[blk3(hk), blk3(hk), blk3(hk), blk3(hk), blk3(heads * c),
                  pl.BlockSpec((nb, snc, SUBLANES, hk), lambda g, i: (g, i, 0, 0)),
                  blk3(hk), state_spec,
                  pl.BlockSpec((1, dk), const)],
        out_specs=[blk3(hk), state_spec],
        out_shape=[jax.ShapeDtypeStruct((batch, seq_pad, hk), F32),
                   jax.ShapeDtypeStruct((batch, heads, dk, dk), F32)],
        scratch_shapes=[pltpu.VMEM((nb, heads, dk, dk), F32),
                        pltpu.VMEM((nb, snc * heads, dk, c), BF16)],
        compiler_params=_cparams("parallel", "arbitrary"),
        name="gdn_chunk_scan",
    )(seq3(u), seq3(w), seq3(qd), seq3(kd), seq3(at), eg, seq3(z), s0, norm_w.reshape(1, dk))
    return o.reshape(rows, hk), s_new


def _gdn_token_kernel(x_ref, st_ref, cw_ref, ba_ref, gp_ref, z_ref, s0_ref, nw_ref,
                      o_ref, sout_ref, *, heads, dk):
    nb = x_ref.shape[0]
    hk = heads * dk
    taps = cw_ref.shape[0]
    acc = cw_ref[taps - 1:taps, :] * x_ref[...]
    for j in range(taps - 1):
        acc = acc + cw_ref[j:j + 1, :] * st_ref[j]
    hcv = _silu(acc)

    ba = ba_ref[...]
    lane = lax.broadcasted_iota(jnp.int32, ba.shape, 1)
    xa = ba + gp_ref[1:2, :]
    softplus = jnp.maximum(xa, 0.0) + jnp.log1p(jnp.exp(-jnp.abs(xa)))
    gates = jnp.where(lane < heads, _sigmoid(ba), -jnp.exp(gp_ref[0:1, :]) * softplus)

    for h in range(heads):
        sl = slice(h * dk, (h + 1) * dk)
        qh = hcv[:, h * dk:(h + 1) * dk]
        kh = hcv[:, hk + h * dk:hk + (h + 1) * dk]
        q = qh * lax.rsqrt(jnp.sum(qh * qh, axis=-1, keepdims=True) + L2_EPS) * (dk ** -0.5)
        k = kh * lax.rsqrt(jnp.sum(kh * kh, axis=-1, keepdims=True) + L2_EPS)
        v = hcv[:, 2 * hk + h * dk:2 * hk + (h + 1) * dk]
        beta = _lane_pick(gates, lane, h)
        decay = jnp.exp(_lane_pick(gates, lane, heads + h))
        k_cols = k.T
        out_rows = []
        for b in range(nb):
            state = s0_ref[b, h]
            row = slice(b, b + 1)
            k_s = _mm(k, state)[row]
            v_new = beta[row] * (v[row] - decay[row] * k_s)
            new_state = state * decay[row] + k_cols[:, b:b + 1] * v_new
            sout_ref[b, h] = new_state
            out_rows.append(_mm(q, new_state)[row])
        o = jnp.concatenate(out_rows, axis=0)
        o_ref[:, sl] = _rms(o, nw_ref[...], RMS_EPS) * _silu(z_ref[:, sl])


def _gated_deltanet_token(qkv, ba, z, conv_state, s0, conv_w, a_log, dt_bias, norm_w, heads, nb):
    batch, cq = qkv.shape
    hk = cq // 3
    dk = hk // heads
    taps = conv_w.shape[0]
    assert batch % nb == 0
    gp = jnp.zeros((SUBLANES, LANES), F32)
    gp = gp.at[0, heads:2 * heads].set(a_log.astype(F32))
    gp = gp.at[1, heads:2 * heads].set(dt_bias.astype(F32))
    const = lambda g: (0, 0)
    rows = lambda width: pl.BlockSpec((nb, width), lambda g: (g, 0))
    state_spec = pl.BlockSpec((nb, heads, dk, dk), lambda g: (g, 0, 0, 0))
    return pl.pallas_call(
        functools.partial(_gdn_token_kernel, heads=heads, dk=dk),
        grid=(batch // nb,),
        in_specs=[rows(cq),
                  pl.BlockSpec((taps - 1, nb, cq), lambda g: (0, g, 0)),
                  pl.BlockSpec((taps, cq), const),
                  rows(LANES),
                  pl.BlockSpec((SUBLANES, LANES), const),
                  rows(hk), state_spec,
                  pl.BlockSpec((1, dk), const)],
        out_specs=[rows(hk), state_spec],
        out_shape=[jax.ShapeDtypeStruct((batch, hk), F32),
                   jax.ShapeDtypeStruct((batch, heads, dk, dk), F32)],
        compiler_params=_cparams("parallel"),
        name="gdn_token",
    )(qkv, jnp.transpose(conv_state, (1, 0, 2)), conv_w, ba, gp, z, s0, norm_w.reshape(1, dk))


def _conv_module_kernel(x_ref, buf_ref, nw_ref, w1_ref, b1_ref, w_ref, b_ref, g_ref, bb_ref,
                        o_ref, tail_ref, xe_ref, xs_ref, *, tt, rb):
    i = pl.program_id(1)
    last_tile = pl.num_programs(1) - 2
    taps = w_ref.shape[0]
    halo = buf_ref.shape[0]
    first = halo - (taps - 1)
    sub = w_ref.shape[1]
    dc = o_ref.shape[1]

    @pl.when(i == 0)
    def _():
        xe_ref[0:tt, :] = jnp.zeros((tt, dc), F32)
        xe_ref[tt:tt + halo, :] = buf_ref[...]

    def glu_stage():
        xn = _rms(x_ref[...], nw_ref[...], RMS_EPS).astype(BF16)
        a = jnp.dot(xn, w1_ref[:, :dc], preferred_element_type=F32) + b1_ref[:, :dc]
        g = jnp.dot(xn, w1_ref[:, dc:], preferred_element_type=F32) + b1_ref[:, dc:]
        xe_ref[halo + tt:halo + 2 * tt, :] = a * _sigmoid(g)

    def conv_stage():
        span = tt + halo - sub
        for ph in range(1, sub):
            xs_ref[ph - 1, 0:span, :] = xe_ref[pl.ds(ph, span), :]

        def window(row, size):
            blk, ph = divmod(row, sub)
            if ph == 0:
                return xe_ref[pl.ds(row, size), :]
            return xs_ref[ph - 1, pl.ds(blk * sub, size), :]

        def weight(j):
            return w_ref[j] if rb == sub else jnp.tile(w_ref[j], (rb // sub, 1))

        for r0 in range(0, tt, rb):
            acc = b_ref[...] + weight(0) * window(first + r0, rb)
            for j in range(1, taps):
                acc = acc + weight(j) * window(first + r0 + j, rb)
            mu = jnp.mean(acc, axis=-1, keepdims=True)
            xc = acc - mu
            var = jnp.mean(xc * xc, axis=-1, keepdims=True)
            y = xc * lax.rsqrt(var + LN_EPS) * g_ref[...] + bb_ref[...]
            o_ref[pl.ds(r0, rb), :] = _silu(y)

    glu_stage()
    conv_stage()

    xe_ref[0:halo, :] = xe_ref[tt:tt + halo, :]
    xe_ref[halo:halo + tt, :] = xe_ref[halo + tt:halo + 2 * tt, :]

    @pl.when(i == last_tile)
    def _():
        rows = tail_ref.shape[0]
        tail_ref[...] = xe_ref[halo + 2 * tt - rows:halo + 2 * tt, :]


def _conv_module(x, buf, norm_w, w_pw1_bf16, b_pw1, w_dw, b_dw, ln_g, ln_b, batch, seq_pad, tt):
    d = x.shape[1]
    ch = w_dw.shape[1]
    taps = w_dw.shape[0]
    halo = -(-(taps - 1) // SUBLANES) * SUBLANES
    bufp = jnp.pad(buf, ((0, 0), (halo - (taps - 1), 0), (0, 0)))
    nt = seq_pad // tt
    rb = min(tt, 2 * SUBLANES)
    tail_rows = min(tt, halo)
    const = lambda b, i: (0, 0)
    vec = lambda a: a.reshape(1, -1)
    w_rep = jnp.broadcast_to(w_dw[:, None, :], (taps, SUBLANES, ch))
    return pl.pallas_call(
        functools.partial(_conv_module_kernel, tt=tt, rb=rb),
        grid=(batch, nt + 1),
        in_specs=[pl.BlockSpec((tt, d), lambda b, i: (b * nt + jnp.minimum(i, nt - 1), 0)),
                  pl.BlockSpec((None, halo, ch), lambda b, i: (b, 0, 0)),
                  pl.BlockSpec((1, d), const),
                  pl.BlockSpec((d, 2 * ch), const),
                  pl.BlockSpec((1, 2 * ch), const),
                  pl.BlockSpec((taps, SUBLANES, ch), lambda b, i: (0, 0, 0)),
                  pl.BlockSpec((1, ch), const), pl.BlockSpec((1, ch), const),
                  pl.BlockSpec((1, ch), const)],
        out_specs=[pl.BlockSpec((tt, ch), lambda b, i: (b * nt + jnp.maximum(i - 1, 0), 0)),
                   pl.BlockSpec((None, tail_rows, ch), lambda b, i: (b, 0, 0))],
        out_shape=[jax.ShapeDtypeStruct((batch * seq_pad, ch), F32),
                   jax.ShapeDtypeStruct((batch, tail_rows, ch), F32)],
        scratch_shapes=[pltpu.VMEM((2 * tt + halo, ch), F32),
                        pltpu.VMEM((SUBLANES - 1, tt + halo - SUBLANES, ch), F32)],
        compiler_params=_cparams("parallel", "arbitrary"),
        name="conv_module",
    )(x, bufp, vec(norm_w), w_pw1_bf16, vec(b_pw1), w_rep, vec(b_dw), vec(ln_g), vec(ln_b))


def _conv_token_kernel(x_ref, buf_ref, nw_ref, w1_ref, b1_ref, w_ref, b_ref, g_ref, bb_ref,
                       o_ref, h_ref):
    dc = o_ref.shape[1]
    taps = w_ref.shape[0]
    xn = _rms(x_ref[...], nw_ref[...], RMS_EPS).astype(BF16)
    a = jnp.dot(xn, w1_ref[:, :dc], preferred_element_type=F32) + b1_ref[:, :dc]
    g = jnp.dot(xn, w1_ref[:, dc:], preferred_element_type=F32) + b1_ref[:, dc:]
    h = a * _sigmoid(g)
    h_ref[...] = h
    acc = b_ref[...] + w_ref[taps - 1:taps, :] * h
    for j in range(taps - 1):
        acc = acc + w_ref[j:j + 1, :] * buf_ref[j]
    mu = jnp.mean(acc, axis=-1, keepdims=True)
    xc = acc - mu
    var = jnp.mean(xc * xc, axis=-1, keepdims=True)
    o_ref[...] = _silu(xc * lax.rsqrt(var + LN_EPS) * g_ref[...] + bb_ref[...])


def _conv_module_token(x, buf, norm_w, w_pw1_bf16, b_pw1, w_dw, b_dw, ln_g, ln_b):
    batch = x.shape[0]
    ch = w_dw.shape[1]
    vec = lambda a: a.reshape(1, -1)
    operands = (x, jnp.transpose(buf, (1, 0, 2)), vec(norm_w), w_pw1_bf16, vec(b_pw1), w_dw,
                vec(b_dw), vec(ln_g), vec(ln_b))
    whole = lambda a: pl.BlockSpec(a.shape, lambda i, nd=a.ndim: (0,) * nd)
    return pl.pallas_call(
        _conv_token_kernel,
        grid=(1,),
        in_specs=[whole(a) for a in operands],
        out_specs=[pl.BlockSpec((batch, ch), lambda i: (0, 0))] * 2,
        out_shape=[jax.ShapeDtypeStruct((batch, ch), F32)] * 2,
        compiler_params=_cparams("arbitrary"),
        name="conv_module_token",
    )(*operands)


def kernel(x_prompt, x_sample, cache_attn_k, cache_attn_v, page_table, state_gdn_conv, state_gdn_s, state_conv_buf, norm_mix, norm_ffn, norm_final, w_in, w_out, gdn_conv_w, gdn_A_log, gdn_dt_bias, gdn_norm_w, lam_q1, lam_k1, lam_q2, lam_k2, diff_subln_w, rel_bias, conv_w_pw1, conv_b_pw1, conv_w_dw, conv_b_dw, conv_ln_g, conv_ln_b, conv_w_pw2, conv_b_pw2, ffn_w_gate, ffn_w_up, ffn_w_down):
    bp, seq, d = x_prompt.shape
    db, dseq, _ = x_sample.shape
    assert dseq == 1
    depth = norm_mix.shape[0]
    ha, dva = cache_attn_k.shape[3], cache_attn_v.shape[4]
    hb, dkb = state_gdn_s.shape[2], state_gdn_s.shape[3]
    c_qkv = state_gdn_conv.shape[3]
    gdn_taps = gdn_conv_w.shape[1]
    conv_taps = conv_w_dw.shape[1]
    d_ff = ffn_w_gate.shape[2]
    mp = bp * seq
    assert 2 * hb <= LANES

    tm_in, tm_ffn = 512, 512
    t_attn = min(512, seq)
    tt_gdn = min(512, seq)
    tt_scan = min(256, seq)
    tt_conv = min(256, seq)
    assert tt_conv >= conv_taps - 1 and seq >= gdn_taps - 1

    xp = x_prompt.reshape(mp, d)
    xs = x_sample.reshape(db, d)
    row = lambda a: a.reshape(1, -1)

    sizes = (ha * dva, ha * dva, ha * dva, c_qkv, hb * dkb, LANES)
    offs = [0]
    for s_ in sizes:
        offs.append(offs[-1] + s_)
    groups = tuple(zip(offs[:-1], sizes))
    inproj_outs = [(wd, 1) for wd in sizes] + [(dva, ha), (dva, ha)]

    def last_rows(a, n):
        return jnp.stack([a[(b + 1) * seq - n:(b + 1) * seq] for b in range(bp)])

    w_gate_bf, w_up_bf, w_down_bf = (w.astype(BF16) for w in (ffn_w_gate, ffn_w_up, ffn_w_down))

    k_p, v_p, k_s, v_s, gc_p, gc_s, gs_p, gs_s, cb_p, cb_s = ([] for _ in range(10))
    for layer in range(depth):
        if layer % 2 == 0:
            e = layer // 2
            lam_init = 0.8 - 0.6 * math.exp(-0.3 * layer)
            w_in_e = jnp.pad(w_in[e], ((0, 0), (0, offs[-1] - w_in.shape[2]))).astype(BF16)
            outs_p, outs_s = _token_call(
                functools.partial(_inproj_body, groups=groups, heads=ha, kv_groups=(1, 2)),
                [xp], [xs], [_whole(row(norm_mix[layer])), _whole(w_in_e)],
                inproj_outs, tm_in, "in_projection")
            qa_p, ka_p, va_p, qkv_p, z_p, ba_p, k4_p, v4_p = outs_p
            qa_s, ka_s, va_s, qkv_s, z_s, ba_s, k4_s, v4_s = outs_s
            lam_vecs = (lam_q1[e], lam_k1[e], lam_q2[e], lam_k2[e])

            oa_p = _attention_prompt(qa_p, ka_p, va_p, rel_bias, lam_vecs, diff_subln_w[e],
                                     bp, seq, ha, lam_init, t=t_attn)
            oa_s = _attention_decode(qa_s, ka_s, va_s, cache_attn_k[e], cache_attn_v[e],
                                     page_table, rel_bias, lam_vecs, diff_subln_w[e],
                                     ha, lam_init, pages=min(32, page_table.shape[1]))

            gdn_w = (gdn_conv_w[e], gdn_A_log[e], gdn_dt_bias[e], gdn_norm_w[e])
            ob_p, s_p = _gated_deltanet(
                qkv_p, ba_p, z_p, jnp.zeros((bp, gdn_taps - 1, c_qkv), F32),
                jnp.zeros((bp, hb, dkb, dkb), F32), *gdn_w, bp, seq, seq, hb,
                tt=tt_gdn, scan_batch=bp, scan_tt=tt_scan)
            ob_s, s_s = _gated_deltanet_token(qkv_s, ba_s, z_s, state_gdn_conv[e],
                                              state_gdn_s[e], *gdn_w, hb, nb=min(db, SUBLANES))

            w_o = w_out[e].astype(BF16)
            mix_p, mix_s = [oa_p, ob_p], [oa_s, ob_s]
            mix_w = [_whole(w_o[:ha * dva]), _whole(w_o[ha * dva:]),
                     _whole(jnp.zeros((1, d), F32))]

            k_p.append(k4_p.reshape(bp, seq, ha, dva))
            v_p.append(v4_p.reshape(bp, seq, ha, dva))
            k_s.append(k4_s.reshape(db, 1, ha, dva))
            v_s.append(v4_s.reshape(db, 1, ha, dva))
            gc_p.append(last_rows(qkv_p, gdn_taps - 1))
            gc_s.append(jnp.concatenate([state_gdn_conv[e], qkv_s.reshape(db, 1, c_qkv)],
                                        axis=1)[:, -(gdn_taps - 1):])
            gs_p.append(s_p)
            gs_s.append(s_s)
        else:
            cidx = layer // 2
            dconv = conv_w_dw.shape[2]
            conv_w = (norm_mix[layer], conv_w_pw1[cidx].astype(BF16), conv_b_pw1[cidx],
                      conv_w_dw[cidx], conv_b_dw[cidx], conv_ln_g[cidx], conv_ln_b[cidx])
            hc_p, tail_p = _conv_module(xp, jnp.zeros((bp, conv_taps - 1, dconv), F32), *conv_w,
                                        bp, seq, tt=tt_conv)
            hc_s, hg_s = _conv_module_token(xs, state_conv_buf[cidx], *conv_w)
            mix_p, mix_s = [hc_p], [hc_s]
            mix_w = [_whole(conv_w_pw2[cidx].astype(BF16)), _whole(row(conv_b_pw2[cidx]))]
            cb_p.append(tail_p[:, tail_p.shape[1] - (conv_taps - 1):])
            cb_s.append(jnp.concatenate([state_conv_buf[cidx], hg_s[:, None, :]],
                                        axis=1)[:, -(conv_taps - 1):])
        (xp,), (xs,) = _token_call(
            functools.partial(_mix_ffn_body, final_norm=(layer == depth - 1)),
            mix_p + [xp], mix_s + [xs],
            [_whole(row(norm_ffn[layer])),
             (w_gate_bf, (None, d, d_ff), (layer, 0, 0)),
             (w_up_bf, (None, d, d_ff), (layer, 0, 0)),
             (w_down_bf, (None, d_ff, d), (layer, 0, 0)),
             _whole(row(norm_final))] + mix_w,
            [(d, 1)], tm_ffn, "mixer_out_swiglu_ffn")

    y_prompt = xp.reshape(bp, seq, d)
    y_sample = xs.reshape(db, 1, d)
    return (y_prompt, y_sample, jnp.stack(k_p), jnp.stack(v_p), jnp.stack(k_s), jnp.stack(v_s),
            jnp.stack(gc_p), jnp.stack(gc_s), jnp.stack(gs_p), jnp.stack(gs_s),
            jnp.stack(cb_p), jnp.stack(cb_s))
```

```python
import functools
import math

import jax
import jax.numpy as jnp
from jax import lax
from jax.experimental import pallas as pl
from jax.experimental.pallas import tpu as pltpu

F32 = jnp.float32
BF16 = jnp.bfloat16

RMS_EPS = 1e-6
LN_EPS = 1e-5
L2_EPS = 1e-6
NUM_BUCKETS = 32
MAX_DISTANCE = 128
GDN_CHUNK = 64
NEG_BIG = -1e30
LOG2E = math.log2(math.e)
LANES = 128
SUBLANES = 8
VMEM_LIMIT = 48 * 1024 * 1024

_HI = lax.Precision.HIGHEST


def _cparams(*sem):
    return pltpu.CompilerParams(dimension_semantics=sem, vmem_limit_bytes=VMEM_LIMIT)


def _mm(a, b):
    return jnp.dot(a.astype(BF16), b.astype(BF16), preferred_element_type=F32)


def _mm_nt(a, b):
    return lax.dot_general(a.astype(BF16), b.astype(BF16), (((1,), (1,)), ((), ())),
                           preferred_element_type=F32)


def _mm_tn(a, b):
    return lax.dot_general(a.astype(BF16), b.astype(BF16), (((0,), (0,)), ((), ())),
                           preferred_element_type=F32)


def _mm_hi(a, b):
    return jnp.dot(a, b, precision=_HI, preferred_element_type=F32)


def _mm_nt_hi(a, b):
    return lax.dot_general(a, b, (((1,), (1,)), ((), ())), precision=_HI,
                           preferred_element_type=F32)


def _rms(x, w, eps):
    return x * lax.rsqrt(jnp.mean(x * x, axis=-1, keepdims=True) + eps) * w


def _sigmoid(x):
    return 1.0 / (1.0 + jnp.exp(-x))


def _silu(x):
    return x * _sigmoid(x)


def _lane_pick(x, lane_idx, k):
    return jnp.sum(jnp.where(lane_idx == k, x, 0.0), axis=-1, keepdims=True)


def _token_call(body, prompt_in, sample_in, shared, outs, tm, name):
    mp, ms = prompt_in[0].shape[0], sample_in[0].shape[0]
    assert mp % tm == 0
    n = mp // tm
    n_in, n_sh, n_out = len(prompt_in), len(shared), len(outs)

    def kern(*refs):
        p_in, s_in = refs[:n_in], refs[n_in:2 * n_in]
        sh = refs[2 * n_in:2 * n_in + n_sh]
        o0 = 2 * n_in + n_sh
        p_out, s_out = refs[o0:o0 + n_out], refs[o0 + n_out:]
        i = pl.program_id(0)

        @pl.when(i < n)
        def _():
            body(p_in, sh, p_out)

        @pl.when(i == n)
        def _():
            body(s_in, sh, s_out)

    prow = lambda i: (jnp.minimum(i, n - 1), 0)
    srow = lambda i: (0, 0)
    in_specs = ([pl.BlockSpec((tm, a.shape[1]), prow) for a in prompt_in]
                + [pl.BlockSpec((ms, a.shape[1]), srow) for a in sample_in]
                + [pl.BlockSpec(bs, (lambda i, idx=idx: idx), pipeline_mode=pl.Buffered(1))
                   for _, bs, idx in shared])
    out_specs = ([pl.BlockSpec((tm * r, w), prow) for w, r in outs]
                 + [pl.BlockSpec((ms * r, w), srow) for w, r in outs])
    out_shape = ([jax.ShapeDtypeStruct((mp * r, w), F32) for w, r in outs]
                 + [jax.ShapeDtypeStruct((ms * r, w), F32) for w, r in outs])
    res = pl.pallas_call(
        kern, grid=(n + 1,), in_specs=in_specs, out_specs=out_specs, out_shape=out_shape,
        compiler_params=_cparams("arbitrary"), name=name,
    )(*prompt_in, *sample_in, *[a for a, _, _ in shared])
    return res[:n_out], res[n_out:]


def _whole(a):
    return (a, a.shape, (0,) * a.ndim)


def _inproj_body(ins, sh, outs, *, groups, heads, kv_groups):
    x_ref, = ins
    nw_ref, w_ref = sh
    xn = _rms(x_ref[...], nw_ref[...], RMS_EPS).astype(BF16)
    rows = x_ref.shape[0]
    n_g = len(groups)
    for g, (o_ref, (off, width)) in enumerate(zip(outs[:n_g], groups)):
        y = jnp.dot(xn, w_ref[:, off:off + width], preferred_element_type=F32)
        o_ref[...] = y
        if g in kv_groups:
            o4_ref = outs[n_g + kv_groups.index(g)]
            dv = width // heads
            for h in range(heads):
                o4_ref[pl.ds(h, rows, stride=heads), :] = y[:, h * dv:(h + 1) * dv]


def _mix_ffn_body(ins, sh, outs, *, final_norm):
    res_ref = ins[-1]
    nw_ref, wg_ref, wu_ref, wd_ref, fw_ref = sh[:5]
    o_ref, = outs
    x = res_ref[...] + sh[-1][...]
    for a_ref, w_ref in zip(ins[:-1], sh[5:-1]):
        x = x + _mm(a_ref[...], w_ref[...])
    xn = _rms(x, nw_ref[...], RMS_EPS).astype(BF16)
    g = jnp.dot(xn, wg_ref[...], preferred_element_type=F32)
    u = jnp.dot(xn, wu_ref[...], preferred_element_type=F32)
    y = x + _mm(_silu(g) * u, wd_ref[...])
    if final_norm:
        y = _rms(y, fw_ref[...], RMS_EPS)
    o_ref[...] = y


def _t5_bucket(n):
    max_exact = NUM_BUCKETS // 2
    nf = jnp.maximum(n, 1).astype(F32)
    large = max_exact + (jnp.log(nf / max_exact) / math.log(MAX_DISTANCE / max_exact)
                         * (NUM_BUCKETS - max_exact)).astype(jnp.int32)
    large = jnp.minimum(large, NUM_BUCKETS - 1)
    return jnp.where(n < max_exact, n, large)


def _lambda(lq1_ref, lk1_ref, lq2_ref, lk2_ref, lam_init):
    s1 = jnp.sum(lq1_ref[...] * lk1_ref[...], axis=-1, keepdims=True)
    s2 = jnp.sum(lq2_ref[...] * lk2_ref[...], axis=-1, keepdims=True)
    return jnp.exp(s1) - jnp.exp(s2) + lam_init


def _attn_prompt_kernel(q_ref, k_ref, v_ref, bias_ref, lq1_ref, lk1_ref, lq2_ref, lk2_ref,
                        sw_ref, o_ref, qt_ref, kb_ref, vt_ref, m_ref, acc_ref, sa_ref, sb_ref, *,
                        t, da, lam_init):
    i = pl.program_id(2)
    nblk = kb_ref.shape[0]
    dv = v_ref.shape[1]

    @pl.when(i == 0)
    def _():
        for c in range(nblk):
            kb_ref[c] = k_ref[c * t:(c + 1) * t, :].astype(BF16)
            vt_ref[c, 0:dv, :] = v_ref[c * t:(c + 1) * t, :].T.astype(BF16)
            vt_ref[c, dv:, :] = jnp.ones((vt_ref.shape[1] - dv, t), BF16)

    qt = (q_ref[...] * (da ** -0.5 * LOG2E)).T
    first_map = lax.broadcasted_iota(jnp.int32, qt.shape, 0) < da
    qt_ref[:, 0:t] = jnp.where(first_map, qt, 0.0).astype(BF16)
    qt_ref[:, t:2 * t] = jnp.where(first_map, 0.0, qt).astype(BF16)
    m_ref[...] = jnp.full(m_ref.shape, NEG_BIG, F32)
    acc_ref[...] = jnp.zeros(acc_ref.shape, F32)

    def scores(j, buf_ref):
        buf_ref[...] = jnp.dot(kb_ref[j], qt_ref[...], preferred_element_type=F32)

    def update(j, buf_ref, bias):
        s = buf_ref[...]
        if bias is not None:
            s = s + jnp.concatenate([bias, bias], axis=1)
        m_prev = m_ref[...]
        m_new = jnp.maximum(m_prev, jnp.max(s, axis=0, keepdims=True))
        p = jnp.exp2(s - m_new).astype(BF16)
        acc_ref[...] = (acc_ref[...] * jnp.exp2(m_prev - m_new)
                        + jnp.dot(vt_ref[j], p, preferred_element_type=F32))
        m_ref[...] = m_new

    prev_bias, diag_bias = bias_ref.at[0], bias_ref.at[1]
    scores(0, sa_ref)

    def far_pair(jj, carry):
        j = 2 * jj
        scores(j + 1, sb_ref)
        update(j, sa_ref, None)
        scores(j + 2, sa_ref)
        update(j + 1, sb_ref, None)
        return carry

    n_far = i - 1
    lax.fori_loop(0, n_far // 2, far_pair, 0)

    @pl.when(i % 2 == 1)
    def _():
        scores(i, sb_ref)
        update(i - 1, sa_ref, prev_bias[...])
        update(i, sb_ref, diag_bias[...])

    @pl.when((i % 2 == 0) & (i >= 2))
    def _():
        scores(i - 1, sb_ref)
        update(i - 2, sa_ref, None)
        scores(i, sa_ref)
        update(i - 1, sb_ref, prev_bias[...])
        update(i, sa_ref, diag_bias[...])

    @pl.when(i == 0)
    def _():
        update(0, sa_ref, diag_bias[...])

    acc = acc_ref[...]
    o12 = acc[0:dv] / acc[dv:dv + 1]
    lam = _lambda(lq1_ref, lk1_ref, lq2_ref, lk2_ref, lam_init)
    o = (o12[:, 0:t] - lam * o12[:, t:2 * t]).T
    o_ref[...] = _rms(o, sw_ref[...], LN_EPS) * (1.0 - lam_init)


def _toeplitz(r, t):
    h, period = r.shape
    flat = jnp.tile(r, (1, t))[:, :t * (period - 1)]
    return flat.reshape(h, t, period - 1)[:, :, :t]


def _prompt_bias_tiles(rel_bias, t):
    d = MAX_DISTANCE
    assert t % d == 0
    nb = t // d
    far = rel_bias[NUM_BUCKETS - 1]
    b1 = jnp.transpose(rel_bias[_t5_bucket(jnp.arange(d))] - far).astype(F32) * LOG2E
    zero = jnp.zeros_like(b1)
    neg = jnp.full_like(b1, NEG_BIG)
    tz = _toeplitz(jnp.concatenate([b1, zero, neg, neg], axis=1), 2 * d)
    g0, g1 = tz[:, :d, :d], tz[:, :d, d:]
    zero_blk, neg_blk = jnp.zeros_like(g0), jnp.full_like(g0, NEG_BIG)

    def diag_block(r, c):
        return neg_blk if c < r else g0 if c == r else g1 if c == r + 1 else zero_blk

    diag = jnp.block([[diag_block(r, c) for c in range(nb)] for r in range(nb)])
    prev = jnp.block([[g1 if (r, c) == (nb - 1, 0) else zero_blk for c in range(nb)]
                      for r in range(nb)])
    return jnp.stack([prev, diag], axis=1)


def _attention_prompt(q, k, v, rel_bias, lam_vecs, subln_w, batch, seq, heads, lam_init, t):
    dv = q.shape[1] // heads
    da = dv // 2
    nq = seq // t
    bias = _prompt_bias_tiles(rel_bias, t)
    vec = lambda a: a.reshape(1, -1)
    const = lambda b, h, i: (0, 0)
    return pl.pallas_call(
        functools.partial(_attn_prompt_kernel, t=t, da=da, lam_init=lam_init),
        grid=(batch, heads, nq),
        in_specs=[pl.BlockSpec((t, dv), lambda b, h, i: (b * nq + i, h)),
                  pl.BlockSpec((seq, dv), lambda b, h, i: (b, h)),
                  pl.BlockSpec((seq, dv), lambda b, h, i: (b, h)),
                  pl.BlockSpec((None, 2, t, t), lambda b, h, i: (h, 0, 0, 0)),
                  pl.BlockSpec((1, da), const), pl.BlockSpec((1, da), const),
                  pl.BlockSpec((1, da), const), pl.BlockSpec((1, da), const),
                  pl.BlockSpec((1, dv), const)],
        out_specs=pl.BlockSpec((t, dv), lambda b, h, i: (b * nq + i, h)),
        out_shape=jax.ShapeDtypeStruct((batch * seq, heads * dv), F32),
        scratch_shapes=[pltpu.VMEM((dv, 2 * t), BF16), pltpu.VMEM((nq, t, dv), BF16),
                        pltpu.VMEM((nq, dv + 2 * SUBLANES, t), BF16),
                        pltpu.VMEM((1, 2 * t), F32),
                        pltpu.VMEM((dv + 2 * SUBLANES, 2 * t), F32),
                        pltpu.VMEM((t, 2 * t), F32), pltpu.VMEM((t, 2 * t), F32)],
        compiler_params=_cparams("parallel", "parallel", "arbitrary"),
        name="diff_attention_prompt",
    )(q, k, v, bias, *[vec(a) for a in lam_vecs], vec(subln_w))


def _attn_decode_kernel(pt_ref, q_ref, kn_ref, vn_ref, bfar_ref, blast_ref, bnew_ref,
                        lq1_ref, lk1_ref, lq2_ref, lk2_ref, sw_ref, *rest,
                        pages, heads, da, lam_init):
    del pt_ref
    k_refs = rest[:pages]
    v_refs = rest[pages:2 * pages]
    o_ref, m_ref, l_ref, acc_ref = rest[2 * pages:]
    j = pl.program_id(1)
    last = pl.num_programs(1) - 1

    @pl.when(j == 0)
    def _():
        m_ref[...] = jnp.full(m_ref.shape, NEG_BIG, F32)
        l_ref[...] = jnp.zeros(l_ref.shape, F32)
        acc_ref[...] = jnp.zeros(acc_ref.shape, F32)

    q = q_ref[...] * (da ** -0.5)
    row = lax.broadcasted_iota(jnp.int32, q.shape, 0)
    lane = lax.broadcasted_iota(jnp.int32, q.shape, 1)
    qs = jnp.where((row < heads) == (lane < da), q, 0.0)
    qs_bf = qs.astype(BF16)

    s = jnp.concatenate([_mm_nt(qs_bf, k_ref[...]) for k_ref in k_refs], axis=1)
    s = s + jnp.where(j == last, blast_ref[...], bfar_ref[...])
    m_prev = m_ref[...]
    m_new = jnp.maximum(m_prev, jnp.max(s, axis=-1, keepdims=True))
    p = jnp.exp(s - m_new)
    alpha = jnp.exp(m_prev - m_new)
    l_ref[...] = alpha * l_ref[...] + jnp.sum(p, axis=-1, keepdims=True)
    rows_per_page = k_refs[0].shape[0]
    pv = acc_ref[...] * alpha
    for idx, v_ref in enumerate(v_refs):
        pv = pv + _mm(p[:, idx * rows_per_page:(idx + 1) * rows_per_page], v_ref[...])
    acc_ref[...] = pv
    m_ref[...] = m_new

    @pl.when(j == last)
    def _():
        s_new = jnp.sum(qs * kn_ref[...], axis=-1, keepdims=True) + bnew_ref[:, 0:1]
        m_prev = m_ref[...]
        m_fin = jnp.maximum(m_prev, s_new)
        p_new = jnp.exp(s_new - m_fin)
        alpha = jnp.exp(m_prev - m_fin)
        l_fin = alpha * l_ref[...] + p_new
        acc = alpha * acc_ref[...] + p_new * vn_ref[...]
        o12 = acc / l_fin
        lam = _lambda(lq1_ref, lk1_ref, lq2_ref, lk2_ref, lam_init)
        o = o12[0:heads] - lam * o12[heads:2 * heads]
        o_ref[...] = _rms(o, sw_ref[...], LN_EPS) * (1.0 - lam_init)


def _attention_decode(q, k_new, v_new, k_pool, v_pool, page_table, rel_bias, lam_vecs,
                      subln_w, heads, lam_init, pages):
    db = q.shape[0]
    n_pool, page, _, dv = k_pool.shape
    da = dv // 2
    n_pages = page_table.shape[1]
    past = n_pages * page
    rpp = page * heads
    span = pages * page
    assert n_pages % pages == 0 and span >= MAX_DISTANCE
    kp = k_pool.reshape(n_pool, rpp, dv)
    vp = v_pool.reshape(n_pool, rpp, dv)

    def two_maps(a):
        a = a.reshape(db, 1, heads, dv)
        return jnp.broadcast_to(a, (db, 2, heads, dv)).reshape(db, 2 * heads, dv)

    rb = rel_bias.astype(F32)
    near = MAX_DISTANCE
    row_head = jnp.arange(2 * heads) % heads
    same = (jnp.arange(span * heads) % heads)[None, :] == row_head[:, None]
    b_far = jnp.where(same, rb[NUM_BUCKETS - 1][row_head][:, None], NEG_BIG)
    tab = jnp.transpose(rb[_t5_bucket(near - jnp.arange(near))])[row_head]
    near_part = jnp.where(same[:, :near * heads], jnp.repeat(tab, heads, axis=1), NEG_BIG)
    b_last = jnp.concatenate([b_far[:, :(span - near) * heads], near_part], axis=1)
    b_new = jnp.tile(jnp.broadcast_to(rb[0][:, None], (heads, LANES)), (2, 1))

    vec = lambda a: a.reshape(1, -1)
    const = lambda s, j, pt: (0, 0)

    def page_spec(idx):
        return pl.BlockSpec((None, rpp, dv), lambda s, j, pt: (pt[s, j * pages + idx], 0, 0))

    grid_spec = pltpu.PrefetchScalarGridSpec(
        num_scalar_prefetch=1,
        grid=(db, n_pages // pages),
        in_specs=([pl.BlockSpec((None, 2 * heads, dv), lambda s, j, pt: (s, 0, 0))] * 3
                  + [pl.BlockSpec((2 * heads, span * heads), const),
                     pl.BlockSpec((2 * heads, span * heads), const),
                     pl.BlockSpec((2 * heads, LANES), const),
                     pl.BlockSpec((1, da), const), pl.BlockSpec((1, da), const),
                     pl.BlockSpec((1, da), const), pl.BlockSpec((1, da), const),
                     pl.BlockSpec((1, dv), const)]
                  + [page_spec(idx) for idx in range(pages)] * 2),
        out_specs=pl.BlockSpec((None, heads, dv), lambda s, j, pt: (s, 0, 0)),
        scratch_shapes=[pltpu.VMEM((2 * heads, 1), F32), pltpu.VMEM((2 * heads, 1), F32),
                        pltpu.VMEM((2 * heads, dv), F32)],
    )
    out = pl.pallas_call(
        functools.partial(_attn_decode_kernel, pages=pages, heads=heads, da=da,
                          lam_init=lam_init),
        grid_spec=grid_spec,
        out_shape=jax.ShapeDtypeStruct((db, heads, dv), F32),
        compiler_params=_cparams("parallel", "arbitrary"),
        name="diff_attention_decode",
    )(page_table, two_maps(q), two_maps(k_new), two_maps(v_new), b_far, b_last, b_new,
      *[vec(a) for a in lam_vecs], vec(subln_w), *([kp] * pages), *([vp] * pages))
    return out.reshape(db, heads * dv)


def _unit_lower_inverse(lmats, eye, levels):
    if levels == 0:
        return [eye for _ in lmats]
    xs = [-m for m in lmats]
    tinvs = [eye + x for x in xs]
    if levels == 1:
        return tinvs
    n = eye.shape[0]
    rs = [_mm(x, x) for x in xs]
    for k in range(1, levels):
        if k < levels - 1:
            both = [_mm(jnp.concatenate([r, t], axis=0), r) for r, t in zip(rs, tinvs)]
            rs = [b[:n] for b in both]
            tinvs = [t + b[n:] for t, b in zip(tinvs, both)]
        else:
            tinvs = [t + _mm(t, r) for t, r in zip(tinvs, rs)]
    return tinvs


def _gdn_pre_kernel(x_ref, st_ref, cw_ref, ba_ref, gp_ref,
                    u_ref, w_ref, qd_ref, kd_ref, at_ref, eg_ref,
                    xe_ref, qn_ref, kn_ref, vv_ref, gb_ref, gc_ref, *, tt, heads, dk):
    c = GDN_CHUNK
    i = pl.program_id(1)
    taps = cw_ref.shape[0]
    halo = SUBLANES
    hk = heads * dk

    @pl.when(i == 0)
    def _():
        xe_ref[0:halo, :] = st_ref[...]

    xe_ref[halo:halo + tt, :] = x_ref[...]
    acc = cw_ref[taps - 1:taps, :] * x_ref[...]
    for j in range(taps - 1):
        acc = acc + cw_ref[j:j + 1, :] * xe_ref[pl.ds(halo - (taps - 1) + j, tt), :]
    xe_ref[0:halo, :] = xe_ref[tt:tt + halo, :]
    hcv = _silu(acc)

    for h in range(heads):
        sl = slice(h * dk, (h + 1) * dk)
        qh = hcv[:, h * dk:(h + 1) * dk]
        kh = hcv[:, hk + h * dk:hk + (h + 1) * dk]
        qn_ref[:, sl] = qh * lax.rsqrt(jnp.sum(qh * qh, axis=-1, keepdims=True) + L2_EPS)
        kn_ref[:, sl] = kh * lax.rsqrt(jnp.sum(kh * kh, axis=-1, keepdims=True) + L2_EPS)
    vv_ref[...] = hcv[:, 2 * hk:]

    ba = ba_ref[...]
    lane = lax.broadcasted_iota(jnp.int32, ba.shape, 1)
    xa = ba + gp_ref[1:2, :]
    softplus = jnp.maximum(xa, 0.0) + jnp.log1p(jnp.exp(-jnp.abs(xa)))
    gates = jnp.where(lane < heads, _sigmoid(ba), -jnp.exp(gp_ref[0:1, :]) * softplus)
    gates = jnp.where(lane < 2 * heads, gates, 0.0)
    gb_ref[...] = gates

    tri_l = (lax.broadcasted_iota(jnp.int32, (c, c), 0)
             >= lax.broadcasted_iota(jnp.int32, (c, c), 1)).astype(F32)
    for ch in range(tt // c):
        gc_ref[ch * c:(ch + 1) * c, :] = _mm_hi(tri_l, gates[ch * c:(ch + 1) * c, :])

    ri = lax.broadcasted_iota(jnp.int32, (c, c), 0)
    ci = lax.broadcasted_iota(jnp.int32, (c, c), 1)
    incl = ri >= ci
    strict = ri > ci
    eye = (ri == ci).astype(F32)
    pick = (lax.broadcasted_iota(jnp.int32, (SUBLANES, LANES), 0)
            == lax.broadcasted_iota(jnp.int32, (SUBLANES, LANES), 1)).astype(F32)
    lane_c = lax.broadcasted_iota(jnp.int32, (c, LANES), 1)
    scale = dk ** -0.5
    levels = (c - 1).bit_length()

    n_chunks = tt // c
    group = max(g for g in (8, 4, 2, 1) if n_chunks % g == 0)

    def group_body(gi, carry):
        probs = []
        for cc in range(group):
            ch = gi * group + cc
            rows = pl.ds(pl.multiple_of(ch * c, c), c)
            gb = gb_ref[rows, :]
            gcc = gc_ref[rows, :]
            gc_rows = _mm_nt_hi(pick, gcc)
            for h in range(heads):
                probs.append(dict(
                    ch=ch, rows=rows, h=h, sl=slice(h * dk, (h + 1) * dk),
                    beta=_lane_pick(gb, lane_c, h),
                    gc_col=_lane_pick(gcc, lane_c, heads + h),
                    gc_row=gc_rows[heads + h:heads + h + 1, :]))
        for p in probs:
            p["decay"] = jnp.where(
                incl, jnp.exp(jnp.where(incl, p["gc_col"] - p["gc_row"], 0.0)), 0.0)
            p["k"] = kn_ref[p["rows"], p["sl"]]
            p["k_beta"] = p["k"] * p["beta"]
        kk = [_mm_nt(p["k_beta"], p["k"]) for p in probs]
        lmats = [jnp.where(strict, m * p["decay"], 0.0) for m, p in zip(kk, probs)]
        tinvs = _unit_lower_inverse(lmats, eye, levels)
        for p in probs:
            p["egc"] = jnp.exp(p["gc_col"])
            p["q"] = qn_ref[p["rows"], p["sl"]] * scale
        us = [_mm(t, vv_ref[p["rows"], p["sl"]] * p["beta"]) for t, p in zip(tinvs, probs)]
        ws = [_mm(t, p["k_beta"] * p["egc"]) for t, p in zip(tinvs, probs)]
        ats = [_mm_nt(p["q"], p["k"]) for p in probs]
        for p, u, w, at in zip(probs, us, ws, ats):
            rows, sl, h = p["rows"], p["sl"], p["h"]
            g_last = p["gc_col"][c - 1:c, :]
            u_ref[rows, sl] = u
            w_ref[rows, sl] = w.astype(w_ref.dtype)
            at_ref[rows, h * c:(h + 1) * c] = (at * p["decay"]).astype(at_ref.dtype)
            kd_ref[rows, sl] = p["k"] * jnp.exp(g_last - p["gc_col"])
            qd_ref[rows, sl] = (p["q"] * p["egc"]).astype(qd_ref.dtype)
            eg_ref[p["ch"], :, sl] = jnp.broadcast_to(jnp.exp(g_last), (SUBLANES, dk))
        return carry

    lax.fori_loop(0, n_chunks // group, group_body, 0)


def _gdn_scan_kernel(u_ref, w_ref, qd_ref, kd_ref, at_ref, eg_ref, z_ref, s0_ref, nw_ref,
                     o_ref, sout_ref, s_ref, kt_ref, *, tt, heads, dk):
    c = GDN_CHUNK
    i = pl.program_id(1)
    nb = u_ref.shape[0]

    @pl.when(i == 0)
    def _():
        s_ref[...] = s0_ref[...]

    for b in range(nb):
        for ch in range(tt // c):
            for h in range(heads):
                kt_ref[b, ch * heads + h] = (
                    kd_ref[b, ch * c:(ch + 1) * c, h * dk:(h + 1) * dk].T.astype(BF16))

    chains = [(b, h) for b in range(nb) for h in range(heads)]
    cols = lambda h: slice(h * dk, (h + 1) * dk)

    def chunk_body(ch, carry):
        rows = pl.ds(pl.multiple_of(ch * c, c), c)
        s_bf = [s_ref[b, h].astype(BF16) for b, h in chains]
        w_s = [_mm(w_ref[b, rows, cols(h)], s) for (b, h), s in zip(chains, s_bf)]
        q_s = [_mm(qd_ref[b, rows, cols(h)], s) for (b, h), s in zip(chains, s_bf)]
        v_new = [(u_ref[b, rows, cols(h)] - ws).astype(BF16) for (b, h), ws in zip(chains, w_s)]
        a_v = [_mm(at_ref[b, rows, h * c:(h + 1) * c], v) for (b, h), v in zip(chains, v_new)]
        k_v = [_mm(kt_ref[b, ch * heads + h], v) for (b, h), v in zip(chains, v_new)]
        for (b, h), qs, av, kv in zip(chains, q_s, a_v, k_v):
            s_ref[b, h] = s_ref[b, h] * eg_ref[b, ch, 0:1, cols(h)] + kv
            o_ref[b, rows, cols(h)] = (_rms(qs + av, nw_ref[...], RMS_EPS)
                                       * _silu(z_ref[b, rows, cols(h)]))
        return carry

    lax.fori_loop(0, tt // c, chunk_body, 0)

    @pl.when(i == pl.num_programs(1) - 1)
    def _():
        sout_ref[...] = s_ref[...]


def _gated_deltanet(qkv, ba, z, conv_state, s0, conv_w, a_log, dt_bias, norm_w,
                    batch, seq_pad, heads, tt, scan_batch, scan_tt):
    cq = qkv.shape[1]
    hk = cq // 3
    dk = hk // heads
    c = GDN_CHUNK
    assert c & (c - 1) == 0 and batch % scan_batch == 0
    nt = seq_pad // tt
    nc = tt // c
    taps = conv_w.shape[0]
    st = jnp.pad(conv_state, ((0, 0), (SUBLANES - (taps - 1), 0), (0, 0)))
    gp = jnp.zeros((SUBLANES, LANES), F32)
    gp = gp.at[0, heads:2 * heads].set(a_log.astype(F32))
    gp = gp.at[1, heads:2 * heads].set(dt_bias.astype(F32))
    row_blk = lambda b, i: (b * nt + i, 0)
    const = lambda b, i: (0, 0)
    tok = lambda width: pl.BlockSpec((tt, width), row_blk)
    rows = batch * seq_pad
    u, w, qd, kd, at, eg = pl.pallas_call(
        functools.partial(_gdn_pre_kernel, tt=tt, heads=heads, dk=dk),
        grid=(batch, nt),
        in_specs=[tok(cq),
                  pl.BlockSpec((None, SUBLANES, cq), lambda b, i: (b, 0, 0)),
                  pl.BlockSpec((taps, cq), const),
                  tok(LANES),
                  pl.BlockSpec((SUBLANES, LANES), const)],
        out_specs=[tok(hk), tok(hk), tok(hk), tok(hk), tok(heads * c),
                   pl.BlockSpec((None, nc, SUBLANES, hk), lambda b, i: (b, i, 0, 0))],
        out_shape=[jax.ShapeDtypeStruct((rows, hk), dt) for dt in (F32, BF16, BF16, F32)]
                  + [jax.ShapeDtypeStruct((rows, heads * c), BF16),
                     jax.ShapeDtypeStruct((batch, nt * nc, SUBLANES, hk), F32)],
        scratch_shapes=[pltpu.VMEM((tt + SUBLANES, cq), F32), pltpu.VMEM((tt, hk), F32),
                        pltpu.VMEM((tt, hk), F32), pltpu.VMEM((tt, hk), F32),
                        pltpu.VMEM((tt, LANES), F32), pltpu.VMEM((tt, LANES), F32)],
        compiler_params=_cparams("parallel", "arbitrary"),
        name="gdn_chunk_prepare",
    )(qkv, st, conv_w, ba, gp)

    nb, stt = scan_batch, scan_tt
    snc = stt // c
    seq3 = lambda a: a.reshape(batch, seq_pad, a.shape[-1])
    blk3 = lambda width: pl.BlockSpec((nb, stt, width), lambda g, i: (g, i, 0))
    state_spec = pl.BlockSpec((nb, heads, dk, dk), lambda g, i: (g, 0, 0, 0))
    o, s_new = pl.pallas_call(
        functools.partial(_gdn_scan_kernel, tt=stt, heads=heads, dk=dk),
        grid=(batch // nb, seq_pad // stt),
        in_specs=[blk3(hk), blk3(hk), blk3(hk), blk3(hk), blk3(heads * c),
                  pl.BlockSpec((nb, snc, SUBLANES, hk), lambda g, i: (g, i, 0, 0)),
                  blk3(hk), state_spec,
                  pl.BlockSpec((1, dk), const)],
        out_specs=[blk3(hk), state_spec],
        out_shape=[jax.ShapeDtypeStruct((batch, seq_pad, hk), F32),
                   jax.ShapeDtypeStruct((batch, heads, dk, dk), F32)],
        scratch_shapes=[pltpu.VMEM((nb, heads, dk, dk), F32),
                        pltpu.VMEM((nb, snc * heads, dk, c), BF16)],
        compiler_params=_cparams("parallel", "arbitrary"),
        name="gdn_chunk_scan",
    )(seq3(u), seq3(w), seq3(qd), seq3(kd), seq3(at), eg, seq3(z), s0, norm_w.reshape(1, dk))
    return o.reshape(rows, hk), s_new


def _gdn_token_kernel(x_ref, st_ref, cw_ref, ba_ref, gp_ref, z_ref, s0_ref, nw_ref,
                      o_ref, sout_ref, *, heads, dk):
    nb = x_ref.shape[0]
    hk = heads * dk
    taps = cw_ref.shape[0]
    acc = cw_ref[taps - 1:taps, :] * x_ref[...]
    for j in range(taps - 1):
        acc = acc + cw_ref[j:j + 1, :] * st_ref[j]
    hcv = _silu(acc)

    ba = ba_ref[...]
    lane = lax.broadcasted_iota(jnp.int32, ba.shape, 1)
    xa = ba + gp_ref[1:2, :]
    softplus = jnp.maximum(xa, 0.0) + jnp.log1p(jnp.exp(-jnp.abs(xa)))
    gates = jnp.where(lane < heads, _sigmoid(ba), -jnp.exp(gp_ref[0:1, :]) * softplus)

    for h in range(heads):
        sl = slice(h * dk, (h + 1) * dk)
        qh = hcv[:, h * dk:(h + 1) * dk]
        kh = hcv[:, hk + h * dk:hk + (h + 1) * dk]
        q = qh * lax.rsqrt(jnp.sum(qh * qh, axis=-1, keepdims=True) + L2_EPS) * (dk ** -0.5)
        k = kh * lax.rsqrt(jnp.sum(kh * kh, axis=-1, keepdims=True) + L2_EPS)
        v = hcv[:, 2 * hk + h * dk:2 * hk + (h + 1) * dk]
        beta = _lane_pick(gates, lane, h)
        decay = jnp.exp(_lane_pick(gates, lane, heads + h))
        k_cols = k.T
        out_rows = []
        for b in range(nb):
            state = s0_ref[b, h]
            row = slice(b, b + 1)
            k_s = _mm(k, state)[row]
            v_new = beta[row] * (v[row] - decay[row] * k_s)
            new_state = state * decay[row] + k_cols[:, b:b + 1] * v_new
            sout_ref[b, h] = new_state
            out_rows.append(_mm(q, new_state)[row])
        o = jnp.concatenate(out_rows, axis=0)
        o_ref[:, sl] = _rms(o, nw_ref[...], RMS_EPS) * _silu(z_ref[:, sl])


def _gated_deltanet_token(qkv, ba, z, conv_state, s0, conv_w, a_log, dt_bias, norm_w, heads, nb):
    batch, cq = qkv.shape
    hk = cq // 3
    dk = hk // heads
    taps = conv_w.shape[0]
    assert batch % nb == 0
    gp = jnp.zeros((SUBLANES, LANES), F32)
    gp = gp.at[0, heads:2 * heads].set(a_log.astype(F32))
    gp = gp.at[1, heads:2 * heads].set(dt_bias.astype(F32))
    const = lambda g: (0, 0)
    rows = lambda width: pl.BlockSpec((nb, width), lambda g: (g, 0))
    state_spec = pl.BlockSpec((nb, heads, dk, dk), lambda g: (g, 0, 0, 0))
    return pl.pallas_call(
        functools.partial(_gdn_token_kernel, heads=heads, dk=dk),
        grid=(batch // nb,),
        in_specs=[rows(cq),
                  pl.BlockSpec((taps - 1, nb, cq), lambda g: (0, g, 0)),
                  pl.BlockSpec((taps, cq), const),
                  rows(LANES),
                  pl.BlockSpec((SUBLANES, LANES), const),
                  rows(hk), state_spec,
                  pl.BlockSpec((1, dk), const)],
        out_specs=[rows(hk), state_spec],
        out_shape=[jax.ShapeDtypeStruct((batch, hk), F32),
                   jax.ShapeDtypeStruct((batch, heads, dk, dk), F32)],
        compiler_params=_cparams("parallel"),
        name="gdn_token",
    )(qkv, jnp.transpose(conv_state, (1, 0, 2)), conv_w, ba, gp, z, s0, norm_w.reshape(1, dk))


def _conv_module_kernel(x_ref, buf_ref, nw_ref, w1_ref, b1_ref, w_ref, b_ref, g_ref, bb_ref,
                        o_ref, tail_ref, xe_ref, xs_ref, *, tt, rb):
    i = pl.program_id(1)
    last_tile = pl.num_programs(1) - 2
    taps = w_ref.shape[0]
    halo = buf_ref.shape[0]
    first = halo - (taps - 1)
    sub = w_ref.shape[1]
    dc = o_ref.shape[1]

    @pl.when(i == 0)
    def _():
        xe_ref[0:tt, :] = jnp.zeros((tt, dc), F32)
        xe_ref[tt:tt + halo, :] = buf_ref[...]

    def glu_stage():
        xn = _rms(x_ref[...], nw_ref[...], RMS_EPS).astype(BF16)
        a = jnp.dot(xn, w1_ref[:, :dc], preferred_element_type=F32) + b1_ref[:, :dc]
        g = jnp.dot(xn, w1_ref[:, dc:], preferred_element_type=F32) + b1_ref[:, dc:]
        xe_ref[halo + tt:halo + 2 * tt, :] = a * _sigmoid(g)

    def conv_stage():
        span = tt + halo - sub
        for ph in range(1, sub):
            xs_ref[ph - 1, 0:span, :] = xe_ref[pl.ds(ph, span), :]

        def window(row, size):
            blk, ph = divmod(row, sub)
            if ph == 0:
                return xe_ref[pl.ds(row, size), :]
            return xs_ref[ph - 1, pl.ds(blk * sub, size), :]

        def weight(j):
            return w_ref[j] if rb == sub else jnp.tile(w_ref[j], (rb // sub, 1))

        for r0 in range(0, tt, rb):
            acc = b_ref[...] + weight(0) * window(first + r0, rb)
            for j in range(1, taps):
                acc = acc + weight(j) * window(first + r0 + j, rb)
            mu = jnp.mean(acc, axis=-1, keepdims=True)
            xc = acc - mu
            var = jnp.mean(xc * xc, axis=-1, keepdims=True)
            y = xc * lax.rsqrt(var + LN_EPS) * g_ref[...] + bb_ref[...]
            o_ref[pl.ds(r0, rb), :] = _silu(y)

    glu_stage()
    conv_stage()

    xe_ref[0:halo, :] = xe_ref[tt:tt + halo, :]
    xe_ref[halo:halo + tt, :] = xe_ref[halo + tt:halo + 2 * tt, :]

    @pl.when(i == last_tile)
    def _():
        rows = tail_ref.shape[0]
        tail_ref[...] = xe_ref[halo + 2 * tt - rows:halo + 2 * tt, :]


def _conv_module(x, buf, norm_w, w_pw1_bf16, b_pw1, w_dw, b_dw, ln_g, ln_b, batch, seq_pad, tt):
    d = x.shape[1]
    ch = w_dw.shape[1]
    taps = w_dw.shape[0]
    halo = -(-(taps - 1) // SUBLANES) * SUBLANES
    bufp = jnp.pad(buf, ((0, 0), (halo - (taps - 1), 0), (0, 0)))
    nt = seq_pad // tt
    rb = min(tt, 2 * SUBLANES)
    tail_rows = min(tt, halo)
    const = lambda b, i: (0, 0)
    vec = lambda a: a.reshape(1, -1)
    w_rep = jnp.broadcast_to(w_dw[:, None, :], (taps, SUBLANES, ch))
    return pl.pallas_call(
        functools.partial(_conv_module_kernel, tt=tt, rb=rb),
        grid=(batch, nt + 1),
        in_specs=[pl.BlockSpec((tt, d), lambda b, i: (b * nt + jnp.minimum(i, nt - 1), 0)),
                  pl.BlockSpec((None, halo, ch), lambda b, i: (b, 0, 0)),
                  pl.BlockSpec((1, d), const),
                  pl.BlockSpec((d, 2 * ch), const),
                  pl.BlockSpec((1, 2 * ch), const),
                  pl.BlockSpec((taps, SUBLANES, ch), lambda b, i: (0, 0, 0)),
                  pl.BlockSpec((1, ch), const), pl.BlockSpec((1, ch), const),
                  pl.BlockSpec((1, ch), const)],
        out_specs=[pl.BlockSpec((tt, ch), lambda b, i: (b * nt + jnp.maximum(i - 1, 0), 0)),
                   pl.BlockSpec((None, tail_rows, ch), lambda b, i: (b, 0, 0))],
        out_shape=[jax.ShapeDtypeStruct((batch * seq_pad, ch), F32),
                   jax.ShapeDtypeStruct((batch, tail_rows, ch), F32)],
        scratch_shapes=[pltpu.VMEM((2 * tt + halo, ch), F32),
                        pltpu.VMEM((SUBLANES - 1, tt + halo - SUBLANES, ch), F32)],
        compiler_params=_cparams("parallel", "arbitrary"),
        name="conv_module",
    )(x, bufp, vec(norm_w), w_pw1_bf16, vec(b_pw1), w_rep, vec(b_dw), vec(ln_g), vec(ln_b))


def _conv_token_kernel(x_ref, buf_ref, nw_ref, w1_ref, b1_ref, w_ref, b_ref, g_ref, bb_ref,
                       o_ref, h_ref):
    dc = o_ref.shape[1]
    taps = w_ref.shape[0]
    xn = _rms(x_ref[...], nw_ref[...], RMS_EPS).astype(BF16)
    a = jnp.dot(xn, w1_ref[:, :dc], preferred_element_type=F32) + b1_ref[:, :dc]
    g = jnp.dot(xn, w1_ref[:, dc:], preferred_element_type=F32) + b1_ref[:, dc:]
    h = a * _sigmoid(g)
    h_ref[...] = h
    acc = b_ref[...] + w_ref[taps - 1:taps, :] * h
    for j in range(taps - 1):
        acc = acc + w_ref[j:j + 1, :] * buf_ref[j]
    mu = jnp.mean(acc, axis=-1, keepdims=True)
    xc = acc - mu
    var = jnp.mean(xc * xc, axis=-1, keepdims=True)
    o_ref[...] = _silu(xc * lax.rsqrt(var + LN_EPS) * g_ref[...] + bb_ref[...])


def _conv_module_token(x, buf, norm_w, w_pw1_bf16, b_pw1, w_dw, b_dw, ln_g, ln_b):
    batch = x.shape[0]
    ch = w_dw.shape[1]
    vec = lambda a: a.reshape(1, -1)
    operands = (x, jnp.transpose(buf, (1, 0, 2)), vec(norm_w), w_pw1_bf16, vec(b_pw1), w_dw,
                vec(b_dw), vec(ln_g), vec(ln_b))
    whole = lambda a: pl.BlockSpec(a.shape, lambda i, nd=a.ndim: (0,) * nd)
    return pl.pallas_call(
        _conv_token_kernel,
        grid=(1,),
        in_specs=[whole(a) for a in operands],
        out_specs=[pl.BlockSpec((batch, ch), lambda i: (0, 0))] * 2,
        out_shape=[jax.ShapeDtypeStruct((batch, ch), F32)] * 2,
        compiler_params=_cparams("arbitrary"),
        name="conv_module_token",
    )(*operands)


def kernel(x_prompt, x_sample, cache_attn_k, cache_attn_v, page_table, state_gdn_conv, state_gdn_s, state_conv_buf, norm_mix, norm_ffn, norm_final, w_in, w_out, gdn_conv_w, gdn_A_log, gdn_dt_bias, gdn_norm_w, lam_q1, lam_k1, lam_q2, lam_k2, diff_subln_w, rel_bias, conv_w_pw1, conv_b_pw1, conv_w_dw, conv_b_dw, conv_ln_g, conv_ln_b, conv_w_pw2, conv_b_pw2, ffn_w_gate, ffn_w_up, ffn_w_down):
    bp, seq, d = x_prompt.shape
    db, dseq, _ = x_sample.shape
    assert dseq == 1
    depth = norm_mix.shape[0]
    ha, dva = cache_attn_k.shape[3], cache_attn_v.shape[4]
    hb, dkb = state_gdn_s.shape[2], state_gdn_s.shape[3]
    c_qkv = state_gdn_conv.shape[3]
    gdn_taps = gdn_conv_w.shape[1]
    conv_taps = conv_w_dw.shape[1]
    d_ff = ffn_w_gate.shape[2]
    mp = bp * seq
    assert 2 * hb <= LANES

    tm_in, tm_ffn = 512, 512
    t_attn = min(512, seq)
    tt_gdn = min(512, seq)
    tt_scan = min(256, seq)
    tt_conv = min(256, seq)
    assert tt_conv >= conv_taps - 1 and seq >= gdn_taps - 1

    xp = x_prompt.reshape(mp, d)
    xs = x_sample.reshape(db, d)
    row = lambda a: a.reshape(1, -1)

    sizes = (ha * dva, ha * dva, ha * dva, c_qkv, hb * dkb, LANES)
    offs = [0]
    for s_ in sizes:
        offs.append(offs[-1] + s_)
    groups = tuple(zip(offs[:-1], sizes))
    inproj_outs = [(wd, 1) for wd in sizes] + [(dva, ha), (dva, ha)]

    def last_rows(a, n):
        return jnp.stack([a[(b + 1) * seq - n:(b + 1) * seq] for b in range(bp)])

    w_gate_bf, w_up_bf, w_down_bf = (w.astype(BF16) for w in (ffn_w_gate, ffn_w_up, ffn_w_down))

    k_p, v_p, k_s, v_s, gc_p, gc_s, gs_p, gs_s, cb_p, cb_s = ([] for _ in range(10))
    for layer in range(depth):
        if layer % 2 == 0:
            e = layer // 2
            lam_init = 0.8 - 0.6 * math.exp(-0.3 * layer)
            w_in_e = jnp.pad(w_in[e], ((0, 0), (0, offs[-1] - w_in.shape[2]))).astype(BF16)
            outs_p, outs_s = _token_call(
                functools.partial(_inproj_body, groups=groups, heads=ha, kv_groups=(1, 2)),
                [xp], [xs], [_whole(row(norm_mix[layer])), _whole(w_in_e)],
                inproj_outs, tm_in, "in_projection")
            qa_p, ka_p, va_p, qkv_p, z_p, ba_p, k4_p, v4_p = outs_p
            qa_s, ka_s, va_s, qkv_s, z_s, ba_s, k4_s, v4_s = outs_s
            lam_vecs = (lam_q1[e], lam_k1[e], lam_q2[e], lam_k2[e])

            oa_p = _attention_prompt(qa_p, ka_p, va_p, rel_bias, lam_vecs, diff_subln_w[e],
                                     bp, seq, ha, lam_init, t=t_attn)
            oa_s = _attention_decode(qa_s, ka_s, va_s, cache_attn_k[e], cache_attn_v[e],
                                     page_table, rel_bias, lam_vecs, diff_subln_w[e],
                                     ha, lam_init, pages=min(32, page_table.shape[1]))

            gdn_w = (gdn_conv_w[e], gdn_A_log[e], gdn_dt_bias[e], gdn_norm_w[e])
            ob_p, s_p = _gated_deltanet(
                qkv_p, ba_p, z_p, jnp.zeros((bp, gdn_taps - 1, c_qkv), F32),
                jnp.zeros((bp, hb, dkb, dkb), F32), *gdn_w, bp, seq, hb,
                tt=tt_gdn, scan_batch=bp, scan_tt=tt_scan)
            ob_s, s_s = _gated_deltanet_token(qkv_s, ba_s, z_s, state_gdn_conv[e],
                                              state_gdn_s[e], *gdn_w, hb, nb=min(db, SUBLANES))

            w_o = w_out[e].astype(BF16)
            mix_p, mix_s = [oa_p, ob_p], [oa_s, ob_s]
            mix_w = [_whole(w_o[:ha * dva]), _whole(w_o[ha * dva:]),
                     _whole(jnp.zeros((1, d), F32))]

            k_p.append(k4_p.reshape(bp, seq, ha, dva))
            v_p.append(v4_p.reshape(bp, seq, ha, dva))
            k_s.append(k4_s.reshape(db, 1, ha, dva))
            v_s.append(v4_s.reshape(db, 1, ha, dva))
            gc_p.append(last_rows(qkv_p, gdn_taps - 1))
            gc_s.append(jnp.concatenate([state_gdn_conv[e], qkv_s.reshape(db, 1, c_qkv)],
                                        axis=1)[:, -(gdn_taps - 1):])
            gs_p.append(s_p)
            gs_s.append(s_s)
        else:
            cidx = layer // 2
            dconv = conv_w_dw.shape[2]
            conv_w = (norm_mix[layer], conv_w_pw1[cidx].astype(BF16), conv_b_pw1[cidx],
                      conv_w_dw[cidx], conv_b_dw[cidx], conv_ln_g[cidx], conv_ln_b[cidx])
            hc_p, tail_p = _conv_module(xp, jnp.zeros((bp, conv_taps - 1, dconv), F32), *conv_w,
                                        bp, seq, tt=tt_conv)
            hc_s, hg_s = _conv_module_token(xs, state_conv_buf[cidx], *conv_w)
            mix_p, mix_s = [hc_p], [hc_s]
            mix_w = [_whole(conv_w_pw2[cidx].astype(BF16)), _whole(row(conv_b_pw2[cidx]))]
            cb_p.append(tail_p[:, tail_p.shape[1] - (conv_taps - 1):])
            cb_s.append(jnp.concatenate([state_conv_buf[cidx], hg_s[:, None, :]],
                                        axis=1)[:, -(conv_taps - 1):])
        (xp,), (xs,) = _token_call(
            functools.partial(_mix_ffn_body, final_norm=(layer == depth - 1)),
            mix_p + [xp], mix_s + [xs],
            [_whole(row(norm_ffn[layer])),
             (w_gate_bf, (None, d, d_ff), (layer, 0, 0)),
             (w_up_bf, (None, d, d_ff), (layer, 0, 0)),
             (w_down_bf, (None, d_ff, d), (layer, 0, 0)),
             _whole(row(norm_final))] + mix_w,
            [(d, 1)], tm_ffn, "mixer_out_swiglu_ffn")

    y_prompt = xp.reshape(bp, seq, d)
    y_sample = xs.reshape(db, 1, d)
    return (y_prompt, y_sample, jnp.stack(k_p), jnp.stack(v_p), jnp.stack(k_s), jnp.stack(v_s),
            jnp.stack(gc_p), jnp.stack(gc_s), jnp.stack(gs_p), jnp.stack(gs_s),
            jnp.stack(cb_p), jnp.stack(cb_s))
```

```python
import functools
import math

import jax
import jax.numpy as jnp
from jax import lax
from jax.experimental import pallas as pl
from jax.experimental.pallas import tpu as pltpu

F32 = jnp.float32
BF16 = jnp.bfloat16

RMS_EPS = 1e-6
LN_EPS = 1e-5
L2_EPS = 1e-6
NUM_BUCKETS = 32
MAX_DISTANCE = 128
GDN_CHUNK = 64
NEG_BIG = -1e30
LOG2E = math.log2(math.e)
LANES = 128
SUBLANES = 8
VMEM_LIMIT = 48 * 1024 * 1024

_HI = lax.Precision.HIGHEST


def _cparams(*sem):
    return pltpu.CompilerParams(dimension_semantics=sem, vmem_limit_bytes=VMEM_LIMIT)


def _mm(a, b):
    return jnp.dot(a.astype(BF16), b.astype(BF16), preferred_element_type=F32)


def _mm_nt(a, b):
    return lax.dot_general(a.astype(BF16), b.astype(BF16), (((1,), (1,)), ((), ())),
                           preferred_element_type=F32)


def _mm_tn(a, b):
    return lax.dot_general(a.astype(BF16), b.astype(BF16), (((0,), (0,)), ((), ())),
                           preferred_element_type=F32)


def _mm_hi(a, b):
    return jnp.dot(a, b, precision=_HI, preferred_element_type=F32)


def _mm_nt_hi(a, b):
    return lax.dot_general(a, b, (((1,), (1,)), ((), ())), precision=_HI,
                           preferred_element_type=F32)


def _rms(x, w, eps):
    return x * lax.rsqrt(jnp.mean(x * x, axis=-1, keepdims=True) + eps) * w


def _sigmoid(x):
    return 1.0 / (1.0 + jnp.exp(-x))


def _silu(x):
    return x * _sigmoid(x)


def _lane_pick(x, lane_idx, k):
    return jnp.sum(jnp.where(lane_idx == k, x, 0.0), axis=-1, keepdims=True)


def _token_call(body, prompt_in, sample_in, shared, outs, tm, name):
    mp, ms = prompt_in[0].shape[0], sample_in[0].shape[0]
    assert mp % tm == 0
    n = mp // tm
    n_in, n_sh, n_out = len(prompt_in), len(shared), len(outs)

    def kern(*refs):
        p_in, s_in = refs[:n_in], refs[n_in:2 * n_in]
        sh = refs[2 * n_in:2 * n_in + n_sh]
        o0 = 2 * n_in + n_sh
        p_out, s_out = refs[o0:o0 + n_out], refs[o0 + n_out:]
        i = pl.program_id(0)

        @pl.when(i < n)
        def _():
            body(p_in, sh, p_out)

        @pl.when(i == n)
        def _():
            body(s_in, sh, s_out)

    prow = lambda i: (jnp.minimum(i, n - 1), 0)
    srow = lambda i: (0, 0)
    in_specs = ([pl.BlockSpec((tm, a.shape[1]), prow) for a in prompt_in]
                + [pl.BlockSpec((ms, a.shape[1]), srow) for a in sample_in]
                + [pl.BlockSpec(bs, (lambda i, idx=idx: idx), pipeline_mode=pl.Buffered(1))
                   for _, bs, idx in shared])
    out_specs = ([pl.BlockSpec((tm * r, w), prow) for w, r, _ in outs]
                 + [pl.BlockSpec((ms * r, w), srow) for w, r, _ in outs])
    out_shape = ([jax.ShapeDtypeStruct((mp * r, w), dt) for w, r, dt in outs]
                 + [jax.ShapeDtypeStruct((ms * r, w), dt) for w, r, dt in outs])
    res = pl.pallas_call(
        kern, grid=(n + 1,), in_specs=in_specs, out_specs=out_specs, out_shape=out_shape,
        compiler_params=_cparams("arbitrary"), name=name,
    )(*prompt_in, *sample_in, *[a for a, _, _ in shared])
    return res[:n_out], res[n_out:]


def _whole(a):
    return (a, a.shape, (0,) * a.ndim)


def _inproj_body(ins, sh, outs, *, groups, scales, heads, kv_groups):
    x_ref, = ins
    nw_ref, w_ref = sh
    xn = _rms(x_ref[...], nw_ref[...], RMS_EPS).astype(BF16)
    rows = x_ref.shape[0]
    n_g = len(groups)
    for g, (o_ref, (off, width)) in enumerate(zip(outs[:n_g], groups)):
        y = jnp.dot(xn, w_ref[:, off:off + width], preferred_element_type=F32)
        o_ref[...] = (y if scales[g] == 1.0 else y * scales[g]).astype(o_ref.dtype)
        if g in kv_groups:
            o4_ref = outs[n_g + kv_groups.index(g)]
            dv = width // heads
            for h in range(heads):
                o4_ref[pl.ds(h, rows, stride=heads), :] = y[:, h * dv:(h + 1) * dv]


def _mix_ffn_body(ins, sh, outs, *, final_norm):
    res_ref = ins[-1]
    nw_ref, wg_ref, wu_ref, wd_ref, fw_ref = sh[:5]
    o_ref, = outs
    x = res_ref[...] + sh[-1][...]
    for a_ref, w_ref in zip(ins[:-1], sh[5:-1]):
        x = x + _mm(a_ref[...], w_ref[...])
    xn = _rms(x, nw_ref[...], RMS_EPS).astype(BF16)
    g = jnp.dot(xn, wg_ref[...], preferred_element_type=F32)
    u = jnp.dot(xn, wu_ref[...], preferred_element_type=F32)
    y = x + _mm(_silu(g) * u, wd_ref[...])
    if final_norm:
        y = _rms(y, fw_ref[...], RMS_EPS)
    o_ref[...] = y


def _t5_bucket(n):
    max_exact = NUM_BUCKETS // 2
    nf = jnp.maximum(n, 1).astype(F32)
    large = max_exact + (jnp.log(nf / max_exact) / math.log(MAX_DISTANCE / max_exact)
                         * (NUM_BUCKETS - max_exact)).astype(jnp.int32)
    large = jnp.minimum(large, NUM_BUCKETS - 1)
    return jnp.where(n < max_exact, n, large)


def _lambda(lq1_ref, lk1_ref, lq2_ref, lk2_ref, lam_init):
    s1 = jnp.sum(lq1_ref[...] * lk1_ref[...], axis=-1, keepdims=True)
    s2 = jnp.sum(lq2_ref[...] * lk2_ref[...], axis=-1, keepdims=True)
    return jnp.exp(s1) - jnp.exp(s2) + lam_init


def _attn_prompt_kernel(q_ref, k_ref, v_ref, bias_ref, lq1_ref, lk1_ref, lq2_ref, lk2_ref,
                        sw_ref, o_ref, qt_ref, vt_ref, m_ref, acc_ref, sa_ref, sb_ref, *,
                        t, da, lam_init):
    i = pl.program_id(2)
    nblk = vt_ref.shape[0]
    dv = v_ref.shape[1]

    @pl.when(i == 0)
    def _():
        for c in range(nblk):
            vt_ref[c, 0:dv, :] = v_ref[c * t:(c + 1) * t, :].astype(F32).T.astype(BF16)
            vt_ref[c, dv:, :] = jnp.ones((vt_ref.shape[1] - dv, t), BF16)

    qt = q_ref[...].astype(F32).T
    first_map = lax.broadcasted_iota(jnp.int32, qt.shape, 0) < da
    qt_ref[:, 0:t] = jnp.where(first_map, qt, 0.0).astype(BF16)
    qt_ref[:, t:2 * t] = jnp.where(first_map, 0.0, qt).astype(BF16)
    m_ref[...] = jnp.full(m_ref.shape, NEG_BIG, F32)
    acc_ref[...] = jnp.zeros(acc_ref.shape, F32)

    def scores(j, buf_ref):
        keys = k_ref[pl.ds(pl.multiple_of(j * t, t), t), :]
        buf_ref[...] = jnp.dot(keys, qt_ref[...], preferred_element_type=F32)

    def update(j, buf_ref, bias):
        s = buf_ref[...]
        if bias is not None:
            s = s + jnp.concatenate([bias, bias], axis=1)
        m_prev = m_ref[...]
        m_new = jnp.maximum(m_prev, jnp.max(s, axis=0, keepdims=True))
        p = jnp.exp2(s - m_new).astype(BF16)
        acc_ref[...] = (acc_ref[...] * jnp.exp2(m_prev - m_new)
                        + jnp.dot(vt_ref[j], p, preferred_element_type=F32))
        m_ref[...] = m_new

    prev_bias, diag_bias = bias_ref.at[0], bias_ref.at[1]
    scores(0, sa_ref)

    def far_pair(jj, carry):
        j = 2 * jj
        scores(j + 1, sb_ref)
        update(j, sa_ref, None)
        scores(j + 2, sa_ref)
        update(j + 1, sb_ref, None)
        return carry

    n_far = i - 1
    lax.fori_loop(0, n_far // 2, far_pair, 0)

    @pl.when(i % 2 == 1)
    def _():
        scores(i, sb_ref)
        update(i - 1, sa_ref, prev_bias[...])
        update(i, sb_ref, diag_bias[...])

    @pl.when((i % 2 == 0) & (i >= 2))
    def _():
        scores(i - 1, sb_ref)
        update(i - 2, sa_ref, None)
        scores(i, sa_ref)
        update(i - 1, sb_ref, prev_bias[...])
        update(i, sa_ref, diag_bias[...])

    @pl.when(i == 0)
    def _():
        update(0, sa_ref, diag_bias[...])

    acc = acc_ref[...]
    o12 = acc[0:dv] / acc[dv:dv + 1]
    lam = _lambda(lq1_ref, lk1_ref, lq2_ref, lk2_ref, lam_init)
    o = (o12[:, 0:t] - lam * o12[:, t:2 * t]).T
    o_ref[...] = (_rms(o, sw_ref[...], LN_EPS) * (1.0 - lam_init)).astype(o_ref.dtype)


def _toeplitz(r, t):
    h, period = r.shape
    flat = jnp.tile(r, (1, t))[:, :t * (period - 1)]
    return flat.reshape(h, t, period - 1)[:, :, :t]


def _prompt_bias_tiles(rel_bias, t):
    d = MAX_DISTANCE
    assert t % d == 0
    nb = t // d
    far = rel_bias[NUM_BUCKETS - 1]
    b1 = jnp.transpose(rel_bias[_t5_bucket(jnp.arange(d))] - far).astype(F32) * LOG2E
    zero = jnp.zeros_like(b1)
    neg = jnp.full_like(b1, NEG_BIG)
    tz = _toeplitz(jnp.concatenate([b1, zero, neg, neg], axis=1), 2 * d)
    g0, g1 = tz[:, :d, :d], tz[:, :d, d:]
    zero_blk, neg_blk = jnp.zeros_like(g0), jnp.full_like(g0, NEG_BIG)

    def diag_block(r, c):
        return neg_blk if c < r else g0 if c == r else g1 if c == r + 1 else zero_blk

    diag = jnp.block([[diag_block(r, c) for c in range(nb)] for r in range(nb)])
    prev = jnp.block([[g1 if (r, c) == (nb - 1, 0) else zero_blk for c in range(nb)]
                      for r in range(nb)])
    return jnp.stack([prev, diag], axis=1)


def _attention_prompt(q, k, v, rel_bias, lam_vecs, subln_w, batch, seq, heads, lam_init, t):
    dv = q.shape[1] // heads
    da = dv // 2
    nq = seq // t
    bias = _prompt_bias_tiles(rel_bias, t)
    vec = lambda a: a.reshape(1, -1)
    const = lambda b, h, i: (0, 0)
    return pl.pallas_call(
        functools.partial(_attn_prompt_kernel, t=t, da=da, lam_init=lam_init),
        grid=(batch, heads, nq),
        in_specs=[pl.BlockSpec((t, dv), lambda b, h, i: (b * nq + i, h)),
                  pl.BlockSpec((seq, dv), lambda b, h, i: (b, h)),
                  pl.BlockSpec((seq, dv), lambda b, h, i: (b, h)),
                  pl.BlockSpec((None, 2, t, t), lambda b, h, i: (h, 0, 0, 0)),
                  pl.BlockSpec((1, da), const), pl.BlockSpec((1, da), const),
                  pl.BlockSpec((1, da), const), pl.BlockSpec((1, da), const),
                  pl.BlockSpec((1, dv), const)],
        out_specs=pl.BlockSpec((t, dv), lambda b, h, i: (b * nq + i, h)),
        out_shape=jax.ShapeDtypeStruct((batch * seq, heads * dv), BF16),
        scratch_shapes=[pltpu.VMEM((dv, 2 * t), BF16),
                        pltpu.VMEM((nq, dv + 2 * SUBLANES, t), BF16),
                        pltpu.VMEM((1, 2 * t), F32),
                        pltpu.VMEM((dv + 2 * SUBLANES, 2 * t), F32),
                        pltpu.VMEM((t, 2 * t), F32), pltpu.VMEM((t, 2 * t), F32)],
        compiler_params=_cparams("parallel", "parallel", "arbitrary"),
        name="diff_attention_prompt",
    )(q, k, v, bias, *[vec(a) for a in lam_vecs], vec(subln_w))


def _attn_decode_kernel(pt_ref, q_ref, kn_ref, vn_ref, bfar_ref, blast_ref, bnew_ref,
                        lq1_ref, lk1_ref, lq2_ref, lk2_ref, sw_ref, *rest,
                        pages, heads, da, lam_init):
    del pt_ref
    k_refs = rest[:pages]
    v_refs = rest[pages:2 * pages]
    o_ref, m_ref, l_ref, acc_ref = rest[2 * pages:]
    j = pl.program_id(1)
    last = pl.num_programs(1) - 1

    @pl.when(j == 0)
    def _():
        m_ref[...] = jnp.full(m_ref.shape, NEG_BIG, F32)
        l_ref[...] = jnp.zeros(l_ref.shape, F32)
        acc_ref[...] = jnp.zeros(acc_ref.shape, F32)

    q = q_ref[...]
    row = lax.broadcasted_iota(jnp.int32, q.shape, 0)
    lane = lax.broadcasted_iota(jnp.int32, q.shape, 1)
    qs = jnp.where((row < heads) == (lane < da), q, 0.0)
    qs_bf = qs.astype(BF16)

    s = jnp.concatenate([_mm_nt(qs_bf, k_ref[...]) for k_ref in k_refs], axis=1)
    s = s + jnp.where(j == last, blast_ref[...], bfar_ref[...])
    m_prev = m_ref[...]
    m_new = jnp.maximum(m_prev, jnp.max(s, axis=-1, keepdims=True))
    p = jnp.exp2(s - m_new)
    alpha = jnp.exp2(m_prev - m_new)
    l_ref[...] = alpha * l_ref[...] + jnp.sum(p, axis=-1, keepdims=True)
    rows_per_page = k_refs[0].shape[0]
    pv = acc_ref[...] * alpha
    for idx, v_ref in enumerate(v_refs):
        pv = pv + _mm(p[:, idx * rows_per_page:(idx + 1) * rows_per_page], v_ref[...])
    acc_ref[...] = pv
    m_ref[...] = m_new

    @pl.when(j == last)
    def _():
        s_new = jnp.sum(qs * kn_ref[...], axis=-1, keepdims=True) + bnew_ref[:, 0:1]
        m_prev = m_ref[...]
        m_fin = jnp.maximum(m_prev, s_new)
        p_new = jnp.exp2(s_new - m_fin)
        alpha = jnp.exp2(m_prev - m_fin)
        l_fin = alpha * l_ref[...] + p_new
        acc = alpha * acc_ref[...] + p_new * vn_ref[...]
        o12 = acc / l_fin
        lam = _lambda(lq1_ref, lk1_ref, lq2_ref, lk2_ref, lam_init)
        o = o12[0:heads] - lam * o12[heads:2 * heads]
        o_ref[...] = _rms(o, sw_ref[...], LN_EPS) * (1.0 - lam_init)


def _attention_decode(q, k_new, v_new, k_pool, v_pool, page_table, rel_bias, lam_vecs,
                      subln_w, heads, lam_init, pages):
    db = q.shape[0]
    n_pool, page, _, dv = k_pool.shape
    da = dv // 2
    n_pages = page_table.shape[1]
    past = n_pages * page
    rpp = page * heads
    span = pages * page
    assert n_pages % pages == 0 and span >= MAX_DISTANCE
    kp = k_pool.reshape(n_pool, rpp, dv)
    vp = v_pool.reshape(n_pool, rpp, dv)

    def two_maps(a):
        a = a.reshape(db, 1, heads, dv)
        return jnp.broadcast_to(a, (db, 2, heads, dv)).reshape(db, 2 * heads, dv)

    rb = rel_bias.astype(F32) * LOG2E
    near = MAX_DISTANCE
    row_head = jnp.arange(2 * heads) % heads
    same = (jnp.arange(span * heads) % heads)[None, :] == row_head[:, None]
    b_far = jnp.where(same, rb[NUM_BUCKETS - 1][row_head][:, None], NEG_BIG)
    tab = jnp.transpose(rb[_t5_bucket(near - jnp.arange(near))])[row_head]
    near_part = jnp.where(same[:, :near * heads], jnp.repeat(tab, heads, axis=1), NEG_BIG)
    b_last = jnp.concatenate([b_far[:, :(span - near) * heads], near_part], axis=1)
    b_new = jnp.tile(jnp.broadcast_to(rb[0][:, None], (heads, LANES)), (2, 1))

    vec = lambda a: a.reshape(1, -1)
    const = lambda s, j, pt: (0, 0)

    def page_spec(idx):
        return pl.BlockSpec((None, rpp, dv), lambda s, j, pt: (pt[s, j * pages + idx], 0, 0))

    grid_spec = pltpu.PrefetchScalarGridSpec(
        num_scalar_prefetch=1,
        grid=(db, n_pages // pages),
        in_specs=([pl.BlockSpec((None, 2 * heads, dv), lambda s, j, pt: (s, 0, 0))] * 3
                  + [pl.BlockSpec((2 * heads, span * heads), const),
                     pl.BlockSpec((2 * heads, span * heads), const),
                     pl.BlockSpec((2 * heads, LANES), const),
                     pl.BlockSpec((1, da), const), pl.BlockSpec((1, da), const),
                     pl.BlockSpec((1, da), const), pl.BlockSpec((1, da), const),
                     pl.BlockSpec((1, dv), const)]
                  + [page_spec(idx) for idx in range(pages)] * 2),
        out_specs=pl.BlockSpec((None, heads, dv), lambda s, j, pt: (s, 0, 0)),
        scratch_shapes=[pltpu.VMEM((2 * heads, 1), F32), pltpu.VMEM((2 * heads, 1), F32),
                        pltpu.VMEM((2 * heads, dv), F32)],
    )
    out = pl.pallas_call(
        functools.partial(_attn_decode_kernel, pages=pages, heads=heads, da=da,
                          lam_init=lam_init),
        grid_spec=grid_spec,
        out_shape=jax.ShapeDtypeStruct((db, heads, dv), F32),
        compiler_params=_cparams("parallel", "arbitrary"),
        name="diff_attention_decode",
    )(page_table, two_maps(q), two_maps(k_new), two_maps(v_new), b_far, b_last, b_new,
      *[vec(a) for a in lam_vecs], vec(subln_w), *([kp] * pages), *([vp] * pages))
    return out.reshape(db, heads * dv)


def _unit_lower_inverse(lmats, eye, levels):
    if levels == 0:
        return [eye for _ in lmats]
    xs = [-m for m in lmats]
    tinvs = [eye + x for x in xs]
    if levels == 1:
        return tinvs
    n = eye.shape[0]
    rs = [_mm(x, x) for x in xs]
    for k in range(1, levels):
        if k < levels - 1:
            both = [_mm(jnp.concatenate([r, t], axis=0), r) for r, t in zip(rs, tinvs)]
            rs = [b[:n] for b in both]
            tinvs = [t + b[n:] for t, b in zip(tinvs, both)]
        else:
            tinvs = [t + _mm(t, r) for t, r in zip(tinvs, rs)]
    return tinvs


def _gdn_pre_kernel(x_ref, st_ref, cw_ref, ba_ref, gp_ref,
                    u_ref, w_ref, qd_ref, kd_ref, at_ref, eg_ref,
                    xe_ref, qn_ref, kn_ref, vv_ref, gb_ref, gc_ref, *, tt, heads, dk):
    c = GDN_CHUNK
    i = pl.program_id(1)
    taps = cw_ref.shape[0]
    halo = SUBLANES
    hk = heads * dk

    @pl.when(i == 0)
    def _():
        xe_ref[0:halo, :] = st_ref[...]

    xe_ref[halo:halo + tt, :] = x_ref[...]
    acc = cw_ref[taps - 1:taps, :] * x_ref[...]
    for j in range(taps - 1):
        acc = acc + cw_ref[j:j + 1, :] * xe_ref[pl.ds(halo - (taps - 1) + j, tt), :]
    xe_ref[0:halo, :] = xe_ref[tt:tt + halo, :]
    hcv = _silu(acc)

    for h in range(heads):
        sl = slice(h * dk, (h + 1) * dk)
        qh = hcv[:, h * dk:(h + 1) * dk]
        kh = hcv[:, hk + h * dk:hk + (h + 1) * dk]
        qn_ref[:, sl] = qh * lax.rsqrt(jnp.sum(qh * qh, axis=-1, keepdims=True) + L2_EPS)
        kn_ref[:, sl] = kh * lax.rsqrt(jnp.sum(kh * kh, axis=-1, keepdims=True) + L2_EPS)
    vv_ref[...] = hcv[:, 2 * hk:]

    ba = ba_ref[...]
    lane = lax.broadcasted_iota(jnp.int32, ba.shape, 1)
    xa = ba + gp_ref[1:2, :]
    softplus = jnp.maximum(xa, 0.0) + jnp.log1p(jnp.exp(-jnp.abs(xa)))
    gates = jnp.where(lane < heads, _sigmoid(ba), -jnp.exp(gp_ref[0:1, :]) * softplus)
    gates = jnp.where(lane < 2 * heads, gates, 0.0)
    gb_ref[...] = gates

    tri_l = (lax.broadcasted_iota(jnp.int32, (c, c), 0)
             >= lax.broadcasted_iota(jnp.int32, (c, c), 1)).astype(F32)
    for ch in range(tt // c):
        gc_ref[ch * c:(ch + 1) * c, :] = _mm_hi(tri_l, gates[ch * c:(ch + 1) * c, :])

    ri = lax.broadcasted_iota(jnp.int32, (c, c), 0)
    ci = lax.broadcasted_iota(jnp.int32, (c, c), 1)
    incl = ri >= ci
    strict = ri > ci
    eye = (ri == ci).astype(F32)
    pick = (lax.broadcasted_iota(jnp.int32, (SUBLANES, LANES), 0)
            == lax.broadcasted_iota(jnp.int32, (SUBLANES, LANES), 1)).astype(F32)
    lane_c = lax.broadcasted_iota(jnp.int32, (c, LANES), 1)
    scale = dk ** -0.5
    levels = (c - 1).bit_length()

    n_chunks = tt // c
    group = max(g for g in (8, 4, 2, 1) if n_chunks % g == 0)

    def group_body(gi, carry):
        probs = []
        for cc in range(group):
            ch = gi * group + cc
            rows = pl.ds(pl.multiple_of(ch * c, c), c)
            gb = gb_ref[rows, :]
            gcc = gc_ref[rows, :]
            gc_rows = _mm_nt_hi(pick, gcc)
            for h in range(heads):
                probs.append(dict(
                    ch=ch, rows=rows, h=h, sl=slice(h * dk, (h + 1) * dk),
                    beta=_lane_pick(gb, lane_c, h),
                    gc_col=_lane_pick(gcc, lane_c, heads + h),
                    gc_row=gc_rows[heads + h:heads + h + 1, :]))
        for p in probs:
            p["decay"] = jnp.where(
                incl, jnp.exp(jnp.where(incl, p["gc_col"] - p["gc_row"], 0.0)), 0.0)
            p["k"] = kn_ref[p["rows"], p["sl"]]
            p["k_beta"] = p["k"] * p["beta"]
        kk = [_mm_nt(p["k_beta"], p["k"]) for p in probs]
        lmats = [jnp.where(strict, m * p["decay"], 0.0) for m, p in zip(kk, probs)]
        tinvs = _unit_lower_inverse(lmats, eye, levels)
        for p in probs:
            p["egc"] = jnp.exp(p["gc_col"])
            p["q"] = qn_ref[p["rows"], p["sl"]] * scale
        us = [_mm(t, vv_ref[p["rows"], p["sl"]] * p["beta"]) for t, p in zip(tinvs, probs)]
        ws = [_mm(t, p["k_beta"] * p["egc"]) for t, p in zip(tinvs, probs)]
        ats = [_mm_nt(p["q"], p["k"]) for p in probs]
        for p, u, w, at in zip(probs, us, ws, ats):
            rows, sl, h = p["rows"], p["sl"], p["h"]
            g_last = p["gc_col"][c - 1:c, :]
            u_ref[rows, sl] = u
            w_ref[rows, sl] = w.astype(w_ref.dtype)
            at_ref[rows, h * c:(h + 1) * c] = (at * p["decay"]).astype(at_ref.dtype)
            kd_ref[rows, sl] = p["k"] * jnp.exp(g_last - p["gc_col"])
            qd_ref[rows, sl] = (p["q"] * p["egc"]).astype(qd_ref.dtype)
            eg_ref[p["ch"], :, sl] = jnp.broadcast_to(jnp.exp(g_last), (SUBLANES, dk))
        return carry

    lax.fori_loop(0, n_chunks // group, group_body, 0)


def _gdn_scan_kernel(u_ref, w_ref, qd_ref, kd_ref, at_ref, eg_ref, z_ref, s0_ref, nw_ref,
                     o_ref, sout_ref, s_ref, kt_ref, *, tt, heads, dk):
    c = GDN_CHUNK
    i = pl.program_id(1)
    nb = u_ref.shape[0]

    @pl.when(i == 0)
    def _():
        s_ref[...] = s0_ref[...]

    for b in range(nb):
        for ch in range(tt // c):
            for h in range(heads):
                kt_ref[b, ch * heads + h] = (
                    kd_ref[b, ch * c:(ch + 1) * c, h * dk:(h + 1) * dk].T.astype(BF16))

    chains = [(b, h) for b in range(nb) for h in range(heads)]
    cols = lambda h: slice(h * dk, (h + 1) * dk)

    def chunk_body(ch, carry):
        rows = pl.ds(pl.multiple_of(ch * c, c), c)
        s_bf = [s_ref[b, h].astype(BF16) for b, h in chains]
        w_s = [_mm(w_ref[b, rows, cols(h)], s) for (b, h), s in zip(chains, s_bf)]
        q_s = [_mm(qd_ref[b, rows, cols(h)], s) for (b, h), s in zip(chains, s_bf)]
        v_new = [(u_ref[b, rows, cols(h)] - ws).astype(BF16) for (b, h), ws in zip(chains, w_s)]
        a_v = [_mm(at_ref[b, rows, h * c:(h + 1) * c], v) for (b, h), v in zip(chains, v_new)]
        k_v = [_mm(kt_ref[b, ch * heads + h], v) for (b, h), v in zip(chains, v_new)]
        for (b, h), qs, av, kv in zip(chains, q_s, a_v, k_v):
            s_ref[b, h] = s_ref[b, h] * eg_ref[b, ch, 0:1, cols(h)] + kv
            o_ref[b, rows, cols(h)] = (_rms(qs + av, nw_ref[...], RMS_EPS)
                                       * _silu(z_ref[b, rows, cols(h)])).astype(o_ref.dtype)
        return carry

    lax.fori_loop(0, tt // c, chunk_body, 0)

    @pl.when(i == pl.num_programs(1) - 1)
    def _():
        sout_ref[...] = s_ref[...]


def _gated_deltanet(qkv, ba, z, conv_state, s0, conv_w, a_log, dt_bias, norm_w,
                    batch, seq_pad, heads, tt, scan_batch, scan_tt):
    cq = qkv.shape[1]
    hk = cq // 3
    dk = hk // heads
    c = GDN_CHUNK
    assert c & (c - 1) == 0 and batch % scan_batch == 0
    nt = seq_pad // tt
    nc = tt // c
    taps = conv_w.shape[0]
    st = jnp.pad(conv_state, ((0, 0), (SUBLANES - (taps - 1), 0), (0, 0)))
    gp = jnp.zeros((SUBLANES, LANES), F32)
    gp = gp.at[0, heads:2 * heads].set(a_log.astype(F32))
    gp = gp.at[1, heads:2 * heads].set(dt_bias.astype(F32))
    row_blk = lambda b, i: (b * nt + i, 0)
    const = lambda b, i: (0, 0)
    tok = lambda width: pl.BlockSpec((tt, width), row_blk)
    rows = batch * seq_pad
    u, w, qd, kd, at, eg = pl.pallas_call(
        functools.partial(_gdn_pre_kernel, tt=tt, heads=heads, dk=dk),
        grid=(batch, nt),
        in_specs=[tok(cq),
                  pl.BlockSpec((None, SUBLANES, cq), lambda b, i: (b, 0, 0)),
                  pl.BlockSpec((taps, cq), const),
                  tok(LANES),
                  pl.BlockSpec((SUBLANES, LANES), const)],
        out_specs=[tok(hk), tok(hk), tok(hk), tok(hk), tok(heads * c),
                   pl.BlockSpec((None, nc, SUBLANES, hk), lambda b, i: (b, i, 0, 0))],
        out_shape=[jax.ShapeDtypeStruct((rows, hk), dt) for dt in (F32, BF16, BF16, F32)]
                  + [jax.ShapeDtypeStruct((rows, heads * c), BF16),
                     jax.ShapeDtypeStruct((batch, nt * nc, SUBLANES, hk), F32)],
        scratch_shapes=[pltpu.VMEM((tt + SUBLANES, cq), F32), pltpu.VMEM((tt, hk), F32),
                        pltpu.VMEM((tt, hk), F32), pltpu.VMEM((tt, hk), F32),
                        pltpu.VMEM((tt, LANES), F32), pltpu.VMEM((tt, LANES), F32)],
        compiler_params=_cparams("parallel", "arbitrary"),
        name="gdn_chunk_prepare",
    )(qkv, st, conv_w, ba, gp)

    nb, stt = scan_batch, scan_tt
    snc = stt // c
    seq3 = lambda a: a.reshape(batch, seq_pad, a.shape[-1])
    blk3 = lambda width: pl.BlockSpec((nb, stt, width), lambda g, i: (g, i, 0))
    state_spec = pl.BlockSpec((nb, heads, dk, dk), lambda g, i: (g, 0, 0, 0))
    o, s_new = pl.pallas_call(
        functools.partial(_gdn_scan_kernel, tt=stt, heads=heads, dk=dk),
        grid=(batch // nb, seq_pad // stt),
        in_specs=[blk3(hk), blk3(hk), blk3(hk), blk3(hk), blk3(heads * c),
                  pl.BlockSpec((nb, snc, SUBLANES, hk), lambda g, i: (g, i, 0, 0)),
                  blk3(hk), state_spec,
                  pl.BlockSpec((1, dk), const)],
        out_specs=[blk3(hk), state_spec],
        out_shape=[jax.ShapeDtypeStruct((batch, seq_pad, hk), BF16),
                   jax.ShapeDtypeStruct((batch, heads, dk, dk), F32)],
        scratch_shapes=[pltpu.VMEM((nb, heads, dk, dk), F32),
                        pltpu.VMEM((nb, snc * heads, dk, c), BF16)],
        compiler_params=_cparams("parallel", "arbitrary"),
        name="gdn_chunk_scan",
    )(seq3(u), seq3(w), seq3(qd), seq3(kd), seq3(at), eg, seq3(z), s0, norm_w.reshape(1, dk))
    return o.reshape(rows, hk), s_new


def _gdn_token_kernel(x_ref, st_ref, cw_ref, ba_ref, gp_ref, z_ref, s0_ref, nw_ref,
                      o_ref, sout_ref, *, heads, dk):
    nb = x_ref.shape[0]
    hk = heads * dk
    taps = cw_ref.shape[0]
    acc = cw_ref[taps - 1:taps, :] * x_ref[...]
    for j in range(taps - 1):
        acc = acc + cw_ref[j:j + 1, :] * st_ref[j]
    hcv = _silu(acc)

    ba = ba_ref[...]
    lane = lax.broadcasted_iota(jnp.int32, ba.shape, 1)
    xa = ba + gp_ref[1:2, :]
    softplus = jnp.maximum(xa, 0.0) + jnp.log1p(jnp.exp(-jnp.abs(xa)))
    gates = jnp.where(lane < heads, _sigmoid(ba), -jnp.exp(gp_ref[0:1, :]) * softplus)

    for h in range(heads):
        sl = slice(h * dk, (h + 1) * dk)
        qh = hcv[:, h * dk:(h + 1) * dk]
        kh = hcv[:, hk + h * dk:hk + (h + 1) * dk]
        q = qh * lax.rsqrt(jnp.sum(qh * qh, axis=-1, keepdims=True) + L2_EPS) * (dk ** -0.5)
        k = kh * lax.rsqrt(jnp.sum(kh * kh, axis=-1, keepdims=True) + L2_EPS)
        v = hcv[:, 2 * hk + h * dk:2 * hk + (h + 1) * dk]
        beta = _lane_pick(gates, lane, h)
        decay = jnp.exp(_lane_pick(gates, lane, heads + h))
        k_cols = k.T
        out_rows = []
        for b in range(nb):
            state = s0_ref[b, h]
            row = slice(b, b + 1)
            k_s = _mm(k, state)[row]
            v_new = beta[row] * (v[row] - decay[row] * k_s)
            new_state = state * decay[row] + k_cols[:, b:b + 1] * v_new
            sout_ref[b, h] = new_state
            out_rows.append(_mm(q, new_state)[row])
        o = jnp.concatenate(out_rows, axis=0)
        o_ref[:, sl] = _rms(o, nw_ref[...], RMS_EPS) * _silu(z_ref[:, sl])


def _gated_deltanet_token(qkv, ba, z, conv_state, s0, conv_w, a_log, dt_bias, norm_w, heads, nb):
    batch, cq = qkv.shape
    hk = cq // 3
    dk = hk // heads
    taps = conv_w.shape[0]
    assert batch % nb == 0
    gp = jnp.zeros((SUBLANES, LANES), F32)
    gp = gp.at[0, heads:2 * heads].set(a_log.astype(F32))
    gp = gp.at[1, heads:2 * heads].set(dt_bias.astype(F32))
    const = lambda g: (0, 0)
    rows = lambda width: pl.BlockSpec((nb, width), lambda g: (g, 0))
    state_spec = pl.BlockSpec((nb, heads, dk, dk), lambda g: (g, 0, 0, 0))
    return pl.pallas_call(
        functools.partial(_gdn_token_kernel, heads=heads, dk=dk),
        grid=(batch // nb,),
        in_specs=[rows(cq),
                  pl.BlockSpec((taps - 1, nb, cq), lambda g: (0, g, 0)),
                  pl.BlockSpec((taps, cq), const),
                  rows(LANES),
                  pl.BlockSpec((SUBLANES, LANES), const),
                  rows(hk), state_spec,
                  pl.BlockSpec((1, dk), const)],
        out_specs=[rows(hk), state_spec],
        out_shape=[jax.ShapeDtypeStruct((batch, hk), F32),
                   jax.ShapeDtypeStruct((batch, heads, dk, dk), F32)],
        compiler_params=_cparams("parallel"),
        name="gdn_token",
    )(qkv, jnp.transpose(conv_state, (1, 0, 2)), conv_w, ba, gp, z, s0, norm_w.reshape(1, dk))


def _conv_module_kernel(x_ref, buf_ref, nw_ref, w1_ref, b1_ref, w_ref, b_ref, g_ref, bb_ref,
                        o_ref, tail_ref, xe_ref, xs_ref, *, tt, rb):
    i = pl.program_id(1)
    last_tile = pl.num_programs(1) - 2
    taps = w_ref.shape[0]
    halo = buf_ref.shape[0]
    first = halo - (taps - 1)
    sub = w_ref.shape[1]
    dc = o_ref.shape[1]

    @pl.when(i == 0)
    def _():
        xe_ref[0:tt, :] = jnp.zeros((tt, dc), F32)
        xe_ref[tt:tt + halo, :] = buf_ref[...]

    def glu_stage():
        xn = _rms(x_ref[...], nw_ref[...], RMS_EPS).astype(BF16)
        a = jnp.dot(xn, w1_ref[:, :dc], preferred_element_type=F32) + b1_ref[:, :dc]
        g = jnp.dot(xn, w1_ref[:, dc:], preferred_element_type=F32) + b1_ref[:, dc:]
        xe_ref[halo + tt:halo + 2 * tt, :] = a * _sigmoid(g)

    def conv_stage():
        span = tt + halo - sub
        for ph in range(1, sub):
            xs_ref[ph - 1, 0:span, :] = xe_ref[pl.ds(ph, span), :]

        def window(row, size):
            blk, ph = divmod(row, sub)
            if ph == 0:
                return xe_ref[pl.ds(row, size), :]
            return xs_ref[ph - 1, pl.ds(blk * sub, size), :]

        def weight(j):
            return w_ref[j] if rb == sub else jnp.tile(w_ref[j], (rb // sub, 1))

        for r0 in range(0, tt, rb):
            acc = b_ref[...] + weight(0) * window(first + r0, rb)
            for j in range(1, taps):
                acc = acc + weight(j) * window(first + r0 + j, rb)
            mu = jnp.mean(acc, axis=-1, keepdims=True)
            xc = acc - mu
            var = jnp.mean(xc * xc, axis=-1, keepdims=True)
            y = xc * lax.rsqrt(var + LN_EPS) * g_ref[...] + bb_ref[...]
            o_ref[pl.ds(r0, rb), :] = _silu(y).astype(o_ref.dtype)

    glu_stage()
    conv_stage()

    xe_ref[0:halo, :] = xe_ref[tt:tt + halo, :]
    xe_ref[halo:halo + tt, :] = xe_ref[halo + tt:halo + 2 * tt, :]

    @pl.when(i == last_tile)
    def _():
        rows = tail_ref.shape[0]
        tail_ref[...] = xe_ref[halo + 2 * tt - rows:halo + 2 * tt, :]


def _conv_module(x, buf, norm_w, w_pw1_bf16, b_pw1, w_dw, b_dw, ln_g, ln_b, batch, seq_pad, tt):
    d = x.shape[1]
    ch = w_dw.shape[1]
    taps = w_dw.shape[0]
    halo = -(-(taps - 1) // SUBLANES) * SUBLANES
    bufp = jnp.pad(buf, ((0, 0), (halo - (taps - 1), 0), (0, 0)))
    nt = seq_pad // tt
    rb = min(tt, 2 * SUBLANES)
    tail_rows = min(tt, halo)
    const = lambda b, i: (0, 0)
    vec = lambda a: a.reshape(1, -1)
    w_rep = jnp.broadcast_to(w_dw[:, None, :], (taps, SUBLANES, ch))
    return pl.pallas_call(
        functools.partial(_conv_module_kernel, tt=tt, rb=rb),
        grid=(batch, nt + 1),
        in_specs=[pl.BlockSpec((tt, d), lambda b, i: (b * nt + jnp.minimum(i, nt - 1), 0)),
                  pl.BlockSpec((None, halo, ch), lambda b, i: (b, 0, 0)),
                  pl.BlockSpec((1, d), const),
                  pl.BlockSpec((d, 2 * ch), const),
                  pl.BlockSpec((1, 2 * ch), const),
                  pl.BlockSpec((taps, SUBLANES, ch), lambda b, i: (0, 0, 0)),
                  pl.BlockSpec((1, ch), const), pl.BlockSpec((1, ch), const),
                  pl.BlockSpec((1, ch), const)],
        out_specs=[pl.BlockSpec((tt, ch), lambda b, i: (b * nt + jnp.maximum(i - 1, 0), 0)),
                   pl.BlockSpec((None, tail_rows, ch), lambda b, i: (b, 0, 0))],
        out_shape=[jax.ShapeDtypeStruct((batch * seq_pad, ch), BF16),
                   jax.ShapeDtypeStruct((batch, tail_rows, ch), F32)],
        scratch_shapes=[pltpu.VMEM((2 * tt + halo, ch), F32),
                        pltpu.VMEM((SUBLANES - 1, tt + halo - SUBLANES, ch), F32)],
        compiler_params=_cparams("parallel", "arbitrary"),
        name="conv_module",
    )(x, bufp, vec(norm_w), w_pw1_bf16, vec(b_pw1), w_rep, vec(b_dw), vec(ln_g), vec(ln_b))


def _conv_token_kernel(x_ref, buf_ref, nw_ref, w1_ref, b1_ref, w_ref, b_ref, g_ref, bb_ref,
                       o_ref, h_ref):
    dc = o_ref.shape[1]
    taps = w_ref.shape[0]
    xn = _rms(x_ref[...], nw_ref[...], RMS_EPS).astype(BF16)
    a = jnp.dot(xn, w1_ref[:, :dc], preferred_element_type=F32) + b1_ref[:, :dc]
    g = jnp.dot(xn, w1_ref[:, dc:], preferred_element_type=F32) + b1_ref[:, dc:]
    h = a * _sigmoid(g)
    h_ref[...] = h
    acc = b_ref[...] + w_ref[taps - 1:taps, :] * h
    for j in range(taps - 1):
        acc = acc + w_ref[j:j + 1, :] * buf_ref[j]
    mu = jnp.mean(acc, axis=-1, keepdims=True)
    xc = acc - mu
    var = jnp.mean(xc * xc, axis=-1, keepdims=True)
    o_ref[...] = _silu(xc * lax.rsqrt(var + LN_EPS) * g_ref[...] + bb_ref[...])


def _conv_module_token(x, buf, norm_w, w_pw1_bf16, b_pw1, w_dw, b_dw, ln_g, ln_b):
    batch = x.shape[0]
    ch = w_dw.shape[1]
    vec = lambda a: a.reshape(1, -1)
    operands = (x, jnp.transpose(buf, (1, 0, 2)), vec(norm_w), w_pw1_bf16, vec(b_pw1), w_dw,
                vec(b_dw), vec(ln_g), vec(ln_b))
    whole = lambda a: pl.BlockSpec(a.shape, lambda i, nd=a.ndim: (0,) * nd)
    return pl.pallas_call(
        _conv_token_kernel,
        grid=(1,),
        in_specs=[whole(a) for a in operands],
        out_specs=[pl.BlockSpec((batch, ch), lambda i: (0, 0))] * 2,
        out_shape=[jax.ShapeDtypeStruct((batch, ch), F32)] * 2,
        compiler_params=_cparams("arbitrary"),
        name="conv_module_token",
    )(*operands)


def kernel(x_prompt, x_sample, cache_attn_k, cache_attn_v, page_table, state_gdn_conv, state_gdn_s, state_conv_buf, norm_mix, norm_ffn, norm_final, w_in, w_out, gdn_conv_w, gdn_A_log, gdn_dt_bias, gdn_norm_w, lam_q1, lam_k1, lam_q2, lam_k2, diff_subln_w, rel_bias, conv_w_pw1, conv_b_pw1, conv_w_dw, conv_b_dw, conv_ln_g, conv_ln_b, conv_w_pw2, conv_b_pw2, ffn_w_gate, ffn_w_up, ffn_w_down):
    bp, seq, d = x_prompt.shape
    db, dseq, _ = x_sample.shape
    assert dseq == 1
    depth = norm_mix.shape[0]
    ha, dva = cache_attn_k.shape[3], cache_attn_v.shape[4]
    hb, dkb = state_gdn_s.shape[2], state_gdn_s.shape[3]
    c_qkv = state_gdn_conv.shape[3]
    gdn_taps = gdn_conv_w.shape[1]
    conv_taps = conv_w_dw.shape[1]
    d_ff = ffn_w_gate.shape[2]
    mp = bp * seq
    assert 2 * hb <= LANES

    tm_in, tm_ffn = 512, 512
    t_attn = min(512, seq)
    tt_gdn = min(512, seq)
    tt_scan = min(256, seq)
    tt_conv = min(256, seq)
    assert tt_conv >= conv_taps - 1 and seq >= gdn_taps - 1

    xp = x_prompt.reshape(mp, d)
    xs = x_sample.reshape(db, d)
    row = lambda a: a.reshape(1, -1)

    sizes = (ha * dva, ha * dva, ha * dva, c_qkv, hb * dkb, LANES)
    offs = [0]
    for s_ in sizes:
        offs.append(offs[-1] + s_)
    groups = tuple(zip(offs[:-1], sizes))
    inproj_dtypes = (BF16, BF16, BF16, F32, F32, F32)
    inproj_scales = ((dva // 2) ** -0.5 * LOG2E, 1.0, 1.0, 1.0, 1.0, 1.0)
    inproj_outs = ([(wd, 1, dt) for wd, dt in zip(sizes, inproj_dtypes)]
                   + [(dva, ha, F32), (dva, ha, F32)])

    def last_rows(a, n):
        return jnp.stack([a[(b + 1) * seq - n:(b + 1) * seq] for b in range(bp)])

    w_gate_bf, w_up_bf, w_down_bf = (w.astype(BF16) for w in (ffn_w_gate, ffn_w_up, ffn_w_down))

    k_p, v_p, k_s, v_s, gc_p, gc_s, gs_p, gs_s, cb_p, cb_s = ([] for _ in range(10))
    for layer in range(depth):
        if layer % 2 == 0:
            e = layer // 2
            lam_init = 0.8 - 0.6 * math.exp(-0.3 * layer)
            w_in_e = jnp.pad(w_in[e], ((0, 0), (0, offs[-1] - w_in.shape[2]))).astype(BF16)
            outs_p, outs_s = _token_call(
                functools.partial(_inproj_body, groups=groups, scales=inproj_scales, heads=ha,
                                  kv_groups=(1, 2)),
                [xp], [xs], [_whole(row(norm_mix[layer])), _whole(w_in_e)],
                inproj_outs, tm_in, "in_projection")
            qa_p, ka_p, va_p, qkv_p, z_p, ba_p, k4_p, v4_p = outs_p
            qa_s, ka_s, va_s, qkv_s, z_s, ba_s, k4_s, v4_s = outs_s
            lam_vecs = (lam_q1[e], lam_k1[e], lam_q2[e], lam_k2[e])

            oa_p = _attention_prompt(qa_p, ka_p, va_p, rel_bias, lam_vecs, diff_subln_w[e],
                                     bp, seq, ha, lam_init, t=t_attn)
            oa_s = _attention_decode(qa_s.astype(F32), k4_s.reshape(db, ha * dva),
                                     v4_s.reshape(db, ha * dva), cache_attn_k[e], cache_attn_v[e],
                                     page_table, rel_bias, lam_vecs, diff_subln_w[e],
                                     ha, lam_init, pages=min(32, page_table.shape[1]))

            gdn_w = (gdn_conv_w[e], gdn_A_log[e], gdn_dt_bias[e], gdn_norm_w[e])
            ob_p, s_p = _gated_deltanet(
                qkv_p, ba_p, z_p, jnp.zeros((bp, gdn_taps - 1, c_qkv), F32),
                jnp.zeros((bp, hb, dkb, dkb), F32), *gdn_w, bp, seq, hb,
                tt=tt_gdn, scan_batch=bp, scan_tt=tt_scan)
            ob_s, s_s = _gated_deltanet_token(qkv_s, ba_s, z_s, state_gdn_conv[e],
                                              state_gdn_s[e], *gdn_w, hb, nb=min(db, SUBLANES))

            w_o = w_out[e].astype(BF16)
            mix_p, mix_s = [oa_p, ob_p], [oa_s, ob_s]
            mix_w = [_whole(w_o[:ha * dva]), _whole(w_o[ha * dva:]),
                     _whole(jnp.zeros((1, d), F32))]

            k_p.append(k4_p.reshape(bp, seq, ha, dva))
            v_p.append(v4_p.reshape(bp, seq, ha, dva))
            k_s.append(k4_s.reshape(db, 1, ha, dva))
            v_s.append(v4_s.reshape(db, 1, ha, dva))
            gc_p.append(last_rows(qkv_p, gdn_taps - 1))
            gc_s.append(jnp.concatenate([state_gdn_conv[e], qkv_s.reshape(db, 1, c_qkv)],
                                        axis=1)[:, -(gdn_taps - 1):])
            gs_p.append(s_p)
            gs_s.append(s_s)
        else:
            cidx = layer // 2
            dconv = conv_w_dw.shape[2]
            conv_w = (norm_mix[layer], conv_w_pw1[cidx].astype(BF16), conv_b_pw1[cidx],
                      conv_w_dw[cidx], conv_b_dw[cidx], conv_ln_g[cidx], conv_ln_b[cidx])
            hc_p, tail_p = _conv_module(xp, jnp.zeros((bp, conv_taps - 1, dconv), F32), *conv_w,
                                        bp, seq, tt=tt_conv)
            hc_s, hg_s = _conv_module_token(xs, state_conv_buf[cidx], *conv_w)
            mix_p, mix_s = [hc_p], [hc_s]
            mix_w = [_whole(conv_w_pw2[cidx].astype(BF16)), _whole(row(conv_b_pw2[cidx]))]
            cb_p.append(tail_p[:, tail_p.shape[1] - (conv_taps - 1):])
            cb_s.append(jnp.concatenate([state_conv_buf[cidx], hg_s[:, None, :]],
                                        axis=1)[:, -(conv_taps - 1):])
        (xp,), (xs,) = _token_call(
            functools.partial(_mix_ffn_body, final_norm=(layer == depth - 1)),
            mix_p + [xp], mix_s + [xs],
            [_whole(row(norm_ffn[layer])),
             (w_gate_bf, (None, d, d_ff), (layer, 0, 0)),
             (w_up_bf, (None, d, d_ff), (layer, 0, 0)),
             (w_down_bf, (None, d_ff, d), (layer, 0, 0)),
             _whole(row(norm_final))] + mix_w,
            [(d, 1, F32)], tm_ffn, "mixer_out_swiglu_ffn")

    y_prompt = xp.reshape(bp, seq, d)
    y_sample = xs.reshape(db, 1, d)
    return (y_prompt, y_sample, jnp.stack(k_p), jnp.stack(v_p), jnp.stack(k_s), jnp.stack(v_s),
            jnp.stack(gc_p), jnp.stack(gc_s), jnp.stack(gs_p), jnp.stack(gs_s),
            jnp.stack(cb_p), jnp.stack(cb_s))
```

```python
import functools
import math

import jax
import jax.numpy as jnp
from jax import lax
from jax.experimental import pallas as pl
from jax.experimental.pallas import tpu as pltpu

F32 = jnp.float32
BF16 = jnp.bfloat16

RMS_EPS = 1e-6
LN_EPS = 1e-5
L2_EPS = 1e-6
NUM_BUCKETS = 32
MAX_DISTANCE = 128
GDN_CHUNK = 64
NEG_BIG = -1e30
LOG2E = math.log2(math.e)
LANES = 128
SUBLANES = 8
VMEM_LIMIT = 48 * 1024 * 1024

_HI = lax.Precision.HIGHEST


def _cparams(*sem):
    return pltpu.CompilerParams(dimension_semantics=sem, vmem_limit_bytes=VMEM_LIMIT)


def _mm(a, b):
    return jnp.dot(a.astype(BF16), b.astype(BF16), preferred_element_type=F32)


def _mm_nt(a, b):
    return lax.dot_general(a.astype(BF16), b.astype(BF16), (((1,), (1,)), ((), ())),
                           preferred_element_type=F32)


def _mm_tn(a, b):
    return lax.dot_general(a.astype(BF16), b.astype(BF16), (((0,), (0,)), ((), ())),
                           preferred_element_type=F32)


def _mm_hi(a, b):
    return jnp.dot(a, b, precision=_HI, preferred_element_type=F32)


def _mm_nt_hi(a, b):
    return lax.dot_general(a, b, (((1,), (1,)), ((), ())), precision=_HI,
                           preferred_element_type=F32)


def _rms(x, w, eps):
    return x * lax.rsqrt(jnp.mean(x * x, axis=-1, keepdims=True) + eps) * w


def _sigmoid(x):
    return 1.0 / (1.0 + jnp.exp(-x))


def _silu(x):
    return x * _sigmoid(x)


def _lane_pick(x, lane_idx, k):
    return jnp.sum(jnp.where(lane_idx == k, x, 0.0), axis=-1, keepdims=True)


def _token_call(body, prompt_in, sample_in, shared, outs, tm, name):
    mp, ms = prompt_in[0].shape[0], sample_in[0].shape[0]
    assert mp % tm == 0
    n = mp // tm
    n_in, n_sh, n_out = len(prompt_in), len(shared), len(outs)

    def kern(*refs):
        p_in, s_in = refs[:n_in], refs[n_in:2 * n_in]
        sh = refs[2 * n_in:2 * n_in + n_sh]
        o0 = 2 * n_in + n_sh
        p_out, s_out = refs[o0:o0 + n_out], refs[o0 + n_out:]
        i = pl.program_id(0)

        @pl.when(i < n)
        def _():
            body(p_in, sh, p_out)

        @pl.when(i == n)
        def _():
            body(s_in, sh, s_out)

    prow = lambda i: (jnp.minimum(i, n - 1), 0)
    srow = lambda i: (0, 0)
    in_specs = ([pl.BlockSpec((tm, a.shape[1]), prow) for a in prompt_in]
                + [pl.BlockSpec((ms, a.shape[1]), srow) for a in sample_in]
                + [pl.BlockSpec(bs, (lambda i, idx=idx: idx), pipeline_mode=pl.Buffered(1))
                   for _, bs, idx in shared])
    out_specs = ([pl.BlockSpec((tm * r, w), prow) for w, r, _ in outs]
                 + [pl.BlockSpec((ms * r, w), srow) for w, r, _ in outs])
    out_shape = ([jax.ShapeDtypeStruct((mp * r, w), dt) for w, r, dt in outs]
                 + [jax.ShapeDtypeStruct((ms * r, w), dt) for w, r, dt in outs])
    res = pl.pallas_call(
        kern, grid=(n + 1,), in_specs=in_specs, out_specs=out_specs, out_shape=out_shape,
        compiler_params=_cparams("arbitrary"), name=name,
    )(*prompt_in, *sample_in, *[a for a, _, _ in shared])
    return res[:n_out], res[n_out:]


def _whole(a):
    return (a, a.shape, (0,) * a.ndim)


def _inproj_body(ins, sh, outs, *, groups, scales, heads, kv_groups):
    x_ref, = ins
    nw_ref, w_ref = sh
    xn = _rms(x_ref[...], nw_ref[...], RMS_EPS).astype(BF16)
    rows = x_ref.shape[0]
    n_g = len(groups)
    for g, (o_ref, (off, width)) in enumerate(zip(outs[:n_g], groups)):
        y = jnp.dot(xn, w_ref[:, off:off + width], preferred_element_type=F32)
        o_ref[...] = (y if scales[g] == 1.0 else y * scales[g]).astype(o_ref.dtype)
        if g in kv_groups:
            o4_ref = outs[n_g + kv_groups.index(g)]
            dv = width // heads
            for h in range(heads):
                o4_ref[pl.ds(h, rows, stride=heads), :] = y[:, h * dv:(h + 1) * dv]


def _mix_ffn_body(ins, sh, outs, *, final_norm):
    res_ref = ins[-1]
    nw_ref, wg_ref, wu_ref, wd_ref, fw_ref = sh[:5]
    o_ref, = outs
    x = res_ref[...] + sh[-1][...]
    for a_ref, w_ref in zip(ins[:-1], sh[5:-1]):
        x = x + _mm(a_ref[...], w_ref[...])
    xn = _rms(x, nw_ref[...], RMS_EPS).astype(BF16)
    g = jnp.dot(xn, wg_ref[...], preferred_element_type=F32)
    u = jnp.dot(xn, wu_ref[...], preferred_element_type=F32)
    y = x + _mm(_silu(g) * u, wd_ref[...])
    if final_norm:
        y = _rms(y, fw_ref[...], RMS_EPS)
    o_ref[...] = y


def _t5_bucket(n):
    max_exact = NUM_BUCKETS // 2
    nf = jnp.maximum(n, 1).astype(F32)
    large = max_exact + (jnp.log(nf / max_exact) / math.log(MAX_DISTANCE / max_exact)
                         * (NUM_BUCKETS - max_exact)).astype(jnp.int32)
    large = jnp.minimum(large, NUM_BUCKETS - 1)
    return jnp.where(n < max_exact, n, large)


def _lambda(lq1_ref, lk1_ref, lq2_ref, lk2_ref, lam_init):
    s1 = jnp.sum(lq1_ref[...] * lk1_ref[...], axis=-1, keepdims=True)
    s2 = jnp.sum(lq2_ref[...] * lk2_ref[...], axis=-1, keepdims=True)
    return jnp.exp(s1) - jnp.exp(s2) + lam_init


def _attn_prompt_kernel(q_ref, k_ref, v_ref, bias_ref, lq1_ref, lk1_ref, lq2_ref, lk2_ref,
                        sw_ref, o_ref, *scratch, t, da, dv, hps, lam_init):
    i = pl.program_id(2)
    qt_refs, vt_refs, m_refs, acc_refs, sa_refs, sb_refs = (
        scratch[n * hps:(n + 1) * hps] for n in range(6))
    nblk = vt_refs[0].shape[0]
    heads = range(hps)
    lanes = lambda hh: slice(hh * dv, (hh + 1) * dv)

    @pl.when(i == 0)
    def _():
        for hh in heads:
            for c in range(nblk):
                vt_refs[hh][c, 0:dv, :] = (
                    v_ref[c * t:(c + 1) * t, lanes(hh)].astype(F32).T.astype(BF16))
                vt_refs[hh][c, dv:, :] = jnp.ones((vt_refs[hh].shape[1] - dv, t), BF16)

    for hh in heads:
        qt = q_ref[:, lanes(hh)].astype(F32).T
        first_map = lax.broadcasted_iota(jnp.int32, qt.shape, 0) < da
        qt_refs[hh][:, 0:t] = jnp.where(first_map, qt, 0.0).astype(BF16)
        qt_refs[hh][:, t:2 * t] = jnp.where(first_map, 0.0, qt).astype(BF16)
        m_refs[hh][...] = jnp.full(m_refs[hh].shape, NEG_BIG, F32)
        acc_refs[hh][...] = jnp.zeros(acc_refs[hh].shape, F32)

    def scores(j, buf_refs):
        rows = pl.ds(pl.multiple_of(j * t, t), t)
        for hh in heads:
            buf_refs[hh][...] = jnp.dot(k_ref[rows, lanes(hh)], qt_refs[hh][...],
                                        preferred_element_type=F32)

    def update(j, buf_refs, bias_idx):
        for hh in heads:
            s = buf_refs[hh][...]
            if bias_idx is not None:
                bias = bias_ref[hh, bias_idx]
                s = s + jnp.concatenate([bias, bias], axis=1)
            m_prev = m_refs[hh][...]
            m_new = jnp.maximum(m_prev, jnp.max(s, axis=0, keepdims=True))
            p = jnp.exp2(s - m_new).astype(BF16)
            acc_refs[hh][...] = (acc_refs[hh][...] * jnp.exp2(m_prev - m_new)
                                 + jnp.dot(vt_refs[hh][j], p, preferred_element_type=F32))
            m_refs[hh][...] = m_new

    prev_bias, diag_bias = 0, 1
    scores(0, sa_refs)

    def far_pair(jj, carry):
        j = 2 * jj
        scores(j + 1, sb_refs)
        update(j, sa_refs, None)
        scores(j + 2, sa_refs)
        update(j + 1, sb_refs, None)
        return carry

    n_far = i - 1
    lax.fori_loop(0, n_far // 2, far_pair, 0)

    @pl.when(i % 2 == 1)
    def _():
        scores(i, sb_refs)
        update(i - 1, sa_refs, prev_bias)
        update(i, sb_refs, diag_bias)

    @pl.when((i % 2 == 0) & (i >= 2))
    def _():
        scores(i - 1, sb_refs)
        update(i - 2, sa_refs, None)
        scores(i, sa_refs)
        update(i - 1, sb_refs, prev_bias)
        update(i, sa_refs, diag_bias)

    @pl.when(i == 0)
    def _():
        update(0, sa_refs, diag_bias)

    lam = _lambda(lq1_ref, lk1_ref, lq2_ref, lk2_ref, lam_init)
    for hh in heads:
        acc = acc_refs[hh][...]
        o12 = acc[0:dv] / acc[dv:dv + 1]
        o = (o12[:, 0:t] - lam * o12[:, t:2 * t]).T
        o_ref[:, lanes(hh)] = (_rms(o, sw_ref[...], LN_EPS) * (1.0 - lam_init)).astype(o_ref.dtype)


def _toeplitz(r, t):
    h, period = r.shape
    flat = jnp.tile(r, (1, t))[:, :t * (period - 1)]
    return flat.reshape(h, t, period - 1)[:, :, :t]


def _prompt_bias_tiles(rel_bias, t):
    d = MAX_DISTANCE
    assert t % d == 0
    nb = t // d
    far = rel_bias[NUM_BUCKETS - 1]
    b1 = jnp.transpose(rel_bias[_t5_bucket(jnp.arange(d))] - far).astype(F32) * LOG2E
    zero = jnp.zeros_like(b1)
    neg = jnp.full_like(b1, NEG_BIG)
    tz = _toeplitz(jnp.concatenate([b1, zero, neg, neg], axis=1), 2 * d)
    g0, g1 = tz[:, :d, :d], tz[:, :d, d:]
    zero_blk, neg_blk = jnp.zeros_like(g0), jnp.full_like(g0, NEG_BIG)

    def diag_block(r, c):
        return neg_blk if c < r else g0 if c == r else g1 if c == r + 1 else zero_blk

    diag = jnp.block([[diag_block(r, c) for c in range(nb)] for r in range(nb)])
    prev = jnp.block([[g1 if (r, c) == (nb - 1, 0) else zero_blk for c in range(nb)]
                      for r in range(nb)])
    return jnp.stack([prev, diag], axis=1)


def _attention_prompt(q, k, v, rel_bias, lam_vecs, subln_w, batch, seq, heads, lam_init, t):
    dv = q.shape[1] // heads
    da = dv // 2
    nq = seq // t
    hps = 2 if heads % 2 == 0 else 1
    gw = hps * dv
    bias = _prompt_bias_tiles(rel_bias, t)
    vec = lambda a: a.reshape(1, -1)
    const = lambda b, g, i: (0, 0)
    per_head = lambda shape, dtype: [pltpu.VMEM(shape, dtype)] * hps
    return pl.pallas_call(
        functools.partial(_attn_prompt_kernel, t=t, da=da, dv=dv, hps=hps, lam_init=lam_init),
        grid=(batch, heads // hps, nq),
        in_specs=[pl.BlockSpec((t, gw), lambda b, g, i: (b * nq + i, g)),
                  pl.BlockSpec((seq, gw), lambda b, g, i: (b, g)),
                  pl.BlockSpec((seq, gw), lambda b, g, i: (b, g)),
                  pl.BlockSpec((hps, 2, t, t), lambda b, g, i: (g, 0, 0, 0)),
                  pl.BlockSpec((1, da), const), pl.BlockSpec((1, da), const),
                  pl.BlockSpec((1, da), const), pl.BlockSpec((1, da), const),
                  pl.BlockSpec((1, dv), const)],
        out_specs=pl.BlockSpec((t, gw), lambda b, g, i: (b * nq + i, g)),
        out_shape=jax.ShapeDtypeStruct((batch * seq, heads * dv), BF16),
        scratch_shapes=(per_head((dv, 2 * t), BF16)
                        + per_head((nq, dv + 2 * SUBLANES, t), BF16)
                        + per_head((1, 2 * t), F32)
                        + per_head((dv + 2 * SUBLANES, 2 * t), F32)
                        + per_head((t, 2 * t), F32) + per_head((t, 2 * t), F32)),
        compiler_params=_cparams("parallel", "parallel", "arbitrary"),
        name="diff_attention_prompt",
    )(q, k, v, bias, *[vec(a) for a in lam_vecs], vec(subln_w))


def _attn_decode_kernel(pt_ref, q_ref, kn_ref, vn_ref, bfar_ref, blast_ref, bnew_ref,
                        lq1_ref, lk1_ref, lq2_ref, lk2_ref, sw_ref, *rest,
                        pages, heads, da, lam_init):
    del pt_ref
    k_refs = rest[:pages]
    v_refs = rest[pages:2 * pages]
    o_ref, m_ref, l_ref, acc_ref = rest[2 * pages:]
    j = pl.program_id(1)
    last = pl.num_programs(1) - 1

    @pl.when(j == 0)
    def _():
        m_ref[...] = jnp.full(m_ref.shape, NEG_BIG, F32)
        l_ref[...] = jnp.zeros(l_ref.shape, F32)
        acc_ref[...] = jnp.zeros(acc_ref.shape, F32)

    q = q_ref[...]
    row = lax.broadcasted_iota(jnp.int32, q.shape, 0)
    lane = lax.broadcasted_iota(jnp.int32, q.shape, 1)
    qs = jnp.where((row < heads) == (lane < da), q, 0.0)
    qs_bf = qs.astype(BF16)

    s = jnp.concatenate([_mm_nt(qs_bf, k_ref[...]) for k_ref in k_refs], axis=1)
    s = s + jnp.where(j == last, blast_ref[...], bfar_ref[...])
    m_prev = m_ref[...]
    m_new = jnp.maximum(m_prev, jnp.max(s, axis=-1, keepdims=True))
    p = jnp.exp2(s - m_new)
    alpha = jnp.exp2(m_prev - m_new)
    l_ref[...] = alpha * l_ref[...] + jnp.sum(p, axis=-1, keepdims=True)
    rows_per_page = k_refs[0].shape[0]
    pv = acc_ref[...] * alpha
    for idx, v_ref in enumerate(v_refs):
        pv = pv + _mm(p[:, idx * rows_per_page:(idx + 1) * rows_per_page], v_ref[...])
    acc_ref[...] = pv
    m_ref[...] = m_new

    @pl.when(j == last)
    def _():
        s_new = jnp.sum(qs * kn_ref[...], axis=-1, keepdims=True) + bnew_ref[:, 0:1]
        m_prev = m_ref[...]
        m_fin = jnp.maximum(m_prev, s_new)
        p_new = jnp.exp2(s_new - m_fin)
        alpha = jnp.exp2(m_prev - m_fin)
        l_fin = alpha * l_ref[...] + p_new
        acc = alpha * acc_ref[...] + p_new * vn_ref[...]
        o12 = acc / l_fin
        lam = _lambda(lq1_ref, lk1_ref, lq2_ref, lk2_ref, lam_init)
        o = o12[0:heads] - lam * o12[heads:2 * heads]
        o_ref[...] = _rms(o, sw_ref[...], LN_EPS) * (1.0 - lam_init)


def _attention_decode(q, k_new, v_new, k_pool, v_pool, page_table, rel_bias, lam_vecs,
                      subln_w, heads, lam_init, pages):
    db = q.shape[0]
    n_pool, page, _, dv = k_pool.shape
    da = dv // 2
    n_pages = page_table.shape[1]
    past = n_pages * page
    rpp = page * heads
    span = pages * page
    assert n_pages % pages == 0 and span >= MAX_DISTANCE
    kp = k_pool.reshape(n_pool, rpp, dv)
    vp = v_pool.reshape(n_pool, rpp, dv)

    def two_maps(a):
        a = a.reshape(db, 1, heads, dv)
        return jnp.broadcast_to(a, (db, 2, heads, dv)).reshape(db, 2 * heads, dv)

    rb = rel_bias.astype(F32) * LOG2E
    near = MAX_DISTANCE
    row_head = jnp.arange(2 * heads) % heads
    same = (jnp.arange(span * heads) % heads)[None, :] == row_head[:, None]
    b_far = jnp.where(same, rb[NUM_BUCKETS - 1][row_head][:, None], NEG_BIG)
    tab = jnp.transpose(rb[_t5_bucket(near - jnp.arange(near))])[row_head]
    near_part = jnp.where(same[:, :near * heads], jnp.repeat(tab, heads, axis=1), NEG_BIG)
    b_last = jnp.concatenate([b_far[:, :(span - near) * heads], near_part], axis=1)
    b_new = jnp.tile(jnp.broadcast_to(rb[0][:, None], (heads, LANES)), (2, 1))

    vec = lambda a: a.reshape(1, -1)
    const = lambda s, j, pt: (0, 0)

    def page_spec(idx):
        return pl.BlockSpec((None, rpp, dv), lambda s, j, pt: (pt[s, j * pages + idx], 0, 0))

    grid_spec = pltpu.PrefetchScalarGridSpec(
        num_scalar_prefetch=1,
        grid=(db, n_pages // pages),
        in_specs=([pl.BlockSpec((None, 2 * heads, dv), lambda s, j, pt: (s, 0, 0))] * 3
                  + [pl.BlockSpec((2 * heads, span * heads), const),
                     pl.BlockSpec((2 * heads, span * heads), const),
                     pl.BlockSpec((2 * heads, LANES), const),
                     pl.BlockSpec((1, da), const), pl.BlockSpec((1, da), const),
                     pl.BlockSpec((1, da), const), pl.BlockSpec((1, da), const),
                     pl.BlockSpec((1, dv), const)]
                  + [page_spec(idx) for idx in range(pages)] * 2),
        out_specs=pl.BlockSpec((None, heads, dv), lambda s, j, pt: (s, 0, 0)),
        scratch_shapes=[pltpu.VMEM((2 * heads, 1), F32), pltpu.VMEM((2 * heads, 1), F32),
                        pltpu.VMEM((2 * heads, dv), F32)],
    )
    out = pl.pallas_call(
        functools.partial(_attn_decode_kernel, pages=pages, heads=heads, da=da,
                          lam_init=lam_init),
        grid_spec=grid_spec,
        out_shape=jax.ShapeDtypeStruct((db, heads, dv), F32),
        compiler_params=_cparams("parallel", "arbitrary"),
        name="diff_attention_decode",
    )(page_table, two_maps(q), two_maps(k_new), two_maps(v_new), b_far, b_last, b_new,
      *[vec(a) for a in lam_vecs], vec(subln_w), *([kp] * pages), *([vp] * pages))
    return out.reshape(db, heads * dv)


def _unit_lower_inverse(lmats, eye, levels):
    if levels == 0:
        return [eye for _ in lmats]
    xs = [-m for m in lmats]
    tinvs = [eye + x for x in xs]
    if levels == 1:
        return tinvs
    n = eye.shape[0]
    rs = [_mm(x, x) for x in xs]
    for k in range(1, levels):
        if k < levels - 1:
            both = [_mm(jnp.concatenate([r, t], axis=0), r) for r, t in zip(rs, tinvs)]
            rs = [b[:n] for b in both]
            tinvs = [t + b[n:] for t, b in zip(tinvs, both)]
        else:
            tinvs = [t + _mm(t, r) for t, r in zip(tinvs, rs)]
    return tinvs


def _gdn_pre_kernel(x_ref, st_ref, cw_ref, ba_ref, gp_ref,
                    u_ref, w_ref, qd_ref, kd_ref, at_ref, eg_ref,
                    xe_ref, qn_ref, kn_ref, vv_ref, gb_ref, gc_ref, *, tt, heads, dk):
    c = GDN_CHUNK
    i = pl.program_id(1)
    taps = cw_ref.shape[0]
    halo = SUBLANES
    hk = heads * dk

    @pl.when(i == 0)
    def _():
        xe_ref[0:halo, :] = st_ref[...]

    xe_ref[halo:halo + tt, :] = x_ref[...]
    acc = cw_ref[taps - 1:taps, :] * x_ref[...]
    for j in range(taps - 1):
        acc = acc + cw_ref[j:j + 1, :] * xe_ref[pl.ds(halo - (taps - 1) + j, tt), :]
    xe_ref[0:halo, :] = xe_ref[tt:tt + halo, :]
    hcv = _silu(acc)

    for h in range(heads):
        sl = slice(h * dk, (h + 1) * dk)
        qh = hcv[:, h * dk:(h + 1) * dk]
        kh = hcv[:, hk + h * dk:hk + (h + 1) * dk]
        qn_ref[:, sl] = qh * lax.rsqrt(jnp.sum(qh * qh, axis=-1, keepdims=True) + L2_EPS)
        kn_ref[:, sl] = kh * lax.rsqrt(jnp.sum(kh * kh, axis=-1, keepdims=True) + L2_EPS)
    vv_ref[...] = hcv[:, 2 * hk:]

    ba = ba_ref[...]
    lane = lax.broadcasted_iota(jnp.int32, ba.shape, 1)
    xa = ba + gp_ref[1:2, :]
    softplus = jnp.maximum(xa, 0.0) + jnp.log1p(jnp.exp(-jnp.abs(xa)))
    gates = jnp.where(lane < heads, _sigmoid(ba), -jnp.exp(gp_ref[0:1, :]) * softplus)
    gates = jnp.where(lane < 2 * heads, gates, 0.0)
    gb_ref[...] = gates

    tri_l = (lax.broadcasted_iota(jnp.int32, (c, c), 0)
             >= lax.broadcasted_iota(jnp.int32, (c, c), 1)).astype(F32)
    for ch in range(tt // c):
        gc_ref[ch * c:(ch + 1) * c, :] = _mm_hi(tri_l, gates[ch * c:(ch + 1) * c, :])

    ri = lax.broadcasted_iota(jnp.int32, (c, c), 0)
    ci = lax.broadcasted_iota(jnp.int32, (c, c), 1)
    incl = ri >= ci
    strict = ri > ci
    eye = (ri == ci).astype(F32)
    pick = (lax.broadcasted_iota(jnp.int32, (SUBLANES, LANES), 0)
            == lax.broadcasted_iota(jnp.int32, (SUBLANES, LANES), 1)).astype(F32)
    lane_c = lax.broadcasted_iota(jnp.int32, (c, LANES), 1)
    scale = dk ** -0.5
    levels = (c - 1).bit_length()

    n_chunks = tt // c
    group = max(g for g in (8, 4, 2, 1) if n_chunks % g == 0)

    def group_body(gi, carry):
        probs = []
        for cc in range(group):
            ch = gi * group + cc
            rows = pl.ds(pl.multiple_of(ch * c, c), c)
            gb = gb_ref[rows, :]
            gcc = gc_ref[rows, :]
            gc_rows = _mm_nt_hi(pick, gcc)
            for h in range(heads):
                probs.append(dict(
                    ch=ch, rows=rows, h=h, sl=slice(h * dk, (h + 1) * dk),
                    beta=_lane_pick(gb, lane_c, h),
                    gc_col=_lane_pick(gcc, lane_c, heads + h),
                    gc_row=gc_rows[heads + h:heads + h + 1, :]))
        for p in probs:
            p["decay"] = jnp.where(
                incl, jnp.exp(jnp.where(incl, p["gc_col"] - p["gc_row"], 0.0)), 0.0)
            p["k"] = kn_ref[p["rows"], p["sl"]]
            p["k_beta"] = p["k"] * p["beta"]
        kk = [_mm_nt(p["k_beta"], p["k"]) for p in probs]
        lmats = [jnp.where(strict, m * p["decay"], 0.0) for m, p in zip(kk, probs)]
        tinvs = _unit_lower_inverse(lmats, eye, levels)
        for p in probs:
            p["egc"] = jnp.exp(p["gc_col"])
            p["q"] = qn_ref[p["rows"], p["sl"]] * scale
        us = [_mm(t, vv_ref[p["rows"], p["sl"]] * p["beta"]) for t, p in zip(tinvs, probs)]
        ws = [_mm(t, p["k_beta"] * p["egc"]) for t, p in zip(tinvs, probs)]
        ats = [_mm_nt(p["q"], p["k"]) for p in probs]
        for p, u, w, at in zip(probs, us, ws, ats):
            rows, sl, h = p["rows"], p["sl"], p["h"]
            g_last = p["gc_col"][c - 1:c, :]
            u_ref[rows, sl] = u
            w_ref[rows, sl] = w.astype(w_ref.dtype)
            at_ref[rows, h * c:(h + 1) * c] = (at * p["decay"]).astype(at_ref.dtype)
            kd_ref[rows, sl] = p["k"] * jnp.exp(g_last - p["gc_col"])
            qd_ref[rows, sl] = (p["q"] * p["egc"]).astype(qd_ref.dtype)
            eg_ref[p["ch"], :, sl] = jnp.broadcast_to(jnp.exp(g_last), (SUBLANES, dk))
        return carry

    lax.fori_loop(0, n_chunks // group, group_body, 0)


def _gdn_scan_kernel(u_ref, w_ref, qd_ref, kd_ref, at_ref, eg_ref, z_ref, s0_ref, nw_ref,
                     o_ref, sout_ref, s_ref, kt_ref, *, tt, heads, dk):
    c = GDN_CHUNK
    i = pl.program_id(1)
    nb = u_ref.shape[0]

    @pl.when(i == 0)
    def _():
        s_ref[...] = s0_ref[...]

    for b in range(nb):
        for ch in range(tt // c):
            for h in range(heads):
                kt_ref[b, ch * heads + h] = (
                    kd_ref[b, ch * c:(ch + 1) * c, h * dk:(h + 1) * dk].T.astype(BF16))

    chains = [(b, h) for b in range(nb) for h in range(heads)]
    cols = lambda h: slice(h * dk, (h + 1) * dk)

    def chunk_body(ch, carry):
        rows = pl.ds(pl.multiple_of(ch * c, c), c)
        s_bf = [s_ref[b, h].astype(BF16) for b, h in chains]
        w_s = [_mm(w_ref[b, rows, cols(h)], s) for (b, h), s in zip(chains, s_bf)]
        q_s = [_mm(qd_ref[b, rows, cols(h)], s) for (b, h), s in zip(chains, s_bf)]
        v_new = [(u_ref[b, rows, cols(h)] - ws).astype(BF16) for (b, h), ws in zip(chains, w_s)]
        a_v = [_mm(at_ref[b, rows, h * c:(h + 1) * c], v) for (b, h), v in zip(chains, v_new)]
        k_v = [_mm(kt_ref[b, ch * heads + h], v) for (b, h), v in zip(chains, v_new)]
        for (b, h), qs, av, kv in zip(chains, q_s, a_v, k_v):
            s_ref[b, h] = s_ref[b, h] * eg_ref[b, ch, 0:1, cols(h)] + kv
            o_ref[b, rows, cols(h)] = (_rms(qs + av, nw_ref[...], RMS_EPS)
                                       * _silu(z_ref[b, rows, cols(h)])).astype(o_ref.dtype)
        return carry

    lax.fori_loop(0, tt // c, chunk_body, 0)

    @pl.when(i == pl.num_programs(1) - 1)
    def _():
        sout_ref[...] = s_ref[...]


def _gated_deltanet(qkv, ba, z, conv_state, s0, conv_w, a_log, dt_bias, norm_w,
                    batch, seq_pad, heads, tt, scan_batch, scan_tt):
    cq = qkv.shape[1]
    hk = cq // 3
    dk = hk // heads
    c = GDN_CHUNK
    assert c & (c - 1) == 0 and batch % scan_batch == 0
    nt = seq_pad // tt
    nc = tt // c
    taps = conv_w.shape[0]
    st = jnp.pad(conv_state, ((0, 0), (SUBLANES - (taps - 1), 0), (0, 0)))
    gp = jnp.zeros((SUBLANES, LANES), F32)
    gp = gp.at[0, heads:2 * heads].set(a_log.astype(F32))
    gp = gp.at[1, heads:2 * heads].set(dt_bias.astype(F32))
    row_blk = lambda b, i: (b * nt + i, 0)
    const = lambda b, i: (0, 0)
    tok = lambda width: pl.BlockSpec((tt, width), row_blk)
    rows = batch * seq_pad
    u, w, qd, kd, at, eg = pl.pallas_call(
        functools.partial(_gdn_pre_kernel, tt=tt, heads=heads, dk=dk),
        grid=(batch, nt),
        in_specs=[tok(cq),
                  pl.BlockSpec((None, SUBLANES, cq), lambda b, i: (b, 0, 0)),
                  pl.BlockSpec((taps, cq), const),
                  tok(LANES),
                  pl.BlockSpec((SUBLANES, LANES), const)],
        out_specs=[tok(hk), tok(hk), tok(hk), tok(hk), tok(heads * c),
                   pl.BlockSpec((None, nc, SUBLANES, hk), lambda b, i: (b, i, 0, 0))],
        out_shape=[jax.ShapeDtypeStruct((rows, hk), dt) for dt in (F32, BF16, BF16, F32)]
                  + [jax.ShapeDtypeStruct((rows, heads * c), BF16),
                     jax.ShapeDtypeStruct((batch, nt * nc, SUBLANES, hk), F32)],
        scratch_shapes=[pltpu.VMEM((tt + SUBLANES, cq), F32), pltpu.VMEM((tt, hk), F32),
                        pltpu.VMEM((tt, hk), F32), pltpu.VMEM((tt, hk), F32),
                        pltpu.VMEM((tt, LANES), F32), pltpu.VMEM((tt, LANES), F32)],
        compiler_params=_cparams("parallel", "arbitrary"),
        name="gdn_chunk_prepare",
    )(qkv, st, conv_w, ba, gp)

    nb, stt = scan_batch, scan_tt
    snc = stt // c
    seq3 = lambda a: a.reshape(batch, seq_pad, a.shape[-1])
    blk3 = lambda width: pl.BlockSpec((nb, stt, width), lambda g, i: (g, i, 0))
    state_spec = pl.BlockSpec((nb, heads, dk, dk), lambda g, i: (g, 0, 0, 0))
    o, s_new = pl.pallas_call(
        functools.partial(_gdn_scan_kernel, tt=stt, heads=heads, dk=dk),
        grid=(batch // nb, seq_pad // stt),
        in_specs=[blk3(hk), blk3(hk), blk3(hk), blk3(hk), blk3(heads * c),
                  pl.BlockSpec((nb, snc, SUBLANES, hk), lambda g, i: (g, i, 0, 0)),
                  blk3(hk), state_spec,
                  pl.BlockSpec((1, dk), const)],
        out_specs=[blk3(hk), state_spec],
        out_shape=[jax.ShapeDtypeStruct((batch, seq_pad, hk), BF16),
                   jax.ShapeDtypeStruct((batch, heads, dk, dk), F32)],
        scratch_shapes=[pltpu.VMEM((nb, heads, dk, dk), F32),
                        pltpu.VMEM((nb, snc * heads, dk, c), BF16)],
        compiler_params=_cparams("parallel", "arbitrary"),
        name="gdn_chunk_scan",
    )(seq3(u), seq3(w), seq3(qd), seq3(kd), seq3(at), eg, seq3(z), s0, norm_w.reshape(1, dk))
    return o.reshape(rows, hk), s_new


def _gdn_token_kernel(x_ref, st_ref, cw_ref, ba_ref, gp_ref, z_ref, s0_ref, nw_ref,
                      o_ref, sout_ref, *, heads, dk):
    nb = x_ref.shape[0]
    hk = heads * dk
    taps = cw_ref.shape[0]
    acc = cw_ref[taps - 1:taps, :] * x_ref[...]
    for j in range(taps - 1):
        acc = acc + cw_ref[j:j + 1, :] * st_ref[j]
    hcv = _silu(acc)

    ba = ba_ref[...]
    lane = lax.broadcasted_iota(jnp.int32, ba.shape, 1)
    xa = ba + gp_ref[1:2, :]
    softplus = jnp.maximum(xa, 0.0) + jnp.log1p(jnp.exp(-jnp.abs(xa)))
    gates = jnp.where(lane < heads, _sigmoid(ba), -jnp.exp(gp_ref[0:1, :]) * softplus)

    for h in range(heads):
        sl = slice(h * dk, (h + 1) * dk)
        qh = hcv[:, h * dk:(h + 1) * dk]
        kh = hcv[:, hk + h * dk:hk + (h + 1) * dk]
        q = qh * lax.rsqrt(jnp.sum(qh * qh, axis=-1, keepdims=True) + L2_EPS) * (dk ** -0.5)
        k = kh * lax.rsqrt(jnp.sum(kh * kh, axis=-1, keepdims=True) + L2_EPS)
        v = hcv[:, 2 * hk + h * dk:2 * hk + (h + 1) * dk]
        beta = _lane_pick(gates, lane, h)
        decay = jnp.exp(_lane_pick(gates, lane, heads + h))
        k_cols = k.T
        out_rows = []
        for b in range(nb):
            state = s0_ref[b, h]
            row = slice(b, b + 1)
            k_s = _mm(k, state)[row]
            v_new = beta[row] * (v[row] - decay[row] * k_s)
            new_state = state * decay[row] + k_cols[:, b:b + 1] * v_new
            sout_ref[b, h] = new_state
            out_rows.append(_mm(q, new_state)[row])
        o = jnp.concatenate(out_rows, axis=0)
        o_ref[:, sl] = _rms(o, nw_ref[...], RMS_EPS) * _silu(z_ref[:, sl])


def _gated_deltanet_token(qkv, ba, z, conv_state, s0, conv_w, a_log, dt_bias, norm_w, heads, nb):
    batch, cq = qkv.shape
    hk = cq // 3
    dk = hk // heads
    taps = conv_w.shape[0]
    assert batch % nb == 0
    gp = jnp.zeros((SUBLANES, LANES), F32)
    gp = gp.at[0, heads:2 * heads].set(a_log.astype(F32))
    gp = gp.at[1, heads:2 * heads].set(dt_bias.astype(F32))
    const = lambda g: (0, 0)
    rows = lambda width: pl.BlockSpec((nb, width), lambda g: (g, 0))
    state_spec = pl.BlockSpec((nb, heads, dk, dk), lambda g: (g, 0, 0, 0))
    return pl.pallas_call(
        functools.partial(_gdn_token_kernel, heads=heads, dk=dk),
        grid=(batch // nb,),
        in_specs=[rows(cq),
                  pl.BlockSpec((taps - 1, nb, cq), lambda g: (0, g, 0)),
                  pl.BlockSpec((taps, cq), const),
                  rows(LANES),
                  pl.BlockSpec((SUBLANES, LANES), const),
                  rows(hk), state_spec,
                  pl.BlockSpec((1, dk), const)],
        out_specs=[rows(hk), state_spec],
        out_shape=[jax.ShapeDtypeStruct((batch, hk), F32),
                   jax.ShapeDtypeStruct((batch, heads, dk, dk), F32)],
        compiler_params=_cparams("parallel"),
        name="gdn_token",
    )(qkv, jnp.transpose(conv_state, (1, 0, 2)), conv_w, ba, gp, z, s0, norm_w.reshape(1, dk))


def _conv_module_kernel(x_ref, buf_ref, nw_ref, w1_ref, b1_ref, w_ref, b_ref, g_ref, bb_ref,
                        o_ref, tail_ref, xe_ref, xs_ref, *, tt, rb):
    i = pl.program_id(1)
    last_tile = pl.num_programs(1) - 2
    taps = w_ref.shape[0]
    halo = buf_ref.shape[0]
    first = halo - (taps - 1)
    sub = w_ref.shape[1]
    dc = o_ref.shape[1]

    @pl.when(i == 0)
    def _():
        xe_ref[0:tt, :] = jnp.zeros((tt, dc), F32)
        xe_ref[tt:tt + halo, :] = buf_ref[...]

    def glu_stage():
        xn = _rms(x_ref[...], nw_ref[...], RMS_EPS).astype(BF16)
        a = jnp.dot(xn, w1_ref[:, :dc], preferred_element_type=F32) + b1_ref[:, :dc]
        g = jnp.dot(xn, w1_ref[:, dc:], preferred_element_type=F32) + b1_ref[:, dc:]
        xe_ref[halo + tt:halo + 2 * tt, :] = a * _sigmoid(g)

    def conv_stage():
        span = tt + halo - sub
        for ph in range(1, sub):
            xs_ref[ph - 1, 0:span, :] = xe_ref[pl.ds(ph, span), :]

        def window(row, size):
            blk, ph = divmod(row, sub)
            if ph == 0:
                return xe_ref[pl.ds(row, size), :]
            return xs_ref[ph - 1, pl.ds(blk * sub, size), :]

        def weight(j):
            return w_ref[j] if rb == sub else jnp.tile(w_ref[j], (rb // sub, 1))

        for r0 in range(0, tt, rb):
            acc = b_ref[...] + weight(0) * window(first + r0, rb)
            for j in range(1, taps):
                acc = acc + weight(j) * window(first + r0 + j, rb)
            mu = jnp.mean(acc, axis=-1, keepdims=True)
            xc = acc - mu
            var = jnp.mean(xc * xc, axis=-1, keepdims=True)
            y = xc * lax.rsqrt(var + LN_EPS) * g_ref[...] + bb_ref[...]
            o_ref[pl.ds(r0, rb), :] = _silu(y).astype(o_ref.dtype)

    glu_stage()
    conv_stage()

    xe_ref[0:halo, :] = xe_ref[tt:tt + halo, :]
    xe_ref[halo:halo + tt, :] = xe_ref[halo + tt:halo + 2 * tt, :]

    @pl.when(i == last_tile)
    def _():
        rows = tail_ref.shape[0]
        tail_ref[...] = xe_ref[halo + 2 * tt - rows:halo + 2 * tt, :]


def _conv_module(x, buf, norm_w, w_pw1_bf16, b_pw1, w_dw, b_dw, ln_g, ln_b, batch, seq_pad, tt):
    d = x.shape[1]
    ch = w_dw.shape[1]
    taps = w_dw.shape[0]
    halo = -(-(taps - 1) // SUBLANES) * SUBLANES
    bufp = jnp.pad(buf, ((0, 0), (halo - (taps - 1), 0), (0, 0)))
    nt = seq_pad // tt
    rb = min(tt, 2 * SUBLANES)
    tail_rows = min(tt, halo)
    const = lambda b, i: (0, 0)
    vec = lambda a: a.reshape(1, -1)
    w_rep = jnp.broadcast_to(w_dw[:, None, :], (taps, SUBLANES, ch))
    return pl.pallas_call(
        functools.partial(_conv_module_kernel, tt=tt, rb=rb),
        grid=(batch, nt + 1),
        in_specs=[pl.BlockSpec((tt, d), lambda b, i: (b * nt + jnp.minimum(i, nt - 1), 0)),
                  pl.BlockSpec((None, halo, ch), lambda b, i: (b, 0, 0)),
                  pl.BlockSpec((1, d), const),
                  pl.BlockSpec((d, 2 * ch), const),
                  pl.BlockSpec((1, 2 * ch), const),
                  pl.BlockSpec((taps, SUBLANES, ch), lambda b, i: (0, 0, 0)),
                  pl.BlockSpec((1, ch), const), pl.BlockSpec((1, ch), const),
                  pl.BlockSpec((1, ch), const)],
        out_specs=[pl.BlockSpec((tt, ch), lambda b, i: (b * nt + jnp.maximum(i - 1, 0), 0)),
                   pl.BlockSpec((None, tail_rows, ch), lambda b, i: (b, 0, 0))],
        out_shape=[jax.ShapeDtypeStruct((batch * seq_pad, ch), BF16),
                   jax.ShapeDtypeStruct((batch, tail_rows, ch), F32)],
        scratch_shapes=[pltpu.VMEM((2 * tt + halo, ch), F32),
                        pltpu.VMEM((SUBLANES - 1, tt + halo - SUBLANES, ch), F32)],
        compiler_params=_cparams("parallel", "arbitrary"),
        name="conv_module",
    )(x, bufp, vec(norm_w), w_pw1_bf16, vec(b_pw1), w_rep, vec(b_dw), vec(ln_g), vec(ln_b))


def _conv_token_kernel(x_ref, buf_ref, nw_ref, w1_ref, b1_ref, w_ref, b_ref, g_ref, bb_ref,
                       o_ref, h_ref):
    dc = o_ref.shape[1]
    taps = w_ref.shape[0]
    xn = _rms(x_ref[...], nw_ref[...], RMS_EPS).astype(BF16)
    a = jnp.dot(xn, w1_ref[:, :dc], preferred_element_type=F32) + b1_ref[:, :dc]
    g = jnp.dot(xn, w1_ref[:, dc:], preferred_element_type=F32) + b1_ref[:, dc:]
    h = a * _sigmoid(g)
    h_ref[...] = h
    acc = b_ref[...] + w_ref[taps - 1:taps, :] * h
    for j in range(taps - 1):
        acc = acc + w_ref[j:j + 1, :] * buf_ref[j]
    mu = jnp.mean(acc, axis=-1, keepdims=True)
    xc = acc - mu
    var = jnp.mean(xc * xc, axis=-1, keepdims=True)
    o_ref[...] = _silu(xc * lax.rsqrt(var + LN_EPS) * g_ref[...] + bb_ref[...])


def _conv_module_token(x, buf, norm_w, w_pw1_bf16, b_pw1, w_dw, b_dw, ln_g, ln_b):
    batch = x.shape[0]
    ch = w_dw.shape[1]
    vec = lambda a: a.reshape(1, -1)
    operands = (x, jnp.transpose(buf, (1, 0, 2)), vec(norm_w), w_pw1_bf16, vec(b_pw1), w_dw,
                vec(b_dw), vec(ln_g), vec(ln_b))
    whole = lambda a: pl.BlockSpec(a.shape, lambda i, nd=a.ndim: (0,) * nd)
    return pl.pallas_call(
        _conv_token_kernel,
        grid=(1,),
        in_specs=[whole(a) for a in operands],
        out_specs=[pl.BlockSpec((batch, ch), lambda i: (0, 0))] * 2,
        out_shape=[jax.ShapeDtypeStruct((batch, ch), F32)] * 2,
        compiler_params=_cparams("arbitrary"),
        name="conv_module_token",
    )(*operands)


def kernel(x_prompt, x_sample, cache_attn_k, cache_attn_v, page_table, state_gdn_conv, state_gdn_s, state_conv_buf, norm_mix, norm_ffn, norm_final, w_in, w_out, gdn_conv_w, gdn_A_log, gdn_dt_bias, gdn_norm_w, lam_q1, lam_k1, lam_q2, lam_k2, diff_subln_w, rel_bias, conv_w_pw1, conv_b_pw1, conv_w_dw, conv_b_dw, conv_ln_g, conv_ln_b, conv_w_pw2, conv_b_pw2, ffn_w_gate, ffn_w_up, ffn_w_down):
    bp, seq, d = x_prompt.shape
    db, dseq, _ = x_sample.shape
    assert dseq == 1
    depth = norm_mix.shape[0]
    ha, dva = cache_attn_k.shape[3], cache_attn_v.shape[4]
    hb, dkb = state_gdn_s.shape[2], state_gdn_s.shape[3]
    c_qkv = state_gdn_conv.shape[3]
    gdn_taps = gdn_conv_w.shape[1]
    conv_taps = conv_w_dw.shape[1]
    d_ff = ffn_w_gate.shape[2]
    mp = bp * seq
    assert 2 * hb <= LANES

    tm_in, tm_ffn = 512, 512
    t_attn = min(512, seq)
    tt_gdn = min(512, seq)
    tt_scan = min(256, seq)
    tt_conv = min(256, seq)
    assert tt_conv >= conv_taps - 1 and seq >= gdn_taps - 1

    xp = x_prompt.reshape(mp, d)
    xs = x_sample.reshape(db, d)
    row = lambda a: a.reshape(1, -1)

    sizes = (ha * dva, ha * dva, ha * dva, c_qkv, hb * dkb, LANES)
    offs = [0]
    for s_ in sizes:
        offs.append(offs[-1] + s_)
    groups = tuple(zip(offs[:-1], sizes))
    inproj_dtypes = (BF16, BF16, BF16, F32, F32, F32)
    inproj_scales = ((dva // 2) ** -0.5 * LOG2E, 1.0, 1.0, 1.0, 1.0, 1.0)
    inproj_outs = ([(wd, 1, dt) for wd, dt in zip(sizes, inproj_dtypes)]
                   + [(dva, ha, F32), (dva, ha, F32)])

    def last_rows(a, n):
        return jnp.stack([a[(b + 1) * seq - n:(b + 1) * seq] for b in range(bp)])

    w_gate_bf, w_up_bf, w_down_bf = (w.astype(BF16) for w in (ffn_w_gate, ffn_w_up, ffn_w_down))

    k_p, v_p, k_s, v_s, gc_p, gc_s, gs_p, gs_s, cb_p, cb_s = ([] for _ in range(10))
    for layer in range(depth):
        if layer % 2 == 0:
            e = layer // 2
            lam_init = 0.8 - 0.6 * math.exp(-0.3 * layer)
            w_in_e = jnp.pad(w_in[e], ((0, 0), (0, offs[-1] - w_in.shape[2]))).astype(BF16)
            outs_p, outs_s = _token_call(
                functools.partial(_inproj_body, groups=groups, scales=inproj_scales, heads=ha,
                                  kv_groups=(1, 2)),
                [xp], [xs], [_whole(row(norm_mix[layer])), _whole(w_in_e)],
                inproj_outs, tm_in, "in_projection")
            qa_p, ka_p, va_p, qkv_p, z_p, ba_p, k4_p, v4_p = outs_p
            qa_s, ka_s, va_s, qkv_s, z_s, ba_s, k4_s, v4_s = outs_s
            lam_vecs = (lam_q1[e], lam_k1[e], lam_q2[e], lam_k2[e])

            oa_p = _attention_prompt(qa_p, ka_p, va_p, rel_bias, lam_vecs, diff_subln_w[e],
                                     bp, seq, ha, lam_init, t=t_attn)
            oa_s = _attention_decode(qa_s.astype(F32), k4_s.reshape(db, ha * dva),
                                     v4_s.reshape(db, ha * dva), cache_attn_k[e], cache_attn_v[e],
                                     page_table, rel_bias, lam_vecs, diff_subln_w[e],
                                     ha, lam_init, pages=min(32, page_table.shape[1]))

            gdn_w = (gdn_conv_w[e], gdn_A_log[e], gdn_dt_bias[e], gdn_norm_w[e])
            ob_p, s_p = _gated_deltanet(
                qkv_p, ba_p, z_p, jnp.zeros((bp, gdn_taps - 1, c_qkv), F32),
                jnp.zeros((bp, hb, dkb, dkb), F32), *gdn_w, bp, seq, hb,
                tt=tt_gdn, scan_batch=bp, scan_tt=tt_scan)
            ob_s, s_s = _gated_deltanet_token(qkv_s, ba_s, z_s, state_gdn_conv[e],
                                              state_gdn_s[e], *gdn_w, hb, nb=min(db, SUBLANES))

            w_o = w_out[e].astype(BF16)
            mix_p, mix_s = [oa_p, ob_p], [oa_s, ob_s]
            mix_w = [_whole(w_o[:ha * dva]), _whole(w_o[ha * dva:]),
                     _whole(jnp.zeros((1, d), F32))]

            k_p.append(k4_p.reshape(bp, seq, ha, dva))
            v_p.append(v4_p.reshape(bp, seq, ha, dva))
            k_s.append(k4_s.reshape(db, 1, ha, dva))
            v_s.append(v4_s.reshape(db, 1, ha, dva))
            gc_p.append(last_rows(qkv_p, gdn_taps - 1))
            gc_s.append(jnp.concatenate([state_gdn_conv[e], qkv_s.reshape(db, 1, c_qkv)],
                                        axis=1)[:, -(gdn_taps - 1):])
            gs_p.append(s_p)
            gs_s.append(s_s)
        else:
            cidx = layer // 2
            dconv = conv_w_dw.shape[2]
            conv_w = (norm_mix[layer], conv_w_pw1[cidx].astype(BF16), conv_b_pw1[cidx],
                      conv_w_dw[cidx], conv_b_dw[cidx], conv_ln_g[cidx], conv_ln_b[cidx])
            hc_p, tail_p = _conv_module(xp, jnp.zeros((bp, conv_taps - 1, dconv), F32), *conv_w,
                                        bp, seq, tt=tt_conv)
            hc_s, hg_s = _conv_module_token(xs, state_conv_buf[cidx], *conv_w)
            mix_p, mix_s = [hc_p], [hc_s]
            mix_w = [_whole(conv_w_pw2[cidx].astype(BF16)), _whole(row(conv_b_pw2[cidx]))]
            cb_p.append(tail_p[:, tail_p.shape[1] - (conv_taps - 1):])
            cb_s.append(jnp.concatenate([state_conv_buf[cidx], hg_s[:, None, :]],
                                        axis=1)[:, -(conv_taps - 1):])
        (xp,), (xs,) = _token_call(
            functools.partial(_mix_ffn_body, final_norm=(layer == depth - 1)),
            mix_p + [xp], mix_s + [xs],
            [_whole(row(norm_ffn[layer])),
             (w_gate_bf, (None, d, d_ff), (layer, 0, 0)),
             (w_up_bf, (None, d, d_ff), (layer, 0, 0)),
             (w_down_bf, (None, d_ff, d), (layer, 0, 0)),
             _whole(row(norm_final))] + mix_w,
            [(d, 1, F32)], tm_ffn, "mixer_out_swiglu_ffn")

    y_prompt = xp.reshape(bp, seq, d)
    y_sample = xs.reshape(db, 1, d)
    return (y_prompt, y_sample, jnp.stack(k_p), jnp.stack(v_p), jnp.stack(k_s), jnp.stack(v_s),
            jnp.stack(gc_p), jnp.stack(gc_s), jnp.stack(gs_p), jnp.stack(gs_s),
            jnp.stack(cb_p), jnp.stack(cb_s))
```

```python
import functools
import math

import jax
import jax.numpy as jnp
from jax import lax
from jax.experimental import pallas as pl
from jax.experimental.pallas import tpu as pltpu

F32 = jnp.float32
BF16 = jnp.bfloat16

RMS_EPS = 1e-6
LN_EPS = 1e-5
L2_EPS = 1e-6
NUM_BUCKETS = 32
MAX_DISTANCE = 128
GDN_CHUNK = 64
NEG_BIG = -1e30
LOG2E = math.log2(math.e)
LANES = 128
SUBLANES = 8
VMEM_LIMIT = 48 * 1024 * 1024


def _cparams(*sem):
    return pltpu.CompilerParams(dimension_semantics=sem, vmem_limit_bytes=VMEM_LIMIT)


def _mm(a, b):
    return jnp.dot(a.astype(BF16), b.astype(BF16), preferred_element_type=F32)


def _mm_nt(a, b):
    return lax.dot_general(a.astype(BF16), b.astype(BF16), (((1,), (1,)), ((), ())),
                           preferred_element_type=F32)


def _mm_tn(a, b):
    return lax.dot_general(a.astype(BF16), b.astype(BF16), (((0,), (0,)), ((), ())),
                           preferred_element_type=F32)


def _split3(x):
    hi = x.astype(BF16)
    r1 = x - hi.astype(F32)
    mid = r1.astype(BF16)
    lo = (r1 - mid.astype(F32)).astype(BF16)
    return hi, mid, lo


def _select_mm(sel, x, nt=False):
    sel = sel.astype(BF16)
    dims = (((1,), (1,)), ((), ())) if nt else (((1,), (0,)), ((), ()))
    parts = [lax.dot_general(sel, p, dims, preferred_element_type=F32) for p in _split3(x)]
    return parts[0] + (parts[1] + parts[2])


def _rms(x, w, eps):
    return x * lax.rsqrt(jnp.mean(x * x, axis=-1, keepdims=True) + eps) * w


def _sigmoid(x):
    return 1.0 / (1.0 + jnp.exp(-x))


def _silu(x):
    return x * _sigmoid(x)


def _lane_pick(x, lane_idx, k):
    return jnp.sum(jnp.where(lane_idx == k, x, 0.0), axis=-1, keepdims=True)


def _token_call(body, prompt_in, sample_in, shared, outs, tm, name):
    mp, ms = prompt_in[0].shape[0], sample_in[0].shape[0]
    assert mp % tm == 0
    n = mp // tm
    n_in, n_sh, n_out = len(prompt_in), len(shared), len(outs)

    def kern(*refs):
        p_in, s_in = refs[:n_in], refs[n_in:2 * n_in]
        sh = refs[2 * n_in:2 * n_in + n_sh]
        o0 = 2 * n_in + n_sh
        p_out, s_out = refs[o0:o0 + n_out], refs[o0 + n_out:]
        i = pl.program_id(0)

        @pl.when(i < n)
        def _():
            body(p_in, sh, p_out)

        @pl.when(i == n)
        def _():
            body(s_in, sh, s_out)

    prow = lambda i: (jnp.minimum(i, n - 1), 0)
    srow = lambda i: (0, 0)
    in_specs = ([pl.BlockSpec((tm, a.shape[1]), prow) for a in prompt_in]
                + [pl.BlockSpec((ms, a.shape[1]), srow) for a in sample_in]
                + [pl.BlockSpec(bs, (lambda i, idx=idx: idx), pipeline_mode=pl.Buffered(1))
                   for _, bs, idx in shared])
    out_specs = ([pl.BlockSpec((tm * r, w), prow) for w, r, _ in outs]
                 + [pl.BlockSpec((ms * r, w), srow) for w, r, _ in outs])
    out_shape = ([jax.ShapeDtypeStruct((mp * r, w), dt) for w, r, dt in outs]
                 + [jax.ShapeDtypeStruct((ms * r, w), dt) for w, r, dt in outs])
    res = pl.pallas_call(
        kern, grid=(n + 1,), in_specs=in_specs, out_specs=out_specs, out_shape=out_shape,
        compiler_params=_cparams("arbitrary"), name=name,
    )(*prompt_in, *sample_in, *[a for a, _, _ in shared])
    return res[:n_out], res[n_out:]


def _whole(a):
    return (a, a.shape, (0,) * a.ndim)


def _inproj_body(ins, sh, outs, *, groups, scales, heads, kv_groups):
    x_ref, = ins
    nw_ref, w_ref = sh
    xn = _rms(x_ref[...], nw_ref[...], RMS_EPS).astype(BF16)
    rows = x_ref.shape[0]
    n_g = len(groups)
    for g, (o_ref, (off, width)) in enumerate(zip(outs[:n_g], groups)):
        y = jnp.dot(xn, w_ref[:, off:off + width], preferred_element_type=F32)
        o_ref[...] = (y if scales[g] == 1.0 else y * scales[g]).astype(o_ref.dtype)
        if g in kv_groups:
            o4_ref = outs[n_g + kv_groups.index(g)]
            dv = width // heads
            for h in range(heads):
                o4_ref[pl.ds(h, rows, stride=heads), :] = y[:, h * dv:(h + 1) * dv]


def _mix_ffn_body(ins, sh, outs, *, final_norm):
    res_ref = ins[-1]
    nw_ref, wg_ref, wu_ref, wd_ref, fw_ref = sh[:5]
    o_ref, = outs
    x = res_ref[...] + sh[-1][...]
    for a_ref, w_ref in zip(ins[:-1], sh[5:-1]):
        x = x + _mm(a_ref[...], w_ref[...])
    xn = _rms(x, nw_ref[...], RMS_EPS).astype(BF16)
    g = jnp.dot(xn, wg_ref[...], preferred_element_type=F32)
    u = jnp.dot(xn, wu_ref[...], preferred_element_type=F32)
    y = x + _mm(_silu(g) * u, wd_ref[...])
    if final_norm:
        y = _rms(y, fw_ref[...], RMS_EPS)
    o_ref[...] = y


def _t5_bucket(n):
    max_exact = NUM_BUCKETS // 2
    nf = jnp.maximum(n, 1).astype(F32)
    large = max_exact + (jnp.log(nf / max_exact) / math.log(MAX_DISTANCE / max_exact)
                         * (NUM_BUCKETS - max_exact)).astype(jnp.int32)
    large = jnp.minimum(large, NUM_BUCKETS - 1)
    return jnp.where(n < max_exact, n, large)


def _lambda(lq1_ref, lk1_ref, lq2_ref, lk2_ref, lam_init):
    s1 = jnp.sum(lq1_ref[...] * lk1_ref[...], axis=-1, keepdims=True)
    s2 = jnp.sum(lq2_ref[...] * lk2_ref[...], axis=-1, keepdims=True)
    return jnp.exp(s1) - jnp.exp(s2) + lam_init


def _attn_prompt_kernel(q_ref, k_ref, v_ref, bias_ref, lq1_ref, lk1_ref, lq2_ref, lk2_ref,
                        sw_ref, o_ref, *scratch, t, da, dv, hps, lam_init):
    i = pl.program_id(2)
    qt_refs, vt_refs, m_refs, acc_refs, sa_refs, sb_refs = (
        scratch[n * hps:(n + 1) * hps] for n in range(6))
    nblk = vt_refs[0].shape[0]
    heads = range(hps)
    lanes = lambda hh: slice(hh * dv, (hh + 1) * dv)

    @pl.when(i == 0)
    def _():
        for hh in heads:
            for c in range(nblk):
                vt_refs[hh][c, 0:dv, :] = (
                    v_ref[c * t:(c + 1) * t, lanes(hh)].astype(F32).T.astype(BF16))
                vt_refs[hh][c, dv:, :] = jnp.ones((vt_refs[hh].shape[1] - dv, t), BF16)

    for hh in heads:
        qt = q_ref[:, lanes(hh)].astype(F32).T
        first_map = lax.broadcasted_iota(jnp.int32, qt.shape, 0) < da
        qt_refs[hh][:, 0:t] = jnp.where(first_map, qt, 0.0).astype(BF16)
        qt_refs[hh][:, t:2 * t] = jnp.where(first_map, 0.0, qt).astype(BF16)
        m_refs[hh][...] = jnp.full(m_refs[hh].shape, NEG_BIG, F32)
        acc_refs[hh][...] = jnp.zeros(acc_refs[hh].shape, F32)

    def scores(j, buf_refs):
        rows = pl.ds(pl.multiple_of(j * t, t), t)
        for hh in heads:
            buf_refs[hh][...] = jnp.dot(k_ref[rows, lanes(hh)], qt_refs[hh][...],
                                        preferred_element_type=F32)

    def update(j, buf_refs, bias_idx):
        for hh in heads:
            s = buf_refs[hh][...]
            if bias_idx is not None:
                bias = bias_ref[hh, bias_idx]
                s = s + jnp.concatenate([bias, bias], axis=1)
            m_prev = m_refs[hh][...]
            m_new = jnp.maximum(m_prev, jnp.max(s, axis=0, keepdims=True))
            p = jnp.exp2(s - m_new).astype(BF16)
            acc_refs[hh][...] = (acc_refs[hh][...] * jnp.exp2(m_prev - m_new)
                                 + jnp.dot(vt_refs[hh][j], p, preferred_element_type=F32))
            m_refs[hh][...] = m_new

    prev_bias, diag_bias = 0, 1
    scores(0, sa_refs)

    def far_pair(jj, carry):
        j = 2 * jj
        scores(j + 1, sb_refs)
        update(j, sa_refs, None)
        scores(j + 2, sa_refs)
        update(j + 1, sb_refs, None)
        return carry

    n_far = i - 1
    lax.fori_loop(0, n_far // 2, far_pair, 0)

    @pl.when(i % 2 == 1)
    def _():
        scores(i, sb_refs)
        update(i - 1, sa_refs, prev_bias)
        update(i, sb_refs, diag_bias)

    @pl.when((i % 2 == 0) & (i >= 2))
    def _():
        scores(i - 1, sb_refs)
        update(i - 2, sa_refs, None)
        scores(i, sa_refs)
        update(i - 1, sb_refs, prev_bias)
        update(i, sa_refs, diag_bias)

    @pl.when(i == 0)
    def _():
        update(0, sa_refs, diag_bias)

    lam = _lambda(lq1_ref, lk1_ref, lq2_ref, lk2_ref, lam_init)
    for hh in heads:
        acc = acc_refs[hh][...]
        o12 = acc[0:dv] / acc[dv:dv + 1]
        o = (o12[:, 0:t] - lam * o12[:, t:2 * t]).T
        o_ref[:, lanes(hh)] = (_rms(o, sw_ref[...], LN_EPS) * (1.0 - lam_init)).astype(o_ref.dtype)


def _toeplitz(r, t):
    h, period = r.shape
    flat = jnp.tile(r, (1, t))[:, :t * (period - 1)]
    return flat.reshape(h, t, period - 1)[:, :, :t]


def _prompt_bias_tiles(rel_bias, t):
    d = MAX_DISTANCE
    assert t % d == 0
    nb = t // d
    far = rel_bias[NUM_BUCKETS - 1]
    b1 = jnp.transpose(rel_bias[_t5_bucket(jnp.arange(d))] - far).astype(F32) * LOG2E
    zero = jnp.zeros_like(b1)
    neg = jnp.full_like(b1, NEG_BIG)
    tz = _toeplitz(jnp.concatenate([b1, zero, neg, neg], axis=1), 2 * d)
    g0, g1 = tz[:, :d, :d], tz[:, :d, d:]
    zero_blk, neg_blk = jnp.zeros_like(g0), jnp.full_like(g0, NEG_BIG)

    def diag_block(r, c):
        return neg_blk if c < r else g0 if c == r else g1 if c == r + 1 else zero_blk

    diag = jnp.block([[diag_block(r, c) for c in range(nb)] for r in range(nb)])
    prev = jnp.block([[g1 if (r, c) == (nb - 1, 0) else zero_blk for c in range(nb)]
                      for r in range(nb)])
    return jnp.stack([prev, diag], axis=1)


def _attention_prompt(q, k, v, rel_bias, lam_vecs, subln_w, batch, seq, heads, lam_init, t):
    dv = q.shape[1] // heads
    da = dv // 2
    nq = seq // t
    hps = 2 if heads % 2 == 0 else 1
    gw = hps * dv
    bias = _prompt_bias_tiles(rel_bias, t)
    vec = lambda a: a.reshape(1, -1)
    const = lambda b, g, i: (0, 0)
    per_head = lambda shape, dtype: [pltpu.VMEM(shape, dtype)] * hps
    return pl.pallas_call(
        functools.partial(_attn_prompt_kernel, t=t, da=da, dv=dv, hps=hps, lam_init=lam_init),
        grid=(batch, heads // hps, nq),
        in_specs=[pl.BlockSpec((t, gw), lambda b, g, i: (b * nq + i, g)),
                  pl.BlockSpec((seq, gw), lambda b, g, i: (b, g)),
                  pl.BlockSpec((seq, gw), lambda b, g, i: (b, g)),
                  pl.BlockSpec((hps, 2, t, t), lambda b, g, i: (g, 0, 0, 0)),
                  pl.BlockSpec((1, da), const), pl.BlockSpec((1, da), const),
                  pl.BlockSpec((1, da), const), pl.BlockSpec((1, da), const),
                  pl.BlockSpec((1, dv), const)],
        out_specs=pl.BlockSpec((t, gw), lambda b, g, i: (b * nq + i, g)),
        out_shape=jax.ShapeDtypeStruct((batch * seq, heads * dv), BF16),
        scratch_shapes=(per_head((dv, 2 * t), BF16)
                        + per_head((nq, dv + 2 * SUBLANES, t), BF16)
                        + per_head((1, 2 * t), F32)
                        + per_head((dv + 2 * SUBLANES, 2 * t), F32)
                        + per_head((t, 2 * t), F32) + per_head((t, 2 * t), F32)),
        compiler_params=_cparams("parallel", "parallel", "arbitrary"),
        name="diff_attention_prompt",
    )(q, k, v, bias, *[vec(a) for a in lam_vecs], vec(subln_w))


def _attn_decode_kernel(pt_ref, q_ref, kn_ref, vn_ref, bfar_ref, blast_ref, bnew_ref,
                        lq1_ref, lk1_ref, lq2_ref, lk2_ref, sw_ref, *rest,
                        pages, heads, da, lam_init):
    del pt_ref
    k_refs = rest[:pages]
    v_refs = rest[pages:2 * pages]
    o_ref, m_ref, l_ref, acc_ref = rest[2 * pages:]
    j = pl.program_id(1)
    last = pl.num_programs(1) - 1

    @pl.when(j == 0)
    def _():
        m_ref[...] = jnp.full(m_ref.shape, NEG_BIG, F32)
        l_ref[...] = jnp.zeros(l_ref.shape, F32)
        acc_ref[...] = jnp.zeros(acc_ref.shape, F32)

    q = q_ref[...]
    row = lax.broadcasted_iota(jnp.int32, q.shape, 0)
    lane = lax.broadcasted_iota(jnp.int32, q.shape, 1)
    qs = jnp.where((row < heads) == (lane < da), q, 0.0)
    qs_bf = qs.astype(BF16)

    s = jnp.concatenate([_mm_nt(qs_bf, k_ref[...]) for k_ref in k_refs], axis=1)
    s = s + jnp.where(j == last, blast_ref[...], bfar_ref[...])
    m_prev = m_ref[...]
    m_new = jnp.maximum(m_prev, jnp.max(s, axis=-1, keepdims=True))
    p = jnp.exp2(s - m_new)
    alpha = jnp.exp2(m_prev - m_new)
    l_ref[...] = alpha * l_ref[...] + jnp.sum(p, axis=-1, keepdims=True)
    rows_per_page = k_refs[0].shape[0]
    pv = acc_ref[...] * alpha
    for idx, v_ref in enumerate(v_refs):
        pv = pv + _mm(p[:, idx * rows_per_page:(idx + 1) * rows_per_page], v_ref[...])
    acc_ref[...] = pv
    m_ref[...] = m_new

    @pl.when(j == last)
    def _():
        s_new = jnp.sum(qs * kn_ref[...], axis=-1, keepdims=True) + bnew_ref[:, 0:1]
        m_prev = m_ref[...]
        m_fin = jnp.maximum(m_prev, s_new)
        p_new = jnp.exp2(s_new - m_fin)
        alpha = jnp.exp2(m_prev - m_fin)
        l_fin = alpha * l_ref[...] + p_new
        acc = alpha * acc_ref[...] + p_new * vn_ref[...]
        o12 = acc / l_fin
        lam = _lambda(lq1_ref, lk1_ref, lq2_ref, lk2_ref, lam_init)
        o = o12[0:heads] - lam * o12[heads:2 * heads]
        o_ref[...] = _rms(o, sw_ref[...], LN_EPS) * (1.0 - lam_init)


def _attention_decode(q, k_new, v_new, k_pool, v_pool, page_table, rel_bias, lam_vecs,
                      subln_w, heads, lam_init, pages):
    db = q.shape[0]
    n_pool, page, _, dv = k_pool.shape
    da = dv // 2
    n_pages = page_table.shape[1]
    past = n_pages * page
    rpp = page * heads
    span = pages * page
    assert n_pages % pages == 0 and span >= MAX_DISTANCE
    kp = k_pool.reshape(n_pool, rpp, dv)
    vp = v_pool.reshape(n_pool, rpp, dv)

    def two_maps(a):
        a = a.reshape(db, 1, heads, dv)
        return jnp.broadcast_to(a, (db, 2, heads, dv)).reshape(db, 2 * heads, dv)

    rb = rel_bias.astype(F32) * LOG2E
    near = MAX_DISTANCE
    row_head = jnp.arange(2 * heads) % heads
    same = (jnp.arange(span * heads) % heads)[None, :] == row_head[:, None]
    b_far = jnp.where(same, rb[NUM_BUCKETS - 1][row_head][:, None], NEG_BIG)
    tab = jnp.transpose(rb[_t5_bucket(near - jnp.arange(near))])[row_head]
    near_part = jnp.where(same[:, :near * heads], jnp.repeat(tab, heads, axis=1), NEG_BIG)
    b_last = jnp.concatenate([b_far[:, :(span - near) * heads], near_part], axis=1)
    b_new = jnp.tile(jnp.broadcast_to(rb[0][:, None], (heads, LANES)), (2, 1))

    vec = lambda a: a.reshape(1, -1)
    const = lambda s, j, pt: (0, 0)

    def page_spec(idx):
        return pl.BlockSpec((None, rpp, dv), lambda s, j, pt: (pt[s, j * pages + idx], 0, 0))

    grid_spec = pltpu.PrefetchScalarGridSpec(
        num_scalar_prefetch=1,
        grid=(db, n_pages // pages),
        in_specs=([pl.BlockSpec((None, 2 * heads, dv), lambda s, j, pt: (s, 0, 0))] * 3
                  + [pl.BlockSpec((2 * heads, span * heads), const),
                     pl.BlockSpec((2 * heads, span * heads), const),
                     pl.BlockSpec((2 * heads, LANES), const),
                     pl.BlockSpec((1, da), const), pl.BlockSpec((1, da), const),
                     pl.BlockSpec((1, da), const), pl.BlockSpec((1, da), const),
                     pl.BlockSpec((1, dv), const)]
                  + [page_spec(idx) for idx in range(pages)] * 2),
        out_specs=pl.BlockSpec((None, heads, dv), lambda s, j, pt: (s, 0, 0)),
        scratch_shapes=[pltpu.VMEM((2 * heads, 1), F32), pltpu.VMEM((2 * heads, 1), F32),
                        pltpu.VMEM((2 * heads, dv), F32)],
    )
    out = pl.pallas_call(
        functools.partial(_attn_decode_kernel, pages=pages, heads=heads, da=da,
                          lam_init=lam_init),
        grid_spec=grid_spec,
        out_shape=jax.ShapeDtypeStruct((db, heads, dv), F32),
        compiler_params=_cparams("parallel", "arbitrary"),
        name="diff_attention_decode",
    )(page_table, two_maps(q), two_maps(k_new), two_maps(v_new), b_far, b_last, b_new,
      *[vec(a) for a in lam_vecs], vec(subln_w), *([kp] * pages), *([vp] * pages))
    return out.reshape(db, heads * dv)


def _unit_lower_inverse(lmats, eye, levels):
    if levels == 0:
        return [eye for _ in lmats]
    xs = [-m for m in lmats]
    tinvs = [eye + x for x in xs]
    if levels == 1:
        return tinvs
    n = eye.shape[0]
    rs = [_mm(x, x) for x in xs]
    for k in range(1, levels):
        if k < levels - 1:
            both = [_mm(jnp.concatenate([r, t], axis=0), r) for r, t in zip(rs, tinvs)]
            rs = [b[:n] for b in both]
            tinvs = [t + b[n:] for t, b in zip(tinvs, both)]
        else:
            tinvs = [t + _mm(t, r) for t, r in zip(tinvs, rs)]
    return tinvs


def _gdn_pre_kernel(x_ref, st_ref, cw_ref, ba_ref, gp_ref,
                    u_ref, w_ref, qd_ref, kd_ref, at_ref, eg_ref,
                    xe_ref, qn_ref, kn_ref, vv_ref, gb_ref, gc_ref, *, tt, heads, dk):
    c = GDN_CHUNK
    i = pl.program_id(1)
    taps = cw_ref.shape[0]
    halo = SUBLANES
    hk = heads * dk

    @pl.when(i == 0)
    def _():
        xe_ref[0:halo, :] = st_ref[...]

    xe_ref[halo:halo + tt, :] = x_ref[...]
    acc = cw_ref[taps - 1:taps, :] * x_ref[...]
    for j in range(taps - 1):
        acc = acc + cw_ref[j:j + 1, :] * xe_ref[pl.ds(halo - (taps - 1) + j, tt), :]
    xe_ref[0:halo, :] = xe_ref[tt:tt + halo, :]
    hcv = _silu(acc)

    for h in range(heads):
        sl = slice(h * dk, (h + 1) * dk)
        qh = hcv[:, h * dk:(h + 1) * dk]
        kh = hcv[:, hk + h * dk:hk + (h + 1) * dk]
        qn_ref[:, sl] = qh * lax.rsqrt(jnp.sum(qh * qh, axis=-1, keepdims=True) + L2_EPS)
        kn_ref[:, sl] = kh * lax.rsqrt(jnp.sum(kh * kh, axis=-1, keepdims=True) + L2_EPS)
    vv_ref[...] = hcv[:, 2 * hk:]

    ba = ba_ref[...]
    lane = lax.broadcasted_iota(jnp.int32, ba.shape, 1)
    xa = ba + gp_ref[1:2, :]
    softplus = jnp.maximum(xa, 0.0) + jnp.log1p(jnp.exp(-jnp.abs(xa)))
    gates = jnp.where(lane < heads, _sigmoid(ba), -jnp.exp(gp_ref[0:1, :]) * softplus)
    gates = jnp.where(lane < 2 * heads, gates, 0.0)
    gb_ref[...] = gates

    tri_l = (lax.broadcasted_iota(jnp.int32, (c, c), 0)
             >= lax.broadcasted_iota(jnp.int32, (c, c), 1)).astype(F32)
    for ch in range(tt // c):
        gc_ref[ch * c:(ch + 1) * c, :] = _select_mm(tri_l, gates[ch * c:(ch + 1) * c, :])

    ri = lax.broadcasted_iota(jnp.int32, (c, c), 0)
    ci = lax.broadcasted_iota(jnp.int32, (c, c), 1)
    incl = ri >= ci
    strict = ri > ci
    eye = (ri == ci).astype(F32)
    pick = (lax.broadcasted_iota(jnp.int32, (SUBLANES, LANES), 0)
            == lax.broadcasted_iota(jnp.int32, (SUBLANES, LANES), 1)).astype(F32)
    lane_c = lax.broadcasted_iota(jnp.int32, (c, LANES), 1)
    scale = dk ** -0.5
    levels = (c - 1).bit_length()

    n_chunks = tt // c
    group = max(g for g in (8, 4, 2, 1) if n_chunks % g == 0)

    def group_body(gi, carry):
        probs = []
        for cc in range(group):
            ch = gi * group + cc
            rows = pl.ds(pl.multiple_of(ch * c, c), c)
            gb = gb_ref[rows, :]
            gcc = gc_ref[rows, :]
            gc_rows = _select_mm(pick, gcc, nt=True)
            for h in range(heads):
                probs.append(dict(
                    ch=ch, rows=rows, h=h, sl=slice(h * dk, (h + 1) * dk),
                    beta=_lane_pick(gb, lane_c, h),
                    gc_col=_lane_pick(gcc, lane_c, heads + h),
                    gc_row=gc_rows[heads + h:heads + h + 1, :]))
        for p in probs:
            p["decay"] = jnp.where(
                incl, jnp.exp(jnp.where(incl, p["gc_col"] - p["gc_row"], 0.0)), 0.0)
            p["k"] = kn_ref[p["rows"], p["sl"]]
            p["k_beta"] = p["k"] * p["beta"]
        kk = [_mm_nt(p["k_beta"], p["k"]) for p in probs]
        lmats = [jnp.where(strict, m * p["decay"], 0.0) for m, p in zip(kk, probs)]
        tinvs = _unit_lower_inverse(lmats, eye, levels)
        for p in probs:
            p["egc"] = jnp.exp(p["gc_col"])
            p["q"] = qn_ref[p["rows"], p["sl"]] * scale
        us = [_mm(t, vv_ref[p["rows"], p["sl"]] * p["beta"]) for t, p in zip(tinvs, probs)]
        ws = [_mm(t, p["k_beta"] * p["egc"]) for t, p in zip(tinvs, probs)]
        ats = [_mm_nt(p["q"], p["k"]) for p in probs]
        for p, u, w, at in zip(probs, us, ws, ats):
            rows, sl, h = p["rows"], p["sl"], p["h"]
            g_last = p["gc_col"][c - 1:c, :]
            u_ref[rows, sl] = u
            w_ref[rows, sl] = w.astype(w_ref.dtype)
            at_ref[rows, h * c:(h + 1) * c] = (at * p["decay"]).astype(at_ref.dtype)
            kd_ref[rows, sl] = p["k"] * jnp.exp(g_last - p["gc_col"])
            qd_ref[rows, sl] = (p["q"] * p["egc"]).astype(qd_ref.dtype)
            eg_ref[p["ch"], :, sl] = jnp.broadcast_to(jnp.exp(g_last), (SUBLANES, dk))
        return carry

    lax.fori_loop(0, n_chunks // group, group_body, 0)


def _gdn_scan_kernel(u_ref, w_ref, qd_ref, kd_ref, at_ref, eg_ref, z_ref, s0_ref, nw_ref,
                     o_ref, sout_ref, s_ref, kt_ref, *, tt, heads, dk):
    c = GDN_CHUNK
    i = pl.program_id(1)
    nb = u_ref.shape[0]

    @pl.when(i == 0)
    def _():
        s_ref[...] = s0_ref[...]

    for b in range(nb):
        for ch in range(tt // c):
            for h in range(heads):
                kt_ref[b, ch * heads + h] = (
                    kd_ref[b, ch * c:(ch + 1) * c, h * dk:(h + 1) * dk].T.astype(BF16))

    chains = [(b, h) for b in range(nb) for h in range(heads)]
    cols = lambda h: slice(h * dk, (h + 1) * dk)

    def chunk_body(ch, carry):
        rows = pl.ds(pl.multiple_of(ch * c, c), c)
        s_bf = [s_ref[b, h].astype(BF16) for b, h in chains]
        w_s = [_mm(w_ref[b, rows, cols(h)], s) for (b, h), s in zip(chains, s_bf)]
        q_s = [_mm(qd_ref[b, rows, cols(h)], s) for (b, h), s in zip(chains, s_bf)]
        v_new = [(u_ref[b, rows, cols(h)] - ws).astype(BF16) for (b, h), ws in zip(chains, w_s)]
        a_v = [_mm(at_ref[b, rows, h * c:(h + 1) * c], v) for (b, h), v in zip(chains, v_new)]
        k_v = [_mm(kt_ref[b, ch * heads + h], v) for (b, h), v in zip(chains, v_new)]
        for (b, h), qs, av, kv in zip(chains, q_s, a_v, k_v):
            s_ref[b, h] = s_ref[b, h] * eg_ref[b, ch, 0:1, cols(h)] + kv
            o_ref[b, rows, cols(h)] = (_rms(qs + av, nw_ref[...], RMS_EPS)
                                       * _silu(z_ref[b, rows, cols(h)])).astype(o_ref.dtype)
        return carry

    lax.fori_loop(0, tt // c, chunk_body, 0)

    @pl.when(i == pl.num_programs(1) - 1)
    def _():
        sout_ref[...] = s_ref[...]


def _gated_deltanet(qkv, ba, z, conv_state, s0, conv_w, a_log, dt_bias, norm_w,
                    batch, seq_pad, heads, tt, scan_batch, scan_tt):
    cq = qkv.shape[1]
    hk = cq // 3
    dk = hk // heads
    c = GDN_CHUNK
    assert c & (c - 1) == 0 and batch % scan_batch == 0
    nt = seq_pad // tt
    nc = tt // c
    taps = conv_w.shape[0]
    st = jnp.pad(conv_state, ((0, 0), (SUBLANES - (taps - 1), 0), (0, 0)))
    gp = jnp.zeros((SUBLANES, LANES), F32)
    gp = gp.at[0, heads:2 * heads].set(a_log.astype(F32))
    gp = gp.at[1, heads:2 * heads].set(dt_bias.astype(F32))
    row_blk = lambda b, i: (b * nt + i, 0)
    const = lambda b, i: (0, 0)
    tok = lambda width: pl.BlockSpec((tt, width), row_blk)
    rows = batch * seq_pad
    u, w, qd, kd, at, eg = pl.pallas_call(
        functools.partial(_gdn_pre_kernel, tt=tt, heads=heads, dk=dk),
        grid=(batch, nt),
        in_specs=[tok(cq),
                  pl.BlockSpec((None, SUBLANES, cq), lambda b, i: (b, 0, 0)),
                  pl.BlockSpec((taps, cq), const),
                  tok(LANES),
                  pl.BlockSpec((SUBLANES, LANES), const)],
        out_specs=[tok(hk), tok(hk), tok(hk), tok(hk), tok(heads * c),
                   pl.BlockSpec((None, nc, SUBLANES, hk), lambda b, i: (b, i, 0, 0))],
        out_shape=[jax.ShapeDtypeStruct((rows, hk), dt) for dt in (F32, BF16, BF16, F32)]
                  + [jax.ShapeDtypeStruct((rows, heads * c), BF16),
                     jax.ShapeDtypeStruct((batch, nt * nc, SUBLANES, hk), F32)],
        scratch_shapes=[pltpu.VMEM((tt + SUBLANES, cq), F32), pltpu.VMEM((tt, hk), F32),
                        pltpu.VMEM((tt, hk), F32), pltpu.VMEM((tt, hk), F32),
                        pltpu.VMEM((tt, LANES), F32), pltpu.VMEM((tt, LANES), F32)],
        compiler_params=_cparams("parallel", "arbitrary"),
        name="gdn_chunk_prepare",
    )(qkv, st, conv_w, ba, gp)

    nb, stt = scan_batch, scan_tt
    snc = stt // c
    seq3 = lambda a: a.reshape(batch, seq_pad, a.shape[-1])
    blk3 = lambda width: pl.BlockSpec((nb, stt, width), lambda g, i: (g, i, 0))
    state_spec = pl.BlockSpec((nb, heads, dk, dk), lambda g, i: (g, 0, 0, 0))
    o, s_new = pl.pallas_call(
        functools.partial(_gdn_scan_kernel, tt=stt, heads=heads, dk=dk),
        grid=(batch // nb, seq_pad // stt),
        in_specs=[blk3(hk), blk3(hk), blk3(hk), blk3(hk), blk3(heads * c),
                  pl.BlockSpec((nb, snc, SUBLANES, hk), lambda g, i: (g, i, 0, 0)),
                  blk3(hk), state_spec,
                  pl.BlockSpec((1, dk), const)],
        out_specs=[blk3(hk), state_spec],
        out_shape=[jax.ShapeDtypeStruct((batch, seq_pad, hk), BF16),
                   jax.ShapeDtypeStruct((batch, heads, dk, dk), F32)],
        scratch_shapes=[pltpu.VMEM((nb, heads, dk, dk), F32),
                        pltpu.VMEM((nb, snc * heads, dk, c), BF16)],
        compiler_params=_cparams("parallel", "arbitrary"),
        name="gdn_chunk_scan",
    )(seq3(u), seq3(w), seq3(qd), seq3(kd), seq3(at), eg, seq3(z), s0, norm_w.reshape(1, dk))
    return o.reshape(rows, hk), s_new


def _gdn_token_kernel(x_ref, st_ref, cw_ref, ba_ref, gp_ref, z_ref, s0_ref, nw_ref,
                      o_ref, sout_ref, *, heads, dk):
    nb = x_ref.shape[0]
    hk = heads * dk
    taps = cw_ref.shape[0]
    acc = cw_ref[taps - 1:taps, :] * x_ref[...]
    for j in range(taps - 1):
        acc = acc + cw_ref[j:j + 1, :] * st_ref[j]
    hcv = _silu(acc)

    ba = ba_ref[...]
    lane = lax.broadcasted_iota(jnp.int32, ba.shape, 1)
    xa = ba + gp_ref[1:2, :]
    softplus = jnp.maximum(xa, 0.0) + jnp.log1p(jnp.exp(-jnp.abs(xa)))
    gates = jnp.where(lane < heads, _sigmoid(ba), -jnp.exp(gp_ref[0:1, :]) * softplus)

    for h in range(heads):
        sl = slice(h * dk, (h + 1) * dk)
        qh = hcv[:, h * dk:(h + 1) * dk]
        kh = hcv[:, hk + h * dk:hk + (h + 1) * dk]
        q = qh * lax.rsqrt(jnp.sum(qh * qh, axis=-1, keepdims=True) + L2_EPS) * (dk ** -0.5)
        k = kh * lax.rsqrt(jnp.sum(kh * kh, axis=-1, keepdims=True) + L2_EPS)
        v = hcv[:, 2 * hk + h * dk:2 * hk + (h + 1) * dk]
        beta = _lane_pick(gates, lane, h)
        decay = jnp.exp(_lane_pick(gates, lane, heads + h))
        k_cols = k.T
        out_rows = []
        for b in range(nb):
            state = s0_ref[b, h]
            row = slice(b, b + 1)
            k_s = _mm(k, state)[row]
            v_new = beta[row] * (v[row] - decay[row] * k_s)
            new_state = state * decay[row] + k_cols[:, b:b + 1] * v_new
            sout_ref[b, h] = new_state
            out_rows.append(_mm(q, new_state)[row])
        o = jnp.concatenate(out_rows, axis=0)
        o_ref[:, sl] = _rms(o, nw_ref[...], RMS_EPS) * _silu(z_ref[:, sl])


def _gated_deltanet_token(qkv, ba, z, conv_state, s0, conv_w, a_log, dt_bias, norm_w, heads, nb):
    batch, cq = qkv.shape
    hk = cq // 3
    dk = hk // heads
    taps = conv_w.shape[0]
    assert batch % nb == 0
    gp = jnp.zeros((SUBLANES, LANES), F32)
    gp = gp.at[0, heads:2 * heads].set(a_log.astype(F32))
    gp = gp.at[1, heads:2 * heads].set(dt_bias.astype(F32))
    const = lambda g: (0, 0)
    rows = lambda width: pl.BlockSpec((nb, width), lambda g: (g, 0))
    state_spec = pl.BlockSpec((nb, heads, dk, dk), lambda g: (g, 0, 0, 0))
    return pl.pallas_call(
        functools.partial(_gdn_token_kernel, heads=heads, dk=dk),
        grid=(batch // nb,),
        in_specs=[rows(cq),
                  pl.BlockSpec((taps - 1, nb, cq), lambda g: (0, g, 0)),
                  pl.BlockSpec((taps, cq), const),
                  rows(LANES),
                  pl.BlockSpec((SUBLANES, LANES), const),
                  rows(hk), state_spec,
                  pl.BlockSpec((1, dk), const)],
        out_specs=[rows(hk), state_spec],
        out_shape=[jax.ShapeDtypeStruct((batch, hk), F32),
                   jax.ShapeDtypeStruct((batch, heads, dk, dk), F32)],
        compiler_params=_cparams("parallel"),
        name="gdn_token",
    )(qkv, jnp.transpose(conv_state, (1, 0, 2)), conv_w, ba, gp, z, s0, norm_w.reshape(1, dk))


def _conv_module_kernel(x_ref, buf_ref, nw_ref, w1_ref, b1_ref, w_ref, b_ref, g_ref, bb_ref,
                        o_ref, tail_ref, xe_ref, xs_ref, *, tt, rb):
    i = pl.program_id(1)
    last_tile = pl.num_programs(1) - 2
    taps = w_ref.shape[0]
    halo = buf_ref.shape[0]
    first = halo - (taps - 1)
    sub = w_ref.shape[1]
    dc = o_ref.shape[1]

    @pl.when(i == 0)
    def _():
        xe_ref[0:tt, :] = jnp.zeros((tt, dc), F32)
        xe_ref[tt:tt + halo, :] = buf_ref[...]

    def glu_stage():
        xn = _rms(x_ref[...], nw_ref[...], RMS_EPS).astype(BF16)
        a = jnp.dot(xn, w1_ref[:, :dc], preferred_element_type=F32) + b1_ref[:, :dc]
        g = jnp.dot(xn, w1_ref[:, dc:], preferred_element_type=F32) + b1_ref[:, dc:]
        xe_ref[halo + tt:halo + 2 * tt, :] = a * _sigmoid(g)

    def conv_stage():
        span = tt + halo - sub
        for ph in range(1, sub):
            xs_ref[ph - 1, 0:span, :] = xe_ref[pl.ds(ph, span), :]

        def window(row, size):
            blk, ph = divmod(row, sub)
            if ph == 0:
                return xe_ref[pl.ds(row, size), :]
            return xs_ref[ph - 1, pl.ds(blk * sub, size), :]

        def weight(j):
            return w_ref[j] if rb == sub else jnp.tile(w_ref[j], (rb // sub, 1))

        for r0 in range(0, tt, rb):
            acc = b_ref[...] + weight(0) * window(first + r0, rb)
            for j in range(1, taps):
                acc = acc + weight(j) * window(first + r0 + j, rb)
            mu = jnp.mean(acc, axis=-1, keepdims=True)
            xc = acc - mu
            var = jnp.mean(xc * xc, axis=-1, keepdims=True)
            y = xc * lax.rsqrt(var + LN_EPS) * g_ref[...] + bb_ref[...]
            o_ref[pl.ds(r0, rb), :] = _silu(y).astype(o_ref.dtype)

    glu_stage()
    conv_stage()

    xe_ref[0:halo, :] = xe_ref[tt:tt + halo, :]
    xe_ref[halo:halo + tt, :] = xe_ref[halo + tt:halo + 2 * tt, :]

    @pl.when(i == last_tile)
    def _():
        rows = tail_ref.shape[0]
        tail_ref[...] = xe_ref[halo + 2 * tt - rows:halo + 2 * tt, :]


def _conv_module(x, buf, norm_w, w_pw1_bf16, b_pw1, w_dw, b_dw, ln_g, ln_b, batch, seq_pad, tt):
    d = x.shape[1]
    ch = w_dw.shape[1]
    taps = w_dw.shape[0]
    halo = -(-(taps - 1) // SUBLANES) * SUBLANES
    bufp = jnp.pad(buf, ((0, 0), (halo - (taps - 1), 0), (0, 0)))
    nt = seq_pad // tt
    rb = min(tt, 2 * SUBLANES)
    tail_rows = min(tt, halo)
    const = lambda b, i: (0, 0)
    vec = lambda a: a.reshape(1, -1)
    w_rep = jnp.broadcast_to(w_dw[:, None, :], (taps, SUBLANES, ch))
    return pl.pallas_call(
        functools.partial(_conv_module_kernel, tt=tt, rb=rb),
        grid=(batch, nt + 1),
        in_specs=[pl.BlockSpec((tt, d), lambda b, i: (b * nt + jnp.minimum(i, nt - 1), 0)),
                  pl.BlockSpec((None, halo, ch), lambda b, i: (b, 0, 0)),
                  pl.BlockSpec((1, d), const),
                  pl.BlockSpec((d, 2 * ch), const),
                  pl.BlockSpec((1, 2 * ch), const),
                  pl.BlockSpec((taps, SUBLANES, ch), lambda b, i: (0, 0, 0)),
                  pl.BlockSpec((1, ch), const), pl.BlockSpec((1, ch), const),
                  pl.BlockSpec((1, ch), const)],
        out_specs=[pl.BlockSpec((tt, ch), lambda b, i: (b * nt + jnp.maximum(i - 1, 0), 0)),
                   pl.BlockSpec((None, tail_rows, ch), lambda b, i: (b, 0, 0))],
        out_shape=[jax.ShapeDtypeStruct((batch * seq_pad, ch), BF16),
                   jax.ShapeDtypeStruct((batch, tail_rows, ch), F32)],
        scratch_shapes=[pltpu.VMEM((2 * tt + halo, ch), F32),
                        pltpu.VMEM((SUBLANES - 1, tt + halo - SUBLANES, ch), F32)],
        compiler_params=_cparams("parallel", "arbitrary"),
        name="conv_module",
    )(x, bufp, vec(norm_w), w_pw1_bf16, vec(b_pw1), w_rep, vec(b_dw), vec(ln_g), vec(ln_b))


def _conv_token_kernel(x_ref, buf_ref, nw_ref, w1_ref, b1_ref, w_ref, b_ref, g_ref, bb_ref,
                       o_ref, h_ref):
    dc = o_ref.shape[1]
    taps = w_ref.shape[0]
    xn = _rms(x_ref[...], nw_ref[...], RMS_EPS).astype(BF16)
    a = jnp.dot(xn, w1_ref[:, :dc], preferred_element_type=F32) + b1_ref[:, :dc]
    g = jnp.dot(xn, w1_ref[:, dc:], preferred_element_type=F32) + b1_ref[:, dc:]
    h = a * _sigmoid(g)
    h_ref[...] = h
    acc = b_ref[...] + w_ref[taps - 1:taps, :] * h
    for j in range(taps - 1):
        acc = acc + w_ref[j:j + 1, :] * buf_ref[j]
    mu = jnp.mean(acc, axis=-1, keepdims=True)
    xc = acc - mu
    var = jnp.mean(xc * xc, axis=-1, keepdims=True)
    o_ref[...] = _silu(xc * lax.rsqrt(var + LN_EPS) * g_ref[...] + bb_ref[...])


def _conv_module_token(x, buf, norm_w, w_pw1_bf16, b_pw1, w_dw, b_dw, ln_g, ln_b):
    batch = x.shape[0]
    ch = w_dw.shape[1]
    vec = lambda a: a.reshape(1, -1)
    operands = (x, jnp.transpose(buf, (1, 0, 2)), vec(norm_w), w_pw1_bf16, vec(b_pw1), w_dw,
                vec(b_dw), vec(ln_g), vec(ln_b))
    whole = lambda a: pl.BlockSpec(a.shape, lambda i, nd=a.ndim: (0,) * nd)
    return pl.pallas_call(
        _conv_token_kernel,
        grid=(1,),
        in_specs=[whole(a) for a in operands],
        out_specs=[pl.BlockSpec((batch, ch), lambda i: (0, 0))] * 2,
        out_shape=[jax.ShapeDtypeStruct((batch, ch), F32)] * 2,
        compiler_params=_cparams("arbitrary"),
        name="conv_module_token",
    )(*operands)


def kernel(x_prompt, x_sample, cache_attn_k, cache_attn_v, page_table, state_gdn_conv, state_gdn_s, state_conv_buf, norm_mix, norm_ffn, norm_final, w_in, w_out, gdn_conv_w, gdn_A_log, gdn_dt_bias, gdn_norm_w, lam_q1, lam_k1, lam_q2, lam_k2, diff_subln_w, rel_bias, conv_w_pw1, conv_b_pw1, conv_w_dw, conv_b_dw, conv_ln_g, conv_ln_b, conv_w_pw2, conv_b_pw2, ffn_w_gate, ffn_w_up, ffn_w_down):
    bp, seq, d = x_prompt.shape
    db, dseq, _ = x_sample.shape
    assert dseq == 1
    depth = norm_mix.shape[0]
    ha, dva = cache_attn_k.shape[3], cache_attn_v.shape[4]
    hb, dkb = state_gdn_s.shape[2], state_gdn_s.shape[3]
    c_qkv = state_gdn_conv.shape[3]
    gdn_taps = gdn_conv_w.shape[1]
    conv_taps = conv_w_dw.shape[1]
    d_ff = ffn_w_gate.shape[2]
    mp = bp * seq
    assert 2 * hb <= LANES

    tm_in, tm_ffn = 512, 512
    t_attn = min(512, seq)
    tt_gdn = min(512, seq)
    tt_scan = min(256, seq)
    tt_conv = min(256, seq)
    assert tt_conv >= conv_taps - 1 and seq >= gdn_taps - 1

    xp = x_prompt.reshape(mp, d)
    xs = x_sample.reshape(db, d)
    row = lambda a: a.reshape(1, -1)

    sizes = (ha * dva, ha * dva, ha * dva, c_qkv, hb * dkb, LANES)
    offs = [0]
    for s_ in sizes:
        offs.append(offs[-1] + s_)
    groups = tuple(zip(offs[:-1], sizes))
    inproj_dtypes = (BF16, BF16, BF16, F32, F32, F32)
    inproj_scales = ((dva // 2) ** -0.5 * LOG2E, 1.0, 1.0, 1.0, 1.0, 1.0)
    inproj_outs = ([(wd, 1, dt) for wd, dt in zip(sizes, inproj_dtypes)]
                   + [(dva, ha, F32), (dva, ha, F32)])

    def last_rows(a, n):
        return jnp.stack([a[(b + 1) * seq - n:(b + 1) * seq] for b in range(bp)])

    w_gate_bf, w_up_bf, w_down_bf = (w.astype(BF16) for w in (ffn_w_gate, ffn_w_up, ffn_w_down))

    k_p, v_p, k_s, v_s, gc_p, gc_s, gs_p, gs_s, cb_p, cb_s = ([] for _ in range(10))
    for layer in range(depth):
        if layer % 2 == 0:
            e = layer // 2
            lam_init = 0.8 - 0.6 * math.exp(-0.3 * layer)
            w_in_e = jnp.pad(w_in[e], ((0, 0), (0, offs[-1] - w_in.shape[2]))).astype(BF16)
            outs_p, outs_s = _token_call(
                functools.partial(_inproj_body, groups=groups, scales=inproj_scales, heads=ha,
                                  kv_groups=(1, 2)),
                [xp], [xs], [_whole(row(norm_mix[layer])), _whole(w_in_e)],
                inproj_outs, tm_in, "in_projection")
            qa_p, ka_p, va_p, qkv_p, z_p, ba_p, k4_p, v4_p = outs_p
            qa_s, ka_s, va_s, qkv_s, z_s, ba_s, k4_s, v4_s = outs_s
            lam_vecs = (lam_q1[e], lam_k1[e], lam_q2[e], lam_k2[e])

            oa_p = _attention_prompt(qa_p, ka_p, va_p, rel_bias, lam_vecs, diff_subln_w[e],
                                     bp, seq, ha, lam_init, t=t_attn)
            oa_s = _attention_decode(qa_s.astype(F32), k4_s.reshape(db, ha * dva),
                                     v4_s.reshape(db, ha * dva), cache_attn_k[e], cache_attn_v[e],
                                     page_table, rel_bias, lam_vecs, diff_subln_w[e],
                                     ha, lam_init, pages=min(32, page_table.shape[1]))

            gdn_w = (gdn_conv_w[e], gdn_A_log[e], gdn_dt_bias[e], gdn_norm_w[e])
            ob_p, s_p = _gated_deltanet(
                qkv_p, ba_p, z_p, jnp.zeros((bp, gdn_taps - 1, c_qkv), F32),
                jnp.zeros((bp, hb, dkb, dkb), F32), *gdn_w, bp, seq, hb,
                tt=tt_gdn, scan_batch=bp, scan_tt=tt_scan)
            ob_s, s_s = _gated_deltanet_token(qkv_s, ba_s, z_s, state_gdn_conv[e],
                                              state_gdn_s[e], *gdn_w, hb, nb=min(db, SUBLANES))

            w_o = w_out[e].astype(BF16)
            mix_p, mix_s = [oa_p, ob_p], [oa_s, ob_s]
            mix_w = [_whole(w_o[:ha * dva]), _whole(w_o[ha * dva:]),
                     _whole(jnp.zeros((1, d), F32))]

            k_p.append(k4_p.reshape(bp, seq, ha, dva))
            v_p.append(v4_p.reshape(bp, seq, ha, dva))
            k_s.append(k4_s.reshape(db, 1, ha, dva))
            v_s.append(v4_s.reshape(db, 1, ha, dva))
            gc_p.append(last_rows(qkv_p, gdn_taps - 1))
            gc_s.append(jnp.concatenate([state_gdn_conv[e], qkv_s.reshape(db, 1, c_qkv)],
                                        axis=1)[:, -(gdn_taps - 1):])
            gs_p.append(s_p)
            gs_s.append(s_s)
        else:
            cidx = layer // 2
            dconv = conv_w_dw.shape[2]
            conv_w = (norm_mix[layer], conv_w_pw1[cidx].astype(BF16), conv_b_pw1[cidx],
                      conv_w_dw[cidx], conv_b_dw[cidx], conv_ln_g[cidx], conv_ln_b[cidx])
            hc_p, tail_p = _conv_module(xp, jnp.zeros((bp, conv_taps - 1, dconv), F32), *conv_w,
                                        bp, seq, tt=tt_conv)
            hc_s, hg_s = _conv_module_token(xs, state_conv_buf[cidx], *conv_w)
            mix_p, mix_s = [hc_p], [hc_s]
            mix_w = [_whole(conv_w_pw2[cidx].astype(BF16)), _whole(row(conv_b_pw2[cidx]))]
            cb_p.append(tail_p[:, tail_p.shape[1] - (conv_taps - 1):])
            cb_s.append(jnp.concatenate([state_conv_buf[cidx], hg_s[:, None, :]],
                                        axis=1)[:, -(conv_taps - 1):])
        (xp,), (xs,) = _token_call(
            functools.partial(_mix_ffn_body, final_norm=(layer == depth - 1)),
            mix_p + [xp], mix_s + [xs],
            [_whole(row(norm_ffn[layer])),
             (w_gate_bf, (None, d, d_ff), (layer, 0, 0)),
             (w_up_bf, (None, d, d_ff), (layer, 0, 0)),
             (w_down_bf, (None, d_ff, d), (layer, 0, 0)),
             _whole(row(norm_final))] + mix_w,
            [(d, 1, F32)], tm_ffn, "mixer_out_swiglu_ffn")

    y_prompt = xp.reshape(bp, seq, d)
    y_sample = xs.reshape(db, 1, d)
    return (y_prompt, y_sample, jnp.stack(k_p), jnp.stack(v_p), jnp.stack(k_s), jnp.stack(v_s),
            jnp.stack(gc_p), jnp.stack(gc_s), jnp.stack(gs_p), jnp.stack(gs_s),
            jnp.stack(cb_p), jnp.stack(cb_s))
```

```python
import functools
import math

import jax
import jax.numpy as jnp
from jax import lax
from jax.experimental import pallas as pl
from jax.experimental.pallas import tpu as pltpu

F32 = jnp.float32
BF16 = jnp.bfloat16

RMS_EPS = 1e-6
LN_EPS = 1e-5
L2_EPS = 1e-6
NUM_BUCKETS = 32
MAX_DISTANCE = 128
GDN_CHUNK = 64
NEG_BIG = -1e30
LOG2E = math.log2(math.e)
LANES = 128
SUBLANES = 8
VMEM_LIMIT = 48 * 1024 * 1024


def _cparams(*sem):
    return pltpu.CompilerParams(dimension_semantics=sem, vmem_limit_bytes=VMEM_LIMIT)


def _mm(a, b):
    return jnp.dot(a.astype(BF16), b.astype(BF16), preferred_element_type=F32)


def _mm_nt(a, b):
    return lax.dot_general(a.astype(BF16), b.astype(BF16), (((1,), (1,)), ((), ())),
                           preferred_element_type=F32)


def _mm_tn(a, b):
    return lax.dot_general(a.astype(BF16), b.astype(BF16), (((0,), (0,)), ((), ())),
                           preferred_element_type=F32)


def _split3(x):
    hi = x.astype(BF16)
    r1 = x - hi.astype(F32)
    mid = r1.astype(BF16)
    lo = (r1 - mid.astype(F32)).astype(BF16)
    return hi, mid, lo


def _select_mm(sel, x, nt=False):
    sel = sel.astype(BF16)
    dims = (((1,), (1,)), ((), ())) if nt else (((1,), (0,)), ((), ()))
    parts = [lax.dot_general(sel, p, dims, preferred_element_type=F32) for p in _split3(x)]
    return parts[0] + (parts[1] + parts[2])


def _rms(x, w, eps):
    return x * lax.rsqrt(jnp.mean(x * x, axis=-1, keepdims=True) + eps) * w


def _sigmoid(x):
    return 1.0 / (1.0 + jnp.exp(-x))


def _silu(x):
    return x * _sigmoid(x)


def _lane_pick(x, lane_idx, k):
    return jnp.sum(jnp.where(lane_idx == k, x, 0.0), axis=-1, keepdims=True)


def _token_call(body, prompt_in, sample_in, shared, outs, tm, name):
    mp, ms = prompt_in[0].shape[0], sample_in[0].shape[0]
    assert mp % tm == 0
    n = mp // tm
    n_in, n_sh, n_out = len(prompt_in), len(shared), len(outs)

    def kern(*refs):
        p_in, s_in = refs[:n_in], refs[n_in:2 * n_in]
        sh = refs[2 * n_in:2 * n_in + n_sh]
        o0 = 2 * n_in + n_sh
        p_out, s_out = refs[o0:o0 + n_out], refs[o0 + n_out:]
        i = pl.program_id(0)

        @pl.when(i < n)
        def _():
            body(p_in, sh, p_out)

        @pl.when(i == n)
        def _():
            body(s_in, sh, s_out)

    prow = lambda i: (jnp.minimum(i, n - 1), 0)
    srow = lambda i: (0, 0)
    in_specs = ([pl.BlockSpec((tm, a.shape[1]), prow) for a in prompt_in]
                + [pl.BlockSpec((ms, a.shape[1]), srow) for a in sample_in]
                + [pl.BlockSpec(bs, (lambda i, idx=idx: idx), pipeline_mode=pl.Buffered(1))
                   for _, bs, idx in shared])
    out_specs = ([pl.BlockSpec((tm * r, w), prow) for w, r, _ in outs]
                 + [pl.BlockSpec((ms * r, w), srow) for w, r, _ in outs])
    out_shape = ([jax.ShapeDtypeStruct((mp * r, w), dt) for w, r, dt in outs]
                 + [jax.ShapeDtypeStruct((ms * r, w), dt) for w, r, dt in outs])
    res = pl.pallas_call(
        kern, grid=(n + 1,), in_specs=in_specs, out_specs=out_specs, out_shape=out_shape,
        compiler_params=_cparams("arbitrary"), name=name,
    )(*prompt_in, *sample_in, *[a for a, _, _ in shared])
    return res[:n_out], res[n_out:]


def _whole(a):
    return (a, a.shape, (0,) * a.ndim)


def _inproj_body(ins, sh, outs, *, groups, scales, heads, kv_groups):
    x_ref, = ins
    nw_ref, w_ref = sh
    xn = _rms(x_ref[...], nw_ref[...], RMS_EPS).astype(BF16)
    rows = x_ref.shape[0]
    n_g = len(groups)
    for g, (o_ref, (off, width)) in enumerate(zip(outs[:n_g], groups)):
        y = jnp.dot(xn, w_ref[:, off:off + width], preferred_element_type=F32)
        o_ref[...] = (y if scales[g] == 1.0 else y * scales[g]).astype(o_ref.dtype)
        if g in kv_groups:
            o4_ref = outs[n_g + kv_groups.index(g)]
            dv = width // heads
            for h in range(heads):
                o4_ref[pl.ds(h, rows, stride=heads), :] = y[:, h * dv:(h + 1) * dv]


def _mix_ffn_body(ins, sh, outs, *, final_norm):
    res_ref = ins[-1]
    nw_ref, wg_ref, wu_ref, wd_ref, fw_ref = sh[:5]
    o_ref, = outs
    x = res_ref[...] + sh[-1][...]
    for a_ref, w_ref in zip(ins[:-1], sh[5:-1]):
        x = x + _mm(a_ref[...], w_ref[...])
    xn = _rms(x, nw_ref[...], RMS_EPS).astype(BF16)
    g = jnp.dot(xn, wg_ref[...], preferred_element_type=F32)
    u = jnp.dot(xn, wu_ref[...], preferred_element_type=F32)
    y = x + _mm(_silu(g) * u, wd_ref[...])
    if final_norm:
        y = _rms(y, fw_ref[...], RMS_EPS)
    o_ref[...] = y


def _t5_bucket(n):
    max_exact = NUM_BUCKETS // 2
    nf = jnp.maximum(n, 1).astype(F32)
    large = max_exact + (jnp.log(nf / max_exact) / math.log(MAX_DISTANCE / max_exact)
                         * (NUM_BUCKETS - max_exact)).astype(jnp.int32)
    large = jnp.minimum(large, NUM_BUCKETS - 1)
    return jnp.where(n < max_exact, n, large)


def _lambda(lq1_ref, lk1_ref, lq2_ref, lk2_ref, lam_init):
    s1 = jnp.sum(lq1_ref[...] * lk1_ref[...], axis=-1, keepdims=True)
    s2 = jnp.sum(lq2_ref[...] * lk2_ref[...], axis=-1, keepdims=True)
    return jnp.exp(s1) - jnp.exp(s2) + lam_init


def _attn_prompt_kernel(q_ref, k_ref, v_ref, bias_ref, lq1_ref, lk1_ref, lq2_ref, lk2_ref,
                        sw_ref, o_ref, *scratch, t, da, dv, hps, lam_init):
    i = pl.program_id(2)
    qt_refs, vt_refs, m_refs, acc_refs, sa_refs, sb_refs = (
        scratch[n * hps:(n + 1) * hps] for n in range(6))
    nblk = vt_refs[0].shape[0]
    heads = range(hps)
    lanes = lambda hh: slice(hh * dv, (hh + 1) * dv)

    @pl.when(i == 0)
    def _():
        for hh in heads:
            for c in range(nblk):
                vt_refs[hh][c, 0:dv, :] = (
                    v_ref[c * t:(c + 1) * t, lanes(hh)].astype(F32).T.astype(BF16))
                vt_refs[hh][c, dv:, :] = jnp.ones((vt_refs[hh].shape[1] - dv, t), BF16)

    for hh in heads:
        qt = q_ref[:, lanes(hh)].astype(F32).T
        first_map = lax.broadcasted_iota(jnp.int32, qt.shape, 0) < da
        qt_refs[hh][:, 0:t] = jnp.where(first_map, qt, 0.0).astype(BF16)
        qt_refs[hh][:, t:2 * t] = jnp.where(first_map, 0.0, qt).astype(BF16)
        m_refs[hh][...] = jnp.full(m_refs[hh].shape, NEG_BIG, F32)
        acc_refs[hh][...] = jnp.zeros(acc_refs[hh].shape, F32)

    def scores(j, buf_refs):
        rows = pl.ds(pl.multiple_of(j * t, t), t)
        for hh in heads:
            buf_refs[hh][...] = jnp.dot(k_ref[rows, lanes(hh)], qt_refs[hh][...],
                                        preferred_element_type=F32)

    def update(j, buf_refs, bias_idx):
        for hh in heads:
            s = buf_refs[hh][...]
            if bias_idx is not None:
                bias = bias_ref[hh, bias_idx]
                s = s + jnp.concatenate([bias, bias], axis=1)
            m_prev = m_refs[hh][...]
            m_new = jnp.maximum(m_prev, jnp.max(s, axis=0, keepdims=True))
            p = jnp.exp2(s - m_new).astype(BF16)
            acc_refs[hh][...] = (acc_refs[hh][...] * jnp.exp2(m_prev - m_new)
                                 + jnp.dot(vt_refs[hh][j], p, preferred_element_type=F32))
            m_refs[hh][...] = m_new

    prev_bias, diag_bias = 0, 1
    scores(0, sa_refs)

    def far_pair(jj, carry):
        j = 2 * jj
        scores(j + 1, sb_refs)
        update(j, sa_refs, None)
        scores(j + 2, sa_refs)
        update(j + 1, sb_refs, None)
        return carry

    n_far = i - 1
    lax.fori_loop(0, n_far // 2, far_pair, 0)

    @pl.when(i % 2 == 1)
    def _():
        scores(i, sb_refs)
        update(i - 1, sa_refs, prev_bias)
        update(i, sb_refs, diag_bias)

    @pl.when((i % 2 == 0) & (i >= 2))
    def _():
        scores(i - 1, sb_refs)
        update(i - 2, sa_refs, None)
        scores(i, sa_refs)
        update(i - 1, sb_refs, prev_bias)
        update(i, sa_refs, diag_bias)

    @pl.when(i == 0)
    def _():
        update(0, sa_refs, diag_bias)

    lam = _lambda(lq1_ref, lk1_ref, lq2_ref, lk2_ref, lam_init)
    for hh in heads:
        acc = acc_refs[hh][...]
        o12 = acc[0:dv] / acc[dv:dv + 1]
        o = (o12[:, 0:t] - lam * o12[:, t:2 * t]).T
        o_ref[:, lanes(hh)] = (_rms(o, sw_ref[...], LN_EPS) * (1.0 - lam_init)).astype(o_ref.dtype)


def _toeplitz(r, t):
    h, period = r.shape
    flat = jnp.tile(r, (1, t))[:, :t * (period - 1)]
    return flat.reshape(h, t, period - 1)[:, :, :t]


def _prompt_bias_tiles(rel_bias, t):
    d = MAX_DISTANCE
    assert t % d == 0
    nb = t // d
    far = rel_bias[NUM_BUCKETS - 1]
    b1 = jnp.transpose(rel_bias[_t5_bucket(jnp.arange(d))] - far).astype(F32) * LOG2E
    zero = jnp.zeros_like(b1)
    neg = jnp.full_like(b1, NEG_BIG)
    tz = _toeplitz(jnp.concatenate([b1, zero, neg, neg], axis=1), 2 * d)
    g0, g1 = tz[:, :d, :d], tz[:, :d, d:]
    zero_blk, neg_blk = jnp.zeros_like(g0), jnp.full_like(g0, NEG_BIG)

    def diag_block(r, c):
        return neg_blk if c < r else g0 if c == r else g1 if c == r + 1 else zero_blk

    diag = jnp.block([[diag_block(r, c) for c in range(nb)] for r in range(nb)])
    prev = jnp.block([[g1 if (r, c) == (nb - 1, 0) else zero_blk for c in range(nb)]
                      for r in range(nb)])
    return jnp.stack([prev, diag], axis=1)


def _attention_prompt(q, k, v, rel_bias, lam_vecs, subln_w, batch, seq, heads, lam_init, t):
    dv = q.shape[1] // heads
    da = dv // 2
    nq = seq // t
    hps = 2 if heads % 2 == 0 else 1
    gw = hps * dv
    bias = _prompt_bias_tiles(rel_bias, t)
    vec = lambda a: a.reshape(1, -1)
    const = lambda b, g, i: (0, 0)
    per_head = lambda shape, dtype: [pltpu.VMEM(shape, dtype)] * hps
    return pl.pallas_call(
        functools.partial(_attn_prompt_kernel, t=t, da=da, dv=dv, hps=hps, lam_init=lam_init),
        grid=(batch, heads // hps, nq),
        in_specs=[pl.BlockSpec((t, gw), lambda b, g, i: (b * nq + i, g)),
                  pl.BlockSpec((seq, gw), lambda b, g, i: (b, g)),
                  pl.BlockSpec((seq, gw), lambda b, g, i: (b, g)),
                  pl.BlockSpec((hps, 2, t, t), lambda b, g, i: (g, 0, 0, 0)),
                  pl.BlockSpec((1, da), const), pl.BlockSpec((1, da), const),
                  pl.BlockSpec((1, da), const), pl.BlockSpec((1, da), const),
                  pl.BlockSpec((1, dv), const)],
        out_specs=pl.BlockSpec((t, gw), lambda b, g, i: (b * nq + i, g)),
        out_shape=jax.ShapeDtypeStruct((batch * seq, heads * dv), BF16),
        scratch_shapes=(per_head((dv, 2 * t), BF16)
                        + per_head((nq, dv + 2 * SUBLANES, t), BF16)
                        + per_head((1, 2 * t), F32)
                        + per_head((dv + 2 * SUBLANES, 2 * t), F32)
                        + per_head((t, 2 * t), F32) + per_head((t, 2 * t), F32)),
        compiler_params=_cparams("parallel", "parallel", "arbitrary"),
        name="diff_attention_prompt",
    )(q, k, v, bias, *[vec(a) for a in lam_vecs], vec(subln_w))


def _attn_decode_kernel(pt_ref, q_ref, kn_ref, vn_ref, bfar_ref, blast_ref, bnew_ref,
                        lq1_ref, lk1_ref, lq2_ref, lk2_ref, sw_ref, *rest,
                        pages, heads, da, lam_init):
    del pt_ref
    k_refs = rest[:pages]
    v_refs = rest[pages:2 * pages]
    o_ref, m_ref, l_ref, acc_ref = rest[2 * pages:]
    j = pl.program_id(1)
    last = pl.num_programs(1) - 1

    @pl.when(j == 0)
    def _():
        m_ref[...] = jnp.full(m_ref.shape, NEG_BIG, F32)
        l_ref[...] = jnp.zeros(l_ref.shape, F32)
        acc_ref[...] = jnp.zeros(acc_ref.shape, F32)

    q = q_ref[...]
    row = lax.broadcasted_iota(jnp.int32, q.shape, 0)
    lane = lax.broadcasted_iota(jnp.int32, q.shape, 1)
    qs = jnp.where((row < heads) == (lane < da), q, 0.0)
    qs_bf = qs.astype(BF16)

    s = jnp.concatenate([_mm_nt(qs_bf, k_ref[...]) for k_ref in k_refs], axis=1)
    s = s + jnp.where(j == last, blast_ref[...], bfar_ref[...])
    m_prev = m_ref[...]
    m_new = jnp.maximum(m_prev, jnp.max(s, axis=-1, keepdims=True))
    p = jnp.exp2(s - m_new)
    alpha = jnp.exp2(m_prev - m_new)
    l_ref[...] = alpha * l_ref[...] + jnp.sum(p, axis=-1, keepdims=True)
    rows_per_page = k_refs[0].shape[0]
    pv = acc_ref[...] * alpha
    for idx, v_ref in enumerate(v_refs):
        pv = pv + _mm(p[:, idx * rows_per_page:(idx + 1) * rows_per_page], v_ref[...])
    acc_ref[...] = pv
    m_ref[...] = m_new

    @pl.when(j == last)
    def _():
        s_new = jnp.sum(qs * kn_ref[...], axis=-1, keepdims=True) + bnew_ref[:, 0:1]
        m_prev = m_ref[...]
        m_fin = jnp.maximum(m_prev, s_new)
        p_new = jnp.exp2(s_new - m_fin)
        alpha = jnp.exp2(m_prev - m_fin)
        l_fin = alpha * l_ref[...] + p_new
        acc = alpha * acc_ref[...] + p_new * vn_ref[...]
        o12 = acc / l_fin
        lam = _lambda(lq1_ref, lk1_ref, lq2_ref, lk2_ref, lam_init)
        o = o12[0:heads] - lam * o12[heads:2 * heads]
        o_ref[...] = _rms(o, sw_ref[...], LN_EPS) * (1.0 - lam_init)


def _attention_decode(q, k_new, v_new, k_pool, v_pool, page_table, rel_bias, lam_vecs,
                      subln_w, heads, lam_init, pages):
    db = q.shape[0]
    n_pool, page, _, dv = k_pool.shape
    da = dv // 2
    n_pages = page_table.shape[1]
    past = n_pages * page
    rpp = page * heads
    span = pages * page
    assert n_pages % pages == 0 and span >= MAX_DISTANCE
    kp = k_pool.reshape(n_pool, rpp, dv)
    vp = v_pool.reshape(n_pool, rpp, dv)

    def two_maps(a):
        a = a.reshape(db, 1, heads, dv)
        return jnp.broadcast_to(a, (db, 2, heads, dv)).reshape(db, 2 * heads, dv)

    rb = rel_bias.astype(F32) * LOG2E
    near = MAX_DISTANCE
    row_head = jnp.arange(2 * heads) % heads
    same = (jnp.arange(span * heads) % heads)[None, :] == row_head[:, None]
    b_far = jnp.where(same, rb[NUM_BUCKETS - 1][row_head][:, None], NEG_BIG)
    tab = jnp.transpose(rb[_t5_bucket(near - jnp.arange(near))])[row_head]
    near_part = jnp.where(same[:, :near * heads], jnp.repeat(tab, heads, axis=1), NEG_BIG)
    b_last = jnp.concatenate([b_far[:, :(span - near) * heads], near_part], axis=1)
    b_new = jnp.tile(jnp.broadcast_to(rb[0][:, None], (heads, LANES)), (2, 1))

    vec = lambda a: a.reshape(1, -1)
    const = lambda s, j, pt: (0, 0)

    def page_spec(idx):
        return pl.BlockSpec((None, rpp, dv), lambda s, j, pt: (pt[s, j * pages + idx], 0, 0))

    grid_spec = pltpu.PrefetchScalarGridSpec(
        num_scalar_prefetch=1,
        grid=(db, n_pages // pages),
        in_specs=([pl.BlockSpec((None, 2 * heads, dv), lambda s, j, pt: (s, 0, 0))] * 3
                  + [pl.BlockSpec((2 * heads, span * heads), const),
                     pl.BlockSpec((2 * heads, span * heads), const),
                     pl.BlockSpec((2 * heads, LANES), const),
                     pl.BlockSpec((1, da), const), pl.BlockSpec((1, da), const),
                     pl.BlockSpec((1, da), const), pl.BlockSpec((1, da), const),
                     pl.BlockSpec((1, dv), const)]
                  + [page_spec(idx) for idx in range(pages)] * 2),
        out_specs=pl.BlockSpec((None, heads, dv), lambda s, j, pt: (s, 0, 0)),
        scratch_shapes=[pltpu.VMEM((2 * heads, 1), F32), pltpu.VMEM((2 * heads, 1), F32),
                        pltpu.VMEM((2 * heads, dv), F32)],
    )
    out = pl.pallas_call(
        functools.partial(_attn_decode_kernel, pages=pages, heads=heads, da=da,
                          lam_init=lam_init),
        grid_spec=grid_spec,
        out_shape=jax.ShapeDtypeStruct((db, heads, dv), F32),
        compiler_params=_cparams("parallel", "arbitrary"),
        name="diff_attention_decode",
    )(page_table, two_maps(q), two_maps(k_new), two_maps(v_new), b_far, b_last, b_new,
      *[vec(a) for a in lam_vecs], vec(subln_w), *([kp] * pages), *([vp] * pages))
    return out.reshape(db, heads * dv)


def _unit_lower_inverse(lmats, eye, levels):
    if levels == 0:
        return [eye for _ in lmats]
    xs = [-m for m in lmats]
    tinvs = [eye + x for x in xs]
    if levels == 1:
        return tinvs
    n = eye.shape[0]
    rs = [_mm(x, x) for x in xs]
    for k in range(1, levels):
        if k < levels - 1:
            both = [_mm(jnp.concatenate([r, t], axis=0), r) for r, t in zip(rs, tinvs)]
            rs = [b[:n] for b in both]
            tinvs = [t + b[n:] for t, b in zip(tinvs, both)]
        else:
            tinvs = [t + _mm(t, r) for t, r in zip(tinvs, rs)]
    return tinvs


def _gdn_pre_kernel(x_ref, st_ref, cw_ref, ba_ref, gp_ref,
                    u_ref, w_ref, qd_ref, kd_ref, at_ref, eg_ref,
                    xe_ref, qn_ref, kn_ref, vv_ref, gb_ref, gc_ref, *, tt, heads, dk):
    c = GDN_CHUNK
    i = pl.program_id(1)
    taps = cw_ref.shape[0]
    halo = SUBLANES
    hk = heads * dk

    @pl.when(i == 0)
    def _():
        xe_ref[0:halo, :] = st_ref[...]

    xe_ref[halo:halo + tt, :] = x_ref[...]
    acc = cw_ref[taps - 1:taps, :] * x_ref[...]
    for j in range(taps - 1):
        acc = acc + cw_ref[j:j + 1, :] * xe_ref[pl.ds(halo - (taps - 1) + j, tt), :]
    xe_ref[0:halo, :] = xe_ref[tt:tt + halo, :]
    hcv = _silu(acc)

    for h in range(heads):
        sl = slice(h * dk, (h + 1) * dk)
        qh = hcv[:, h * dk:(h + 1) * dk]
        kh = hcv[:, hk + h * dk:hk + (h + 1) * dk]
        qn_ref[:, sl] = qh * lax.rsqrt(jnp.sum(qh * qh, axis=-1, keepdims=True) + L2_EPS)
        kn_ref[:, sl] = kh * lax.rsqrt(jnp.sum(kh * kh, axis=-1, keepdims=True) + L2_EPS)
    vv_ref[...] = hcv[:, 2 * hk:]

    ba = ba_ref[...]
    lane = lax.broadcasted_iota(jnp.int32, ba.shape, 1)
    xa = ba + gp_ref[1:2, :]
    softplus = jnp.maximum(xa, 0.0) + jnp.log1p(jnp.exp(-jnp.abs(xa)))
    gates = jnp.where(lane < heads, _sigmoid(ba), -jnp.exp(gp_ref[0:1, :]) * softplus)
    gates = jnp.where(lane < 2 * heads, gates, 0.0)
    gb_ref[...] = gates

    tri_l = (lax.broadcasted_iota(jnp.int32, (c, c), 0)
             >= lax.broadcasted_iota(jnp.int32, (c, c), 1)).astype(F32)
    for ch in range(tt // c):
        gc_ref[ch * c:(ch + 1) * c, :] = _select_mm(tri_l, gates[ch * c:(ch + 1) * c, :])

    ri = lax.broadcasted_iota(jnp.int32, (c, c), 0)
    ci = lax.broadcasted_iota(jnp.int32, (c, c), 1)
    incl = ri >= ci
    strict = ri > ci
    eye = (ri == ci).astype(F32)
    pick = (lax.broadcasted_iota(jnp.int32, (SUBLANES, LANES), 0)
            == lax.broadcasted_iota(jnp.int32, (SUBLANES, LANES), 1)).astype(F32)
    lane_c = lax.broadcasted_iota(jnp.int32, (c, LANES), 1)
    scale = dk ** -0.5
    levels = (c - 1).bit_length()

    n_chunks = tt // c
    group = max(g for g in (8, 4, 2, 1) if n_chunks % g == 0)

    def group_body(gi, carry):
        probs = []
        for cc in range(group):
            ch = gi * group + cc
            rows = pl.ds(pl.multiple_of(ch * c, c), c)
            gb = gb_ref[rows, :]
            gcc = gc_ref[rows, :]
            gc_rows = _select_mm(pick, gcc, nt=True)
            for h in range(heads):
                probs.append(dict(
                    ch=ch, rows=rows, h=h, sl=slice(h * dk, (h + 1) * dk),
                    beta=_lane_pick(gb, lane_c, h),
                    gc_col=_lane_pick(gcc, lane_c, heads + h),
                    gc_row=gc_rows[heads + h:heads + h + 1, :]))
        for p in probs:
            p["decay"] = jnp.where(
                incl, jnp.exp(jnp.where(incl, p["gc_col"] - p["gc_row"], 0.0)), 0.0)
            p["k"] = kn_ref[p["rows"], p["sl"]]
            p["k_beta"] = p["k"] * p["beta"]
        kk = [_mm_nt(p["k_beta"], p["k"]) for p in probs]
        lmats = [jnp.where(strict, m * p["decay"], 0.0) for m, p in zip(kk, probs)]
        tinvs = _unit_lower_inverse(lmats, eye, levels)
        for p in probs:
            p["egc"] = jnp.exp(p["gc_col"])
            p["q"] = qn_ref[p["rows"], p["sl"]] * scale
        us = [_mm(t, vv_ref[p["rows"], p["sl"]] * p["beta"]) for t, p in zip(tinvs, probs)]
        ws = [_mm(t, p["k_beta"] * p["egc"]) for t, p in zip(tinvs, probs)]
        ats = [_mm_nt(p["q"], p["k"]) for p in probs]
        for p, u, w, at in zip(probs, us, ws, ats):
            rows, sl, h = p["rows"], p["sl"], p["h"]
            g_last = p["gc_col"][c - 1:c, :]
            u_ref[rows, sl] = u
            w_ref[rows, sl] = w.astype(w_ref.dtype)
            at_ref[rows, h * c:(h + 1) * c] = (at * p["decay"]).astype(at_ref.dtype)
            kd_ref[rows, sl] = p["k"] * jnp.exp(g_last - p["gc_col"])
            qd_ref[rows, sl] = (p["q"] * p["egc"]).astype(qd_ref.dtype)
            eg_ref[p["ch"], :, sl] = jnp.broadcast_to(jnp.exp(g_last), (SUBLANES, dk))
        return carry

    lax.fori_loop(0, n_chunks // group, group_body, 0)


def _gdn_scan_kernel(u_ref, w_ref, qd_ref, kd_ref, at_ref, eg_ref, z_ref, s0_ref, nw_ref,
                     o_ref, sout_ref, s_ref, kt_ref, *, tt, heads, dk):
    c = GDN_CHUNK
    i = pl.program_id(1)
    nb = u_ref.shape[0]

    @pl.when(i == 0)
    def _():
        s_ref[...] = s0_ref[...]

    for b in range(nb):
        for ch in range(tt // c):
            for h in range(heads):
                kt_ref[b, ch * heads + h] = (
                    kd_ref[b, ch * c:(ch + 1) * c, h * dk:(h + 1) * dk].T.astype(BF16))

    chains = [(b, h) for b in range(nb) for h in range(heads)]
    cols = lambda h: slice(h * dk, (h + 1) * dk)

    def chunk_body(ch, carry):
        rows = pl.ds(pl.multiple_of(ch * c, c), c)
        s_bf = [s_ref[b, h].astype(BF16) for b, h in chains]
        w_s = [_mm(w_ref[b, rows, cols(h)], s) for (b, h), s in zip(chains, s_bf)]
        q_s = [_mm(qd_ref[b, rows, cols(h)], s) for (b, h), s in zip(chains, s_bf)]
        v_new = [(u_ref[b, rows, cols(h)] - ws).astype(BF16) for (b, h), ws in zip(chains, w_s)]
        a_v = [_mm(at_ref[b, rows, h * c:(h + 1) * c], v) for (b, h), v in zip(chains, v_new)]
        k_v = [_mm(kt_ref[b, ch * heads + h], v) for (b, h), v in zip(chains, v_new)]
        for (b, h), qs, av, kv in zip(chains, q_s, a_v, k_v):
            s_ref[b, h] = s_ref[b, h] * eg_ref[b, ch, 0:1, cols(h)] + kv
            o_ref[b, rows, cols(h)] = (_rms(qs + av, nw_ref[...], RMS_EPS)
                                       * _silu(z_ref[b, rows, cols(h)])).astype(o_ref.dtype)
        return carry

    for ch in range(tt // c):
        chunk_body(ch, 0)

    @pl.when(i == pl.num_programs(1) - 1)
    def _():
        sout_ref[...] = s_ref[...]


def _gated_deltanet(qkv, ba, z, conv_state, s0, conv_w, a_log, dt_bias, norm_w,
                    batch, seq_pad, heads, tt, scan_batch, scan_tt):
    cq = qkv.shape[1]
    hk = cq // 3
    dk = hk // heads
    c = GDN_CHUNK
    assert c & (c - 1) == 0 and batch % scan_batch == 0
    nt = seq_pad // tt
    nc = tt // c
    taps = conv_w.shape[0]
    st = jnp.pad(conv_state, ((0, 0), (SUBLANES - (taps - 1), 0), (0, 0)))
    gp = jnp.zeros((SUBLANES, LANES), F32)
    gp = gp.at[0, heads:2 * heads].set(a_log.astype(F32))
    gp = gp.at[1, heads:2 * heads].set(dt_bias.astype(F32))
    row_blk = lambda b, i: (b * nt + i, 0)
    const = lambda b, i: (0, 0)
    tok = lambda width: pl.BlockSpec((tt, width), row_blk)
    rows = batch * seq_pad
    u, w, qd, kd, at, eg = pl.pallas_call(
        functools.partial(_gdn_pre_kernel, tt=tt, heads=heads, dk=dk),
        grid=(batch, nt),
        in_specs=[tok(cq),
                  pl.BlockSpec((None, SUBLANES, cq), lambda b, i: (b, 0, 0)),
                  pl.BlockSpec((taps, cq), const),
                  tok(LANES),
                  pl.BlockSpec((SUBLANES, LANES), const)],
        out_specs=[tok(hk), tok(hk), tok(hk), tok(hk), tok(heads * c),
                   pl.BlockSpec((None, nc, SUBLANES, hk), lambda b, i: (b, i, 0, 0))],
        out_shape=[jax.ShapeDtypeStruct((rows, hk), dt) for dt in (F32, BF16, BF16, F32)]
                  + [jax.ShapeDtypeStruct((rows, heads * c), BF16),
                     jax.ShapeDtypeStruct((batch, nt * nc, SUBLANES, hk), F32)],
        scratch_shapes=[pltpu.VMEM((tt + SUBLANES, cq), F32), pltpu.VMEM((tt, hk), F32),
                        pltpu.VMEM((tt, hk), F32), pltpu.VMEM((tt, hk), F32),
                        pltpu.VMEM((tt, LANES), F32), pltpu.VMEM((tt, LANES), F32)],
        compiler_params=_cparams("parallel", "arbitrary"),
        name="gdn_chunk_prepare",
    )(qkv, st, conv_w, ba, gp)

    nb, stt = scan_batch, scan_tt
    snc = stt // c
    seq3 = lambda a: a.reshape(batch, seq_pad, a.shape[-1])
    blk3 = lambda width: pl.BlockSpec((nb, stt, width), lambda g, i: (g, i, 0))
    state_spec = pl.BlockSpec((nb, heads, dk, dk), lambda g, i: (g, 0, 0, 0))
    o, s_new = pl.pallas_call(
        functools.partial(_gdn_scan_kernel, tt=stt, heads=heads, dk=dk),
        grid=(batch // nb, seq_pad // stt),
        in_specs=[blk3(hk), blk3(hk), blk3(hk), blk3(hk), blk3(heads * c),
                  pl.BlockSpec((nb, snc, SUBLANES, hk), lambda g, i: (g, i, 0, 0)),
                  blk3(hk), state_spec,
                  pl.BlockSpec((1, dk), const)],
        out_specs=[blk3(hk), state_spec],
        out_shape=[jax.ShapeDtypeStruct((batch, seq_pad, hk), BF16),
                   jax.ShapeDtypeStruct((batch, heads, dk, dk), F32)],
        scratch_shapes=[pltpu.VMEM((nb, heads, dk, dk), F32),
                        pltpu.VMEM((nb, snc * heads, dk, c), BF16)],
        compiler_params=_cparams("parallel", "arbitrary"),
        name="gdn_chunk_scan",
    )(seq3(u), seq3(w), seq3(qd), seq3(kd), seq3(at), eg, seq3(z), s0, norm_w.reshape(1, dk))
    return o.reshape(rows, hk), s_new


def _gdn_token_kernel(x_ref, st_ref, cw_ref, ba_ref, gp_ref, z_ref, s0_ref, nw_ref,
                      o_ref, sout_ref, *, heads, dk):
    nb = x_ref.shape[0]
    hk = heads * dk
    taps = cw_ref.shape[0]
    acc = cw_ref[taps - 1:taps, :] * x_ref[...]
    for j in range(taps - 1):
        acc = acc + cw_ref[j:j + 1, :] * st_ref[j]
    hcv = _silu(acc)

    ba = ba_ref[...]
    lane = lax.broadcasted_iota(jnp.int32, ba.shape, 1)
    xa = ba + gp_ref[1:2, :]
    softplus = jnp.maximum(xa, 0.0) + jnp.log1p(jnp.exp(-jnp.abs(xa)))
    gates = jnp.where(lane < heads, _sigmoid(ba), -jnp.exp(gp_ref[0:1, :]) * softplus)

    for h in range(heads):
        sl = slice(h * dk, (h + 1) * dk)
        qh = hcv[:, h * dk:(h + 1) * dk]
        kh = hcv[:, hk + h * dk:hk + (h + 1) * dk]
        q = qh * lax.rsqrt(jnp.sum(qh * qh, axis=-1, keepdims=True) + L2_EPS) * (dk ** -0.5)
        k = kh * lax.rsqrt(jnp.sum(kh * kh, axis=-1, keepdims=True) + L2_EPS)
        v = hcv[:, 2 * hk + h * dk:2 * hk + (h + 1) * dk]
        beta = _lane_pick(gates, lane, h)
        decay = jnp.exp(_lane_pick(gates, lane, heads + h))
        k_cols = k.T
        out_rows = []
        for b in range(nb):
            state = s0_ref[b, h]
            row = slice(b, b + 1)
            k_s = _mm(k, state)[row]
            v_new = beta[row] * (v[row] - decay[row] * k_s)
            new_state = state * decay[row] + k_cols[:, b:b + 1] * v_new
            sout_ref[b, h] = new_state
            out_rows.append(_mm(q, new_state)[row])
        o = jnp.concatenate(out_rows, axis=0)
        o_ref[:, sl] = _rms(o, nw_ref[...], RMS_EPS) * _silu(z_ref[:, sl])


def _gated_deltanet_token(qkv, ba, z, conv_state, s0, conv_w, a_log, dt_bias, norm_w, heads, nb):
    batch, cq = qkv.shape
    hk = cq // 3
    dk = hk // heads
    taps = conv_w.shape[0]
    assert batch % nb == 0
    gp = jnp.zeros((SUBLANES, LANES), F32)
    gp = gp.at[0, heads:2 * heads].set(a_log.astype(F32))
    gp = gp.at[1, heads:2 * heads].set(dt_bias.astype(F32))
    const = lambda g: (0, 0)
    rows = lambda width: pl.BlockSpec((nb, width), lambda g: (g, 0))
    state_spec = pl.BlockSpec((nb, heads, dk, dk), lambda g: (g, 0, 0, 0))
    return pl.pallas_call(
        functools.partial(_gdn_token_kernel, heads=heads, dk=dk),
        grid=(batch // nb,),
        in_specs=[rows(cq),
                  pl.BlockSpec((taps - 1, nb, cq), lambda g: (0, g, 0)),
                  pl.BlockSpec((taps, cq), const),
                  rows(LANES),
                  pl.BlockSpec((SUBLANES, LANES), const),
                  rows(hk), state_spec,
                  pl.BlockSpec((1, dk), const)],
        out_specs=[rows(hk), state_spec],
        out_shape=[jax.ShapeDtypeStruct((batch, hk), F32),
                   jax.ShapeDtypeStruct((batch, heads, dk, dk), F32)],
        compiler_params=_cparams("parallel"),
        name="gdn_token",
    )(qkv, jnp.transpose(conv_state, (1, 0, 2)), conv_w, ba, gp, z, s0, norm_w.reshape(1, dk))


def _conv_module_kernel(x_ref, buf_ref, nw_ref, w1_ref, b1_ref, w_ref, b_ref, g_ref, bb_ref,
                        o_ref, tail_ref, xe_ref, xs_ref, *, tt, rb):
    i = pl.program_id(1)
    last_tile = pl.num_programs(1) - 2
    taps = w_ref.shape[0]
    halo = buf_ref.shape[0]
    first = halo - (taps - 1)
    sub = w_ref.shape[1]
    dc = o_ref.shape[1]

    @pl.when(i == 0)
    def _():
        xe_ref[0:tt, :] = jnp.zeros((tt, dc), F32)
        xe_ref[tt:tt + halo, :] = buf_ref[...]

    def glu_stage():
        xn = _rms(x_ref[...], nw_ref[...], RMS_EPS).astype(BF16)
        a = jnp.dot(xn, w1_ref[:, :dc], preferred_element_type=F32) + b1_ref[:, :dc]
        g = jnp.dot(xn, w1_ref[:, dc:], preferred_element_type=F32) + b1_ref[:, dc:]
        xe_ref[halo + tt:halo + 2 * tt, :] = a * _sigmoid(g)

    def conv_stage():
        span = tt + halo - sub
        for ph in range(1, sub):
            xs_ref[ph - 1, 0:span, :] = xe_ref[pl.ds(ph, span), :]

        def window(row, size):
            blk, ph = divmod(row, sub)
            if ph == 0:
                return xe_ref[pl.ds(row, size), :]
            return xs_ref[ph - 1, pl.ds(blk * sub, size), :]

        def weight(j):
            return w_ref[j] if rb == sub else jnp.tile(w_ref[j], (rb // sub, 1))

        for r0 in range(0, tt, rb):
            acc = b_ref[...] + weight(0) * window(first + r0, rb)
            for j in range(1, taps):
                acc = acc + weight(j) * window(first + r0 + j, rb)
            mu = jnp.mean(acc, axis=-1, keepdims=True)
            xc = acc - mu
            var = jnp.mean(xc * xc, axis=-1, keepdims=True)
            y = xc * lax.rsqrt(var + LN_EPS) * g_ref[...] + bb_ref[...]
            o_ref[pl.ds(r0, rb), :] = _silu(y).astype(o_ref.dtype)

    glu_stage()
    conv_stage()

    xe_ref[0:halo, :] = xe_ref[tt:tt + halo, :]
    xe_ref[halo:halo + tt, :] = xe_ref[halo + tt:halo + 2 * tt, :]

    @pl.when(i == last_tile)
    def _():
        rows = tail_ref.shape[0]
        tail_ref[...] = xe_ref[halo + 2 * tt - rows:halo + 2 * tt, :]


def _conv_module(x, buf, norm_w, w_pw1_bf16, b_pw1, w_dw, b_dw, ln_g, ln_b, batch, seq_pad, tt):
    d = x.shape[1]
    ch = w_dw.shape[1]
    taps = w_dw.shape[0]
    halo = -(-(taps - 1) // SUBLANES) * SUBLANES
    bufp = jnp.pad(buf, ((0, 0), (halo - (taps - 1), 0), (0, 0)))
    nt = seq_pad // tt
    rb = min(tt, 2 * SUBLANES)
    tail_rows = min(tt, halo)
    const = lambda b, i: (0, 0)
    vec = lambda a: a.reshape(1, -1)
    w_rep = jnp.broadcast_to(w_dw[:, None, :], (taps, SUBLANES, ch))
    return pl.pallas_call(
        functools.partial(_conv_module_kernel, tt=tt, rb=rb),
        grid=(batch, nt + 1),
        in_specs=[pl.BlockSpec((tt, d), lambda b, i: (b * nt + jnp.minimum(i, nt - 1), 0)),
                  pl.BlockSpec((None, halo, ch), lambda b, i: (b, 0, 0)),
                  pl.BlockSpec((1, d), const),
                  pl.BlockSpec((d, 2 * ch), const),
                  pl.BlockSpec((1, 2 * ch), const),
                  pl.BlockSpec((taps, SUBLANES, ch), lambda b, i: (0, 0, 0)),
                  pl.BlockSpec((1, ch), const), pl.BlockSpec((1, ch), const),
                  pl.BlockSpec((1, ch), const)],
        out_specs=[pl.BlockSpec((tt, ch), lambda b, i: (b * nt + jnp.maximum(i - 1, 0), 0)),
                   pl.BlockSpec((None, tail_rows, ch), lambda b, i: (b, 0, 0))],
        out_shape=[jax.ShapeDtypeStruct((batch * seq_pad, ch), BF16),
                   jax.ShapeDtypeStruct((batch, tail_rows, ch), F32)],
        scratch_shapes=[pltpu.VMEM((2 * tt + halo, ch), F32),
                        pltpu.VMEM((SUBLANES - 1, tt + halo - SUBLANES, ch), F32)],
        compiler_params=_cparams("parallel", "arbitrary"),
        name="conv_module",
    )(x, bufp, vec(norm_w), w_pw1_bf16, vec(b_pw1), w_rep, vec(b_dw), vec(ln_g), vec(ln_b))


def _conv_token_kernel(x_ref, buf_ref, nw_ref, w1_ref, b1_ref, w_ref, b_ref, g_ref, bb_ref,
                       o_ref, h_ref):
    dc = o_ref.shape[1]
    taps = w_ref.shape[0]
    xn = _rms(x_ref[...], nw_ref[...], RMS_EPS).astype(BF16)
    a = jnp.dot(xn, w1_ref[:, :dc], preferred_element_type=F32) + b1_ref[:, :dc]
    g = jnp.dot(xn, w1_ref[:, dc:], preferred_element_type=F32) + b1_ref[:, dc:]
    h = a * _sigmoid(g)
    h_ref[...] = h
    acc = b_ref[...] + w_ref[taps - 1:taps, :] * h
    for j in range(taps - 1):
        acc = acc + w_ref[j:j + 1, :] * buf_ref[j]
    mu = jnp.mean(acc, axis=-1, keepdims=True)
    xc = acc - mu
    var = jnp.mean(xc * xc, axis=-1, keepdims=True)
    o_ref[...] = _silu(xc * lax.rsqrt(var + LN_EPS) * g_ref[...] + bb_ref[...])


def _conv_module_token(x, buf, norm_w, w_pw1_bf16, b_pw1, w_dw, b_dw, ln_g, ln_b):
    batch = x.shape[0]
    ch = w_dw.shape[1]
    vec = lambda a: a.reshape(1, -1)
    operands = (x, jnp.transpose(buf, (1, 0, 2)), vec(norm_w), w_pw1_bf16, vec(b_pw1), w_dw,
                vec(b_dw), vec(ln_g), vec(ln_b))
    whole = lambda a: pl.BlockSpec(a.shape, lambda i, nd=a.ndim: (0,) * nd)
    return pl.pallas_call(
        _conv_token_kernel,
        grid=(1,),
        in_specs=[whole(a) for a in operands],
        out_specs=[pl.BlockSpec((batch, ch), lambda i: (0, 0))] * 2,
        out_shape=[jax.ShapeDtypeStruct((batch, ch), F32)] * 2,
        compiler_params=_cparams("arbitrary"),
        name="conv_module_token",
    )(*operands)


def kernel(x_prompt, x_sample, cache_attn_k, cache_attn_v, page_table, state_gdn_conv, state_gdn_s, state_conv_buf, norm_mix, norm_ffn, norm_final, w_in, w_out, gdn_conv_w, gdn_A_log, gdn_dt_bias, gdn_norm_w, lam_q1, lam_k1, lam_q2, lam_k2, diff_subln_w, rel_bias, conv_w_pw1, conv_b_pw1, conv_w_dw, conv_b_dw, conv_ln_g, conv_ln_b, conv_w_pw2, conv_b_pw2, ffn_w_gate, ffn_w_up, ffn_w_down):
    bp, seq, d = x_prompt.shape
    db, dseq, _ = x_sample.shape
    assert dseq == 1
    depth = norm_mix.shape[0]
    ha, dva = cache_attn_k.shape[3], cache_attn_v.shape[4]
    hb, dkb = state_gdn_s.shape[2], state_gdn_s.shape[3]
    c_qkv = state_gdn_conv.shape[3]
    gdn_taps = gdn_conv_w.shape[1]
    conv_taps = conv_w_dw.shape[1]
    d_ff = ffn_w_gate.shape[2]
    mp = bp * seq
    assert 2 * hb <= LANES

    tm_in, tm_ffn = 512, 512
    t_attn = min(512, seq)
    tt_gdn = min(512, seq)
    tt_scan = min(256, seq)
    tt_conv = min(256, seq)
    assert tt_conv >= conv_taps - 1 and seq >= gdn_taps - 1

    xp = x_prompt.reshape(mp, d)
    xs = x_sample.reshape(db, d)
    row = lambda a: a.reshape(1, -1)

    sizes = (ha * dva, ha * dva, ha * dva, c_qkv, hb * dkb, LANES)
    offs = [0]
    for s_ in sizes:
        offs.append(offs[-1] + s_)
    groups = tuple(zip(offs[:-1], sizes))
    inproj_dtypes = (BF16, BF16, BF16, F32, F32, F32)
    inproj_scales = ((dva // 2) ** -0.5 * LOG2E, 1.0, 1.0, 1.0, 1.0, 1.0)
    inproj_outs = ([(wd, 1, dt) for wd, dt in zip(sizes, inproj_dtypes)]
                   + [(dva, ha, F32), (dva, ha, F32)])

    def last_rows(a, n):
        return jnp.stack([a[(b + 1) * seq - n:(b + 1) * seq] for b in range(bp)])

    w_gate_bf, w_up_bf, w_down_bf = (w.astype(BF16) for w in (ffn_w_gate, ffn_w_up, ffn_w_down))

    k_p, v_p, k_s, v_s, gc_p, gc_s, gs_p, gs_s, cb_p, cb_s = ([] for _ in range(10))
    for layer in range(depth):
        if layer % 2 == 0:
            e = layer // 2
            lam_init = 0.8 - 0.6 * math.exp(-0.3 * layer)
            w_in_e = jnp.pad(w_in[e], ((0, 0), (0, offs[-1] - w_in.shape[2]))).astype(BF16)
            outs_p, outs_s = _token_call(
                functools.partial(_inproj_body, groups=groups, scales=inproj_scales, heads=ha,
                                  kv_groups=(1, 2)),
                [xp], [xs], [_whole(row(norm_mix[layer])), _whole(w_in_e)],
                inproj_outs, tm_in, "in_projection")
            qa_p, ka_p, va_p, qkv_p, z_p, ba_p, k4_p, v4_p = outs_p
            qa_s, ka_s, va_s, qkv_s, z_s, ba_s, k4_s, v4_s = outs_s
            lam_vecs = (lam_q1[e], lam_k1[e], lam_q2[e], lam_k2[e])

            oa_p = _attention_prompt(qa_p, ka_p, va_p, rel_bias, lam_vecs, diff_subln_w[e],
                                     bp, seq, ha, lam_init, t=t_attn)
            oa_s = _attention_decode(qa_s.astype(F32), k4_s.reshape(db, ha * dva),
                                     v4_s.reshape(db, ha * dva), cache_attn_k[e], cache_attn_v[e],
                                     page_table, rel_bias, lam_vecs, diff_subln_w[e],
                                     ha, lam_init, pages=min(32, page_table.shape[1]))

            gdn_w = (gdn_conv_w[e], gdn_A_log[e], gdn_dt_bias[e], gdn_norm_w[e])
            ob_p, s_p = _gated_deltanet(
                qkv_p, ba_p, z_p, jnp.zeros((bp, gdn_taps - 1, c_qkv), F32),
                jnp.zeros((bp, hb, dkb, dkb), F32), *gdn_w, bp, seq, hb,
                tt=tt_gdn, scan_batch=bp, scan_tt=tt_scan)
            ob_s, s_s = _gated_deltanet_token(qkv_s, ba_s, z_s, state_gdn_conv[e],
                                              state_gdn_s[e], *gdn_w, hb, nb=min(db, SUBLANES))

            w_o = w_out[e].astype(BF16)
            mix_p, mix_s = [oa_p, ob_p], [oa_s, ob_s]
            mix_w = [_whole(w_o[:ha * dva]), _whole(w_o[ha * dva:]),
                     _whole(jnp.zeros((1, d), F32))]

            k_p.append(k4_p.reshape(bp, seq, ha, dva))
            v_p.append(v4_p.reshape(bp, seq, ha, dva))
            k_s.append(k4_s.reshape(db, 1, ha, dva))
            v_s.append(v4_s.reshape(db, 1, ha, dva))
            gc_p.append(last_rows(qkv_p, gdn_taps - 1))
            gc_s.append(jnp.concatenate([state_gdn_conv[e], qkv_s.reshape(db, 1, c_qkv)],
                                        axis=1)[:, -(gdn_taps - 1):])
            gs_p.append(s_p)
            gs_s.append(s_s)
        else:
            cidx = layer // 2
            dconv = conv_w_dw.shape[2]
            conv_w = (norm_mix[layer], conv_w_pw1[cidx].astype(BF16), conv_b_pw1[cidx],
                      conv_w_dw[cidx], conv_b_dw[cidx], conv_ln_g[cidx], conv_ln_b[cidx])
            hc_p, tail_p = _conv_module(xp, jnp.zeros((bp, conv_taps - 1, dconv), F32), *conv_w,
                                        bp, seq, tt=tt_conv)
            hc_s, hg_s = _conv_module_token(xs, state_conv_buf[cidx], *conv_w)
            mix_p, mix_s = [hc_p], [hc_s]
            mix_w = [_whole(conv_w_pw2[cidx].astype(BF16)), _whole(row(conv_b_pw2[cidx]))]
            cb_p.append(tail_p[:, tail_p.shape[1] - (conv_taps - 1):])
            cb_s.append(jnp.concatenate([state_conv_buf[cidx], hg_s[:, None, :]],
                                        axis=1)[:, -(conv_taps - 1):])
        (xp,), (xs,) = _token_call(
            functools.partial(_mix_ffn_body, final_norm=(layer == depth - 1)),
            mix_p + [xp], mix_s + [xs],
            [_whole(row(norm_ffn[layer])),
             (w_gate_bf, (None, d, d_ff), (layer, 0, 0)),
             (w_up_bf, (None, d, d_ff), (layer, 0, 0)),
             (w_down_bf, (None, d_ff, d), (layer, 0, 0)),
             _whole(row(norm_final))] + mix_w,
            [(d, 1, F32)], tm_ffn, "mixer_out_swiglu_ffn")

    y_prompt = xp.reshape(bp, seq, d)
    y_sample = xs.reshape(db, 1, d)
    return (y_prompt, y_sample, jnp.stack(k_p), jnp.stack(v_p), jnp.stack(k_s), jnp.stack(v_s),
            jnp.stack(gc_p), jnp.stack(gc_s), jnp.stack(gs_p), jnp.stack(gs_s),
            jnp.stack(cb_p), jnp.stack(cb_s))
```

```python
import functools
import math

import jax
import jax.numpy as jnp
from jax import lax
from jax.experimental import pallas as pl
from jax.experimental.pallas import tpu as pltpu

F32 = jnp.float32
BF16 = jnp.bfloat16

RMS_EPS = 1e-6
LN_EPS = 1e-5
L2_EPS = 1e-6
NUM_BUCKETS = 32
MAX_DISTANCE = 128
GDN_CHUNK = 64
NEG_BIG = -1e30
LOG2E = math.log2(math.e)
LANES = 128
SUBLANES = 8
VMEM_LIMIT = 48 * 1024 * 1024


def _cparams(*sem):
    return pltpu.CompilerParams(dimension_semantics=sem, vmem_limit_bytes=VMEM_LIMIT)


def _mm(a, b):
    return jnp.dot(a.astype(BF16), b.astype(BF16), preferred_element_type=F32)


def _mm_nt(a, b):
    return lax.dot_general(a.astype(BF16), b.astype(BF16), (((1,), (1,)), ((), ())),
                           preferred_element_type=F32)


def _mm_tn(a, b):
    return lax.dot_general(a.astype(BF16), b.astype(BF16), (((0,), (0,)), ((), ())),
                           preferred_element_type=F32)


def _split3(x):
    hi = x.astype(BF16)
    r1 = x - hi.astype(F32)
    mid = r1.astype(BF16)
    lo = (r1 - mid.astype(F32)).astype(BF16)
    return hi, mid, lo


def _select_mm(sel, x, nt=False):
    sel = sel.astype(BF16)
    dims = (((1,), (1,)), ((), ())) if nt else (((1,), (0,)), ((), ()))
    parts = [lax.dot_general(sel, p, dims, preferred_element_type=F32) for p in _split3(x)]
    return parts[0] + (parts[1] + parts[2])


def _rms(x, w, eps):
    return x * lax.rsqrt(jnp.mean(x * x, axis=-1, keepdims=True) + eps) * w


def _sigmoid(x):
    return 1.0 / (1.0 + jnp.exp(-x))


def _silu(x):
    return x * _sigmoid(x)


def _lane_pick(x, lane_idx, k):
    return jnp.sum(jnp.where(lane_idx == k, x, 0.0), axis=-1, keepdims=True)


def _token_call(body, prompt_in, sample_in, shared, outs, tm, name):
    mp, ms = prompt_in[0].shape[0], sample_in[0].shape[0]
    assert mp % tm == 0
    n = mp // tm
    n_in, n_sh, n_out = len(prompt_in), len(shared), len(outs)

    def kern(*refs):
        p_in, s_in = refs[:n_in], refs[n_in:2 * n_in]
        sh = refs[2 * n_in:2 * n_in + n_sh]
        o0 = 2 * n_in + n_sh
        p_out, s_out = refs[o0:o0 + n_out], refs[o0 + n_out:]
        i = pl.program_id(0)

        @pl.when(i < n)
        def _():
            body(p_in, sh, p_out)

        @pl.when(i == n)
        def _():
            body(s_in, sh, s_out)

    prow = lambda i: (jnp.minimum(i, n - 1), 0)
    srow = lambda i: (0, 0)
    in_specs = ([pl.BlockSpec((tm, a.shape[1]), prow) for a in prompt_in]
                + [pl.BlockSpec((ms, a.shape[1]), srow) for a in sample_in]
                + [pl.BlockSpec(bs, (lambda i, idx=idx: idx), pipeline_mode=pl.Buffered(1))
                   for _, bs, idx in shared])
    out_specs = ([pl.BlockSpec((tm * r, w), prow) for w, r, _ in outs]
                 + [pl.BlockSpec((ms * r, w), srow) for w, r, _ in outs])
    out_shape = ([jax.ShapeDtypeStruct((mp * r, w), dt) for w, r, dt in outs]
                 + [jax.ShapeDtypeStruct((ms * r, w), dt) for w, r, dt in outs])
    res = pl.pallas_call(
        kern, grid=(n + 1,), in_specs=in_specs, out_specs=out_specs, out_shape=out_shape,
        compiler_params=_cparams("arbitrary"), name=name,
    )(*prompt_in, *sample_in, *[a for a, _, _ in shared])
    return res[:n_out], res[n_out:]


def _whole(a):
    return (a, a.shape, (0,) * a.ndim)


def _inproj_body(ins, sh, outs, *, groups, scales, heads, kv_groups):
    x_ref, = ins
    nw_ref, w_ref = sh
    xn = _rms(x_ref[...], nw_ref[...], RMS_EPS).astype(BF16)
    rows = x_ref.shape[0]
    n_g = len(groups)
    for g, (o_ref, (off, width)) in enumerate(zip(outs[:n_g], groups)):
        y = jnp.dot(xn, w_ref[:, off:off + width], preferred_element_type=F32)
        o_ref[...] = (y if scales[g] == 1.0 else y * scales[g]).astype(o_ref.dtype)
        if g in kv_groups:
            o4_ref = outs[n_g + kv_groups.index(g)]
            dv = width // heads
            for h in range(heads):
                o4_ref[pl.ds(h, rows, stride=heads), :] = y[:, h * dv:(h + 1) * dv]


def _mix_ffn_body(ins, sh, outs, *, final_norm):
    res_ref = ins[-1]
    nw_ref, wg_ref, wu_ref, wd_ref, fw_ref = sh[:5]
    o_ref, = outs
    x = res_ref[...] + sh[-1][...]
    for a_ref, w_ref in zip(ins[:-1], sh[5:-1]):
        x = x + _mm(a_ref[...], w_ref[...])
    xn = _rms(x, nw_ref[...], RMS_EPS).astype(BF16)
    g = jnp.dot(xn, wg_ref[...], preferred_element_type=F32)
    u = jnp.dot(xn, wu_ref[...], preferred_element_type=F32)
    y = x + _mm(_silu(g) * u, wd_ref[...])
    if final_norm:
        y = _rms(y, fw_ref[...], RMS_EPS)
    o_ref[...] = y


def _t5_bucket(n):
    max_exact = NUM_BUCKETS // 2
    nf = jnp.maximum(n, 1).astype(F32)
    large = max_exact + (jnp.log(nf / max_exact) / math.log(MAX_DISTANCE / max_exact)
                         * (NUM_BUCKETS - max_exact)).astype(jnp.int32)
    large = jnp.minimum(large, NUM_BUCKETS - 1)
    return jnp.where(n < max_exact, n, large)


def _lambda(lq1_ref, lk1_ref, lq2_ref, lk2_ref, lam_init):
    s1 = jnp.sum(lq1_ref[...] * lk1_ref[...], axis=-1, keepdims=True)
    s2 = jnp.sum(lq2_ref[...] * lk2_ref[...], axis=-1, keepdims=True)
    return jnp.exp(s1) - jnp.exp(s2) + lam_init


def _attn_prompt_kernel(q_ref, k_ref, v_ref, bias_ref, lq1_ref, lk1_ref, lq2_ref, lk2_ref,
                        sw_ref, o_ref, *scratch, t, da, dv, hps, lam_init):
    i = pl.program_id(2)
    qt_refs, vt_refs, m_refs, acc_refs, sa_refs, sb_refs = (
        scratch[n * hps:(n + 1) * hps] for n in range(6))
    nblk = vt_refs[0].shape[0]
    heads = range(hps)
    lanes = lambda hh: slice(hh * dv, (hh + 1) * dv)

    @pl.when(i == 0)
    def _():
        for hh in heads:
            for c in range(nblk):
                vt_refs[hh][c, 0:dv, :] = (
                    v_ref[c * t:(c + 1) * t, lanes(hh)].astype(F32).T.astype(BF16))
                vt_refs[hh][c, dv:, :] = jnp.ones((vt_refs[hh].shape[1] - dv, t), BF16)

    for hh in heads:
        qt = q_ref[:, lanes(hh)].astype(F32).T
        first_map = lax.broadcasted_iota(jnp.int32, qt.shape, 0) < da
        qt_refs[hh][:, 0:t] = jnp.where(first_map, qt, 0.0).astype(BF16)
        qt_refs[hh][:, t:2 * t] = jnp.where(first_map, 0.0, qt).astype(BF16)
        m_refs[hh][...] = jnp.full(m_refs[hh].shape, NEG_BIG, F32)
        acc_refs[hh][...] = jnp.zeros(acc_refs[hh].shape, F32)

    def scores(j, buf_refs):
        rows = pl.ds(pl.multiple_of(j * t, t), t)
        for hh in heads:
            buf_refs[hh][...] = jnp.dot(k_ref[rows, lanes(hh)], qt_refs[hh][...],
                                        preferred_element_type=F32)

    def update(j, buf_refs, bias_idx):
        for hh in heads:
            s = buf_refs[hh][...]
            if bias_idx is not None:
                bias = bias_ref[hh, bias_idx]
                s = s + jnp.concatenate([bias, bias], axis=1)
            m_prev = m_refs[hh][...]
            m_new = jnp.maximum(m_prev, jnp.max(s, axis=0, keepdims=True))
            p = jnp.exp2(s - m_new).astype(BF16)
            acc_refs[hh][...] = (acc_refs[hh][...] * jnp.exp2(m_prev - m_new)
                                 + jnp.dot(vt_refs[hh][j], p, preferred_element_type=F32))
            m_refs[hh][...] = m_new

    prev_bias, diag_bias = 0, 1
    scores(0, sa_refs)

    def far_pair(jj, carry):
        j = 2 * jj
        scores(j + 1, sb_refs)
        update(j, sa_refs, None)
        scores(j + 2, sa_refs)
        update(j + 1, sb_refs, None)
        return carry

    n_far = i - 1
    lax.fori_loop(0, n_far // 2, far_pair, 0)

    @pl.when(i % 2 == 1)
    def _():
        scores(i, sb_refs)
        update(i - 1, sa_refs, prev_bias)
        update(i, sb_refs, diag_bias)

    @pl.when((i % 2 == 0) & (i >= 2))
    def _():
        scores(i - 1, sb_refs)
        update(i - 2, sa_refs, None)
        scores(i, sa_refs)
        update(i - 1, sb_refs, prev_bias)
        update(i, sa_refs, diag_bias)

    @pl.when(i == 0)
    def _():
        update(0, sa_refs, diag_bias)

    lam = _lambda(lq1_ref, lk1_ref, lq2_ref, lk2_ref, lam_init)
    for hh in heads:
        acc = acc_refs[hh][...]
        o12 = acc[0:dv] / acc[dv:dv + 1]
        o = (o12[:, 0:t] - lam * o12[:, t:2 * t]).T
        o_ref[:, lanes(hh)] = (_rms(o, sw_ref[...], LN_EPS) * (1.0 - lam_init)).astype(o_ref.dtype)


def _toeplitz(r, t):
    h, period = r.shape
    flat = jnp.tile(r, (1, t))[:, :t * (period - 1)]
    return flat.reshape(h, t, period - 1)[:, :, :t]


def _prompt_bias_tiles(rel_bias, t):
    d = MAX_DISTANCE
    assert t % d == 0
    nb = t // d
    far = rel_bias[NUM_BUCKETS - 1]
    b1 = jnp.transpose(rel_bias[_t5_bucket(jnp.arange(d))] - far).astype(F32) * LOG2E
    zero = jnp.zeros_like(b1)
    neg = jnp.full_like(b1, NEG_BIG)
    tz = _toeplitz(jnp.concatenate([b1, zero, neg, neg], axis=1), 2 * d)
    g0, g1 = tz[:, :d, :d], tz[:, :d, d:]
    zero_blk, neg_blk = jnp.zeros_like(g0), jnp.full_like(g0, NEG_BIG)

    def diag_block(r, c):
        return neg_blk if c < r else g0 if c == r else g1 if c == r + 1 else zero_blk

    diag = jnp.block([[diag_block(r, c) for c in range(nb)] for r in range(nb)])
    prev = jnp.block([[g1 if (r, c) == (nb - 1, 0) else zero_blk for c in range(nb)]
                      for r in range(nb)])
    return jnp.stack([prev, diag], axis=1)


def _attention_prompt(q, k, v, rel_bias, lam_vecs, subln_w, batch, seq, heads, lam_init, t):
    dv = q.shape[1] // heads
    da = dv // 2
    nq = seq // t
    hps = max(g for g in (4, 2, 1) if heads % g == 0)
    gw = hps * dv
    bias = _prompt_bias_tiles(rel_bias, t)
    vec = lambda a: a.reshape(1, -1)
    const = lambda b, g, i: (0, 0)
    per_head = lambda shape, dtype: [pltpu.VMEM(shape, dtype)] * hps
    return pl.pallas_call(
        functools.partial(_attn_prompt_kernel, t=t, da=da, dv=dv, hps=hps, lam_init=lam_init),
        grid=(batch, heads // hps, nq),
        in_specs=[pl.BlockSpec((t, gw), lambda b, g, i: (b * nq + i, g)),
                  pl.BlockSpec((seq, gw), lambda b, g, i: (b, g)),
                  pl.BlockSpec((seq, gw), lambda b, g, i: (b, g)),
                  pl.BlockSpec((hps, 2, t, t), lambda b, g, i: (g, 0, 0, 0)),
                  pl.BlockSpec((1, da), const), pl.BlockSpec((1, da), const),
                  pl.BlockSpec((1, da), const), pl.BlockSpec((1, da), const),
                  pl.BlockSpec((1, dv), const)],
        out_specs=pl.BlockSpec((t, gw), lambda b, g, i: (b * nq + i, g)),
        out_shape=jax.ShapeDtypeStruct((batch * seq, heads * dv), BF16),
        scratch_shapes=(per_head((dv, 2 * t), BF16)
                        + per_head((nq, dv + 2 * SUBLANES, t), BF16)
                        + per_head((1, 2 * t), F32)
                        + per_head((dv + 2 * SUBLANES, 2 * t), F32)
                        + per_head((t, 2 * t), F32) + per_head((t, 2 * t), F32)),
        compiler_params=_cparams("parallel", "parallel", "arbitrary"),
        name="diff_attention_prompt",
    )(q, k, v, bias, *[vec(a) for a in lam_vecs], vec(subln_w))


def _attn_decode_kernel(pt_ref, q_ref, kn_ref, vn_ref, bfar_ref, blast_ref, bnew_ref,
                        lq1_ref, lk1_ref, lq2_ref, lk2_ref, sw_ref, *rest,
                        pages, heads, da, lam_init):
    del pt_ref
    k_refs = rest[:pages]
    v_refs = rest[pages:2 * pages]
    o_ref, m_ref, l_ref, acc_ref = rest[2 * pages:]
    j = pl.program_id(1)
    last = pl.num_programs(1) - 1

    @pl.when(j == 0)
    def _():
        m_ref[...] = jnp.full(m_ref.shape, NEG_BIG, F32)
        l_ref[...] = jnp.zeros(l_ref.shape, F32)
        acc_ref[...] = jnp.zeros(acc_ref.shape, F32)

    q = q_ref[...]
    row = lax.broadcasted_iota(jnp.int32, q.shape, 0)
    lane = lax.broadcasted_iota(jnp.int32, q.shape, 1)
    qs = jnp.where((row < heads) == (lane < da), q, 0.0)
    qs_bf = qs.astype(BF16)

    s = jnp.concatenate([_mm_nt(qs_bf, k_ref[...]) for k_ref in k_refs], axis=1)
    s = s + jnp.where(j == last, blast_ref[...], bfar_ref[...])
    m_prev = m_ref[...]
    m_new = jnp.maximum(m_prev, jnp.max(s, axis=-1, keepdims=True))
    p = jnp.exp2(s - m_new)
    alpha = jnp.exp2(m_prev - m_new)
    l_ref[...] = alpha * l_ref[...] + jnp.sum(p, axis=-1, keepdims=True)
    rows_per_page = k_refs[0].shape[0]
    pv = acc_ref[...] * alpha
    for idx, v_ref in enumerate(v_refs):
        pv = pv + _mm(p[:, idx * rows_per_page:(idx + 1) * rows_per_page], v_ref[...])
    acc_ref[...] = pv
    m_ref[...] = m_new

    @pl.when(j == last)
    def _():
        s_new = jnp.sum(qs * kn_ref[...], axis=-1, keepdims=True) + bnew_ref[:, 0:1]
        m_prev = m_ref[...]
        m_fin = jnp.maximum(m_prev, s_new)
        p_new = jnp.exp2(s_new - m_fin)
        alpha = jnp.exp2(m_prev - m_fin)
        l_fin = alpha * l_ref[...] + p_new
        acc = alpha * acc_ref[...] + p_new * vn_ref[...]
        o12 = acc / l_fin
        lam = _lambda(lq1_ref, lk1_ref, lq2_ref, lk2_ref, lam_init)
        o = o12[0:heads] - lam * o12[heads:2 * heads]
        o_ref[...] = _rms(o, sw_ref[...], LN_EPS) * (1.0 - lam_init)


def _attention_decode(q, k_new, v_new, k_pool, v_pool, page_table, rel_bias, lam_vecs,
                      subln_w, heads, lam_init, pages):
    db = q.shape[0]
    n_pool, page, _, dv = k_pool.shape
    da = dv // 2
    n_pages = page_table.shape[1]
    past = n_pages * page
    rpp = page * heads
    span = pages * page
    assert n_pages % pages == 0 and span >= MAX_DISTANCE
    kp = k_pool.reshape(n_pool, rpp, dv)
    vp = v_pool.reshape(n_pool, rpp, dv)

    def two_maps(a):
        a = a.reshape(db, 1, heads, dv)
        return jnp.broadcast_to(a, (db, 2, heads, dv)).reshape(db, 2 * heads, dv)

    rb = rel_bias.astype(F32) * LOG2E
    near = MAX_DISTANCE
    row_head = jnp.arange(2 * heads) % heads
    same = (jnp.arange(span * heads) % heads)[None, :] == row_head[:, None]
    b_far = jnp.where(same, rb[NUM_BUCKETS - 1][row_head][:, None], NEG_BIG)
    tab = jnp.transpose(rb[_t5_bucket(near - jnp.arange(near))])[row_head]
    near_part = jnp.where(same[:, :near * heads], jnp.repeat(tab, heads, axis=1), NEG_BIG)
    b_last = jnp.concatenate([b_far[:, :(span - near) * heads], near_part], axis=1)
    b_new = jnp.tile(jnp.broadcast_to(rb[0][:, None], (heads, LANES)), (2, 1))

    vec = lambda a: a.reshape(1, -1)
    const = lambda s, j, pt: (0, 0)

    def page_spec(idx):
        return pl.BlockSpec((None, rpp, dv), lambda s, j, pt: (pt[s, j * pages + idx], 0, 0))

    grid_spec = pltpu.PrefetchScalarGridSpec(
        num_scalar_prefetch=1,
        grid=(db, n_pages // pages),
        in_specs=([pl.BlockSpec((None, 2 * heads, dv), lambda s, j, pt: (s, 0, 0))] * 3
                  + [pl.BlockSpec((2 * heads, span * heads), const),
                     pl.BlockSpec((2 * heads, span * heads), const),
                     pl.BlockSpec((2 * heads, LANES), const),
                     pl.BlockSpec((1, da), const), pl.BlockSpec((1, da), const),
                     pl.BlockSpec((1, da), const), pl.BlockSpec((1, da), const),
                     pl.BlockSpec((1, dv), const)]
                  + [page_spec(idx) for idx in range(pages)] * 2),
        out_specs=pl.BlockSpec((None, heads, dv), lambda s, j, pt: (s, 0, 0)),
        scratch_shapes=[pltpu.VMEM((2 * heads, 1), F32), pltpu.VMEM((2 * heads, 1), F32),
                        pltpu.VMEM((2 * heads, dv), F32)],
    )
    out = pl.pallas_call(
        functools.partial(_attn_decode_kernel, pages=pages, heads=heads, da=da,
                          lam_init=lam_init),
        grid_spec=grid_spec,
        out_shape=jax.ShapeDtypeStruct((db, heads, dv), F32),
        compiler_params=_cparams("parallel", "arbitrary"),
        name="diff_attention_decode",
    )(page_table, two_maps(q), two_maps(k_new), two_maps(v_new), b_far, b_last, b_new,
      *[vec(a) for a in lam_vecs], vec(subln_w), *([kp] * pages), *([vp] * pages))
    return out.reshape(db, heads * dv)


def _unit_lower_inverse(lmats, eye, levels):
    if levels == 0:
        return [eye for _ in lmats]
    xs = [-m for m in lmats]
    tinvs = [eye + x for x in xs]
    if levels == 1:
        return tinvs
    n = eye.shape[0]
    rs = [_mm(x, x) for x in xs]
    for k in range(1, levels):
        if k < levels - 1:
            both = [_mm(jnp.concatenate([r, t], axis=0), r) for r, t in zip(rs, tinvs)]
            rs = [b[:n] for b in both]
            tinvs = [t + b[n:] for t, b in zip(tinvs, both)]
        else:
            tinvs = [t + _mm(t, r) for t, r in zip(tinvs, rs)]
    return tinvs


def _gdn_pre_kernel(x_ref, st_ref, cw_ref, ba_ref, gp_ref,
                    u_ref, w_ref, qd_ref, kd_ref, at_ref, eg_ref,
                    xe_ref, qn_ref, kn_ref, vv_ref, gb_ref, gc_ref, *, tt, heads, dk):
    c = GDN_CHUNK
    i = pl.program_id(1)
    taps = cw_ref.shape[0]
    halo = SUBLANES
    hk = heads * dk

    @pl.when(i == 0)
    def _():
        xe_ref[0:halo, :] = st_ref[...]

    xe_ref[halo:halo + tt, :] = x_ref[...]
    acc = cw_ref[taps - 1:taps, :] * x_ref[...]
    for j in range(taps - 1):
        acc = acc + cw_ref[j:j + 1, :] * xe_ref[pl.ds(halo - (taps - 1) + j, tt), :]
    xe_ref[0:halo, :] = xe_ref[tt:tt + halo, :]
    hcv = _silu(acc)

    for h in range(heads):
        sl = slice(h * dk, (h + 1) * dk)
        qh = hcv[:, h * dk:(h + 1) * dk]
        kh = hcv[:, hk + h * dk:hk + (h + 1) * dk]
        qn_ref[:, sl] = qh * lax.rsqrt(jnp.sum(qh * qh, axis=-1, keepdims=True) + L2_EPS)
        kn_ref[:, sl] = kh * lax.rsqrt(jnp.sum(kh * kh, axis=-1, keepdims=True) + L2_EPS)
    vv_ref[...] = hcv[:, 2 * hk:]

    ba = ba_ref[...]
    lane = lax.broadcasted_iota(jnp.int32, ba.shape, 1)
    xa = ba + gp_ref[1:2, :]
    softplus = jnp.maximum(xa, 0.0) + jnp.log1p(jnp.exp(-jnp.abs(xa)))
    gates = jnp.where(lane < heads, _sigmoid(ba), -jnp.exp(gp_ref[0:1, :]) * softplus)
    gates = jnp.where(lane < 2 * heads, gates, 0.0)
    gb_ref[...] = gates

    tri_l = (lax.broadcasted_iota(jnp.int32, (c, c), 0)
             >= lax.broadcasted_iota(jnp.int32, (c, c), 1)).astype(F32)
    for ch in range(tt // c):
        gc_ref[ch * c:(ch + 1) * c, :] = _select_mm(tri_l, gates[ch * c:(ch + 1) * c, :])

    ri = lax.broadcasted_iota(jnp.int32, (c, c), 0)
    ci = lax.broadcasted_iota(jnp.int32, (c, c), 1)
    incl = ri >= ci
    strict = ri > ci
    eye = (ri == ci).astype(F32)
    pick = (lax.broadcasted_iota(jnp.int32, (SUBLANES, LANES), 0)
            == lax.broadcasted_iota(jnp.int32, (SUBLANES, LANES), 1)).astype(F32)
    lane_c = lax.broadcasted_iota(jnp.int32, (c, LANES), 1)
    scale = dk ** -0.5
    levels = (c - 1).bit_length()

    n_chunks = tt // c
    group = max(g for g in (8, 4, 2, 1) if n_chunks % g == 0)

    def group_body(gi, carry):
        probs = []
        for cc in range(group):
            ch = gi * group + cc
            rows = pl.ds(pl.multiple_of(ch * c, c), c)
            gb = gb_ref[rows, :]
            gcc = gc_ref[rows, :]
            gc_rows = _select_mm(pick, gcc, nt=True)
            for h in range(heads):
                probs.append(dict(
                    ch=ch, rows=rows, h=h, sl=slice(h * dk, (h + 1) * dk),
                    beta=_lane_pick(gb, lane_c, h),
                    gc_col=_lane_pick(gcc, lane_c, heads + h),
                    gc_row=gc_rows[heads + h:heads + h + 1, :]))
        for p in probs:
            p["decay"] = jnp.where(
                incl, jnp.exp(jnp.where(incl, p["gc_col"] - p["gc_row"], 0.0)), 0.0)
            p["k"] = kn_ref[p["rows"], p["sl"]]
            p["k_beta"] = p["k"] * p["beta"]
        kk = [_mm_nt(p["k_beta"], p["k"]) for p in probs]
        lmats = [jnp.where(strict, m * p["decay"], 0.0) for m, p in zip(kk, probs)]
        tinvs = _unit_lower_inverse(lmats, eye, levels)
        for p in probs:
            p["egc"] = jnp.exp(p["gc_col"])
            p["q"] = qn_ref[p["rows"], p["sl"]] * scale
        us = [_mm(t, vv_ref[p["rows"], p["sl"]] * p["beta"]) for t, p in zip(tinvs, probs)]
        ws = [_mm(t, p["k_beta"] * p["egc"]) for t, p in zip(tinvs, probs)]
        ats = [_mm_nt(p["q"], p["k"]) for p in probs]
        for p, u, w, at in zip(probs, us, ws, ats):
            rows, sl, h = p["rows"], p["sl"], p["h"]
            g_last = p["gc_col"][c - 1:c, :]
            u_ref[rows, sl] = u
            w_ref[rows, sl] = w.astype(w_ref.dtype)
            at_ref[rows, h * c:(h + 1) * c] = (at * p["decay"]).astype(at_ref.dtype)
            kd_ref[rows, sl] = p["k"] * jnp.exp(g_last - p["gc_col"])
            qd_ref[rows, sl] = (p["q"] * p["egc"]).astype(qd_ref.dtype)
            eg_ref[p["ch"], :, sl] = jnp.broadcast_to(jnp.exp(g_last), (SUBLANES, dk))
        return carry

    lax.fori_loop(0, n_chunks // group, group_body, 0)


def _gdn_scan_kernel(u_ref, w_ref, qd_ref, kd_ref, at_ref, eg_ref, z_ref, s0_ref, nw_ref,
                     o_ref, sout_ref, s_ref, kt_ref, *, tt, heads, dk):
    c = GDN_CHUNK
    i = pl.program_id(1)
    nb = u_ref.shape[0]

    @pl.when(i == 0)
    def _():
        s_ref[...] = s0_ref[...]

    for b in range(nb):
        for ch in range(tt // c):
            for h in range(heads):
                kt_ref[b, ch * heads + h] = (
                    kd_ref[b, ch * c:(ch + 1) * c, h * dk:(h + 1) * dk].T.astype(BF16))

    chains = [(b, h) for b in range(nb) for h in range(heads)]
    cols = lambda h: slice(h * dk, (h + 1) * dk)

    def chunk_body(ch, carry):
        rows = pl.ds(pl.multiple_of(ch * c, c), c)
        s_bf = [s_ref[b, h].astype(BF16) for b, h in chains]
        w_s = [_mm(w_ref[b, rows, cols(h)], s) for (b, h), s in zip(chains, s_bf)]
        q_s = [_mm(qd_ref[b, rows, cols(h)], s) for (b, h), s in zip(chains, s_bf)]
        v_new = [(u_ref[b, rows, cols(h)] - ws).astype(BF16) for (b, h), ws in zip(chains, w_s)]
        a_v = [_mm(at_ref[b, rows, h * c:(h + 1) * c], v) for (b, h), v in zip(chains, v_new)]
        k_v = [_mm(kt_ref[b, ch * heads + h], v) for (b, h), v in zip(chains, v_new)]
        for (b, h), qs, av, kv in zip(chains, q_s, a_v, k_v):
            s_ref[b, h] = s_ref[b, h] * eg_ref[b, ch, 0:1, cols(h)] + kv
            o_ref[b, rows, cols(h)] = (_rms(qs + av, nw_ref[...], RMS_EPS)
                                       * _silu(z_ref[b, rows, cols(h)])).astype(o_ref.dtype)
        return carry

    for ch in range(tt // c):
        chunk_body(ch, 0)

    @pl.when(i == pl.num_programs(1) - 1)
    def _():
        sout_ref[...] = s_ref[...]


def _gated_deltanet(qkv, ba, z, conv_state, s0, conv_w, a_log, dt_bias, norm_w,
                    batch, seq_pad, heads, tt, scan_batch, scan_tt):
    cq = qkv.shape[1]
    hk = cq // 3
    dk = hk // heads
    c = GDN_CHUNK
    assert c & (c - 1) == 0 and batch % scan_batch == 0
    nt = seq_pad // tt
    nc = tt // c
    taps = conv_w.shape[0]
    st = jnp.pad(conv_state, ((0, 0), (SUBLANES - (taps - 1), 0), (0, 0)))
    gp = jnp.zeros((SUBLANES, LANES), F32)
    gp = gp.at[0, heads:2 * heads].set(a_log.astype(F32))
    gp = gp.at[1, heads:2 * heads].set(dt_bias.astype(F32))
    row_blk = lambda b, i: (b * nt + i, 0)
    const = lambda b, i: (0, 0)
    tok = lambda width: pl.BlockSpec((tt, width), row_blk)
    rows = batch * seq_pad
    u, w, qd, kd, at, eg = pl.pallas_call(
        functools.partial(_gdn_pre_kernel, tt=tt, heads=heads, dk=dk),
        grid=(batch, nt),
        in_specs=[tok(cq),
                  pl.BlockSpec((None, SUBLANES, cq), lambda b, i: (b, 0, 0)),
                  pl.BlockSpec((taps, cq), const),
                  tok(LANES),
                  pl.BlockSpec((SUBLANES, LANES), const)],
        out_specs=[tok(hk), tok(hk), tok(hk), tok(hk), tok(heads * c),
                   pl.BlockSpec((None, nc, SUBLANES, hk), lambda b, i: (b, i, 0, 0))],
        out_shape=[jax.ShapeDtypeStruct((rows, hk), dt) for dt in (F32, BF16, BF16, F32)]
                  + [jax.ShapeDtypeStruct((rows, heads * c), BF16),
                     jax.ShapeDtypeStruct((batch, nt * nc, SUBLANES, hk), F32)],
        scratch_shapes=[pltpu.VMEM((tt + SUBLANES, cq), F32), pltpu.VMEM((tt, hk), F32),
                        pltpu.VMEM((tt, hk), F32), pltpu.VMEM((tt, hk), F32),
                        pltpu.VMEM((tt, LANES), F32), pltpu.VMEM((tt, LANES), F32)],
        compiler_params=_cparams("parallel", "arbitrary"),
        name="gdn_chunk_prepare",
    )(qkv, st, conv_w, ba, gp)

    nb, stt = scan_batch, scan_tt
    snc = stt // c
    seq3 = lambda a: a.reshape(batch, seq_pad, a.shape[-1])
    blk3 = lambda width: pl.BlockSpec((nb, stt, width), lambda g, i: (g, i, 0))
    state_spec = pl.BlockSpec((nb, heads, dk, dk), lambda g, i: (g, 0, 0, 0))
    o, s_new = pl.pallas_call(
        functools.partial(_gdn_scan_kernel, tt=stt, heads=heads, dk=dk),
        grid=(batch // nb, seq_pad // stt),
        in_specs=[blk3(hk), blk3(hk), blk3(hk), blk3(hk), blk3(heads * c),
                  pl.BlockSpec((nb, snc, SUBLANES, hk), lambda g, i: (g, i, 0, 0)),
                  blk3(hk), state_spec,
                  pl.BlockSpec((1, dk), const)],
        out_specs=[blk3(hk), state_spec],
        out_shape=[jax.ShapeDtypeStruct((batch, seq_pad, hk), BF16),
                   jax.ShapeDtypeStruct((batch, heads, dk, dk), F32)],
        scratch_shapes=[pltpu.VMEM((nb, heads, dk, dk), F32),
                        pltpu.VMEM((nb, snc * heads, dk, c), BF16)],
        compiler_params=_cparams("parallel", "arbitrary"),
        name="gdn_chunk_scan",
    )(seq3(u), seq3(w), seq3(qd), seq3(kd), seq3(at), eg, seq3(z), s0, norm_w.reshape(1, dk))
    return o.reshape(rows, hk), s_new


def _gdn_token_kernel(x_ref, st_ref, cw_ref, ba_ref, gp_ref, z_ref, s0_ref, nw_ref,
                      o_ref, sout_ref, *, heads, dk):
    nb = x_ref.shape[0]
    hk = heads * dk
    taps = cw_ref.shape[0]
    acc = cw_ref[taps - 1:taps, :] * x_ref[...]
    for j in range(taps - 1):
        acc = acc + cw_ref[j:j + 1, :] * st_ref[j]
    hcv = _silu(acc)

    ba = ba_ref[...]
    lane = lax.broadcasted_iota(jnp.int32, ba.shape, 1)
    xa = ba + gp_ref[1:2, :]
    softplus = jnp.maximum(xa, 0.0) + jnp.log1p(jnp.exp(-jnp.abs(xa)))
    gates = jnp.where(lane < heads, _sigmoid(ba), -jnp.exp(gp_ref[0:1, :]) * softplus)

    for h in range(heads):
        sl = slice(h * dk, (h + 1) * dk)
        qh = hcv[:, h * dk:(h + 1) * dk]
        kh = hcv[:, hk + h * dk:hk + (h + 1) * dk]
        q = qh * lax.rsqrt(jnp.sum(qh * qh, axis=-1, keepdims=True) + L2_EPS) * (dk ** -0.5)
        k = kh * lax.rsqrt(jnp.sum(kh * kh, axis=-1, keepdims=True) + L2_EPS)
        v = hcv[:, 2 * hk + h * dk:2 * hk + (h + 1) * dk]
        beta = _lane_pick(gates, lane, h)
        decay = jnp.exp(_lane_pick(gates, lane, heads + h))
        k_cols = k.T
        out_rows = []
        for b in range(nb):
            state = s0_ref[b, h]
            row = slice(b, b + 1)
            k_s = _mm(k, state)[row]
            v_new = beta[row] * (v[row] - decay[row] * k_s)
            new_state = state * decay[row] + k_cols[:, b:b + 1] * v_new
            sout_ref[b, h] = new_state
            out_rows.append(_mm(q, new_state)[row])
        o = jnp.concatenate(out_rows, axis=0)
        o_ref[:, sl] = _rms(o, nw_ref[...], RMS_EPS) * _silu(z_ref[:, sl])


def _gated_deltanet_token(qkv, ba, z, conv_state, s0, conv_w, a_log, dt_bias, norm_w, heads, nb):
    batch, cq = qkv.shape
    hk = cq // 3
    dk = hk // heads
    taps = conv_w.shape[0]
    assert batch % nb == 0
    gp = jnp.zeros((SUBLANES, LANES), F32)
    gp = gp.at[0, heads:2 * heads].set(a_log.astype(F32))
    gp = gp.at[1, heads:2 * heads].set(dt_bias.astype(F32))
    const = lambda g: (0, 0)
    rows = lambda width: pl.BlockSpec((nb, width), lambda g: (g, 0))
    state_spec = pl.BlockSpec((nb, heads, dk, dk), lambda g: (g, 0, 0, 0))
    return pl.pallas_call(
        functools.partial(_gdn_token_kernel, heads=heads, dk=dk),
        grid=(batch // nb,),
        in_specs=[rows(cq),
                  pl.BlockSpec((taps - 1, nb, cq), lambda g: (0, g, 0)),
                  pl.BlockSpec((taps, cq), const),
                  rows(LANES),
                  pl.BlockSpec((SUBLANES, LANES), const),
                  rows(hk), state_spec,
                  pl.BlockSpec((1, dk), const)],
        out_specs=[rows(hk), state_spec],
        out_shape=[jax.ShapeDtypeStruct((batch, hk), F32),
                   jax.ShapeDtypeStruct((batch, heads, dk, dk), F32)],
        compiler_params=_cparams("parallel"),
        name="gdn_token",
    )(qkv, jnp.transpose(conv_state, (1, 0, 2)), conv_w, ba, gp, z, s0, norm_w.reshape(1, dk))


def _conv_module_kernel(x_ref, buf_ref, nw_ref, w1_ref, b1_ref, w_ref, b_ref, g_ref, bb_ref,
                        o_ref, tail_ref, xe_ref, xs_ref, *, tt, rb):
    i = pl.program_id(1)
    last_tile = pl.num_programs(1) - 2
    taps = w_ref.shape[0]
    halo = buf_ref.shape[0]
    first = halo - (taps - 1)
    sub = w_ref.shape[1]
    dc = o_ref.shape[1]

    @pl.when(i == 0)
    def _():
        xe_ref[0:tt, :] = jnp.zeros((tt, dc), F32)
        xe_ref[tt:tt + halo, :] = buf_ref[...]

    def glu_stage():
        xn = _rms(x_ref[...], nw_ref[...], RMS_EPS).astype(BF16)
        a = jnp.dot(xn, w1_ref[:, :dc], preferred_element_type=F32) + b1_ref[:, :dc]
        g = jnp.dot(xn, w1_ref[:, dc:], preferred_element_type=F32) + b1_ref[:, dc:]
        xe_ref[halo + tt:halo + 2 * tt, :] = a * _sigmoid(g)

    def conv_stage():
        span = tt + halo - sub
        for ph in range(1, sub):
            xs_ref[ph - 1, 0:span, :] = xe_ref[pl.ds(ph, span), :]

        def window(row, size):
            blk, ph = divmod(row, sub)
            if ph == 0:
                return xe_ref[pl.ds(row, size), :]
            return xs_ref[ph - 1, pl.ds(blk * sub, size), :]

        def weight(j):
            return w_ref[j] if rb == sub else jnp.tile(w_ref[j], (rb // sub, 1))

        for r0 in range(0, tt, rb):
            acc = b_ref[...] + weight(0) * window(first + r0, rb)
            for j in range(1, taps):
                acc = acc + weight(j) * window(first + r0 + j, rb)
            mu = jnp.mean(acc, axis=-1, keepdims=True)
            xc = acc - mu
            var = jnp.mean(xc * xc, axis=-1, keepdims=True)
            y = xc * lax.rsqrt(var + LN_EPS) * g_ref[...] + bb_ref[...]
            o_ref[pl.ds(r0, rb), :] = _silu(y).astype(o_ref.dtype)

    glu_stage()
    conv_stage()

    xe_ref[0:halo, :] = xe_ref[tt:tt + halo, :]
    xe_ref[halo:halo + tt, :] = xe_ref[halo + tt:halo + 2 * tt, :]

    @pl.when(i == last_tile)
    def _():
        rows = tail_ref.shape[0]
        tail_ref[...] = xe_ref[halo + 2 * tt - rows:halo + 2 * tt, :]


def _conv_module(x, buf, norm_w, w_pw1_bf16, b_pw1, w_dw, b_dw, ln_g, ln_b, batch, seq_pad, tt):
    d = x.shape[1]
    ch = w_dw.shape[1]
    taps = w_dw.shape[0]
    halo = -(-(taps - 1) // SUBLANES) * SUBLANES
    bufp = jnp.pad(buf, ((0, 0), (halo - (taps - 1), 0), (0, 0)))
    nt = seq_pad // tt
    rb = min(tt, 2 * SUBLANES)
    tail_rows = min(tt, halo)
    const = lambda b, i: (0, 0)
    vec = lambda a: a.reshape(1, -1)
    w_rep = jnp.broadcast_to(w_dw[:, None, :], (taps, SUBLANES, ch))
    return pl.pallas_call(
        functools.partial(_conv_module_kernel, tt=tt, rb=rb),
        grid=(batch, nt + 1),
        in_specs=[pl.BlockSpec((tt, d), lambda b, i: (b * nt + jnp.minimum(i, nt - 1), 0)),
                  pl.BlockSpec((None, halo, ch), lambda b, i: (b, 0, 0)),
                  pl.BlockSpec((1, d), const),
                  pl.BlockSpec((d, 2 * ch), const),
                  pl.BlockSpec((1, 2 * ch), const),
                  pl.BlockSpec((taps, SUBLANES, ch), lambda b, i: (0, 0, 0)),
                  pl.BlockSpec((1, ch), const), pl.BlockSpec((1, ch), const),
                  pl.BlockSpec((1, ch), const)],
        out_specs=[pl.BlockSpec((tt, ch), lambda b, i: (b * nt + jnp.maximum(i - 1, 0), 0)),
                   pl.BlockSpec((None, tail_rows, ch), lambda b, i: (b, 0, 0))],
        out_shape=[jax.ShapeDtypeStruct((batch * seq_pad, ch), BF16),
                   jax.ShapeDtypeStruct((batch, tail_rows, ch), F32)],
        scratch_shapes=[pltpu.VMEM((2 * tt + halo, ch), F32),
                        pltpu.VMEM((SUBLANES - 1, tt + halo - SUBLANES, ch), F32)],
        compiler_params=_cparams("parallel", "arbitrary"),
        name="conv_module",
    )(x, bufp, vec(norm_w), w_pw1_bf16, vec(b_pw1), w_rep, vec(b_dw), vec(ln_g), vec(ln_b))


def _conv_token_kernel(x_ref, buf_ref, nw_ref, w1_ref, b1_ref, w_ref, b_ref, g_ref, bb_ref,
                       o_ref, h_ref):
    dc = o_ref.shape[1]
    taps = w_ref.shape[0]
    xn = _rms(x_ref[...], nw_ref[...], RMS_EPS).astype(BF16)
    a = jnp.dot(xn, w1_ref[:, :dc], preferred_element_type=F32) + b1_ref[:, :dc]
    g = jnp.dot(xn, w1_ref[:, dc:], preferred_element_type=F32) + b1_ref[:, dc:]
    h = a * _sigmoid(g)
    h_ref[...] = h
    acc = b_ref[...] + w_ref[taps - 1:taps, :] * h
    for j in range(taps - 1):
        acc = acc + w_ref[j:j + 1, :] * buf_ref[j]
    mu = jnp.mean(acc, axis=-1, keepdims=True)
    xc = acc - mu
    var = jnp.mean(xc * xc, axis=-1, keepdims=True)
    o_ref[...] = _silu(xc * lax.rsqrt(var + LN_EPS) * g_ref[...] + bb_ref[...])


def _conv_module_token(x, buf, norm_w, w_pw1_bf16, b_pw1, w_dw, b_dw, ln_g, ln_b):
    batch = x.shape[0]
    ch = w_dw.shape[1]
    vec = lambda a: a.reshape(1, -1)
    operands = (x, jnp.transpose(buf, (1, 0, 2)), vec(norm_w), w_pw1_bf16, vec(b_pw1), w_dw,
                vec(b_dw), vec(ln_g), vec(ln_b))
    whole = lambda a: pl.BlockSpec(a.shape, lambda i, nd=a.ndim: (0,) * nd)
    return pl.pallas_call(
        _conv_token_kernel,
        grid=(1,),
        in_specs=[whole(a) for a in operands],
        out_specs=[pl.BlockSpec((batch, ch), lambda i: (0, 0))] * 2,
        out_shape=[jax.ShapeDtypeStruct((batch, ch), F32)] * 2,
        compiler_params=_cparams("arbitrary"),
        name="conv_module_token",
    )(*operands)


def kernel(x_prompt, x_sample, cache_attn_k, cache_attn_v, page_table, state_gdn_conv, state_gdn_s, state_conv_buf, norm_mix, norm_ffn, norm_final, w_in, w_out, gdn_conv_w, gdn_A_log, gdn_dt_bias, gdn_norm_w, lam_q1, lam_k1, lam_q2, lam_k2, diff_subln_w, rel_bias, conv_w_pw1, conv_b_pw1, conv_w_dw, conv_b_dw, conv_ln_g, conv_ln_b, conv_w_pw2, conv_b_pw2, ffn_w_gate, ffn_w_up, ffn_w_down):
    bp, seq, d = x_prompt.shape
    db, dseq, _ = x_sample.shape
    assert dseq == 1
    depth = norm_mix.shape[0]
    ha, dva = cache_attn_k.shape[3], cache_attn_v.shape[4]
    hb, dkb = state_gdn_s.shape[2], state_gdn_s.shape[3]
    c_qkv = state_gdn_conv.shape[3]
    gdn_taps = gdn_conv_w.shape[1]
    conv_taps = conv_w_dw.shape[1]
    d_ff = ffn_w_gate.shape[2]
    mp = bp * seq
    assert 2 * hb <= LANES

    tm_in, tm_ffn = 512, 512
    t_attn = min(256, seq)
    tt_gdn = min(512, seq)
    tt_scan = min(256, seq)
    tt_conv = min(256, seq)
    assert tt_conv >= conv_taps - 1 and seq >= gdn_taps - 1

    xp = x_prompt.reshape(mp, d)
    xs = x_sample.reshape(db, d)
    row = lambda a: a.reshape(1, -1)

    sizes = (ha * dva, ha * dva, ha * dva, c_qkv, hb * dkb, LANES)
    offs = [0]
    for s_ in sizes:
        offs.append(offs[-1] + s_)
    groups = tuple(zip(offs[:-1], sizes))
    inproj_dtypes = (BF16, BF16, BF16, F32, F32, F32)
    inproj_scales = ((dva // 2) ** -0.5 * LOG2E, 1.0, 1.0, 1.0, 1.0, 1.0)
    inproj_outs = ([(wd, 1, dt) for wd, dt in zip(sizes, inproj_dtypes)]
                   + [(dva, ha, F32), (dva, ha, F32)])

    def last_rows(a, n):
        return jnp.stack([a[(b + 1) * seq - n:(b + 1) * seq] for b in range(bp)])

    w_gate_bf, w_up_bf, w_down_bf = (w.astype(BF16) for w in (ffn_w_gate, ffn_w_up, ffn_w_down))

    k_p, v_p, k_s, v_s, gc_p, gc_s, gs_p, gs_s, cb_p, cb_s = ([] for _ in range(10))
    for layer in range(depth):
        if layer % 2 == 0:
            e = layer // 2
            lam_init = 0.8 - 0.6 * math.exp(-0.3 * layer)
            w_in_e = jnp.pad(w_in[e], ((0, 0), (0, offs[-1] - w_in.shape[2]))).astype(BF16)
            outs_p, outs_s = _token_call(
                functools.partial(_inproj_body, groups=groups, scales=inproj_scales, heads=ha,
                                  kv_groups=(1, 2)),
                [xp], [xs], [_whole(row(norm_mix[layer])), _whole(w_in_e)],
                inproj_outs, tm_in, "in_projection")
            qa_p, ka_p, va_p, qkv_p, z_p, ba_p, k4_p, v4_p = outs_p
            qa_s, ka_s, va_s, qkv_s, z_s, ba_s, k4_s, v4_s = outs_s
            lam_vecs = (lam_q1[e], lam_k1[e], lam_q2[e], lam_k2[e])

            oa_p = _attention_prompt(qa_p, ka_p, va_p, rel_bias, lam_vecs, diff_subln_w[e],
                                     bp, seq, ha, lam_init, t=t_attn)
            oa_s = _attention_decode(qa_s.astype(F32), k4_s.reshape(db, ha * dva),
                                     v4_s.reshape(db, ha * dva), cache_attn_k[e], cache_attn_v[e],
                                     page_table, rel_bias, lam_vecs, diff_subln_w[e],
                                     ha, lam_init, pages=min(32, page_table.shape[1]))

            gdn_w = (gdn_conv_w[e], gdn_A_log[e], gdn_dt_bias[e], gdn_norm_w[e])
            ob_p, s_p = _gated_deltanet(
                qkv_p, ba_p, z_p, jnp.zeros((bp, gdn_taps - 1, c_qkv), F32),
                jnp.zeros((bp, hb, dkb, dkb), F32), *gdn_w, bp, seq, hb,
                tt=tt_gdn, scan_batch=bp, scan_tt=tt_scan)
            ob_s, s_s = _gated_deltanet_token(qkv_s, ba_s, z_s, state_gdn_conv[e],
                                              state_gdn_s[e], *gdn_w, hb, nb=min(db, SUBLANES))

            w_o = w_out[e].astype(BF16)
            mix_p, mix_s = [oa_p, ob_p], [oa_s, ob_s]
            mix_w = [_whole(w_o[:ha * dva]), _whole(w_o[ha * dva:]),
                     _whole(jnp.zeros((1, d), F32))]

            k_p.append(k4_p.reshape(bp, seq, ha, dva))
            v_p.append(v4_p.reshape(bp, seq, ha, dva))
            k_s.append(k4_s.reshape(db, 1, ha, dva))
            v_s.append(v4_s.reshape(db, 1, ha, dva))
            gc_p.append(last_rows(qkv_p, gdn_taps - 1))
            gc_s.append(jnp.concatenate([state_gdn_conv[e], qkv_s.reshape(db, 1, c_qkv)],
                                        axis=1)[:, -(gdn_taps - 1):])
            gs_p.append(s_p)
            gs_s.append(s_s)
        else:
            cidx = layer // 2
            dconv = conv_w_dw.shape[2]
            conv_w = (norm_mix[layer], conv_w_pw1[cidx].astype(BF16), conv_b_pw1[cidx],
                      conv_w_dw[cidx], conv_b_dw[cidx], conv_ln_g[cidx], conv_ln_b[cidx])
            hc_p, tail_p = _conv_module(xp, jnp.zeros((bp, conv_taps - 1, dconv), F32), *conv_w,
                                        bp, seq, tt=tt_conv)
            hc_s, hg_s = _conv_module_token(xs, state_conv_buf[cidx], *conv_w)
            mix_p, mix_s = [hc_p], [hc_s]
            mix_w = [_whole(conv_w_pw2[cidx].astype(BF16)), _whole(row(conv_b_pw2[cidx]))]
            cb_p.append(tail_p[:, tail_p.shape[1] - (conv_taps - 1):])
            cb_s.append(jnp.concatenate([state_conv_buf[cidx], hg_s[:, None, :]],
                                        axis=1)[:, -(conv_taps - 1):])
        (xp,), (xs,) = _token_call(
            functools.partial(_mix_ffn_body, final_norm=(layer == depth - 1)),
            mix_p + [xp], mix_s + [xs],
            [_whole(row(norm_ffn[layer])),
             (w_gate_bf, (None, d, d_ff), (layer, 0, 0)),
             (w_up_bf, (None, d, d_ff), (layer, 0, 0)),
             (w_down_bf, (None, d_ff, d), (layer, 0, 0)),
             _whole(row(norm_final))] + mix_w,
            [(d, 1, F32)], tm_ffn, "mixer_out_swiglu_ffn")

    y_prompt = xp.reshape(bp, seq, d)
    y_sample = xs.reshape(db, 1, d)
    return (y_prompt, y_sample, jnp.stack(k_p), jnp.stack(v_p), jnp.stack(k_s), jnp.stack(v_s),
            jnp.stack(gc_p), jnp.stack(gc_s), jnp.stack(gs_p), jnp.stack(gs_s),
            jnp.stack(cb_p), jnp.stack(cb_s))
```
